```python
import math
import jax
import jax.numpy as jnp
from jax import lax

D_MODEL = 1024
BATCH = 8
SEQ = 4096
DEPTH = 4

N_MIXERS = 4
GRID_W = 64
Q_BLOCK = 128
EPS = 1e-6

ATT_HEADS = 16
ATT_KV_HEADS = 4
ATT_HEAD_DIM = D_MODEL // ATT_HEADS
ROPE_THETA = 10000.0

RET_HEADS = 4
RET_QK_DIM = D_MODEL // RET_HEADS
RET_V_DIM = 2 * RET_QK_DIM
RET_CHUNK = 128
RET_THETA = 10000.0

CONV_WIDTH = 31
CONV_PAD = CONV_WIDTH // 2

DIFF_HEADS = 8
DIFF_HEAD_DIM = D_MODEL // (2 * DIFF_HEADS)
T5_BUCKETS = 32
T5_MAX_DIST = 128

D_FF = 2816
N_EXPERTS = 8
TOP_K = 2
D_EXPERT = 3584

N_ATT = (DEPTH + 3) // 4
N_RET = (DEPTH + 2) // 4
N_CNV = (DEPTH + 1) // 4
N_DIF = DEPTH // 4
N_DENSE = (DEPTH + 1) // 2
N_MOE = DEPTH // 2

kernel_name = 'hybrid_interleaved_encoder_trunk'


def rms_norm(x, gain=None):
    xf = x.astype(jnp.float32)
    y = xf * lax.rsqrt(jnp.mean(xf * xf, axis=-1, keepdims=True) + EPS)
    if gain is not None:
        y = y * gain.astype(jnp.float32)
    return y.astype(x.dtype)


def layer_norm(x, gain, bias):
    xf = x.astype(jnp.float32)
    mu = jnp.mean(xf, axis=-1, keepdims=True)
    var = jnp.mean(jnp.square(xf - mu), axis=-1, keepdims=True)
    y = (xf - mu) * lax.rsqrt(var + EPS) * gain.astype(jnp.float32) + bias.astype(jnp.float32)
    return y.astype(x.dtype)


def rotate(x, ang):
    half = x.shape[-1] // 2
    cos = jnp.cos(ang)[:, None, :]
    sin = jnp.sin(ang)[:, None, :]
    xf = x.astype(jnp.float32)
    x1, x2 = xf[..., :half], xf[..., half:]
    return jnp.concatenate([x1 * cos - x2 * sin, x1 * sin + x2 * cos], axis=-1).astype(x.dtype)


def axial_rope(x, row, col):
    half = x.shape[-1] // 2
    n_freq = half // 2
    inv = ROPE_THETA ** (-jnp.arange(n_freq, dtype=jnp.float32) / n_freq)
    ang_r = row.astype(jnp.float32)[:, None] * inv
    ang_c = col.astype(jnp.float32)[:, None] * inv
    return jnp.concatenate([rotate(x[..., :half], ang_r), rotate(x[..., half:], ang_c)], axis=-1)


def swiglu(h, w_gate, w_up, w_down):
    return (jax.nn.silu(h @ w_gate) * (h @ w_up)) @ w_down


def gqa_axial_attention(h, w_qkv, q_gain, k_gain, w_o, row, col):
    b, s, _ = h.shape
    n_blk = s // Q_BLOCK
    groups = ATT_HEADS // ATT_KV_HEADS
    q_cols = ATT_HEADS * ATT_HEAD_DIM
    kv_cols = ATT_KV_HEADS * ATT_HEAD_DIM
    qkv = h @ w_qkv
    q = qkv[..., :q_cols].reshape(b, s, ATT_HEADS, ATT_HEAD_DIM)
    k = qkv[..., q_cols:q_cols + kv_cols].reshape(b, s, ATT_KV_HEADS, ATT_HEAD_DIM)
    v = qkv[..., q_cols + kv_cols:].reshape(b, s, ATT_KV_HEADS, ATT_HEAD_DIM)
    q = axial_rope(rms_norm(q, q_gain), row, col)
    k = axial_rope(rms_norm(k, k_gain), row, col)
    scale = ATT_HEAD_DIM ** -0.5
    q_blocks = q.reshape(b, n_blk, Q_BLOCK, ATT_KV_HEADS, groups, ATT_HEAD_DIM).transpose(1, 0, 2, 3, 4, 5)

    def attend(q_blk):
        logits = jnp.einsum('bqkgd,bskd->bkgqs', q_blk, k).astype(jnp.float32) * scale
        probs = jax.nn.softmax(logits, axis=-1).astype(v.dtype)
        return jnp.einsum('bkgqs,bskd->bqkgd', probs, v)

    o = lax.map(attend, q_blocks)
    o = o.transpose(1, 0, 2, 3, 4, 5).reshape(b, s, q_cols)
    return o @ w_o


def retention_scan(q, k, v, log_gamma, include_diag):
    b, nh, s, dk = q.shape
    dv = v.shape[-1]
    n_chunk = s // RET_CHUNK
    idx = jnp.arange(RET_CHUNK, dtype=jnp.float32)
    dist = idx[:, None] - idx[None, :]
    keep = dist >= 0 if include_diag else dist > 0
    lg = log_gamma[:, None, None]
    decay = jnp.where(keep[None], jnp.exp(lg * jnp.maximum(dist, 0.0)[None]), 0.0)
    qc = q.astype(jnp.float32).reshape(b, nh, n_chunk, RET_CHUNK, dk)
    kc = k.astype(jnp.float32).reshape(b, nh, n_chunk, RET_CHUNK, dk)
    vc = v.astype(jnp.float32).reshape(b, nh, n_chunk, RET_CHUNK, dv)
    scores = jnp.einsum('bhncd,bhnmd->bhncm', qc, kc) * decay[None, :, None]
    o_intra = jnp.einsum('bhncm,bhnmv->bhncv', scores, vc)
    lgh = log_gamma[:, None]
    xi = jnp.exp(lgh * (idx + 1.0))[None, :, :, None]
    zeta = jnp.exp(lgh * (RET_CHUNK - 1.0 - idx))[None, :, :, None]
    chunk_decay = jnp.exp(log_gamma * RET_CHUNK)[None, :, None, None]

    def step(state, inp):
        q_i, k_i, v_i = inp
        o_i = jnp.einsum('bhcd,bhdv->bhcv', q_i * xi, state)
        state = state * chunk_decay + jnp.einsum('bhcd,bhcv->bhdv', k_i * zeta, v_i)
        return state, o_i

    init = jnp.zeros((b, nh, dk, dv), jnp.float32)
    xs = (jnp.moveaxis(qc, 2, 0), jnp.moveaxis(kc, 2, 0), jnp.moveaxis(vc, 2, 0))
    _, o_cross = lax.scan(step, init, xs)
    return (o_intra + jnp.moveaxis(o_cross, 0, 2)).reshape(b, nh, s, dv)


def retention_mixer(h, w_in, decay_logit, gn_gain, w_o):
    b, s, _ = h.shape
    qk_cols = RET_HEADS * RET_QK_DIM
    v_cols = RET_HEADS * RET_V_DIM
    proj = h @ w_in
    q = proj[..., :qk_cols].reshape(b, s, RET_HEADS, RET_QK_DIM)
    k = proj[..., qk_cols:2 * qk_cols].reshape(b, s, RET_HEADS, RET_QK_DIM)
    v = proj[..., 2 * qk_cols:2 * qk_cols + v_cols].reshape(b, s, RET_HEADS, RET_V_DIM)
    g = proj[..., 2 * qk_cols + v_cols:]
    pos = jnp.arange(s, dtype=jnp.float32)
    inv = 1.0 / (RET_THETA ** jnp.linspace(0.0, 1.0, RET_QK_DIM // 2, dtype=jnp.float32))
    ang = pos[:, None] * inv
    q = rotate(q, ang).transpose(0, 2, 1, 3)
    k = (rotate(k, ang) * RET_QK_DIM ** -0.5).transpose(0, 2, 1, 3)
    v = v.transpose(0, 2, 1, 3)
    log_gamma = jax.nn.log_sigmoid(decay_logit.astype(jnp.float32))
    fwd = retention_scan(q, k, v, log_gamma[0], True)
    bwd = jnp.flip(retention_scan(jnp.flip(q, 2), jnp.flip(k, 2), jnp.flip(v, 2), log_gamma[1], False), 2)
    y = rms_norm((fwd + bwd).transpose(0, 2, 1, 3), gn_gain).astype(h.dtype)
    y = y.reshape(b, s, v_cols) * jax.nn.silu(g)
    return y @ w_o


def conformer_conv(h, w_pw1, b_pw1, w_dw, b_dw, ln_gain, ln_bias, w_pw2, b_pw2):
    u = h @ w_pw1 + b_pw1
    a, gate = jnp.split(u, 2, axis=-1)
    u = (a * jax.nn.sigmoid(gate)).astype(w_dw.dtype)
    u = lax.conv_general_dilated(u, w_dw, window_strides=(1,), padding=((CONV_PAD, CONV_PAD),),
                                 dimension_numbers=('NWC', 'WIO', 'NWC'),
                                 feature_group_count=u.shape[-1]) + b_dw
    u = jax.nn.silu(layer_norm(u, ln_gain, ln_bias))
    return u @ w_pw2 + b_pw2


def t5_bucket(rel):
    half = T5_BUCKETS // 2
    exact = half // 2
    n = jnp.abs(rel)
    log_ratio = jnp.log(jnp.maximum(n, 1).astype(jnp.float32) / exact) / math.log(T5_MAX_DIST / exact)
    large = jnp.minimum(exact + (log_ratio * (half - exact)).astype(jnp.int32), half - 1)
    return jnp.where(rel > 0, half, 0) + jnp.where(n < exact, n, large)


def diff_attention(h, w_qkv, lam_q1, lam_k1, lam_q2, lam_k2, subln_gain, w_o, rel_bias, lambda_init):
    b, s, d = h.shape
    n_blk = s // Q_BLOCK
    f32 = jnp.float32
    qkv = h @ w_qkv
    q = qkv[..., :d].reshape(b, s, DIFF_HEADS, 2, DIFF_HEAD_DIM)
    k = qkv[..., d:2 * d].reshape(b, s, DIFF_HEADS, 2, DIFF_HEAD_DIM)
    v = qkv[..., 2 * d:].reshape(b, s, DIFF_HEADS, 2 * DIFF_HEAD_DIM)
    lam = (jnp.exp(jnp.sum(lam_q1.astype(f32) * lam_k1.astype(f32)))
           - jnp.exp(jnp.sum(lam_q2.astype(f32) * lam_k2.astype(f32))) + lambda_init)
    scale = DIFF_HEAD_DIM ** -0.5
    q_blocks = q.reshape(b, n_blk, Q_BLOCK, DIFF_HEADS, 2, DIFF_HEAD_DIM).transpose(1, 0, 2, 3, 4, 5)
    starts = jnp.arange(n_blk, dtype=jnp.int32) * Q_BLOCK
    key_pos = jnp.arange(s, dtype=jnp.int32)
    table = rel_bias.astype(f32)

    def attend(args):
        q_blk, start = args
        q_pos = start + jnp.arange(Q_BLOCK, dtype=jnp.int32)
        bias = jnp.moveaxis(table[t5_bucket(key_pos[None, :] - q_pos[:, None])], -1, 0)
        logits = jnp.einsum('bqhcd,bshcd->bhcqs', q_blk, k).astype(f32) * scale + bias[None, :, None]
        probs = jax.nn.softmax(logits, axis=-1)
        weights = (probs[:, :, 0] - lam * probs[:, :, 1]).astype(v.dtype)
        return jnp.einsum('bhqs,bshv->bqhv', weights, v)

    o = lax.map(attend, (q_blocks, starts))
    o = o.transpose(1, 0, 2, 3, 4).reshape(b, s, DIFF_HEADS, 2 * DIFF_HEAD_DIM)
    o = rms_norm(o, subln_gain) * (1.0 - lambda_init)
    return o.reshape(b, s, d) @ w_o


def moe_swiglu(h, w_router, b_router, w_gate, w_up, w_down):
    logits = (h @ w_router).astype(jnp.float32) + b_router.astype(jnp.float32)
    top_v, top_i = lax.top_k(logits, TOP_K)
    top_w = jax.nn.softmax(top_v, axis=-1)
    combine = jnp.sum(jax.nn.one_hot(top_i, N_EXPERTS, dtype=jnp.float32) * top_w[..., None], axis=-2)
    combine = combine.astype(h.dtype)
    out = jnp.zeros_like(h)
    for e in range(N_EXPERTS):
        out = out + combine[..., e:e + 1] * swiglu(h, w_gate[e], w_up[e], w_down[e])
    return out


def setup_inputs(seed: int = 0) -> dict:
    key = jax.random.key(seed)
    ks = iter(jax.random.split(key, 48))
    d = D_MODEL

    def nrm(shape, scale):
        return jax.random.normal(next(ks), shape, jnp.float32) * scale

    def gain(shape):
        return 1.0 + nrm(shape, 0.02)

    att_cols = (ATT_HEADS + 2 * ATT_KV_HEADS) * ATT_HEAD_DIM
    att_o = ATT_HEADS * ATT_HEAD_DIM
    ret_cols = 2 * RET_HEADS * RET_QK_DIM + 2 * RET_HEADS * RET_V_DIM
    ret_v_cols = RET_HEADS * RET_V_DIM
    gamma_logit = jnp.log(2.0 ** (5.0 + jnp.arange(RET_HEADS, dtype=jnp.float32)) - 1.0)
    return {
        'x': nrm((BATCH, SEQ, d), 1.0),
        'c': nrm((BATCH, d), 1.0),
        'mod_w': nrm((DEPTH, d, 6 * d), 0.5 * d ** -0.5),
        'mod_b': nrm((DEPTH, 6 * d), 0.02),
        'att_w_qkv': nrm((N_ATT, d, att_cols), d ** -0.5),
        'att_q_gain': gain((N_ATT, ATT_HEAD_DIM)),
        'att_k_gain': gain((N_ATT, ATT_HEAD_DIM)),
        'att_w_o': nrm((N_ATT, att_o, d), att_o ** -0.5),
        'ret_w_in': nrm((N_RET, d, ret_cols), d ** -0.5),
        'ret_decay_logit': gamma_logit + nrm((N_RET, 2, RET_HEADS), 0.01),
        'ret_gn_gain': gain((N_RET, RET_V_DIM)),
        'ret_w_o': nrm((N_RET, ret_v_cols, d), ret_v_cols ** -0.5),
        'cnv_w_pw1': nrm((N_CNV, d, 2 * d), d ** -0.5),
        'cnv_b_pw1': nrm((N_CNV, 2 * d), 0.02),
        'cnv_w_dw': nrm((N_CNV, CONV_WIDTH, 1, d), CONV_WIDTH ** -0.5),
        'cnv_b_dw': nrm((N_CNV, d), 0.02),
        'cnv_ln_gain': gain((N_CNV, d)),
        'cnv_ln_bias': nrm((N_CNV, d), 0.02),
        'cnv_w_pw2': nrm((N_CNV, d, d), d ** -0.5),
        'cnv_b_pw2': nrm((N_CNV, d), 0.02),
        'dif_w_qkv': nrm((N_DIF, d, 3 * d), d ** -0.5),
        'dif_lam_q1': nrm((N_DIF, DIFF_HEAD_DIM), 0.1),
        'dif_lam_k1': nrm((N_DIF, DIFF_HEAD_DIM), 0.1),
        'dif_lam_q2': nrm((N_DIF, DIFF_HEAD_DIM), 0.1),
        'dif_lam_k2': nrm((N_DIF, DIFF_HEAD_DIM), 0.1),
        'dif_subln_gain': gain((N_DIF, 2 * DIFF_HEAD_DIM)),
        'dif_w_o': nrm((N_DIF, d, d), d ** -0.5),
        'rel_bias': nrm((T5_BUCKETS, DIFF_HEADS), 0.5),
        'ffn_w_gate': nrm((N_DENSE, d, D_FF), d ** -0.5),
        'ffn_w_up': nrm((N_DENSE, d, D_FF), d ** -0.5),
        'ffn_w_down': nrm((N_DENSE, D_FF, d), D_FF ** -0.5),
        'moe_w_router': nrm((N_MOE, d, N_EXPERTS), d ** -0.5),
        'moe_b_router': nrm((N_MOE, N_EXPERTS), 0.01),
        'moe_w_gate': nrm((N_MOE, N_EXPERTS, d, D_EXPERT), d ** -0.5),
        'moe_w_up': nrm((N_MOE, N_EXPERTS, d, D_EXPERT), d ** -0.5),
        'moe_w_down': nrm((N_MOE, N_EXPERTS, D_EXPERT, d), D_EXPERT ** -0.5),
        'final_gain': gain((d,)),
    }


def reference(x, c, mod_w, mod_b, att_w_qkv, att_q_gain, att_k_gain, att_w_o,
              ret_w_in, ret_decay_logit, ret_gn_gain, ret_w_o,
              cnv_w_pw1, cnv_b_pw1, cnv_w_dw, cnv_b_dw, cnv_ln_gain, cnv_ln_bias, cnv_w_pw2, cnv_b_pw2,
              dif_w_qkv, dif_lam_q1, dif_lam_k1, dif_lam_q2, dif_lam_k2, dif_subln_gain, dif_w_o,
              rel_bias, ffn_w_gate, ffn_w_up, ffn_w_down,
              moe_w_router, moe_b_router, moe_w_gate, moe_w_up, moe_w_down, final_gain):
    s = x.shape[1]
    n_rows = s // GRID_W
    row = jnp.repeat(jnp.arange(n_rows, dtype=jnp.int32), GRID_W)
    col = jnp.tile(jnp.arange(GRID_W, dtype=jnp.int32), n_rows)
    cond = jax.nn.silu(c)
    for i in range(DEPTH):
        mod = (cond @ mod_w[i] + mod_b[i])[:, None, :]
        sh1, sc1, g1, sh2, sc2, g2 = jnp.split(mod, 6, axis=-1)
        h = rms_norm(x) * (1.0 + sc1) + sh1
        kind, j = i % N_MIXERS, i // N_MIXERS
        if kind == 0:
            y = gqa_axial_attention(h, att_w_qkv[j], att_q_gain[j], att_k_gain[j], att_w_o[j], row, col)
        elif kind == 1:
            y = retention_mixer(h, ret_w_in[j], ret_decay_logit[j], ret_gn_gain[j], ret_w_o[j])
        elif kind == 2:
            y = conformer_conv(h, cnv_w_pw1[j], cnv_b_pw1[j], cnv_w_dw[j], cnv_b_dw[j],
                               cnv_ln_gain[j], cnv_ln_bias[j], cnv_w_pw2[j], cnv_b_pw2[j])
        else:
            lambda_init = 0.8 - 0.6 * math.exp(-0.3 * i)
            y = diff_attention(h, dif_w_qkv[j], dif_lam_q1[j], dif_lam_k1[j], dif_lam_q2[j], dif_lam_k2[j],
                               dif_subln_gain[j], dif_w_o[j], rel_bias, lambda_init)
        x = x + g1 * y
        h = rms_norm(x) * (1.0 + sc2) + sh2
        if i % 2 == 0:
            y = swiglu(h, ffn_w_gate[i // 2], ffn_w_up[i // 2], ffn_w_down[i // 2])
        else:
            m = i // 2
            y = moe_swiglu(h, moe_w_router[m], moe_b_router[m], moe_w_gate[m], moe_w_up[m], moe_w_down[m])
        x = x + g2 * y
    return rms_norm(x, final_gain)
```

```python
import functools
import math

import jax
import jax.numpy as jnp
from jax import lax
from jax.experimental import pallas as pl
from jax.experimental.pallas import tpu as pltpu

F32 = jnp.float32
BF16 = jnp.bfloat16

EPS = 1e-6
GRID_W = 64

ATT_HEADS = 16
ATT_KV_HEADS = 4
ATT_HEAD_DIM = 64
ROPE_THETA = 10000.0

RET_HEADS = 4
RET_QK_DIM = 256
RET_V_DIM = 512
RET_THETA = 10000.0
RET_CHUNK = 256

CONV_WIDTH = 31
CONV_PAD = CONV_WIDTH // 2
CONV_HALO = 16

DIFF_HEADS = 8
DIFF_HEAD_DIM = 64
T5_BUCKETS = 32
T5_MAX_DIST = 128

N_EXPERTS = 8
LANES = 128
KEY_CHUNK = 256
VMEM_LIMIT_BYTES = 56 * 1024 * 1024


def _cparams(*sem):
    return pltpu.CompilerParams(dimension_semantics=sem, vmem_limit_bytes=VMEM_LIMIT_BYTES)


def _silu(v):
    return v * jax.nn.sigmoid(v)


def _norm_mod(x, mod, which):
    shift = mod[3 * which:3 * which + 1]
    scale = mod[3 * which + 1:3 * which + 2]
    y = x * lax.rsqrt(jnp.mean(x * x, axis=-1, keepdims=True) + EPS)
    return y * (1.0 + scale) + shift


def _split_dot(a, b):
    hi = a.astype(BF16)
    lo = (a - hi.astype(F32)).astype(BF16)
    return (jnp.dot(hi, b, preferred_element_type=F32) + jnp.dot(lo, b, preferred_element_type=F32))


def _mod_kernel(c_ref, w_ref, b_ref, o_ref):
    cond = _silu(c_ref[...])
    o_ref[0] = jnp.dot(cond, w_ref[0], preferred_element_type=F32, precision=lax.Precision.HIGHEST) + b_ref[0]


def _modulation(c, mod_w, mod_b):
    depth, d, d6 = mod_w.shape
    b = c.shape[0]
    tn = d
    out = pl.pallas_call(
        _mod_kernel,
        grid=(depth, d6 // tn),
        in_specs=[pl.BlockSpec((b, d), lambda i, j: (0, 0)),
                  pl.BlockSpec((1, d, tn), lambda i, j: (i, 0, j)),
                  pl.BlockSpec((1, 1, tn), lambda i, j: (i, 0, j))],
        out_specs=pl.BlockSpec((1, b, tn), lambda i, j: (i, 0, j)),
        out_shape=jax.ShapeDtypeStruct((depth, b, d6), F32),
        compiler_params=_cparams("arbitrary", "arbitrary"),
        name="modulation",
    )(c, mod_w, mod_b.reshape(depth, 1, d6))
    return out.reshape(depth, b, 6, d)


def _proj_kernel(x_ref, mod_ref, w_ref, *rest, which, n_extra, epilogue):
    extras = rest[:n_extra]
    outs = rest[n_extra:-1]
    h_scr = rest[-1]
    j = pl.program_id(1)

    @pl.when(j == 0)
    def _():
        h_scr[...] = _norm_mod(x_ref[...], mod_ref[0], which).astype(BF16)

    acc = jnp.dot(h_scr[...], w_ref[...], preferred_element_type=F32)
    epilogue(acc, j, extras, outs)


def _project(x, mod, which, w, extras, extra_specs, epilogue, out_shapes, out_specs, tm, tn, seq, name):
    n, d = x.shape
    c = w.shape[1]
    tiles_per_seq = seq // tm
    kern = functools.partial(_proj_kernel, which=which, n_extra=len(extras), epilogue=epilogue)
    return pl.pallas_call(
        kern,
        grid=(n // tm, c // tn),
        in_specs=[pl.BlockSpec((tm, d), lambda i, j: (i, 0)),
                  pl.BlockSpec((1, 6, d), lambda i, j: (i // tiles_per_seq, 0, 0)),
                  pl.BlockSpec((d, tn), lambda i, j: (0, j))] + extra_specs,
        out_specs=out_specs,
        out_shape=out_shapes,
        scratch_shapes=[pltpu.VMEM((tm, d), BF16)],
        compiler_params=_cparams("parallel", "arbitrary"),
        name=name,
    )(x, mod, w, *extras)


def _plain_epilogue(acc, j, extras, outs):
    outs[0][...] = acc.astype(outs[0].dtype)


def _oproj_kernel(*refs, which, gated, has_bias):
    refs = list(refs)
    y_ref = refs.pop(0)
    g_ref = refs.pop(0) if gated else None
    w_ref = refs.pop(0)
    b_ref = refs.pop(0) if has_bias else None
    x_ref, mod_ref, o_ref = refs
    y = y_ref[...]
    if gated:
        y = (y.astype(F32) * _silu(g_ref[...].astype(F32))).astype(BF16)
    z = jnp.dot(y, w_ref[...], preferred_element_type=F32)
    if has_bias:
        z = z + b_ref[...]
    gate = mod_ref[0][3 * which + 2:3 * which + 3]
    o_ref[...] = x_ref[...] + gate * z


def _out_project(y, y_col, k, w, x, mod, which, seq, tm, name, gate_src=None, gate_col=0, bias=None):
    n, d = x.shape
    tiles_per_seq = seq // tm
    args = [y]
    specs = [pl.BlockSpec((tm, k), lambda i: (i, y_col))]
    if gate_src is not None:
        args.append(gate_src)
        specs.append(pl.BlockSpec((tm, k), lambda i: (i, gate_col)))
    args.append(w)
    specs.append(pl.BlockSpec((k, d), lambda i: (0, 0)))
    if bias is not None:
        args.append(bias.reshape(1, d))
        specs.append(pl.BlockSpec((1, d), lambda i: (0, 0)))
    args += [x, mod]
    specs += [pl.BlockSpec((tm, d), lambda i: (i, 0)),
              pl.BlockSpec((1, 6, d), lambda i: (i // tiles_per_seq, 0, 0))]
    kern = functools.partial(_oproj_kernel, which=which, gated=gate_src is not None, has_bias=bias is not None)
    return pl.pallas_call(
        kern,
        grid=(n // tm,),
        in_specs=specs,
        out_specs=pl.BlockSpec((tm, d), lambda i: (i, 0)),
        out_shape=jax.ShapeDtypeStruct((n, d), F32),
        compiler_params=_cparams("parallel"),
        name=name,
    )(*args)


def _ffn_kernel(x_ref, mod_ref, wg_ref, wu_ref, wd_ref, o_ref, h_scr, acc_scr):
    f = pl.program_id(1)

    @pl.when(f == 0)
    def _():
        h_scr[...] = _norm_mod(x_ref[...], mod_ref[0], 1).astype(BF16)
        acc_scr[...] = jnp.zeros_like(acc_scr)

    h = h_scr[...]
    g = jnp.dot(h, wg_ref[...], preferred_element_type=F32)
    u = jnp.dot(h, wu_ref[...], preferred_element_type=F32)
    a = (_silu(g) * u).astype(BF16)
    acc_scr[...] += jnp.dot(a, wd_ref[...], preferred_element_type=F32)

    @pl.when(f == pl.num_programs(1) - 1)
    def _():
        o_ref[...] = x_ref[...] + mod_ref[0][5:6] * acc_scr[...]


def _ffn(x, mod, wg, wu, wd, seq, tm, tf):
    n, d = x.shape
    ff = wg.shape[1]
    tiles_per_seq = seq // tm
    return pl.pallas_call(
        _ffn_kernel,
        grid=(n // tm, ff // tf),
        in_specs=[pl.BlockSpec((tm, d), lambda i, f: (i, 0)),
                  pl.BlockSpec((1, 6, d), lambda i, f: (i // tiles_per_seq, 0, 0)),
                  pl.BlockSpec((d, tf), lambda i, f: (0, f)),
                  pl.BlockSpec((d, tf), lambda i, f: (0, f)),
                  pl.BlockSpec((tf, d), lambda i, f: (f, 0))],
        out_specs=pl.BlockSpec((tm, d), lambda i, f: (i, 0)),
        out_shape=jax.ShapeDtypeStruct((n, d), F32),
        scratch_shapes=[pltpu.VMEM((tm, d), BF16), pltpu.VMEM((tm, d), F32)],
        compiler_params=_cparams("parallel", "arbitrary"),
        name="dense_ffn",
    )(x, mod, wg, wu, wd)


def _router_kernel(x_ref, mod_ref, wr_ref, br_ref, comb_ref):
    h = _norm_mod(x_ref[...], mod_ref[0], 1)
    logits = jnp.dot(h, wr_ref[...], preferred_element_type=F32, precision=lax.Precision.HIGHEST) + br_ref[...]
    lane = lax.broadcasted_iota(jnp.int32, logits.shape, 1)
    neg = jnp.float32(-jnp.inf)
    logits = jnp.where(lane < N_EXPERTS, logits, neg)
    m1 = jnp.max(logits, axis=-1, keepdims=True)
    i1 = jnp.min(jnp.where(logits == m1, lane, LANES), axis=-1, keepdims=True)
    rest = jnp.where(lane == i1, neg, logits)
    m2 = jnp.max(rest, axis=-1, keepdims=True)
    i2 = jnp.min(jnp.where(rest == m2, lane, LANES), axis=-1, keepdims=True)
    e2 = jnp.exp(m2 - m1)
    w1 = 1.0 / (1.0 + e2)
    w2 = e2 / (1.0 + e2)
    comb_ref[...] = jnp.where(lane == i1, w1, 0.0) + jnp.where(lane == i2, w2, 0.0)


def _router(x, mod, w_router, b_router, seq, tm):
    n, d = x.shape
    tiles_per_seq = seq // tm
    wr = jnp.pad(w_router, ((0, 0), (0, LANES - N_EXPERTS)))
    br = jnp.pad(b_router, (0, LANES - N_EXPERTS)).reshape(1, LANES)
    return pl.pallas_call(
        _router_kernel,
        grid=(n // tm,),
        in_specs=[pl.BlockSpec((tm, d), lambda i: (i, 0)),
                  pl.BlockSpec((1, 6, d), lambda i: (i // tiles_per_seq, 0, 0)),
                  pl.BlockSpec((d, LANES), lambda i: (0, 0)),
                  pl.BlockSpec((1, LANES), lambda i: (0, 0))],
        out_specs=pl.BlockSpec((tm, LANES), lambda i: (i, 0)),
        out_shape=jax.ShapeDtypeStruct((n, LANES), F32),
        compiler_params=_cparams("parallel"),
        name="moe_router",
    )(x, mod, wr, br)


def _moe_dense_kernel(x_ref, mod_ref, comb_ref, wg_ref, wu_ref, wd_ref, o_ref, h_scr, acc_scr):
    e = pl.program_id(1)
    f = pl.program_id(2)

    @pl.when((e == 0) & (f == 0))
    def _():
        h_scr[...] = _norm_mod(x_ref[...], mod_ref[0], 1).astype(BF16)
        acc_scr[...] = jnp.zeros_like(acc_scr)

    comb = comb_ref[...]
    lane = lax.broadcasted_iota(jnp.int32, comb.shape, 1)
    cw = jnp.sum(jnp.where(lane == e, comb, 0.0), axis=-1, keepdims=True)
    h = h_scr[...]
    g = jnp.dot(h, wg_ref[0], preferred_element_type=F32)
    u = jnp.dot(h, wu_ref[0], preferred_element_type=F32)
    a = (_silu(g) * u * cw).astype(BF16)
    acc_scr[...] += jnp.dot(a, wd_ref[0], preferred_element_type=F32)

    @pl.when((e == pl.num_programs(1) - 1) & (f == pl.num_programs(2) - 1))
    def _():
        o_ref[...] = x_ref[...] + mod_ref[0][5:6] * acc_scr[...]


def _moe_dense(x, mod, comb, wg, wu, wd, seq, tm, tf):
    n, d = x.shape
    n_e, _, ff = wg.shape
    tiles_per_seq = seq // tm
    return pl.pallas_call(
        _moe_dense_kernel,
        grid=(n // tm, n_e, ff // tf),
        in_specs=[pl.BlockSpec((tm, d), lambda i, e, f: (i, 0)),
                  pl.BlockSpec((1, 6, d), lambda i, e, f: (i // tiles_per_seq, 0, 0)),
                  pl.BlockSpec((tm, LANES), lambda i, e, f: (i, 0)),
                  pl.BlockSpec((1, d, tf), lambda i, e, f: (e, 0, f)),
                  pl.BlockSpec((1, d, tf), lambda i, e, f: (e, 0, f)),
                  pl.BlockSpec((1, tf, d), lambda i, e, f: (e, f, 0))],
        out_specs=pl.BlockSpec((tm, d), lambda i, e, f: (i, 0)),
        out_shape=jax.ShapeDtypeStruct((n, d), F32),
        scratch_shapes=[pltpu.VMEM((tm, d), BF16), pltpu.VMEM((tm, d), F32)],
        compiler_params=_cparams("parallel", "arbitrary", "arbitrary"),
        name="moe_experts",
    )(x, mod, comb, wg, wu, wd)


def _moe(x, mod, w_router, b_router, w_gate, w_up, w_down, seq):
    comb = _router(x, mod, w_router, b_router, seq, 512)
    return _moe_dense(x, mod, comb, w_gate.astype(BF16), w_up.astype(BF16), w_down.astype(BF16), seq, 512, 896)


def _softmax_pv(lhs, k_ref, v_ref, s_scr, n_chunks, bias_fn=None):
    rows = lhs.shape[0]

    def logits_body(j, m):
        start = pl.multiple_of(j * KEY_CHUNK, KEY_CHUNK)
        kc = k_ref[pl.ds(start, KEY_CHUNK), :]
        s = lax.dot_general(lhs, kc, (((1,), (1,)), ((), ())), preferred_element_type=F32)
        if bias_fn is not None:
            s = bias_fn(s, j)
        s_scr[j] = s
        return jnp.maximum(m, s)

    m = lax.fori_loop(0, n_chunks, logits_body, jnp.full((rows, KEY_CHUNK), -jnp.inf, F32))
    m = jnp.max(m, axis=-1, keepdims=True)

    def pv_body(j, carry):
        l, acc = carry
        start = pl.multiple_of(j * KEY_CHUNK, KEY_CHUNK)
        p = jnp.exp(s_scr[j] - m)
        vc = v_ref[pl.ds(start, KEY_CHUNK), :]
        acc = acc + jnp.dot(p.astype(BF16), vc, preferred_element_type=F32)
        return l + p, acc

    l, acc = lax.fori_loop(0, n_chunks, pv_body,
                           (jnp.zeros((rows, KEY_CHUNK), F32), jnp.zeros((rows, LANES), F32)))
    l = jnp.sum(l, axis=-1, keepdims=True)
    return acc / l


def _att_qkv_epilogue(acc, j, extras, outs, n_qk):
    gain_ref, cos_ref, sin_ref, g1_ref, g2_ref = extras
    q_ref, k_ref, v_ref = outs
    qk = acc[:, :n_qk]
    ms = _split_dot(qk * qk, g1_ref[...])
    r = lax.rsqrt(ms + EPS)
    y = qk * _split_dot(r, g2_ref[...]) * gain_ref[...]
    cos = cos_ref[...]
    sin = sin_ref[...]
    lane = lax.broadcasted_iota(jnp.int32, cos.shape, 1)
    first_half = (lane & (ATT_HEAD_DIM // 2)) == 0
    nq = q_ref.shape[1]
    for c in range(n_qk // LANES):
        yc = y[:, c * LANES:(c + 1) * LANES]
        partner = jnp.where(first_half, pltpu.roll(yc, LANES - ATT_HEAD_DIM // 2, 1),
                            pltpu.roll(yc, ATT_HEAD_DIM // 2, 1))
        out = (yc * cos + partner * sin).astype(BF16)
        if c * LANES < nq:
            q_ref[:, c * LANES:(c + 1) * LANES] = out
        else:
            k_ref[:, c * LANES - nq:(c + 1) * LANES - nq] = out
    v_ref[...] = acc[:, n_qk:].astype(BF16)


def _att_kernel(q_ref, k_ref, v_ref, o_ref, s_scr, *, n_chunks):
    tq = q_ref.shape[0]
    q = q_ref[...]
    lane = lax.broadcasted_iota(jnp.int32, (tq, LANES), 1)
    low = lane < ATT_HEAD_DIM
    zero = jnp.zeros((tq, LANES), BF16)
    parts = []
    for pair in range(2):
        qp = q[:, pair * LANES:(pair + 1) * LANES]
        parts.append(jnp.where(low, qp, zero))
        parts.append(jnp.where(low, zero, qp))
    lhs = jnp.concatenate(parts, axis=0)
    o = _softmax_pv(lhs, k_ref, v_ref, s_scr, n_chunks)
    for pair in range(2):
        o_ref[:, pair * LANES:(pair + 1) * LANES] = jnp.where(
            low, o[(2 * pair) * tq:(2 * pair + 1) * tq], o[(2 * pair + 1) * tq:(2 * pair + 2) * tq]).astype(BF16)


def _attention_mixer(x, mod, w_qkv, q_gain, k_gain, w_o, batch, seq):
    n, d = x.shape
    hd = ATT_HEAD_DIM
    half = hd // 2
    n_freq = half // 2
    groups = ATT_HEADS // ATT_KV_HEADS
    perm = jnp.concatenate([jnp.arange(n_freq), half + jnp.arange(n_freq),
                            n_freq + jnp.arange(n_freq), half + n_freq + jnp.arange(n_freq)])
    q_cols = (jnp.arange(ATT_HEADS)[:, None] * hd + perm[None, :]).reshape(-1)
    k_base = ATT_HEADS * hd
    v_base = k_base + ATT_KV_HEADS * hd
    kv_rep = jnp.repeat(jnp.arange(ATT_KV_HEADS), 2)
    k_cols = (k_base + kv_rep[:, None] * hd + perm[None, :]).reshape(-1)
    v_cols = (v_base + kv_rep[:, None] * hd + jnp.arange(hd)[None, :]).reshape(-1)
    w = w_qkv[:, jnp.concatenate([q_cols, k_cols, v_cols])].astype(BF16)
    n_q = ATT_HEADS * hd
    n_kv = 2 * ATT_KV_HEADS * hd
    n_qk = n_q + n_kv
    gain_row = jnp.concatenate([jnp.tile(q_gain[perm] * hd ** -0.5, ATT_HEADS),
                                jnp.tile(k_gain[perm], 2 * ATT_KV_HEADS)]).reshape(1, n_qk).astype(F32)
    pos = jnp.arange(seq, dtype=jnp.int32)
    row = (pos // GRID_W).astype(F32)
    col = (pos % GRID_W).astype(F32)
    inv = ROPE_THETA ** (-jnp.arange(n_freq, dtype=F32) / n_freq)
    ang = jnp.concatenate([row[:, None] * inv, col[:, None] * inv], axis=-1)
    cos_t = jnp.tile(jnp.cos(ang), (1, 2 * LANES // hd))
    sin_t = jnp.tile(jnp.concatenate([-jnp.sin(ang), jnp.sin(ang)], axis=-1), (1, LANES // hd))
    grp = jnp.arange(n_qk) // hd
    g1 = (grp[:, None] == jnp.arange(LANES)[None, :]).astype(F32) / hd
    g2 = (jnp.arange(LANES)[:, None] == grp[None, :]).astype(F32)
    tm = 512
    tiles_per_seq = seq // tm
    extras = [gain_row, cos_t, sin_t, g1.astype(BF16), g2.astype(BF16)]
    extra_specs = [pl.BlockSpec((1, n_qk), lambda i, j: (0, 0)),
                   pl.BlockSpec((tm, LANES), lambda i, j: (i % tiles_per_seq, 0)),
                   pl.BlockSpec((tm, LANES), lambda i, j: (i % tiles_per_seq, 0)),
                   pl.BlockSpec((n_qk, LANES), lambda i, j: (0, 0)),
                   pl.BlockSpec((LANES, n_qk), lambda i, j: (0, 0))]
    q, k, v = _project(
        x, mod, 0, w, extras, extra_specs, functools.partial(_att_qkv_epilogue, n_qk=n_qk),
        [jax.ShapeDtypeStruct((n, n_q), BF16), jax.ShapeDtypeStruct((n, n_kv), BF16),
         jax.ShapeDtypeStruct((n, n_kv), BF16)],
        [pl.BlockSpec((tm, n_q), lambda i, j: (i, 0)), pl.BlockSpec((tm, n_kv), lambda i, j: (i, 0)),
         pl.BlockSpec((tm, n_kv), lambda i, j: (i, 0))],
        tm, w.shape[1], seq, "att_qkv")

    tq = 128
    n_chunks = seq // KEY_CHUNK
    q_tiles = seq // tq
    o = pl.pallas_call(
        functools.partial(_att_kernel, n_chunks=n_chunks),
        grid=(batch, ATT_KV_HEADS, q_tiles),
        in_specs=[pl.BlockSpec((tq, groups * hd), lambda b, g, i: (b * q_tiles + i, g)),
                  pl.BlockSpec((seq, LANES), lambda b, g, i: (b, g)),
                  pl.BlockSpec((seq, LANES), lambda b, g, i: (b, g))],
        out_specs=pl.BlockSpec((tq, groups * hd), lambda b, g, i: (b * q_tiles + i, g)),
        out_shape=jax.ShapeDtypeStruct((n, n_q), BF16),
        scratch_shapes=[pltpu.VMEM((n_chunks, groups * tq, KEY_CHUNK), F32)],
        compiler_params=_cparams("parallel", "parallel", "arbitrary"),
        name="gqa_attention",
    )(q, k, v)
    return _out_project(o, 0, n_q, w_o.astype(BF16), x, mod, 0, seq, 512, "att_out")


def _ret_proj_epilogue(acc, j, extras, outs):
    cos_ref, sin_ref = extras
    o_ref = outs[0]

    @pl.when(j < 2)
    def _():
        cos = cos_ref[...]
        sin = sin_ref[...]
        half = RET_QK_DIM // 2
        for hh in range(RET_HEADS):
            x1 = acc[:, hh * RET_QK_DIM:hh * RET_QK_DIM + half]
            x2 = acc[:, hh * RET_QK_DIM + half:(hh + 1) * RET_QK_DIM]
            o_ref[:, hh * RET_QK_DIM:hh * RET_QK_DIM + half] = (x1 * cos - x2 * sin).astype(BF16)
            o_ref[:, hh * RET_QK_DIM + half:(hh + 1) * RET_QK_DIM] = (x1 * sin + x2 * cos).astype(BF16)

    @pl.when(j >= 2)
    def _():
        o_ref[...] = acc.astype(BF16)


def _ret_kernel(logit_ref, q_ref, k_ref, v_ref, gain_ref, o_ref, acc_scr, state_scr, *, n_chunks):
    h = pl.program_id(1)
    c_len = RET_CHUNK

    def log_gamma(direction, shape):
        z = jnp.full(shape, logit_ref[direction, h], F32)
        return jnp.minimum(z, 0.0) - jnp.log1p(jnp.exp(-jnp.abs(z)))

    ri = lax.broadcasted_iota(jnp.int32, (c_len, c_len), 0)
    ci = lax.broadcasted_iota(jnp.int32, (c_len, c_len), 1)
    dist = (ri - ci).astype(F32)
    lg_f = log_gamma(0, (c_len, c_len))
    lg_b = log_gamma(1, (c_len, c_len))
    decay = jnp.where(dist >= 0, jnp.exp(lg_f * jnp.maximum(dist, 0.0)), jnp.exp(lg_b * jnp.maximum(-dist, 0.0)))
    idx = lax.broadcasted_iota(jnp.int32, (c_len, 1), 0).astype(F32)
    lgf = log_gamma(0, (c_len, 1))
    lgb = log_gamma(1, (c_len, 1))
    xi_f = jnp.exp(lgf * (idx + 1.0))
    zeta_f = jnp.exp(lgf * (c_len - 1.0 - idx))
    cd_f = jnp.exp(lgf[0:1] * c_len)
    xi_b = jnp.exp(lgb * (c_len - idx))
    zeta_b = jnp.exp(lgb * idx)
    cd_b = jnp.exp(lgb[0:1] * c_len)

    def chunk(cidx):
        start = pl.multiple_of(cidx * c_len, c_len)
        return (q_ref[pl.ds(start, c_len), :], k_ref[pl.ds(start, c_len), :], v_ref[pl.ds(start, c_len), :], start)

    def update_state(kc, vc, zeta, cd):
        zv = (vc.astype(F32) * zeta).astype(BF16)
        kv = lax.dot_general(kc, zv, (((0,), (0,)), ((), ())), preferred_element_type=F32)
        state_scr[...] = state_scr[...] * cd + kv

    state_scr[...] = jnp.zeros_like(state_scr)

    def fwd_body(cidx, carry):
        qc, kc, vc, start = chunk(cidx)
        scores = lax.dot_general(qc, kc, (((1,), (1,)), ((), ())), preferred_element_type=F32) * decay
        o = jnp.dot(scores.astype(BF16), vc, preferred_element_type=F32)
        o = o + xi_f * jnp.dot(qc, state_scr[...].astype(BF16), preferred_element_type=F32)
        acc_scr[pl.ds(start, c_len), :] = o
        update_state(kc, vc, zeta_f, cd_f)
        return carry

    lax.fori_loop(0, n_chunks, fwd_body, 0)
    state_scr[...] = jnp.zeros_like(state_scr)

    def bwd_body(t, carry):
        cidx = n_chunks - 1 - t
        qc, kc, vc, start = chunk(cidx)
        o = xi_b * jnp.dot(qc, state_scr[...].astype(BF16), preferred_element_type=F32)
        acc_scr[pl.ds(start, c_len), :] += o
        update_state(kc, vc, zeta_b, cd_b)
        return carry

    lax.fori_loop(0, n_chunks, bwd_body, 0)

    def norm_body(cidx, carry):
        start = pl.multiple_of(cidx * c_len, c_len)
        y = acc_scr[pl.ds(start, c_len), :]
        y = y * lax.rsqrt(jnp.mean(y * y, axis=-1, keepdims=True) + EPS) * gain_ref[...]
        o_ref[pl.ds(start, c_len), :] = y.astype(BF16)
        return carry

    lax.fori_loop(0, n_chunks, norm_body, 0)


def _retention_mixer(x, mod, w_in, decay_logit, gn_gain, w_o, batch, seq):
    n, d = x.shape
    qk_cols = RET_HEADS * RET_QK_DIM
    v_cols = RET_HEADS * RET_V_DIM
    col_scale = jnp.concatenate([jnp.ones((qk_cols,), F32), jnp.full((qk_cols,), RET_QK_DIM ** -0.5, F32),
                                 jnp.ones((2 * v_cols,), F32)])
    w = (w_in * col_scale[None, :]).astype(BF16)
    pos = jnp.arange(seq, dtype=F32)
    inv = 1.0 / (RET_THETA ** jnp.linspace(0.0, 1.0, RET_QK_DIM // 2, dtype=F32))
    ang = pos[:, None] * inv
    tm, tn = 512, qk_cols
    tiles_per_seq = seq // tm
    half = RET_QK_DIM // 2
    extras = [jnp.cos(ang), jnp.sin(ang)]
    extra_specs = [pl.BlockSpec((tm, half), lambda i, j: (i % tiles_per_seq, 0)),
                   pl.BlockSpec((tm, half), lambda i, j: (i % tiles_per_seq, 0))]
    proj = _project(x, mod, 0, w, extras, extra_specs, _ret_proj_epilogue,
                    jax.ShapeDtypeStruct((n, w.shape[1]), BF16), pl.BlockSpec((tm, tn), lambda i, j: (i, j)),
                    tm, tn, seq, "ret_proj")

    n_chunks = seq // RET_CHUNK
    k_off = qk_cols // RET_QK_DIM
    v_off = 2 * qk_cols // RET_V_DIM
    y = pl.pallas_call(
        functools.partial(_ret_kernel, n_chunks=n_chunks),
        grid=(batch, RET_HEADS),
        in_specs=[pl.BlockSpec(memory_space=pltpu.SMEM),
                  pl.BlockSpec((seq, RET_QK_DIM), lambda b, h: (b, h)),
                  pl.BlockSpec((seq, RET_QK_DIM), lambda b, h: (b, k_off + h)),
                  pl.BlockSpec((seq, RET_V_DIM), lambda b, h: (b, v_off + h)),
                  pl.BlockSpec((1, RET_V_DIM), lambda b, h: (0, 0))],
        out_specs=pl.BlockSpec((seq, RET_V_DIM), lambda b, h: (b, h)),
        out_shape=jax.ShapeDtypeStruct((n, v_cols), BF16),
        scratch_shapes=[pltpu.VMEM((seq, RET_V_DIM), F32), pltpu.VMEM((RET_QK_DIM, RET_V_DIM), F32)],
        compiler_params=_cparams("parallel", "arbitrary"),
        name="retention",
    )(decay_logit.astype(F32), proj, proj, proj, gn_gain.reshape(1, RET_V_DIM).astype(F32))
    gate_col = (2 * qk_cols + v_cols) // v_cols
    return _out_project(y, 0, v_cols, w_o.astype(BF16), x, mod, 0, seq, 512, "ret_out",
                        gate_src=proj, gate_col=gate_col)


def _conv_pw1_epilogue(acc, j, extras, outs):
    b_ref = extras[0]
    u = acc + b_ref[...]
    dm = u.shape[1] // 2
    outs[0][...] = (u[:, :dm] * jax.nn.sigmoid(u[:, dm:])).astype(BF16)


def _conv_kernel(prev_ref, cur_ref, next_ref, wdw_ref, bdw_ref, lng_ref, lnb_ref, w2_ref, b2_ref, x_ref, mod_ref,
                 o_ref, ext_scr, c_scr, *, tiles_per_seq, row_block):
    tm = cur_ref.shape[0]
    t = pl.program_id(0) % tiles_per_seq
    keep_prev = jnp.where(t == 0, 0.0, 1.0)
    keep_next = jnp.where(t == tiles_per_seq - 1, 0.0, 1.0)
    ext_scr[0:CONV_HALO, :] = prev_ref[...].astype(F32) * keep_prev
    ext_scr[CONV_HALO:CONV_HALO + tm, :] = cur_ref[...].astype(F32)
    ext_scr[CONV_HALO + tm:, :] = next_ref[...].astype(F32) * keep_next
    base = CONV_HALO - CONV_PAD
    for rb in range(tm // row_block):
        r0 = rb * row_block
        acc = jnp.zeros((row_block, cur_ref.shape[1]), F32) + bdw_ref[...]
        for j in range(CONV_WIDTH):
            acc = acc + wdw_ref[j:j + 1, :] * ext_scr[r0 + base + j:r0 + base + j + row_block, :]
        c_scr[r0:r0 + row_block, :] = acc
    u = c_scr[...]
    mu = jnp.mean(u, axis=-1, keepdims=True)
    var = jnp.mean(jnp.square(u - mu), axis=-1, keepdims=True)
    u = (u - mu) * lax.rsqrt(var + EPS) * lng_ref[...] + lnb_ref[...]
    u = _silu(u).astype(BF16)
    z = jnp.dot(u, w2_ref[...], preferred_element_type=F32) + b2_ref[...]
    o_ref[...] = x_ref[...] + mod_ref[0][2:3] * z


def _conv_mixer(x, mod, w_pw1, b_pw1, w_dw, b_dw, ln_gain, ln_bias, w_pw2, b_pw2, seq):
    n, d = x.shape
    tm = 512
    u = _project(x, mod, 0, w_pw1.astype(BF16), [b_pw1.reshape(1, 2 * d).astype(F32)],
                 [pl.BlockSpec((1, 2 * d), lambda i, j: (0, 0))], _conv_pw1_epilogue,
                 jax.ShapeDtypeStruct((n, d), BF16), pl.BlockSpec((tm, d), lambda i, j: (i, 0)),
                 tm, 2 * d, seq, "conv_pw1")
    tm = 256
    tiles_per_seq = seq // tm
    halo_per_tile = tm // CONV_HALO
    n_halo = n // CONV_HALO
    wdw = jnp.pad(w_dw.reshape(CONV_WIDTH, d).astype(F32), ((0, 1), (0, 0)))
    vec = lambda a: a.reshape(1, d).astype(F32)
    row_spec = pl.BlockSpec((1, d), lambda i: (0, 0))
    return pl.pallas_call(
        functools.partial(_conv_kernel, tiles_per_seq=tiles_per_seq, row_block=32),
        grid=(n // tm,),
        in_specs=[pl.BlockSpec((CONV_HALO, d), lambda i: (jnp.maximum(i * halo_per_tile - 1, 0), 0)),
                  pl.BlockSpec((tm, d), lambda i: (i, 0)),
                  pl.BlockSpec((CONV_HALO, d), lambda i: (jnp.minimum((i + 1) * halo_per_tile, n_halo - 1), 0)),
                  pl.BlockSpec((CONV_WIDTH + 1, d), lambda i: (0, 0)),
                  row_spec, row_spec, row_spec,
                  pl.BlockSpec((d, d), lambda i: (0, 0)),
                  row_spec,
                  pl.BlockSpec((tm, d), lambda i: (i, 0)),
                  pl.BlockSpec((1, 6, d), lambda i: (i // tiles_per_seq, 0, 0))],
        out_specs=pl.BlockSpec((tm, d), lambda i: (i, 0)),
        out_shape=jax.ShapeDtypeStruct((n, d), F32),
        scratch_shapes=[pltpu.VMEM((tm + 2 * CONV_HALO, d), F32), pltpu.VMEM((tm, d), F32)],
        compiler_params=_cparams("parallel"),
        name="conv_module",
    )(u, u, u, wdw, vec(b_dw), vec(ln_gain), vec(ln_bias), w_pw2.astype(BF16), vec(b_pw2), x, mod)


def _t5_bucket(rel):
    half = T5_BUCKETS // 2
    exact = half // 2
    n = jnp.abs(rel)
    log_ratio = jnp.log(jnp.maximum(n, 1).astype(F32) / exact) / math.log(T5_MAX_DIST / exact)
    large = jnp.minimum(exact + (log_ratio * (half - exact)).astype(jnp.int32), half - 1)
    return jnp.where(rel > 0, half, 0) + jnp.where(n < exact, n, large)


def _bias_tile_kernel(table_ref, idx_ref, o_ref):
    h = pl.program_id(0)
    idx = idx_ref[0]
    acc = jnp.zeros(idx.shape, F32)
    for b in range(T5_BUCKETS):
        acc = jnp.where(idx == b, table_ref[b, h], acc)
    o_ref[0, 0] = acc


def _diff_kernel(q_ref, k_ref, v_ref, bias_ref, lam_ref, gain_ref, o_ref, s_scr, *, n_chunks, out_scale,
                 lambda_init):
    tq = q_ref.shape[0]
    qi = pl.program_id(2)
    q = q_ref[...]
    lane = lax.broadcasted_iota(jnp.int32, (tq, LANES), 1)
    low = lane < DIFF_HEAD_DIM
    zero = jnp.zeros((tq, LANES), BF16)
    lhs = jnp.concatenate([jnp.where(low, q, zero), jnp.where(low, zero, q)], axis=0)

    def add_bias(s, j):
        tile = bias_ref[0, jnp.clip(j - qi, -2, 2) + 2]
        return s + jnp.concatenate([tile, tile], axis=0)

    o = _softmax_pv(lhs, k_ref, v_ref, s_scr, n_chunks, add_bias)
    lam_v = lam_ref[...]
    lam = (jnp.exp(jnp.sum(lam_v[0:1] * lam_v[1:2], axis=-1, keepdims=True))
           - jnp.exp(jnp.sum(lam_v[2:3] * lam_v[3:4], axis=-1, keepdims=True)) + lambda_init)
    y = o[:tq] - lam * o[tq:]
    y = y * lax.rsqrt(jnp.mean(y * y, axis=-1, keepdims=True) + EPS) * gain_ref[...] * out_scale
    o_ref[...] = y.astype(BF16)


def _diff_mixer(x, mod, w_qkv, lam_q1, lam_k1, lam_q2, lam_k2, subln_gain, w_o, rel_bias, lambda_init,
                batch, seq):
    n, d = x.shape
    col_scale = jnp.concatenate([jnp.full((d,), DIFF_HEAD_DIM ** -0.5, F32), jnp.ones((2 * d,), F32)])
    w = (w_qkv * col_scale[None, :]).astype(BF16)
    tm = 512
    qkv = _project(x, mod, 0, w, [], [], _plain_epilogue,
                   jax.ShapeDtypeStruct((n, 3 * d), BF16), pl.BlockSpec((tm, d), lambda i, j: (i, j)),
                   tm, d, seq, "diff_qkv")

    tq = KEY_CHUNK
    ar = jnp.arange(KEY_CHUNK, dtype=jnp.int32)
    offs = (jnp.arange(5, dtype=jnp.int32) - 2) * KEY_CHUNK
    rel = (ar[None, None, :] + offs[:, None, None]) - ar[None, :, None]
    bucket = _t5_bucket(rel).astype(jnp.int32)
    bias = pl.pallas_call(
        _bias_tile_kernel,
        grid=(DIFF_HEADS, 5),
        in_specs=[pl.BlockSpec(memory_space=pltpu.SMEM),
                  pl.BlockSpec((1, tq, KEY_CHUNK), lambda h, t: (t, 0, 0))],
        out_specs=pl.BlockSpec((1, 1, tq, KEY_CHUNK), lambda h, t: (h, t, 0, 0)),
        out_shape=jax.ShapeDtypeStruct((DIFF_HEADS, 5, tq, KEY_CHUNK), F32),
        compiler_params=_cparams("arbitrary", "arbitrary"),
        name="rel_bias_tiles",
    )(rel_bias.astype(F32), bucket)

    lam = jnp.stack([lam_q1, lam_k1, lam_q2, lam_k2]).astype(F32)
    n_chunks = seq // KEY_CHUNK
    q_tiles = seq // tq
    o = pl.pallas_call(
        functools.partial(_diff_kernel, n_chunks=n_chunks, out_scale=1.0 - lambda_init, lambda_init=lambda_init),
        grid=(batch, DIFF_HEADS, q_tiles),
        in_specs=[pl.BlockSpec((tq, LANES), lambda b, h, i: (b * q_tiles + i, h)),
                  pl.BlockSpec((seq, LANES), lambda b, h, i: (b, DIFF_HEADS + h)),
                  pl.BlockSpec((seq, LANES), lambda b, h, i: (b, 2 * DIFF_HEADS + h)),
                  pl.BlockSpec((1, 5, tq, KEY_CHUNK), lambda b, h, i: (h, 0, 0, 0)),
                  pl.BlockSpec((4, DIFF_HEAD_DIM), lambda b, h, i: (0, 0)),
                  pl.BlockSpec((1, LANES), lambda b, h, i: (0, 0))],
        out_specs=pl.BlockSpec((tq, LANES), lambda b, h, i: (b * q_tiles + i, h)),
        out_shape=jax.ShapeDtypeStruct((n, d), BF16),
        scratch_shapes=[pltpu.VMEM((n_chunks, 2 * tq, KEY_CHUNK), F32)],
        compiler_params=_cparams("parallel", "parallel", "arbitrary"),
        name="diff_attention",
    )(qkv, qkv, qkv, bias, lam, subln_gain.reshape(1, LANES).astype(F32))
    return _out_project(o, 0, d, w_o.astype(BF16), x, mod, 0, seq, 512, "diff_out")


def _final_norm_kernel(x_ref, g_ref, o_ref):
    x = x_ref[...]
    o_ref[...] = x * lax.rsqrt(jnp.mean(x * x, axis=-1, keepdims=True) + EPS) * g_ref[...]


def _final_norm(x, gain, tm):
    n, d = x.shape
    return pl.pallas_call(
        _final_norm_kernel,
        grid=(n // tm,),
        in_specs=[pl.BlockSpec((tm, d), lambda i: (i, 0)), pl.BlockSpec((1, d), lambda i: (0, 0))],
        out_specs=pl.BlockSpec((tm, d), lambda i: (i, 0)),
        out_shape=jax.ShapeDtypeStruct((n, d), F32),
        compiler_params=_cparams("parallel"),
        name="final_norm",
    )(x, gain.reshape(1, d).astype(F32))


def kernel(x, c, mod_w, mod_b, att_w_qkv, att_q_gain, att_k_gain, att_w_o, ret_w_in, ret_decay_logit, ret_gn_gain, ret_w_o, cnv_w_pw1, cnv_b_pw1, cnv_w_dw, cnv_b_dw, cnv_ln_gain, cnv_ln_bias, cnv_w_pw2, cnv_b_pw2, dif_w_qkv, dif_lam_q1, dif_lam_k1, dif_lam_q2, dif_lam_k2, dif_subln_gain, dif_w_o, rel_bias, ffn_w_gate, ffn_w_up, ffn_w_down, moe_w_router, moe_b_router, moe_w_gate, moe_w_up, moe_w_down, final_gain):
    batch, seq, d = x.shape
    depth = mod_w.shape[0]
    xf = x.reshape(batch * seq, d)
    mods = _modulation(c, mod_w, mod_b)
    for i in range(depth):
        mod = mods[i]
        kind, j = i % 4, i // 4
        if kind == 0:
            xf = _attention_mixer(xf, mod, att_w_qkv[j], att_q_gain[j], att_k_gain[j], att_w_o[j], batch, seq)
        elif kind == 1:
            xf = _retention_mixer(xf, mod, ret_w_in[j], ret_decay_logit[j], ret_gn_gain[j], ret_w_o[j], batch, seq)
        elif kind == 2:
            xf = _conv_mixer(xf, mod, cnv_w_pw1[j], cnv_b_pw1[j], cnv_w_dw[j], cnv_b_dw[j], cnv_ln_gain[j],
                             cnv_ln_bias[j], cnv_w_pw2[j], cnv_b_pw2[j], seq)
        else:
            lambda_init = 0.8 - 0.6 * math.exp(-0.3 * i)
            xf = _diff_mixer(xf, mod, dif_w_qkv[j], dif_lam_q1[j], dif_lam_k1[j], dif_lam_q2[j], dif_lam_k2[j],
                             dif_subln_gain[j], dif_w_o[j], rel_bias, lambda_init, batch, seq)
        if i % 2 == 0:
            m = i // 2
            xf = _ffn(xf, mod, ffn_w_gate[m].astype(BF16), ffn_w_up[m].astype(BF16), ffn_w_down[m].astype(BF16),
                      seq, 512, 1408)
        else:
            m = i // 2
            xf = _moe(xf, mod, moe_w_router[m], moe_b_router[m], moe_w_gate[m], moe_w_up[m], moe_w_down[m], seq)
    return _final_norm(xf, final_gain, 512).reshape(batch, seq, d)
```

```python
import functools
import math

import jax
import jax.numpy as jnp
from jax import lax
from jax.experimental import pallas as pl
from jax.experimental.pallas import tpu as pltpu

F32 = jnp.float32
BF16 = jnp.bfloat16

EPS = 1e-6
GRID_W = 64

ATT_HEADS = 16
ATT_KV_HEADS = 4
ATT_HEAD_DIM = 64
ROPE_THETA = 10000.0

RET_HEADS = 4
RET_QK_DIM = 256
RET_V_DIM = 512
RET_THETA = 10000.0
RET_CHUNK = 256

CONV_WIDTH = 31
CONV_PAD = CONV_WIDTH // 2
CONV_HALO = 16

DIFF_HEADS = 8
DIFF_HEAD_DIM = 64
T5_BUCKETS = 32
T5_MAX_DIST = 128

N_EXPERTS = 8
LANES = 128
KEY_CHUNK = 256
LOG2_E = 1.4426950408889634
VMEM_LIMIT_BYTES = 56 * 1024 * 1024


def _cparams(*sem):
    return pltpu.CompilerParams(dimension_semantics=sem, vmem_limit_bytes=VMEM_LIMIT_BYTES)


def _silu(v):
    return v * jax.nn.sigmoid(v)


def _norm_mod(x, mod, which):
    shift = mod[3 * which:3 * which + 1]
    scale = mod[3 * which + 1:3 * which + 2]
    y = x * lax.rsqrt(jnp.mean(x * x, axis=-1, keepdims=True) + EPS)
    return y * (1.0 + scale) + shift


def _split_dot(a, b):
    hi = a.astype(BF16)
    lo = (a - hi.astype(F32)).astype(BF16)
    return (jnp.dot(hi, b, preferred_element_type=F32) + jnp.dot(lo, b, preferred_element_type=F32))


def _mod_kernel(c_ref, w_ref, b_ref, o_ref):
    cond = _silu(c_ref[...])
    o_ref[0] = jnp.dot(cond, w_ref[0], preferred_element_type=F32, precision=lax.Precision.HIGHEST) + b_ref[0]


def _modulation(c, mod_w, mod_b):
    depth, d, d6 = mod_w.shape
    b = c.shape[0]
    tn = d
    out = pl.pallas_call(
        _mod_kernel,
        grid=(depth, d6 // tn),
        in_specs=[pl.BlockSpec((b, d), lambda i, j: (0, 0)),
                  pl.BlockSpec((1, d, tn), lambda i, j: (i, 0, j)),
                  pl.BlockSpec((1, 1, tn), lambda i, j: (i, 0, j))],
        out_specs=pl.BlockSpec((1, b, tn), lambda i, j: (i, 0, j)),
        out_shape=jax.ShapeDtypeStruct((depth, b, d6), F32),
        compiler_params=_cparams("arbitrary", "arbitrary"),
        name="modulation",
    )(c, mod_w, mod_b.reshape(depth, 1, d6))
    return out.reshape(depth, b, 6, d)


def _proj_kernel(x_ref, mod_ref, w_ref, *rest, which, n_extra, epilogue):
    extras = rest[:n_extra]
    outs = rest[n_extra:-1]
    h_scr = rest[-1]
    j = pl.program_id(1)

    @pl.when(j == 0)
    def _():
        h_scr[...] = _norm_mod(x_ref[...], mod_ref[0], which).astype(BF16)

    acc = jnp.dot(h_scr[...], w_ref[...], preferred_element_type=F32)
    epilogue(acc, j, extras, outs)


def _project(x, mod, which, w, extras, extra_specs, epilogue, out_shapes, out_specs, tm, tn, seq, name):
    n, d = x.shape
    c = w.shape[1]
    tiles_per_seq = seq // tm
    kern = functools.partial(_proj_kernel, which=which, n_extra=len(extras), epilogue=epilogue)
    return pl.pallas_call(
        kern,
        grid=(n // tm, c // tn),
        in_specs=[pl.BlockSpec((tm, d), lambda i, j: (i, 0)),
                  pl.BlockSpec((1, 6, d), lambda i, j: (i // tiles_per_seq, 0, 0)),
                  pl.BlockSpec((d, tn), lambda i, j: (0, j))] + extra_specs,
        out_specs=out_specs,
        out_shape=out_shapes,
        scratch_shapes=[pltpu.VMEM((tm, d), BF16)],
        compiler_params=_cparams("parallel", "arbitrary"),
        name=name,
    )(x, mod, w, *extras)


def _plain_epilogue(acc, j, extras, outs):
    outs[0][...] = acc.astype(outs[0].dtype)


def _oproj_kernel(*refs, which, gated, has_bias):
    refs = list(refs)
    y_ref = refs.pop(0)
    g_ref = refs.pop(0) if gated else None
    w_ref = refs.pop(0)
    b_ref = refs.pop(0) if has_bias else None
    x_ref, mod_ref, o_ref = refs
    y = y_ref[...]
    if gated:
        y = (y.astype(F32) * _silu(g_ref[...].astype(F32))).astype(BF16)
    z = jnp.dot(y, w_ref[...], preferred_element_type=F32)
    if has_bias:
        z = z + b_ref[...]
    gate = mod_ref[0][3 * which + 2:3 * which + 3]
    o_ref[...] = x_ref[...] + gate * z


def _out_project(y, y_col, k, w, x, mod, which, seq, tm, name, gate_src=None, gate_col=0, bias=None):
    n, d = x.shape
    tiles_per_seq = seq // tm
    args = [y]
    specs = [pl.BlockSpec((tm, k), lambda i: (i, y_col))]
    if gate_src is not None:
        args.append(gate_src)
        specs.append(pl.BlockSpec((tm, k), lambda i: (i, gate_col)))
    args.append(w)
    specs.append(pl.BlockSpec((k, d), lambda i: (0, 0)))
    if bias is not None:
        args.append(bias.reshape(1, d))
        specs.append(pl.BlockSpec((1, d), lambda i: (0, 0)))
    args += [x, mod]
    specs += [pl.BlockSpec((tm, d), lambda i: (i, 0)),
              pl.BlockSpec((1, 6, d), lambda i: (i // tiles_per_seq, 0, 0))]
    kern = functools.partial(_oproj_kernel, which=which, gated=gate_src is not None, has_bias=bias is not None)
    return pl.pallas_call(
        kern,
        grid=(n // tm,),
        in_specs=specs,
        out_specs=pl.BlockSpec((tm, d), lambda i: (i, 0)),
        out_shape=jax.ShapeDtypeStruct((n, d), F32),
        compiler_params=_cparams("parallel"),
        name=name,
    )(*args)


def _ffn_kernel(x_ref, mod_ref, wg_ref, wu_ref, wd_ref, o_ref, h_scr, acc_scr):
    f = pl.program_id(1)

    @pl.when(f == 0)
    def _():
        h_scr[...] = _norm_mod(x_ref[...], mod_ref[0], 1).astype(BF16)
        acc_scr[...] = jnp.zeros_like(acc_scr)

    h = h_scr[...]
    g = jnp.dot(h, wg_ref[...], preferred_element_type=F32)
    u = jnp.dot(h, wu_ref[...], preferred_element_type=F32)
    a = (_silu(g) * u).astype(BF16)
    acc_scr[...] += jnp.dot(a, wd_ref[...], preferred_element_type=F32)

    @pl.when(f == pl.num_programs(1) - 1)
    def _():
        o_ref[...] = x_ref[...] + mod_ref[0][5:6] * acc_scr[...]


def _ffn(x, mod, wg, wu, wd, seq, tm, tf):
    n, d = x.shape
    ff = wg.shape[1]
    tiles_per_seq = seq // tm
    return pl.pallas_call(
        _ffn_kernel,
        grid=(n // tm, ff // tf),
        in_specs=[pl.BlockSpec((tm, d), lambda i, f: (i, 0)),
                  pl.BlockSpec((1, 6, d), lambda i, f: (i // tiles_per_seq, 0, 0)),
                  pl.BlockSpec((d, tf), lambda i, f: (0, f)),
                  pl.BlockSpec((d, tf), lambda i, f: (0, f)),
                  pl.BlockSpec((tf, d), lambda i, f: (f, 0))],
        out_specs=pl.BlockSpec((tm, d), lambda i, f: (i, 0)),
        out_shape=jax.ShapeDtypeStruct((n, d), F32),
        scratch_shapes=[pltpu.VMEM((tm, d), BF16), pltpu.VMEM((tm, d), F32)],
        compiler_params=_cparams("parallel", "arbitrary"),
        name="dense_ffn",
    )(x, mod, wg, wu, wd)


def _router_kernel(x_ref, mod_ref, wr_ref, br_ref, sel_ref, route_ref):
    h = _norm_mod(x_ref[...], mod_ref[0], 1)
    logits = jnp.dot(h, wr_ref[...], preferred_element_type=F32, precision=lax.Precision.HIGHEST) + br_ref[...]
    lane = lax.broadcasted_iota(jnp.int32, logits.shape, 1)
    neg = jnp.float32(-jnp.inf)
    logits = jnp.where(lane < N_EXPERTS, logits, neg)
    m1 = jnp.max(logits, axis=-1, keepdims=True)
    i1 = jnp.min(jnp.where(logits == m1, lane, LANES), axis=-1, keepdims=True)
    rest = jnp.where(lane == i1, neg, logits)
    m2 = jnp.max(rest, axis=-1, keepdims=True)
    i2 = jnp.min(jnp.where(rest == m2, lane, LANES), axis=-1, keepdims=True)
    e2 = jnp.exp(m2 - m1)
    w1 = 1.0 / (1.0 + e2)
    w2 = e2 / (1.0 + e2)
    sel_ref[...] = jnp.where((lane == i1) | (lane == i2), 1.0, 0.0).astype(BF16)
    route = jnp.where(lane == 0, i1.astype(F32), jnp.where(lane == 1, i2.astype(F32), 0.0))
    route_ref[...] = jnp.where(lane == 2, w1, jnp.where(lane == 3, w2, route))


def _router(x, mod, w_router, b_router, seq, tm):
    n, d = x.shape
    tiles_per_seq = seq // tm
    wr = jnp.pad(w_router, ((0, 0), (0, LANES - N_EXPERTS)))
    br = jnp.pad(b_router, (0, LANES - N_EXPERTS)).reshape(1, LANES)
    return pl.pallas_call(
        _router_kernel,
        grid=(n // tm,),
        in_specs=[pl.BlockSpec((tm, d), lambda i: (i, 0)),
                  pl.BlockSpec((1, 6, d), lambda i: (i // tiles_per_seq, 0, 0)),
                  pl.BlockSpec((d, LANES), lambda i: (0, 0)),
                  pl.BlockSpec((1, LANES), lambda i: (0, 0))],
        out_specs=[pl.BlockSpec((tm, LANES), lambda i: (i, 0)), pl.BlockSpec((tm, LANES), lambda i: (i, 0))],
        out_shape=[jax.ShapeDtypeStruct((n, LANES), BF16), jax.ShapeDtypeStruct((n, LANES), F32)],
        compiler_params=_cparams("parallel"),
        name="moe_router",
    )(x, mod, wr, br)


def _rank_kernel(sel_ref, route_ref, tri_ref, rank_ref, count_ref, carry_scr):
    @pl.when(pl.program_id(0) == 0)
    def _():
        carry_scr[...] = jnp.zeros_like(carry_scr)

    sel = sel_ref[...]
    incl = jnp.dot(tri_ref[...], sel, preferred_element_type=F32)
    excl = incl - sel.astype(F32) + carry_scr[...]
    route = route_ref[...]
    lane = lax.broadcasted_iota(jnp.int32, excl.shape, 1)
    lane_f = lane.astype(F32)
    r1 = jnp.sum(jnp.where(lane_f == route[:, 0:1], excl, 0.0), axis=-1, keepdims=True)
    r2 = jnp.sum(jnp.where(lane_f == route[:, 1:2], excl, 0.0), axis=-1, keepdims=True)
    rank_ref[...] = jnp.where(lane == 0, r1, jnp.where(lane == 1, r2, 0.0))
    carry_scr[...] += incl[incl.shape[0] - 1:, :]
    count_ref[...] = carry_scr[...]


def _dispatch_kernel(d1_ref, d2_ref, x_ref, mod_ref, zeros_ref, xs_ref, h_scr, sems):
    del zeros_ref
    tm = x_ref.shape[0]
    h_scr[...] = _norm_mod(x_ref[...], mod_ref[0], 1)

    def row_copy(r, dest, which):
        return pltpu.make_async_copy(h_scr.at[pl.ds(r, 1), :], xs_ref.at[pl.ds(dest, 1), :], sems.at[which])

    def issue(r, carry):
        row_copy(r, d1_ref[0, 0, r], 0).start()
        row_copy(r, d2_ref[0, 0, r], 1).start()
        return carry

    lax.fori_loop(0, tm, issue, 0)

    def drain(r, carry):
        row_copy(r, d1_ref[0, 0, r], 0).wait()
        row_copy(r, d2_ref[0, 0, r], 1).wait()
        return carry

    lax.fori_loop(0, tm, drain, 0)


def _expert_kernel(tile_e_ref, n_act_ref, x_ref, wg_ref, wu_ref, wd_ref, o_ref, h_scr, acc_scr):
    del tile_e_ref
    s = pl.program_id(0)
    f = pl.program_id(1)

    @pl.when(s < n_act_ref[0])
    def _():
        @pl.when(f == 0)
        def _():
            h_scr[...] = x_ref[...].astype(BF16)
            acc_scr[...] = jnp.zeros_like(acc_scr)

        h = h_scr[...]
        g = jnp.dot(h, wg_ref[0], preferred_element_type=F32)
        u = jnp.dot(h, wu_ref[0], preferred_element_type=F32)
        a = (_silu(g) * u).astype(BF16)
        acc_scr[...] += jnp.dot(a, wd_ref[0], preferred_element_type=F32)

        @pl.when(f == pl.num_programs(1) - 1)
        def _():
            o_ref[...] = acc_scr[...]

    @pl.when(s >= n_act_ref[0])
    def _():
        o_ref[...] = jnp.zeros_like(o_ref)


def _combine_kernel(d1_ref, d2_ref, route_ref, x_ref, mod_ref, ys_ref, o_ref, buf, sems):
    tm = x_ref.shape[0]

    def row_copy(r, src, which):
        return pltpu.make_async_copy(ys_ref.at[pl.ds(src, 1), :], buf.at[which, pl.ds(r, 1), :], sems.at[which])

    def issue(r, carry):
        row_copy(r, d1_ref[0, 0, r], 0).start()
        row_copy(r, d2_ref[0, 0, r], 1).start()
        return carry

    lax.fori_loop(0, tm, issue, 0)

    def drain(r, carry):
        row_copy(r, d1_ref[0, 0, r], 0).wait()
        row_copy(r, d2_ref[0, 0, r], 1).wait()
        return carry

    lax.fori_loop(0, tm, drain, 0)
    route = route_ref[...]
    y = route[:, 2:3] * buf[0] + route[:, 3:4] * buf[1]
    o_ref[...] = x_ref[...] + mod_ref[0][5:6] * y


def _moe(x, mod, w_router, b_router, w_gate, w_up, w_down, seq):
    n, d = x.shape
    n_e, _, ff = w_gate.shape
    tm = 512
    te = 512
    tf = 896
    tiles_per_seq = seq // tm
    n_tok_tiles = n // tm
    n_slots = 2 * n + n_e * te
    n_tiles = n_slots // te
    f_tiles = ff // tf

    sel, route = _router(x, mod, w_router, b_router, seq, tm)
    tri = (jnp.arange(tm)[:, None] >= jnp.arange(tm)[None, :]).astype(BF16)
    rank, counts = pl.pallas_call(
        _rank_kernel,
        grid=(n_tok_tiles,),
        in_specs=[pl.BlockSpec((tm, LANES), lambda i: (i, 0)),
                  pl.BlockSpec((tm, LANES), lambda i: (i, 0)),
                  pl.BlockSpec((tm, tm), lambda i: (0, 0))],
        out_specs=[pl.BlockSpec((tm, LANES), lambda i: (i, 0)), pl.BlockSpec((1, LANES), lambda i: (0, 0))],
        out_shape=[jax.ShapeDtypeStruct((n, LANES), F32), jax.ShapeDtypeStruct((1, LANES), F32)],
        scratch_shapes=[pltpu.VMEM((1, LANES), F32)],
        compiler_params=_cparams("arbitrary"),
        name="moe_rank",
    )(sel, route, tri)

    counts = counts[0, :n_e].astype(jnp.int32)
    padded = ((counts + te - 1) // te) * te
    ends = jnp.cumsum(padded)
    offsets = ends - padded
    n_act = (ends[-1] // te).astype(jnp.int32)
    tile_start = jnp.arange(n_tiles, dtype=jnp.int32) * te
    tile_e = jnp.sum((tile_start[:, None] >= ends[None, :]).astype(jnp.int32), axis=1)
    tile_e = jnp.minimum(tile_e, tile_e[jnp.maximum(n_act - 1, 0)]).astype(jnp.int32)
    e1 = route[:, 0].astype(jnp.int32)
    e2 = route[:, 1].astype(jnp.int32)
    dest1 = (offsets[e1] + rank[:, 0].astype(jnp.int32)).reshape(n_tok_tiles, 1, tm)
    dest2 = (offsets[e2] + rank[:, 1].astype(jnp.int32)).reshape(n_tok_tiles, 1, tm)
    n_act = n_act.reshape(1)

    slot_spec = pl.BlockSpec((1, 1, tm), lambda i: (i, 0, 0), memory_space=pltpu.SMEM)
    xs = pl.pallas_call(
        _dispatch_kernel,
        grid=(n_tok_tiles,),
        in_specs=[slot_spec, slot_spec,
                  pl.BlockSpec((tm, d), lambda i: (i, 0)),
                  pl.BlockSpec((1, 6, d), lambda i: (i // tiles_per_seq, 0, 0)),
                  pl.BlockSpec(memory_space=pl.ANY)],
        out_specs=pl.BlockSpec(memory_space=pl.ANY),
        out_shape=jax.ShapeDtypeStruct((n_slots, d), F32),
        scratch_shapes=[pltpu.VMEM((tm, d), F32), pltpu.SemaphoreType.DMA((2,))],
        input_output_aliases={4: 0},
        compiler_params=_cparams("arbitrary"),
        name="moe_dispatch",
    )(dest1, dest2, x, mod, jnp.zeros((n_slots, d), F32))

    def live(s, n_act_ref):
        return jnp.minimum(s, n_act_ref[0] - 1)

    def f_idx(s, f, n_act_ref):
        return jnp.where(s < n_act_ref[0], f, f_tiles - 1)

    ys = pl.pallas_call(
        _expert_kernel,
        grid_spec=pltpu.PrefetchScalarGridSpec(
            num_scalar_prefetch=2,
            grid=(n_tiles, f_tiles),
            in_specs=[pl.BlockSpec((te, d), lambda s, f, te_ref, na_ref: (live(s, na_ref), 0)),
                      pl.BlockSpec((1, d, tf), lambda s, f, te_ref, na_ref: (te_ref[s], 0, f_idx(s, f, na_ref))),
                      pl.BlockSpec((1, d, tf), lambda s, f, te_ref, na_ref: (te_ref[s], 0, f_idx(s, f, na_ref))),
                      pl.BlockSpec((1, tf, d), lambda s, f, te_ref, na_ref: (te_ref[s], f_idx(s, f, na_ref), 0))],
            out_specs=pl.BlockSpec((te, d), lambda s, f, te_ref, na_ref: (s, 0)),
            scratch_shapes=[pltpu.VMEM((te, d), BF16), pltpu.VMEM((te, d), F32)]),
        out_shape=jax.ShapeDtypeStruct((n_slots, d), F32),
        compiler_params=_cparams("arbitrary", "arbitrary"),
        name="moe_experts",
    )(tile_e, n_act, xs, w_gate.astype(BF16), w_up.astype(BF16), w_down.astype(BF16))

    return pl.pallas_call(
        _combine_kernel,
        grid=(n_tok_tiles,),
        in_specs=[slot_spec, slot_spec,
                  pl.BlockSpec((tm, LANES), lambda i: (i, 0)),
                  pl.BlockSpec((tm, d), lambda i: (i, 0)),
                  pl.BlockSpec((1, 6, d), lambda i: (i // tiles_per_seq, 0, 0)),
                  pl.BlockSpec(memory_space=pl.ANY)],
        out_specs=pl.BlockSpec((tm, d), lambda i: (i, 0)),
        out_shape=jax.ShapeDtypeStruct((n, d), F32),
        scratch_shapes=[pltpu.VMEM((2, tm, d), F32), pltpu.SemaphoreType.DMA((2,))],
        compiler_params=_cparams("arbitrary"),
        name="moe_combine",
    )(dest1, dest2, route, x, mod, ys)


def _softmax_pv(lhs, k_ref, v_ref, s_scr, m_scr, acc_scr, n_chunks, bias_fn=None):
    rows = lhs.shape[0]
    group = 8
    m_scr[...] = jnp.full(m_scr.shape, -jnp.inf, F32)

    def logits_body(t, carry):
        for u in range(group):
            j = group * t + u
            start = pl.multiple_of(j * KEY_CHUNK, KEY_CHUNK)
            kc = k_ref[pl.ds(start, KEY_CHUNK), :]
            s = lax.dot_general(lhs, kc, (((1,), (1,)), ((), ())), preferred_element_type=F32)
            if bias_fn is not None:
                s = bias_fn(s, j)
            s_scr[j] = s
            m_scr[...] = jnp.maximum(m_scr[...], jnp.maximum(s[:, :LANES], s[:, LANES:]))
        return carry

    lax.fori_loop(0, n_chunks // group, logits_body, 0)
    m_scr[...] = jnp.broadcast_to(jnp.max(m_scr[...], axis=-1, keepdims=True), m_scr.shape)
    acc_scr[...] = jnp.zeros_like(acc_scr)
    ones = jnp.ones((group * KEY_CHUNK, LANES), BF16)

    def pv_body(t, carry):
        mb = m_scr[...]
        parts = []
        for u in range(group):
            s = s_scr[group * t + u]
            parts.append(jnp.exp2(s[:, :LANES] - mb).astype(BF16))
            parts.append(jnp.exp2(s[:, LANES:] - mb).astype(BF16))
        p = jnp.concatenate(parts, axis=1)
        start = pl.multiple_of(t * (group * KEY_CHUNK), group * KEY_CHUNK)
        v_aug = jnp.concatenate([v_ref[pl.ds(start, group * KEY_CHUNK), :], ones], axis=1)
        acc_scr[...] += jnp.dot(p, v_aug, preferred_element_type=F32)
        return carry

    lax.fori_loop(0, n_chunks // group, pv_body, 0)
    acc = acc_scr[...]
    return acc[:, :LANES] / acc[:, LANES:]


def _att_qkv_epilogue(acc, j, extras, outs, n_qk):
    gain_ref, cos_ref, sin_ref, g1_ref, g2_ref = extras
    q_ref, k_ref, v_ref = outs
    qk = acc[:, :n_qk]
    ms = _split_dot(qk * qk, g1_ref[...])
    r = lax.rsqrt(ms + EPS)
    y = qk * _split_dot(r, g2_ref[...]) * gain_ref[...]
    cos = cos_ref[...]
    sin = sin_ref[...]
    lane = lax.broadcasted_iota(jnp.int32, cos.shape, 1)
    first_half = (lane & (ATT_HEAD_DIM // 2)) == 0
    nq = q_ref.shape[1]
    for c in range(n_qk // LANES):
        yc = y[:, c * LANES:(c + 1) * LANES]
        partner = jnp.where(first_half, pltpu.roll(yc, LANES - ATT_HEAD_DIM // 2, 1),
                            pltpu.roll(yc, ATT_HEAD_DIM // 2, 1))
        out = (yc * cos + partner * sin).astype(BF16)
        if c * LANES < nq:
            q_ref[:, c * LANES:(c + 1) * LANES] = out
        else:
            k_ref[:, c * LANES - nq:(c + 1) * LANES - nq] = out
    v_ref[...] = acc[:, n_qk:].astype(BF16)


def _att_kernel(q_ref, k_ref, v_ref, o_ref, s_scr, m_scr, acc_scr, *, n_chunks):
    tq = q_ref.shape[0]
    q = q_ref[...]
    lane = lax.broadcasted_iota(jnp.int32, (tq, LANES), 1)
    low = lane < ATT_HEAD_DIM
    zero = jnp.zeros((tq, LANES), BF16)
    parts = []
    for pair in range(2):
        qp = q[:, pair * LANES:(pair + 1) * LANES]
        parts.append(jnp.where(low, qp, zero))
        parts.append(jnp.where(low, zero, qp))
    lhs = jnp.concatenate(parts, axis=0)
    o = _softmax_pv(lhs, k_ref, v_ref, s_scr, m_scr, acc_scr, n_chunks)
    for pair in range(2):
        o_ref[:, pair * LANES:(pair + 1) * LANES] = jnp.where(
            low, o[(2 * pair) * tq:(2 * pair + 1) * tq], o[(2 * pair + 1) * tq:(2 * pair + 2) * tq]).astype(BF16)


def _attention_mixer(x, mod, w_qkv, q_gain, k_gain, w_o, batch, seq):
    n, d = x.shape
    hd = ATT_HEAD_DIM
    half = hd // 2
    n_freq = half // 2
    groups = ATT_HEADS // ATT_KV_HEADS
    perm = jnp.concatenate([jnp.arange(n_freq), half + jnp.arange(n_freq),
                            n_freq + jnp.arange(n_freq), half + n_freq + jnp.arange(n_freq)])
    q_cols = (jnp.arange(ATT_HEADS)[:, None] * hd + perm[None, :]).reshape(-1)
    k_base = ATT_HEADS * hd
    v_base = k_base + ATT_KV_HEADS * hd
    kv_rep = jnp.repeat(jnp.arange(ATT_KV_HEADS), 2)
    k_cols = (k_base + kv_rep[:, None] * hd + perm[None, :]).reshape(-1)
    v_cols = (v_base + kv_rep[:, None] * hd + jnp.arange(hd)[None, :]).reshape(-1)
    w = w_qkv[:, jnp.concatenate([q_cols, k_cols, v_cols])].astype(BF16)
    n_q = ATT_HEADS * hd
    n_kv = 2 * ATT_KV_HEADS * hd
    n_qk = n_q + n_kv
    gain_row = jnp.concatenate([jnp.tile(q_gain[perm] * (hd ** -0.5 * LOG2_E), ATT_HEADS),
                                jnp.tile(k_gain[perm], 2 * ATT_KV_HEADS)]).reshape(1, n_qk).astype(F32)
    pos = jnp.arange(seq, dtype=jnp.int32)
    row = (pos // GRID_W).astype(F32)
    col = (pos % GRID_W).astype(F32)
    inv = ROPE_THETA ** (-jnp.arange(n_freq, dtype=F32) / n_freq)
    ang = jnp.concatenate([row[:, None] * inv, col[:, None] * inv], axis=-1)
    cos_t = jnp.tile(jnp.cos(ang), (1, 2 * LANES // hd))
    sin_t = jnp.tile(jnp.concatenate([-jnp.sin(ang), jnp.sin(ang)], axis=-1), (1, LANES // hd))
    grp = jnp.arange(n_qk) // hd
    g1 = (grp[:, None] == jnp.arange(LANES)[None, :]).astype(F32) / hd
    g2 = (jnp.arange(LANES)[:, None] == grp[None, :]).astype(F32)
    tm = 512
    tiles_per_seq = seq // tm
    extras = [gain_row, cos_t, sin_t, g1.astype(BF16), g2.astype(BF16)]
    extra_specs = [pl.BlockSpec((1, n_qk), lambda i, j: (0, 0)),
                   pl.BlockSpec((tm, LANES), lambda i, j: (i % tiles_per_seq, 0)),
                   pl.BlockSpec((tm, LANES), lambda i, j: (i % tiles_per_seq, 0)),
                   pl.BlockSpec((n_qk, LANES), lambda i, j: (0, 0)),
                   pl.BlockSpec((LANES, n_qk), lambda i, j: (0, 0))]
    q, k, v = _project(
        x, mod, 0, w, extras, extra_specs, functools.partial(_att_qkv_epilogue, n_qk=n_qk),
        [jax.ShapeDtypeStruct((n, n_q), BF16), jax.ShapeDtypeStruct((n, n_kv), BF16),
         jax.ShapeDtypeStruct((n, n_kv), BF16)],
        [pl.BlockSpec((tm, n_q), lambda i, j: (i, 0)), pl.BlockSpec((tm, n_kv), lambda i, j: (i, 0)),
         pl.BlockSpec((tm, n_kv), lambda i, j: (i, 0))],
        tm, w.shape[1], seq, "att_qkv")

    tq = 128
    n_chunks = seq // KEY_CHUNK
    q_tiles = seq // tq
    o = pl.pallas_call(
        functools.partial(_att_kernel, n_chunks=n_chunks),
        grid=(batch, ATT_KV_HEADS, q_tiles),
        in_specs=[pl.BlockSpec((tq, groups * hd), lambda b, g, i: (b * q_tiles + i, g)),
                  pl.BlockSpec((seq, LANES), lambda b, g, i: (b, g)),
                  pl.BlockSpec((seq, LANES), lambda b, g, i: (b, g))],
        out_specs=pl.BlockSpec((tq, groups * hd), lambda b, g, i: (b * q_tiles + i, g)),
        out_shape=jax.ShapeDtypeStruct((n, n_q), BF16),
        scratch_shapes=[pltpu.VMEM((n_chunks, groups * tq, KEY_CHUNK), F32),
                        pltpu.VMEM((groups * tq, LANES), F32), pltpu.VMEM((groups * tq, 2 * LANES), F32)],
        compiler_params=_cparams("parallel", "parallel", "arbitrary"),
        name="gqa_attention",
    )(q, k, v)
    return _out_project(o, 0, n_q, w_o.astype(BF16), x, mod, 0, seq, 512, "att_out")


def _ret_proj_epilogue(acc, j, extras, outs):
    cos_ref, sin_ref = extras
    o_ref = outs[0]

    @pl.when(j < 2)
    def _():
        cos = cos_ref[...]
        sin = sin_ref[...]
        half = RET_QK_DIM // 2
        for hh in range(RET_HEADS):
            x1 = acc[:, hh * RET_QK_DIM:hh * RET_QK_DIM + half]
            x2 = acc[:, hh * RET_QK_DIM + half:(hh + 1) * RET_QK_DIM]
            o_ref[:, hh * RET_QK_DIM:hh * RET_QK_DIM + half] = (x1 * cos - x2 * sin).astype(BF16)
            o_ref[:, hh * RET_QK_DIM + half:(hh + 1) * RET_QK_DIM] = (x1 * sin + x2 * cos).astype(BF16)

    @pl.when(j >= 2)
    def _():
        o_ref[...] = acc.astype(BF16)


def _ret_kernel(logit_ref, q_ref, k_ref, v_ref, gain_ref, o_ref, acc_scr, state_scr, *, n_chunks):
    h = pl.program_id(1)
    c_len = RET_CHUNK

    def log_gamma(direction, shape):
        z = jnp.full(shape, logit_ref[direction, h], F32)
        return jnp.minimum(z, 0.0) - jnp.log1p(jnp.exp(-jnp.abs(z)))

    ri = lax.broadcasted_iota(jnp.int32, (c_len, c_len), 0)
    ci = lax.broadcasted_iota(jnp.int32, (c_len, c_len), 1)
    dist = (ri - ci).astype(F32)
    lg_f = log_gamma(0, (c_len, c_len))
    lg_b = log_gamma(1, (c_len, c_len))
    decay = jnp.where(dist >= 0, jnp.exp(lg_f * jnp.maximum(dist, 0.0)), jnp.exp(lg_b * jnp.maximum(-dist, 0.0)))
    idx = lax.broadcasted_iota(jnp.int32, (c_len, 1), 0).astype(F32)
    lgf = log_gamma(0, (c_len, 1))
    lgb = log_gamma(1, (c_len, 1))
    xi_f = jnp.exp(lgf * (idx + 1.0))
    zeta_f = jnp.exp(lgf * (c_len - 1.0 - idx))
    cd_f = jnp.exp(lgf[0:1] * c_len)
    xi_b = jnp.exp(lgb * (c_len - idx))
    zeta_b = jnp.exp(lgb * idx)
    cd_b = jnp.exp(lgb[0:1] * c_len)

    def chunk(cidx):
        start = pl.multiple_of(cidx * c_len, c_len)
        return (q_ref[pl.ds(start, c_len), :], k_ref[pl.ds(start, c_len), :], v_ref[pl.ds(start, c_len), :], start)

    def update_state(kc, vc, zeta, cd):
        zv = (vc.astype(F32) * zeta).astype(BF16)
        kv = lax.dot_general(kc, zv, (((0,), (0,)), ((), ())), preferred_element_type=F32)
        state_scr[...] = state_scr[...] * cd + kv

    state_scr[...] = jnp.zeros_like(state_scr)

    def fwd_body(cidx, carry):
        qc, kc, vc, start = chunk(cidx)
        scores = lax.dot_general(qc, kc, (((1,), (1,)), ((), ())), preferred_element_type=F32) * decay
        o = jnp.dot(scores.astype(BF16), vc, preferred_element_type=F32)
        o = o + xi_f * jnp.dot(qc, state_scr[...].astype(BF16), preferred_element_type=F32)
        acc_scr[pl.ds(start, c_len), :] = o
        update_state(kc, vc, zeta_f, cd_f)
        return carry

    lax.fori_loop(0, n_chunks, fwd_body, 0)
    state_scr[...] = jnp.zeros_like(state_scr)

    def bwd_body(t, carry):
        cidx = n_chunks - 1 - t
        qc, kc, vc, start = chunk(cidx)
        o = xi_b * jnp.dot(qc, state_scr[...].astype(BF16), preferred_element_type=F32)
        acc_scr[pl.ds(start, c_len), :] += o
        update_state(kc, vc, zeta_b, cd_b)
        return carry

    lax.fori_loop(0, n_chunks, bwd_body, 0)

    def norm_body(cidx, carry):
        start = pl.multiple_of(cidx * c_len, c_len)
        y = acc_scr[pl.ds(start, c_len), :]
        y = y * lax.rsqrt(jnp.mean(y * y, axis=-1, keepdims=True) + EPS) * gain_ref[...]
        o_ref[pl.ds(start, c_len), :] = y.astype(BF16)
        return carry

    lax.fori_loop(0, n_chunks, norm_body, 0)


def _retention_mixer(x, mod, w_in, decay_logit, gn_gain, w_o, batch, seq):
    n, d = x.shape
    qk_cols = RET_HEADS * RET_QK_DIM
    v_cols = RET_HEADS * RET_V_DIM
    col_scale = jnp.concatenate([jnp.ones((qk_cols,), F32), jnp.full((qk_cols,), RET_QK_DIM ** -0.5, F32),
                                 jnp.ones((2 * v_cols,), F32)])
    w = (w_in * col_scale[None, :]).astype(BF16)
    pos = jnp.arange(seq, dtype=F32)
    inv = 1.0 / (RET_THETA ** jnp.linspace(0.0, 1.0, RET_QK_DIM // 2, dtype=F32))
    ang = pos[:, None] * inv
    tm, tn = 512, qk_cols
    tiles_per_seq = seq // tm
    half = RET_QK_DIM // 2
    extras = [jnp.cos(ang), jnp.sin(ang)]
    extra_specs = [pl.BlockSpec((tm, half), lambda i, j: (i % tiles_per_seq, 0)),
                   pl.BlockSpec((tm, half), lambda i, j: (i % tiles_per_seq, 0))]
    proj = _project(x, mod, 0, w, extras, extra_specs, _ret_proj_epilogue,
                    jax.ShapeDtypeStruct((n, w.shape[1]), BF16), pl.BlockSpec((tm, tn), lambda i, j: (i, j)),
                    tm, tn, seq, "ret_proj")

    n_chunks = seq // RET_CHUNK
    k_off = qk_cols // RET_QK_DIM
    v_off = 2 * qk_cols // RET_V_DIM
    y = pl.pallas_call(
        functools.partial(_ret_kernel, n_chunks=n_chunks),
        grid=(batch, RET_HEADS),
        in_specs=[pl.BlockSpec(memory_space=pltpu.SMEM),
                  pl.BlockSpec((seq, RET_QK_DIM), lambda b, h: (b, h)),
                  pl.BlockSpec((seq, RET_QK_DIM), lambda b, h: (b, k_off + h)),
                  pl.BlockSpec((seq, RET_V_DIM), lambda b, h: (b, v_off + h)),
                  pl.BlockSpec((1, RET_V_DIM), lambda b, h: (0, 0))],
        out_specs=pl.BlockSpec((seq, RET_V_DIM), lambda b, h: (b, h)),
        out_shape=jax.ShapeDtypeStruct((n, v_cols), BF16),
        scratch_shapes=[pltpu.VMEM((seq, RET_V_DIM), F32), pltpu.VMEM((RET_QK_DIM, RET_V_DIM), F32)],
        compiler_params=_cparams("parallel", "arbitrary"),
        name="retention",
    )(decay_logit.astype(F32), proj, proj, proj, gn_gain.reshape(1, RET_V_DIM).astype(F32))
    gate_col = (2 * qk_cols + v_cols) // v_cols
    return _out_project(y, 0, v_cols, w_o.astype(BF16), x, mod, 0, seq, 512, "ret_out",
                        gate_src=proj, gate_col=gate_col)


def _conv_pw1_epilogue(acc, j, extras, outs):
    b_ref = extras[0]
    u = acc + b_ref[...]
    dm = u.shape[1] // 2
    outs[0][...] = (u[:, :dm] * jax.nn.sigmoid(u[:, dm:])).astype(BF16)


def _conv_kernel(prev_ref, cur_ref, next_ref, wdw_ref, bdw_ref, lng_ref, lnb_ref, w2_ref, b2_ref, x_ref, mod_ref,
                 o_ref, ext_scr, c_scr, *, tiles_per_seq, row_block):
    tm = cur_ref.shape[0]
    t = pl.program_id(0) % tiles_per_seq
    keep_prev = jnp.where(t == 0, 0.0, 1.0)
    keep_next = jnp.where(t == tiles_per_seq - 1, 0.0, 1.0)
    ext_scr[0:CONV_HALO, :] = prev_ref[...].astype(F32) * keep_prev
    ext_scr[CONV_HALO:CONV_HALO + tm, :] = cur_ref[...].astype(F32)
    ext_scr[CONV_HALO + tm:, :] = next_ref[...].astype(F32) * keep_next
    base = CONV_HALO - CONV_PAD
    for rb in range(tm // row_block):
        r0 = rb * row_block
        acc = jnp.zeros((row_block, cur_ref.shape[1]), F32) + bdw_ref[...]
        for j in range(CONV_WIDTH):
            acc = acc + wdw_ref[j:j + 1, :] * ext_scr[r0 + base + j:r0 + base + j + row_block, :]
        c_scr[r0:r0 + row_block, :] = acc
    u = c_scr[...]
    mu = jnp.mean(u, axis=-1, keepdims=True)
    var = jnp.mean(jnp.square(u - mu), axis=-1, keepdims=True)
    u = (u - mu) * lax.rsqrt(var + EPS) * lng_ref[...] + lnb_ref[...]
    u = _silu(u).astype(BF16)
    z = jnp.dot(u, w2_ref[...], preferred_element_type=F32) + b2_ref[...]
    o_ref[...] = x_ref[...] + mod_ref[0][2:3] * z


def _conv_mixer(x, mod, w_pw1, b_pw1, w_dw, b_dw, ln_gain, ln_bias, w_pw2, b_pw2, seq):
    n, d = x.shape
    tm = 512
    u = _project(x, mod, 0, w_pw1.astype(BF16), [b_pw1.reshape(1, 2 * d).astype(F32)],
                 [pl.BlockSpec((1, 2 * d), lambda i, j: (0, 0))], _conv_pw1_epilogue,
                 jax.ShapeDtypeStruct((n, d), BF16), pl.BlockSpec((tm, d), lambda i, j: (i, 0)),
                 tm, 2 * d, seq, "conv_pw1")
    tm = 256
    tiles_per_seq = seq // tm
    halo_per_tile = tm // CONV_HALO
    n_halo = n // CONV_HALO
    wdw = jnp.pad(w_dw.reshape(CONV_WIDTH, d).astype(F32), ((0, 1), (0, 0)))
    vec = lambda a: a.reshape(1, d).astype(F32)
    row_spec = pl.BlockSpec((1, d), lambda i: (0, 0))
    return pl.pallas_call(
        functools.partial(_conv_kernel, tiles_per_seq=tiles_per_seq, row_block=32),
        grid=(n // tm,),
        in_specs=[pl.BlockSpec((CONV_HALO, d), lambda i: (jnp.maximum(i * halo_per_tile - 1, 0), 0)),
                  pl.BlockSpec((tm, d), lambda i: (i, 0)),
                  pl.BlockSpec((CONV_HALO, d), lambda i: (jnp.minimum((i + 1) * halo_per_tile, n_halo - 1), 0)),
                  pl.BlockSpec((CONV_WIDTH + 1, d), lambda i: (0, 0)),
                  row_spec, row_spec, row_spec,
                  pl.BlockSpec((d, d), lambda i: (0, 0)),
                  row_spec,
                  pl.BlockSpec((tm, d), lambda i: (i, 0)),
                  pl.BlockSpec((1, 6, d), lambda i: (i // tiles_per_seq, 0, 0))],
        out_specs=pl.BlockSpec((tm, d), lambda i: (i, 0)),
        out_shape=jax.ShapeDtypeStruct((n, d), F32),
        scratch_shapes=[pltpu.VMEM((tm + 2 * CONV_HALO, d), F32), pltpu.VMEM((tm, d), F32)],
        compiler_params=_cparams("parallel"),
        name="conv_module",
    )(u, u, u, wdw, vec(b_dw), vec(ln_gain), vec(ln_bias), w_pw2.astype(BF16), vec(b_pw2), x, mod)


def _t5_bucket(rel):
    half = T5_BUCKETS // 2
    exact = half // 2
    n = jnp.abs(rel)
    log_ratio = jnp.log(jnp.maximum(n, 1).astype(F32) / exact) / math.log(T5_MAX_DIST / exact)
    large = jnp.minimum(exact + (log_ratio * (half - exact)).astype(jnp.int32), half - 1)
    return jnp.where(rel > 0, half, 0) + jnp.where(n < exact, n, large)


def _bias_tile_kernel(table_ref, idx_ref, o_ref):
    h = pl.program_id(0)
    idx = idx_ref[0]
    acc = jnp.zeros(idx.shape, F32)
    for b in range(T5_BUCKETS):
        acc = jnp.where(idx == b, table_ref[b, h], acc)
    o_ref[0, 0] = acc * LOG2_E


def _diff_kernel(q_ref, k_ref, v_ref, bias_ref, lam_ref, gain_ref, o_ref, s_scr, m_scr, acc_scr, *, n_chunks, out_scale,
                 lambda_init):
    tq = q_ref.shape[0]
    qi = pl.program_id(2)
    q = q_ref[...]
    lane = lax.broadcasted_iota(jnp.int32, (tq, LANES), 1)
    low = lane < DIFF_HEAD_DIM
    zero = jnp.zeros((tq, LANES), BF16)
    lhs = jnp.concatenate([jnp.where(low, q, zero), jnp.where(low, zero, q)], axis=0)

    def add_bias(s, j):
        tile = bias_ref[0, jnp.clip(j - qi, -2, 2) + 2]
        return s + jnp.concatenate([tile, tile], axis=0)

    o = _softmax_pv(lhs, k_ref, v_ref, s_scr, m_scr, acc_scr, n_chunks, add_bias)
    lam_v = lam_ref[...]
    lam = (jnp.exp(jnp.sum(lam_v[0:1] * lam_v[1:2], axis=-1, keepdims=True))
           - jnp.exp(jnp.sum(lam_v[2:3] * lam_v[3:4], axis=-1, keepdims=True)) + lambda_init)
    y = o[:tq] - lam * o[tq:]
    y = y * lax.rsqrt(jnp.mean(y * y, axis=-1, keepdims=True) + EPS) * gain_ref[...] * out_scale
    o_ref[...] = y.astype(BF16)


def _diff_mixer(x, mod, w_qkv, lam_q1, lam_k1, lam_q2, lam_k2, subln_gain, w_o, rel_bias, lambda_init,
                batch, seq):
    n, d = x.shape
    col_scale = jnp.concatenate([jnp.full((d,), DIFF_HEAD_DIM ** -0.5 * LOG2_E, F32), jnp.ones((2 * d,), F32)])
    w = (w_qkv * col_scale[None, :]).astype(BF16)
    tm = 512
    qkv = _project(x, mod, 0, w, [], [], _plain_epilogue,
                   jax.ShapeDtypeStruct((n, 3 * d), BF16), pl.BlockSpec((tm, d), lambda i, j: (i, j)),
                   tm, d, seq, "diff_qkv")

    tq = KEY_CHUNK
    ar = jnp.arange(KEY_CHUNK, dtype=jnp.int32)
    offs = (jnp.arange(5, dtype=jnp.int32) - 2) * KEY_CHUNK
    rel = (ar[None, None, :] + offs[:, None, None]) - ar[None, :, None]
    bucket = _t5_bucket(rel).astype(jnp.int32)
    bias = pl.pallas_call(
        _bias_tile_kernel,
        grid=(DIFF_HEADS, 5),
        in_specs=[pl.BlockSpec(memory_space=pltpu.SMEM),
                  pl.BlockSpec((1, tq, KEY_CHUNK), lambda h, t: (t, 0, 0))],
        out_specs=pl.BlockSpec((1, 1, tq, KEY_CHUNK), lambda h, t: (h, t, 0, 0)),
        out_shape=jax.ShapeDtypeStruct((DIFF_HEADS, 5, tq, KEY_CHUNK), F32),
        compiler_params=_cparams("arbitrary", "arbitrary"),
        name="rel_bias_tiles",
    )(rel_bias.astype(F32), bucket)

    lam = jnp.stack([lam_q1, lam_k1, lam_q2, lam_k2]).astype(F32)
    n_chunks = seq // KEY_CHUNK
    q_tiles = seq // tq
    o = pl.pallas_call(
        functools.partial(_diff_kernel, n_chunks=n_chunks, out_scale=1.0 - lambda_init, lambda_init=lambda_init),
        grid=(batch, DIFF_HEADS, q_tiles),
        in_specs=[pl.BlockSpec((tq, LANES), lambda b, h, i: (b * q_tiles + i, h)),
                  pl.BlockSpec((seq, LANES), lambda b, h, i: (b, DIFF_HEADS + h)),
                  pl.BlockSpec((seq, LANES), lambda b, h, i: (b, 2 * DIFF_HEADS + h)),
                  pl.BlockSpec((1, 5, tq, KEY_CHUNK), lambda b, h, i: (h, 0, 0, 0)),
                  pl.BlockSpec((4, DIFF_HEAD_DIM), lambda b, h, i: (0, 0)),
                  pl.BlockSpec((1, LANES), lambda b, h, i: (0, 0))],
        out_specs=pl.BlockSpec((tq, LANES), lambda b, h, i: (b * q_tiles + i, h)),
        out_shape=jax.ShapeDtypeStruct((n, d), BF16),
        scratch_shapes=[pltpu.VMEM((n_chunks, 2 * tq, KEY_CHUNK), F32),
                        pltpu.VMEM((2 * tq, LANES), F32), pltpu.VMEM((2 * tq, 2 * LANES), F32)],
        compiler_params=_cparams("parallel", "parallel", "arbitrary"),
        name="diff_attention",
    )(qkv, qkv, qkv, bias, lam, subln_gain.reshape(1, LANES).astype(F32))
    return _out_project(o, 0, d, w_o.astype(BF16), x, mod, 0, seq, 512, "diff_out")


def _final_norm_kernel(x_ref, g_ref, o_ref):
    x = x_ref[...]
    o_ref[...] = x * lax.rsqrt(jnp.mean(x * x, axis=-1, keepdims=True) + EPS) * g_ref[...]


def _final_norm(x, gain, tm):
    n, d = x.shape
    return pl.pallas_call(
        _final_norm_kernel,
        grid=(n // tm,),
        in_specs=[pl.BlockSpec((tm, d), lambda i: (i, 0)), pl.BlockSpec((1, d), lambda i: (0, 0))],
        out_specs=pl.BlockSpec((tm, d), lambda i: (i, 0)),
        out_shape=jax.ShapeDtypeStruct((n, d), F32),
        compiler_params=_cparams("parallel"),
        name="final_norm",
    )(x, gain.reshape(1, d).astype(F32))


def kernel(x, c, mod_w, mod_b, att_w_qkv, att_q_gain, att_k_gain, att_w_o, ret_w_in, ret_decay_logit, ret_gn_gain, ret_w_o, cnv_w_pw1, cnv_b_pw1, cnv_w_dw, cnv_b_dw, cnv_ln_gain, cnv_ln_bias, cnv_w_pw2, cnv_b_pw2, dif_w_qkv, dif_lam_q1, dif_lam_k1, dif_lam_q2, dif_lam_k2, dif_subln_gain, dif_w_o, rel_bias, ffn_w_gate, ffn_w_up, ffn_w_down, moe_w_router, moe_b_router, moe_w_gate, moe_w_up, moe_w_down, final_gain):
    batch, seq, d = x.shape
    depth = mod_w.shape[0]
    xf = x.reshape(batch * seq, d)
    mods = _modulation(c, mod_w, mod_b)
    for i in range(depth):
        mod = mods[i]
        kind, j = i % 4, i // 4
        if kind == 0:
            xf = _attention_mixer(xf, mod, att_w_qkv[j], att_q_gain[j], att_k_gain[j], att_w_o[j], batch, seq)
        elif kind == 1:
            xf = _retention_mixer(xf, mod, ret_w_in[j], ret_decay_logit[j], ret_gn_gain[j], ret_w_o[j], batch, seq)
        elif kind == 2:
            xf = _conv_mixer(xf, mod, cnv_w_pw1[j], cnv_b_pw1[j], cnv_w_dw[j], cnv_b_dw[j], cnv_ln_gain[j],
                             cnv_ln_bias[j], cnv_w_pw2[j], cnv_b_pw2[j], seq)
        else:
            lambda_init = 0.8 - 0.6 * math.exp(-0.3 * i)
            xf = _diff_mixer(xf, mod, dif_w_qkv[j], dif_lam_q1[j], dif_lam_k1[j], dif_lam_q2[j], dif_lam_k2[j],
                             dif_subln_gain[j], dif_w_o[j], rel_bias, lambda_init, batch, seq)
        if i % 2 == 0:
            m = i // 2
            xf = _ffn(xf, mod, ffn_w_gate[m].astype(BF16), ffn_w_up[m].astype(BF16), ffn_w_down[m].astype(BF16),
                      seq, 512, 1408)
        else:
            m = i // 2
            xf = _moe(xf, mod, moe_w_router[m], moe_b_router[m], moe_w_gate[m], moe_w_up[m], moe_w_down[m], seq)
    return _final_norm(xf, final_gain, 512).reshape(batch, seq, d)
```

```python
import functools
import math

import jax
import jax.numpy as jnp
from jax import lax
from jax.experimental import pallas as pl
from jax.experimental.pallas import tpu as pltpu

F32 = jnp.float32
BF16 = jnp.bfloat16

EPS = 1e-6
GRID_W = 64

ATT_HEADS = 16
ATT_KV_HEADS = 4
ATT_HEAD_DIM = 64
ROPE_THETA = 10000.0

RET_HEADS = 4
RET_QK_DIM = 256
RET_V_DIM = 512
RET_THETA = 10000.0
RET_CHUNK = 256

CONV_WIDTH = 31
CONV_PAD = CONV_WIDTH // 2
CONV_HALO = 16

DIFF_HEADS = 8
DIFF_HEAD_DIM = 64
T5_BUCKETS = 32
T5_MAX_DIST = 128

N_EXPERTS = 8
LANES = 128
SUBLANES = 8
KEY_CHUNK = 256
LOG2_E = 1.4426950408889634
VMEM_LIMIT_BYTES = 56 * 1024 * 1024


def _cparams(*sem):
    return pltpu.CompilerParams(dimension_semantics=sem, vmem_limit_bytes=VMEM_LIMIT_BYTES)


def _silu(v):
    return v * jax.nn.sigmoid(v)


def _norm_mod(x, mod, which):
    shift = mod[3 * which:3 * which + 1]
    scale = mod[3 * which + 1:3 * which + 2]
    y = x * lax.rsqrt(jnp.mean(x * x, axis=-1, keepdims=True) + EPS)
    return y * (1.0 + scale) + shift


def _split_dot(a, b):
    hi = a.astype(BF16)
    lo = (a - hi.astype(F32)).astype(BF16)
    return (jnp.dot(hi, b, preferred_element_type=F32) + jnp.dot(lo, b, preferred_element_type=F32))


def _mod_kernel(c_ref, w_ref, b_ref, o_ref):
    cond = _silu(c_ref[...])
    o_ref[0] = jnp.dot(cond, w_ref[0], preferred_element_type=F32, precision=lax.Precision.HIGHEST) + b_ref[0]


def _modulation(c, mod_w, mod_b):
    depth, d, d6 = mod_w.shape
    b = c.shape[0]
    tn = d
    out = pl.pallas_call(
        _mod_kernel,
        grid=(depth, d6 // tn),
        in_specs=[pl.BlockSpec((b, d), lambda i, j: (0, 0)),
                  pl.BlockSpec((1, d, tn), lambda i, j: (i, 0, j)),
                  pl.BlockSpec((1, 1, tn), lambda i, j: (i, 0, j))],
        out_specs=pl.BlockSpec((1, b, tn), lambda i, j: (i, 0, j)),
        out_shape=jax.ShapeDtypeStruct((depth, b, d6), F32),
        compiler_params=_cparams("arbitrary", "arbitrary"),
        name="modulation",
    )(c, mod_w, mod_b.reshape(depth, 1, d6))
    return out.reshape(depth, b, 6, d)


def _proj_kernel(x_ref, mod_ref, w_ref, *rest, which, n_extra, epilogue):
    extras = rest[:n_extra]
    outs = rest[n_extra:-1]
    h_scr = rest[-1]
    j = pl.program_id(1)

    @pl.when(j == 0)
    def _():
        h_scr[...] = _norm_mod(x_ref[...], mod_ref[0], which).astype(BF16)

    acc = jnp.dot(h_scr[...], w_ref[...], preferred_element_type=F32)
    epilogue(acc, j, extras, outs)


def _project(x, mod, which, w, extras, extra_specs, epilogue, out_shapes, out_specs, tm, tn, seq, name):
    n, d = x.shape
    c = w.shape[1]
    tiles_per_seq = seq // tm
    kern = functools.partial(_proj_kernel, which=which, n_extra=len(extras), epilogue=epilogue)
    return pl.pallas_call(
        kern,
        grid=(n // tm, c // tn),
        in_specs=[pl.BlockSpec((tm, d), lambda i, j: (i, 0)),
                  pl.BlockSpec((1, 6, d), lambda i, j: (i // tiles_per_seq, 0, 0)),
                  pl.BlockSpec((d, tn), lambda i, j: (0, j))] + extra_specs,
        out_specs=out_specs,
        out_shape=out_shapes,
        scratch_shapes=[pltpu.VMEM((tm, d), BF16)],
        compiler_params=_cparams("parallel", "arbitrary"),
        name=name,
    )(x, mod, w, *extras)


def _plain_epilogue(acc, j, extras, outs):
    outs[0][...] = acc.astype(outs[0].dtype)


def _oproj_kernel(*refs, which, gated, has_bias):
    refs = list(refs)
    y_ref = refs.pop(0)
    g_ref = refs.pop(0) if gated else None
    w_ref = refs.pop(0)
    b_ref = refs.pop(0) if has_bias else None
    x_ref, mod_ref, o_ref = refs
    y = y_ref[...]
    if gated:
        y = (y.astype(F32) * _silu(g_ref[...].astype(F32))).astype(BF16)
    z = jnp.dot(y, w_ref[...], preferred_element_type=F32)
    if has_bias:
        z = z + b_ref[...]
    gate = mod_ref[0][3 * which + 2:3 * which + 3]
    o_ref[...] = x_ref[...] + gate * z


def _out_project(y, y_col, k, w, x, mod, which, seq, tm, name, gate_src=None, gate_col=0, bias=None):
    n, d = x.shape
    tiles_per_seq = seq // tm
    args = [y]
    specs = [pl.BlockSpec((tm, k), lambda i: (i, y_col))]
    if gate_src is not None:
        args.append(gate_src)
        specs.append(pl.BlockSpec((tm, k), lambda i: (i, gate_col)))
    args.append(w)
    specs.append(pl.BlockSpec((k, d), lambda i: (0, 0)))
    if bias is not None:
        args.append(bias.reshape(1, d))
        specs.append(pl.BlockSpec((1, d), lambda i: (0, 0)))
    args += [x, mod]
    specs += [pl.BlockSpec((tm, d), lambda i: (i, 0)),
              pl.BlockSpec((1, 6, d), lambda i: (i // tiles_per_seq, 0, 0))]
    kern = functools.partial(_oproj_kernel, which=which, gated=gate_src is not None, has_bias=bias is not None)
    return pl.pallas_call(
        kern,
        grid=(n // tm,),
        in_specs=specs,
        out_specs=pl.BlockSpec((tm, d), lambda i: (i, 0)),
        out_shape=jax.ShapeDtypeStruct((n, d), F32),
        compiler_params=_cparams("parallel"),
        name=name,
    )(*args)


def _ffn_kernel(x_ref, mod_ref, wg_ref, wu_ref, wd_ref, o_ref, h_scr, acc_scr):
    f = pl.program_id(1)

    @pl.when(f == 0)
    def _():
        h_scr[...] = _norm_mod(x_ref[...], mod_ref[0], 1).astype(BF16)
        acc_scr[...] = jnp.zeros_like(acc_scr)

    h = h_scr[...]
    g = jnp.dot(h, wg_ref[...], preferred_element_type=F32)
    u = jnp.dot(h, wu_ref[...], preferred_element_type=F32)
    a = (_silu(g) * u).astype(BF16)
    acc_scr[...] += jnp.dot(a, wd_ref[...], preferred_element_type=F32)

    @pl.when(f == pl.num_programs(1) - 1)
    def _():
        o_ref[...] = x_ref[...] + mod_ref[0][5:6] * acc_scr[...]


def _ffn(x, mod, wg, wu, wd, seq, tm, tf):
    n, d = x.shape
    ff = wg.shape[1]
    tiles_per_seq = seq // tm
    return pl.pallas_call(
        _ffn_kernel,
        grid=(n // tm, ff // tf),
        in_specs=[pl.BlockSpec((tm, d), lambda i, f: (i, 0)),
                  pl.BlockSpec((1, 6, d), lambda i, f: (i // tiles_per_seq, 0, 0)),
                  pl.BlockSpec((d, tf), lambda i, f: (0, f)),
                  pl.BlockSpec((d, tf), lambda i, f: (0, f)),
                  pl.BlockSpec((tf, d), lambda i, f: (f, 0))],
        out_specs=pl.BlockSpec((tm, d), lambda i, f: (i, 0)),
        out_shape=jax.ShapeDtypeStruct((n, d), F32),
        scratch_shapes=[pltpu.VMEM((tm, d), BF16), pltpu.VMEM((tm, d), F32)],
        compiler_params=_cparams("parallel", "arbitrary"),
        name="dense_ffn",
    )(x, mod, wg, wu, wd)


def _router_kernel(x_ref, mod_ref, wr_ref, br_ref, sel_ref, route_ref):
    h = _norm_mod(x_ref[...], mod_ref[0], 1)
    logits = jnp.dot(h, wr_ref[...], preferred_element_type=F32, precision=lax.Precision.HIGHEST) + br_ref[...]
    lane = lax.broadcasted_iota(jnp.int32, logits.shape, 1)
    neg = jnp.float32(-jnp.inf)
    logits = jnp.where(lane < N_EXPERTS, logits, neg)
    m1 = jnp.max(logits, axis=-1, keepdims=True)
    i1 = jnp.min(jnp.where(logits == m1, lane, LANES), axis=-1, keepdims=True)
    rest = jnp.where(lane == i1, neg, logits)
    m2 = jnp.max(rest, axis=-1, keepdims=True)
    i2 = jnp.min(jnp.where(rest == m2, lane, LANES), axis=-1, keepdims=True)
    e2 = jnp.exp(m2 - m1)
    w1 = 1.0 / (1.0 + e2)
    w2 = e2 / (1.0 + e2)
    sel_ref[...] = jnp.where((lane == i1) | (lane == i2), 1.0, 0.0).astype(BF16)
    route = jnp.where(lane == 0, i1.astype(F32), jnp.where(lane == 1, i2.astype(F32), 0.0))
    route_ref[...] = jnp.where(lane == 2, w1, jnp.where(lane == 3, w2, route))


def _router(x, mod, w_router, b_router, seq, tm):
    n, d = x.shape
    tiles_per_seq = seq // tm
    wr = jnp.pad(w_router, ((0, 0), (0, LANES - N_EXPERTS)))
    br = jnp.pad(b_router, (0, LANES - N_EXPERTS)).reshape(1, LANES)
    return pl.pallas_call(
        _router_kernel,
        grid=(n // tm,),
        in_specs=[pl.BlockSpec((tm, d), lambda i: (i, 0)),
                  pl.BlockSpec((1, 6, d), lambda i: (i // tiles_per_seq, 0, 0)),
                  pl.BlockSpec((d, LANES), lambda i: (0, 0)),
                  pl.BlockSpec((1, LANES), lambda i: (0, 0))],
        out_specs=[pl.BlockSpec((tm, LANES), lambda i: (i, 0)), pl.BlockSpec((tm, LANES), lambda i: (i, 0))],
        out_shape=[jax.ShapeDtypeStruct((n, LANES), BF16), jax.ShapeDtypeStruct((n, LANES), F32)],
        compiler_params=_cparams("parallel"),
        name="moe_router",
    )(x, mod, wr, br)


def _rank_kernel(sel_ref, route_ref, tri_ref, rank_ref, count_ref, carry_scr):
    @pl.when(pl.program_id(0) == 0)
    def _():
        carry_scr[...] = jnp.zeros_like(carry_scr)

    sel = sel_ref[...]
    incl = jnp.dot(tri_ref[...], sel, preferred_element_type=F32)
    excl = incl - sel.astype(F32) + carry_scr[...]
    route = route_ref[...]
    lane = lax.broadcasted_iota(jnp.int32, excl.shape, 1)
    lane_f = lane.astype(F32)
    r1 = jnp.sum(jnp.where(lane_f == route[:, 0:1], excl, 0.0), axis=-1, keepdims=True)
    r2 = jnp.sum(jnp.where(lane_f == route[:, 1:2], excl, 0.0), axis=-1, keepdims=True)
    rank_ref[...] = jnp.where(lane == 0, r1, jnp.where(lane == 1, r2, 0.0))
    carry_scr[...] += incl[incl.shape[0] - 1:, :]
    count_ref[...] = carry_scr[...]


def _dispatch_kernel(d1_ref, d2_ref, x_ref, mod_ref, zeros_ref, xs_ref, h_scr, sems):
    del zeros_ref
    tm = x_ref.shape[0]
    h_scr[...] = _norm_mod(x_ref[...], mod_ref[0], 1)

    def row_copy(r, dest, which):
        return pltpu.make_async_copy(h_scr.at[pl.ds(r, 1), :], xs_ref.at[pl.ds(dest, 1), :], sems.at[which])

    def issue(r, carry):
        row_copy(r, d1_ref[0, 0, r], 0).start()
        row_copy(r, d2_ref[0, 0, r], 1).start()
        return carry

    lax.fori_loop(0, tm, issue, 0)
    for which in range(2):
        pltpu.make_async_copy(h_scr, xs_ref.at[pl.ds(0, tm), :], sems.at[which]).wait()


def _expert_kernel(tile_e_ref, n_act_ref, x_ref, wg_ref, wu_ref, wd_ref, o_ref, h_scr, acc_scr):
    del tile_e_ref
    s = pl.program_id(0)
    f = pl.program_id(1)

    @pl.when(s < n_act_ref[0])
    def _():
        @pl.when(f == 0)
        def _():
            h_scr[...] = x_ref[...].astype(BF16)
            acc_scr[...] = jnp.zeros_like(acc_scr)

        h = h_scr[...]
        g = jnp.dot(h, wg_ref[0], preferred_element_type=F32)
        u = jnp.dot(h, wu_ref[0], preferred_element_type=F32)
        a = (_silu(g) * u).astype(BF16)
        acc_scr[...] += jnp.dot(a, wd_ref[0], preferred_element_type=F32)

        @pl.when(f == pl.num_programs(1) - 1)
        def _():
            o_ref[...] = acc_scr[...]

    @pl.when(s >= n_act_ref[0])
    def _():
        o_ref[...] = jnp.zeros_like(o_ref)


def _combine_kernel(d1_ref, d2_ref, route_ref, x_ref, mod_ref, ys_ref, o_ref, buf, sems):
    tm = x_ref.shape[0]

    def row_copy(r, src, which):
        return pltpu.make_async_copy(ys_ref.at[pl.ds(src, 1), :], buf.at[which, pl.ds(r, 1), :], sems.at[which])

    def issue(r, carry):
        row_copy(r, d1_ref[0, 0, r], 0).start()
        row_copy(r, d2_ref[0, 0, r], 1).start()
        return carry

    lax.fori_loop(0, tm, issue, 0)
    for which in range(2):
        pltpu.make_async_copy(ys_ref.at[pl.ds(0, tm), :], buf.at[which], sems.at[which]).wait()
    route = route_ref[...]
    y = route[:, 2:3] * buf[0] + route[:, 3:4] * buf[1]
    o_ref[...] = x_ref[...] + mod_ref[0][5:6] * y


def _moe(x, mod, w_router, b_router, w_gate, w_up, w_down, seq):
    n, d = x.shape
    n_e, _, ff = w_gate.shape
    tm = 512
    te = 512
    tf = 896
    tiles_per_seq = seq // tm
    n_tok_tiles = n // tm
    n_slots = 2 * n + n_e * te
    n_tiles = n_slots // te
    f_tiles = ff // tf

    sel, route = _router(x, mod, w_router, b_router, seq, tm)
    tri = (jnp.arange(tm)[:, None] >= jnp.arange(tm)[None, :]).astype(BF16)
    rank, counts = pl.pallas_call(
        _rank_kernel,
        grid=(n_tok_tiles,),
        in_specs=[pl.BlockSpec((tm, LANES), lambda i: (i, 0)),
                  pl.BlockSpec((tm, LANES), lambda i: (i, 0)),
                  pl.BlockSpec((tm, tm), lambda i: (0, 0))],
        out_specs=[pl.BlockSpec((tm, LANES), lambda i: (i, 0)), pl.BlockSpec((1, LANES), lambda i: (0, 0))],
        out_shape=[jax.ShapeDtypeStruct((n, LANES), F32), jax.ShapeDtypeStruct((1, LANES), F32)],
        scratch_shapes=[pltpu.VMEM((1, LANES), F32)],
        compiler_params=_cparams("arbitrary"),
        name="moe_rank",
    )(sel, route, tri)

    counts = counts[0, :n_e].astype(jnp.int32)
    padded = ((counts + te - 1) // te) * te
    ends = jnp.cumsum(padded)
    offsets = ends - padded
    n_act = (ends[-1] // te).astype(jnp.int32)
    tile_start = jnp.arange(n_tiles, dtype=jnp.int32) * te
    tile_e = jnp.sum((tile_start[:, None] >= ends[None, :]).astype(jnp.int32), axis=1)
    tile_e = jnp.minimum(tile_e, tile_e[jnp.maximum(n_act - 1, 0)]).astype(jnp.int32)
    e1 = route[:, 0].astype(jnp.int32)
    e2 = route[:, 1].astype(jnp.int32)
    dest1 = (offsets[e1] + rank[:, 0].astype(jnp.int32)).reshape(n_tok_tiles, 1, tm)
    dest2 = (offsets[e2] + rank[:, 1].astype(jnp.int32)).reshape(n_tok_tiles, 1, tm)
    n_act = n_act.reshape(1)

    slot_spec = pl.BlockSpec((1, 1, tm), lambda i: (i, 0, 0), memory_space=pltpu.SMEM)
    xs = pl.pallas_call(
        _dispatch_kernel,
        grid=(n_tok_tiles,),
        in_specs=[slot_spec, slot_spec,
                  pl.BlockSpec((tm, d), lambda i: (i, 0)),
                  pl.BlockSpec((1, 6, d), lambda i: (i // tiles_per_seq, 0, 0)),
                  pl.BlockSpec(memory_space=pl.ANY)],
        out_specs=pl.BlockSpec(memory_space=pl.ANY),
        out_shape=jax.ShapeDtypeStruct((n_slots, d), F32),
        scratch_shapes=[pltpu.VMEM((tm, d), F32), pltpu.SemaphoreType.DMA((2,))],
        input_output_aliases={4: 0},
        compiler_params=_cparams("arbitrary"),
        name="moe_dispatch",
    )(dest1, dest2, x, mod, jnp.zeros((n_slots, d), F32))

    def live(s, n_act_ref):
        return jnp.minimum(s, jnp.maximum(n_act_ref[0] - 1, 0))

    def f_idx(s, f, n_act_ref):
        return jnp.where(s < n_act_ref[0], f, f_tiles - 1)

    ys = pl.pallas_call(
        _expert_kernel,
        grid_spec=pltpu.PrefetchScalarGridSpec(
            num_scalar_prefetch=2,
            grid=(n_tiles, f_tiles),
            in_specs=[pl.BlockSpec((te, d), lambda s, f, te_ref, na_ref: (live(s, na_ref), 0)),
                      pl.BlockSpec((1, d, tf), lambda s, f, te_ref, na_ref: (te_ref[s], 0, f_idx(s, f, na_ref))),
                      pl.BlockSpec((1, d, tf), lambda s, f, te_ref, na_ref: (te_ref[s], 0, f_idx(s, f, na_ref))),
                      pl.BlockSpec((1, tf, d), lambda s, f, te_ref, na_ref: (te_ref[s], f_idx(s, f, na_ref), 0))],
            out_specs=pl.BlockSpec((te, d), lambda s, f, te_ref, na_ref: (s, 0)),
            scratch_shapes=[pltpu.VMEM((te, d), BF16), pltpu.VMEM((te, d), F32)]),
        out_shape=jax.ShapeDtypeStruct((n_slots, d), F32),
        compiler_params=_cparams("arbitrary", "arbitrary"),
        name="moe_experts",
    )(tile_e, n_act, xs, w_gate.astype(BF16), w_up.astype(BF16), w_down.astype(BF16))

    return pl.pallas_call(
        _combine_kernel,
        grid=(n_tok_tiles,),
        in_specs=[slot_spec, slot_spec,
                  pl.BlockSpec((tm, LANES), lambda i: (i, 0)),
                  pl.BlockSpec((tm, d), lambda i: (i, 0)),
                  pl.BlockSpec((1, 6, d), lambda i: (i // tiles_per_seq, 0, 0)),
                  pl.BlockSpec(memory_space=pl.ANY)],
        out_specs=pl.BlockSpec((tm, d), lambda i: (i, 0)),
        out_shape=jax.ShapeDtypeStruct((n, d), F32),
        scratch_shapes=[pltpu.VMEM((2, tm, d), F32), pltpu.SemaphoreType.DMA((2,))],
        compiler_params=_cparams("arbitrary"),
        name="moe_combine",
    )(dest1, dest2, route, x, mod, ys)


def _softmax_pv(lhs, k_ref, v_ref, s_scr, m_scr, acc_scr, n_chunks, bias_fn=None):
    rows = lhs.shape[0]
    group = min(8, n_chunks)
    m_scr[...] = jnp.full(m_scr.shape, -jnp.inf, F32)

    def logits_body(t, carry):
        for u in range(group):
            j = group * t + u
            start = pl.multiple_of(j * KEY_CHUNK, KEY_CHUNK)
            kc = k_ref[pl.ds(start, KEY_CHUNK), :]
            s = lax.dot_general(lhs, kc, (((1,), (1,)), ((), ())), preferred_element_type=F32)
            if bias_fn is not None:
                s = bias_fn(s, j)
            s_scr[j] = s
            m_scr[...] = jnp.maximum(m_scr[...], jnp.maximum(s[:, :LANES], s[:, LANES:]))
        return carry

    lax.fori_loop(0, n_chunks // group, logits_body, 0)
    m_scr[...] = jnp.broadcast_to(jnp.max(m_scr[...], axis=-1, keepdims=True), m_scr.shape)
    acc_scr[...] = jnp.zeros_like(acc_scr)
    ones = jnp.ones((group * KEY_CHUNK, LANES), BF16)

    def pv_body(t, carry):
        mb = m_scr[...]
        parts = []
        for u in range(group):
            s = s_scr[group * t + u]
            parts.append(jnp.exp2(s[:, :LANES] - mb).astype(BF16))
            parts.append(jnp.exp2(s[:, LANES:] - mb).astype(BF16))
        p = jnp.concatenate(parts, axis=1)
        start = pl.multiple_of(t * (group * KEY_CHUNK), group * KEY_CHUNK)
        v_aug = jnp.concatenate([v_ref[pl.ds(start, group * KEY_CHUNK), :], ones], axis=1)
        acc_scr[...] += jnp.dot(p, v_aug, preferred_element_type=F32)
        return carry

    lax.fori_loop(0, n_chunks // group, pv_body, 0)
    acc = acc_scr[...]
    return acc[:, :LANES] / acc[:, LANES:]


def _att_qkv_epilogue(acc, j, extras, outs, n_qk):
    gain_ref, cos_ref, sin_ref, g1_ref, g2_ref = extras
    q_ref, k_ref, v_ref = outs
    qk = acc[:, :n_qk]
    ms = _split_dot(qk * qk, g1_ref[...])
    r = lax.rsqrt(ms + EPS)
    y = qk * _split_dot(r, g2_ref[...]) * gain_ref[...]
    cos = cos_ref[...]
    sin = sin_ref[...]
    lane = lax.broadcasted_iota(jnp.int32, cos.shape, 1)
    first_half = (lane & (ATT_HEAD_DIM // 2)) == 0
    nq = q_ref.shape[1]
    for c in range(n_qk // LANES):
        yc = y[:, c * LANES:(c + 1) * LANES]
        partner = jnp.where(first_half, pltpu.roll(yc, LANES - ATT_HEAD_DIM // 2, 1),
                            pltpu.roll(yc, ATT_HEAD_DIM // 2, 1))
        out = (yc * cos + partner * sin).astype(BF16)
        if c * LANES < nq:
            q_ref[:, c * LANES:(c + 1) * LANES] = out
        else:
            k_ref[:, c * LANES - nq:(c + 1) * LANES - nq] = out
    v_ref[...] = acc[:, n_qk:].astype(BF16)


def _att_kernel(q_ref, k_ref, v_ref, o_ref, s_scr, m_scr, acc_scr, *, n_chunks):
    tq = q_ref.shape[0]
    q = q_ref[...]
    lane = lax.broadcasted_iota(jnp.int32, (tq, LANES), 1)
    low = lane < ATT_HEAD_DIM
    zero = jnp.zeros((tq, LANES), BF16)
    parts = []
    for pair in range(2):
        qp = q[:, pair * LANES:(pair + 1) * LANES]
        parts.append(jnp.where(low, qp, zero))
        parts.append(jnp.where(low, zero, qp))
    lhs = jnp.concatenate(parts, axis=0)
    o = _softmax_pv(lhs, k_ref, v_ref, s_scr, m_scr, acc_scr, n_chunks)
    for pair in range(2):
        o_ref[:, pair * LANES:(pair + 1) * LANES] = jnp.where(
            low, o[(2 * pair) * tq:(2 * pair + 1) * tq], o[(2 * pair + 1) * tq:(2 * pair + 2) * tq]).astype(BF16)


def _attention_mixer(x, mod, w_qkv, q_gain, k_gain, w_o, batch, seq):
    n, d = x.shape
    hd = ATT_HEAD_DIM
    half = hd // 2
    n_freq = half // 2
    groups = ATT_HEADS // ATT_KV_HEADS
    perm = jnp.concatenate([jnp.arange(n_freq), half + jnp.arange(n_freq),
                            n_freq + jnp.arange(n_freq), half + n_freq + jnp.arange(n_freq)])
    q_cols = (jnp.arange(ATT_HEADS)[:, None] * hd + perm[None, :]).reshape(-1)
    k_base = ATT_HEADS * hd
    v_base = k_base + ATT_KV_HEADS * hd
    kv_rep = jnp.repeat(jnp.arange(ATT_KV_HEADS), 2)
    k_cols = (k_base + kv_rep[:, None] * hd + perm[None, :]).reshape(-1)
    v_cols = (v_base + kv_rep[:, None] * hd + jnp.arange(hd)[None, :]).reshape(-1)
    w = w_qkv[:, jnp.concatenate([q_cols, k_cols, v_cols])].astype(BF16)
    n_q = ATT_HEADS * hd
    n_kv = 2 * ATT_KV_HEADS * hd
    n_qk = n_q + n_kv
    gain_row = jnp.concatenate([jnp.tile(q_gain[perm] * (hd ** -0.5 * LOG2_E), ATT_HEADS),
                                jnp.tile(k_gain[perm], 2 * ATT_KV_HEADS)]).reshape(1, n_qk).astype(F32)
    pos = jnp.arange(seq, dtype=jnp.int32)
    row = (pos // GRID_W).astype(F32)
    col = (pos % GRID_W).astype(F32)
    inv = ROPE_THETA ** (-jnp.arange(n_freq, dtype=F32) / n_freq)
    ang = jnp.concatenate([row[:, None] * inv, col[:, None] * inv], axis=-1)
    cos_t = jnp.tile(jnp.cos(ang), (1, 2 * LANES // hd))
    sin_t = jnp.tile(jnp.concatenate([-jnp.sin(ang), jnp.sin(ang)], axis=-1), (1, LANES // hd))
    grp = jnp.arange(n_qk) // hd
    g1 = (grp[:, None] == jnp.arange(LANES)[None, :]).astype(F32) / hd
    g2 = (jnp.arange(LANES)[:, None] == grp[None, :]).astype(F32)
    tm = 512
    tiles_per_seq = seq // tm
    extras = [gain_row, cos_t, sin_t, g1.astype(BF16), g2.astype(BF16)]
    extra_specs = [pl.BlockSpec((1, n_qk), lambda i, j: (0, 0)),
                   pl.BlockSpec((tm, LANES), lambda i, j: (i % tiles_per_seq, 0)),
                   pl.BlockSpec((tm, LANES), lambda i, j: (i % tiles_per_seq, 0)),
                   pl.BlockSpec((n_qk, LANES), lambda i, j: (0, 0)),
                   pl.BlockSpec((LANES, n_qk), lambda i, j: (0, 0))]
    q, k, v = _project(
        x, mod, 0, w, extras, extra_specs, functools.partial(_att_qkv_epilogue, n_qk=n_qk),
        [jax.ShapeDtypeStruct((n, n_q), BF16), jax.ShapeDtypeStruct((n, n_kv), BF16),
         jax.ShapeDtypeStruct((n, n_kv), BF16)],
        [pl.BlockSpec((tm, n_q), lambda i, j: (i, 0)), pl.BlockSpec((tm, n_kv), lambda i, j: (i, 0)),
         pl.BlockSpec((tm, n_kv), lambda i, j: (i, 0))],
        tm, w.shape[1], seq, "att_qkv")

    tq = 128
    n_chunks = seq // KEY_CHUNK
    q_tiles = seq // tq
    o = pl.pallas_call(
        functools.partial(_att_kernel, n_chunks=n_chunks),
        grid=(batch, ATT_KV_HEADS, q_tiles),
        in_specs=[pl.BlockSpec((tq, groups * hd), lambda b, g, i: (b * q_tiles + i, g)),
                  pl.BlockSpec((seq, LANES), lambda b, g, i: (b, g)),
                  pl.BlockSpec((seq, LANES), lambda b, g, i: (b, g))],
        out_specs=pl.BlockSpec((tq, groups * hd), lambda b, g, i: (b * q_tiles + i, g)),
        out_shape=jax.ShapeDtypeStruct((n, n_q), BF16),
        scratch_shapes=[pltpu.VMEM((n_chunks, groups * tq, KEY_CHUNK), F32),
                        pltpu.VMEM((groups * tq, LANES), F32), pltpu.VMEM((groups * tq, 2 * LANES), F32)],
        compiler_params=_cparams("parallel", "parallel", "arbitrary"),
        name="gqa_attention",
    )(q, k, v)
    return _out_project(o, 0, n_q, w_o.astype(BF16), x, mod, 0, seq, 1024, "att_out")


def _ret_proj_epilogue(acc, j, extras, outs):
    cos_ref, sin_ref = extras
    o_ref = outs[0]

    @pl.when(j < 2)
    def _():
        cos = cos_ref[...]
        sin = sin_ref[...]
        half = RET_QK_DIM // 2
        for hh in range(RET_HEADS):
            x1 = acc[:, hh * RET_QK_DIM:hh * RET_QK_DIM + half]
            x2 = acc[:, hh * RET_QK_DIM + half:(hh + 1) * RET_QK_DIM]
            o_ref[:, hh * RET_QK_DIM:hh * RET_QK_DIM + half] = (x1 * cos - x2 * sin).astype(BF16)
            o_ref[:, hh * RET_QK_DIM + half:(hh + 1) * RET_QK_DIM] = (x1 * sin + x2 * cos).astype(BF16)

    @pl.when(j >= 2)
    def _():
        o_ref[...] = acc.astype(BF16)


def _ret_kernel(logit_ref, q_ref, k_ref, v_ref, gain_ref, o_ref, acc_scr, state_scr, *, n_chunks):
    h = pl.program_id(1)
    c_len = RET_CHUNK

    def log_gamma(direction, shape):
        z = jnp.full(shape, logit_ref[direction, h], F32)
        return jnp.minimum(z, 0.0) - jnp.log1p(jnp.exp(-jnp.abs(z)))

    ri = lax.broadcasted_iota(jnp.int32, (c_len, c_len), 0)
    ci = lax.broadcasted_iota(jnp.int32, (c_len, c_len), 1)
    dist = (ri - ci).astype(F32)
    lg_f = log_gamma(0, (c_len, c_len))
    lg_b = log_gamma(1, (c_len, c_len))
    decay = jnp.where(dist >= 0, jnp.exp(lg_f * jnp.maximum(dist, 0.0)), jnp.exp(lg_b * jnp.maximum(-dist, 0.0)))
    idx = lax.broadcasted_iota(jnp.int32, (c_len, 1), 0).astype(F32)
    lgf = log_gamma(0, (c_len, 1))
    lgb = log_gamma(1, (c_len, 1))
    xi_f = jnp.exp(lgf * (idx + 1.0))
    zeta_f = jnp.exp(lgf * (c_len - 1.0 - idx))
    cd_f = jnp.exp(lgf[0:1] * c_len)
    xi_b = jnp.exp(lgb * (c_len - idx))
    zeta_b = jnp.exp(lgb * idx)
    cd_b = jnp.exp(lgb[0:1] * c_len)

    def chunk(cidx):
        start = pl.multiple_of(cidx * c_len, c_len)
        return (q_ref[pl.ds(start, c_len), :], k_ref[pl.ds(start, c_len), :], v_ref[pl.ds(start, c_len), :], start)

    def update_state(kc, vc, zeta, cd):
        zv = (vc.astype(F32) * zeta).astype(BF16)
        kv = lax.dot_general(kc, zv, (((0,), (0,)), ((), ())), preferred_element_type=F32)
        state_scr[...] = state_scr[...] * cd + kv

    state_scr[...] = jnp.zeros_like(state_scr)

    def fwd_body(cidx, carry):
        qc, kc, vc, start = chunk(cidx)
        scores = lax.dot_general(qc, kc, (((1,), (1,)), ((), ())), preferred_element_type=F32) * decay
        o = jnp.dot(scores.astype(BF16), vc, preferred_element_type=F32)
        o = o + xi_f * jnp.dot(qc, state_scr[...].astype(BF16), preferred_element_type=F32)
        acc_scr[pl.ds(start, c_len), :] = o
        update_state(kc, vc, zeta_f, cd_f)
        return carry

    lax.fori_loop(0, n_chunks, fwd_body, 0)
    state_scr[...] = jnp.zeros_like(state_scr)

    def bwd_body(t, carry):
        cidx = n_chunks - 1 - t
        qc, kc, vc, start = chunk(cidx)
        o = xi_b * jnp.dot(qc, state_scr[...].astype(BF16), preferred_element_type=F32)
        acc_scr[pl.ds(start, c_len), :] += o
        update_state(kc, vc, zeta_b, cd_b)
        return carry

    lax.fori_loop(0, n_chunks, bwd_body, 0)

    def norm_body(cidx, carry):
        start = pl.multiple_of(cidx * c_len, c_len)
        y = acc_scr[pl.ds(start, c_len), :]
        y = y * lax.rsqrt(jnp.mean(y * y, axis=-1, keepdims=True) + EPS) * gain_ref[...]
        o_ref[pl.ds(start, c_len), :] = y.astype(BF16)
        return carry

    lax.fori_loop(0, n_chunks, norm_body, 0)


def _retention_mixer(x, mod, w_in, decay_logit, gn_gain, w_o, batch, seq):
    n, d = x.shape
    qk_cols = RET_HEADS * RET_QK_DIM
    v_cols = RET_HEADS * RET_V_DIM
    col_scale = jnp.concatenate([jnp.ones((qk_cols,), F32), jnp.full((qk_cols,), RET_QK_DIM ** -0.5, F32),
                                 jnp.ones((2 * v_cols,), F32)])
    w = (w_in * col_scale[None, :]).astype(BF16)
    pos = jnp.arange(seq, dtype=F32)
    inv = 1.0 / (RET_THETA ** jnp.linspace(0.0, 1.0, RET_QK_DIM // 2, dtype=F32))
    ang = pos[:, None] * inv
    tm, tn = 1024, qk_cols
    tiles_per_seq = seq // tm
    half = RET_QK_DIM // 2
    extras = [jnp.cos(ang), jnp.sin(ang)]
    extra_specs = [pl.BlockSpec((tm, half), lambda i, j: (i % tiles_per_seq, 0)),
                   pl.BlockSpec((tm, half), lambda i, j: (i % tiles_per_seq, 0))]
    proj = _project(x, mod, 0, w, extras, extra_specs, _ret_proj_epilogue,
                    jax.ShapeDtypeStruct((n, w.shape[1]), BF16), pl.BlockSpec((tm, tn), lambda i, j: (i, j)),
                    tm, tn, seq, "ret_proj")

    n_chunks = seq // RET_CHUNK
    k_off = qk_cols // RET_QK_DIM
    v_off = 2 * qk_cols // RET_V_DIM
    y = pl.pallas_call(
        functools.partial(_ret_kernel, n_chunks=n_chunks),
        grid=(batch, RET_HEADS),
        in_specs=[pl.BlockSpec(memory_space=pltpu.SMEM),
                  pl.BlockSpec((seq, RET_QK_DIM), lambda b, h: (b, h)),
                  pl.BlockSpec((seq, RET_QK_DIM), lambda b, h: (b, k_off + h)),
                  pl.BlockSpec((seq, RET_V_DIM), lambda b, h: (b, v_off + h)),
                  pl.BlockSpec((1, RET_V_DIM), lambda b, h: (0, 0))],
        out_specs=pl.BlockSpec((seq, RET_V_DIM), lambda b, h: (b, h)),
        out_shape=jax.ShapeDtypeStruct((n, v_cols), BF16),
        scratch_shapes=[pltpu.VMEM((seq, RET_V_DIM), F32), pltpu.VMEM((RET_QK_DIM, RET_V_DIM), F32)],
        compiler_params=_cparams("parallel", "arbitrary"),
        name="retention",
    )(decay_logit.astype(F32), proj, proj, proj, gn_gain.reshape(1, RET_V_DIM).astype(F32))
    gate_col = (2 * qk_cols + v_cols) // v_cols
    return _out_project(y, 0, v_cols, w_o.astype(BF16), x, mod, 0, seq, 512, "ret_out",
                        gate_src=proj, gate_col=gate_col)


def _conv_pw1_epilogue(acc, j, extras, outs):
    b_ref = extras[0]
    u = acc + b_ref[...]
    dm = u.shape[1] // 2
    outs[0][...] = (u[:, :dm] * jax.nn.sigmoid(u[:, dm:])).astype(BF16)


def _conv_kernel(prev_ref, cur_ref, next_ref, wdw_ref, bdw_ref, lng_ref, lnb_ref, w2_ref, b2_ref, x_ref, mod_ref,
                 o_ref, ext_scr, shift_scr, c_scr, *, tiles_per_seq, row_block):
    tm = cur_ref.shape[0]
    t = pl.program_id(0) % tiles_per_seq
    keep_prev = jnp.where(t == 0, 0.0, 1.0)
    keep_next = jnp.where(t == tiles_per_seq - 1, 0.0, 1.0)
    ext_scr[0:CONV_HALO, :] = prev_ref[...].astype(F32) * keep_prev
    ext_scr[CONV_HALO:CONV_HALO + tm, :] = cur_ref[...].astype(F32)
    ext_scr[CONV_HALO + tm:, :] = next_ref[...].astype(F32) * keep_next
    base = CONV_HALO - CONV_PAD
    span = tm + 2 * CONV_HALO - SUBLANES
    for r in range(1, SUBLANES):
        shift_scr[r - 1, 0:span, :] = ext_scr[r:r + span, :]
    for rb in range(tm // row_block):
        r0 = rb * row_block
        acc = jnp.zeros((row_block, cur_ref.shape[1]), F32) + bdw_ref[...]
        for j in range(CONV_WIDTH):
            phase = (base + j) % SUBLANES
            start = r0 + base + j - phase
            if phase == 0:
                tap = ext_scr[start:start + row_block, :]
            else:
                tap = shift_scr[phase - 1, start:start + row_block, :]
            acc = acc + wdw_ref[j:j + 1, :] * tap
        c_scr[r0:r0 + row_block, :] = acc
    u = c_scr[...]
    mu = jnp.mean(u, axis=-1, keepdims=True)
    var = jnp.mean(jnp.square(u - mu), axis=-1, keepdims=True)
    u = (u - mu) * lax.rsqrt(var + EPS) * lng_ref[...] + lnb_ref[...]
    u = _silu(u).astype(BF16)
    z = jnp.dot(u, w2_ref[...], preferred_element_type=F32) + b2_ref[...]
    o_ref[...] = x_ref[...] + mod_ref[0][2:3] * z


def _conv_mixer(x, mod, w_pw1, b_pw1, w_dw, b_dw, ln_gain, ln_bias, w_pw2, b_pw2, seq):
    n, d = x.shape
    tm = 512
    u = _project(x, mod, 0, w_pw1.astype(BF16), [b_pw1.reshape(1, 2 * d).astype(F32)],
                 [pl.BlockSpec((1, 2 * d), lambda i, j: (0, 0))], _conv_pw1_epilogue,
                 jax.ShapeDtypeStruct((n, d), BF16), pl.BlockSpec((tm, d), lambda i, j: (i, 0)),
                 tm, 2 * d, seq, "conv_pw1")
    tm = 256
    tiles_per_seq = seq // tm
    halo_per_tile = tm // CONV_HALO
    n_halo = n // CONV_HALO
    wdw = jnp.pad(w_dw.reshape(CONV_WIDTH, d).astype(F32), ((0, 1), (0, 0)))
    vec = lambda a: a.reshape(1, d).astype(F32)
    row_spec = pl.BlockSpec((1, d), lambda i: (0, 0))
    return pl.pallas_call(
        functools.partial(_conv_kernel, tiles_per_seq=tiles_per_seq, row_block=32),
        grid=(n // tm,),
        in_specs=[pl.BlockSpec((CONV_HALO, d), lambda i: (jnp.maximum(i * halo_per_tile - 1, 0), 0)),
                  pl.BlockSpec((tm, d), lambda i: (i, 0)),
                  pl.BlockSpec((CONV_HALO, d), lambda i: (jnp.minimum((i + 1) * halo_per_tile, n_halo - 1), 0)),
                  pl.BlockSpec((CONV_WIDTH + 1, d), lambda i: (0, 0)),
                  row_spec, row_spec, row_spec,
                  pl.BlockSpec((d, d), lambda i: (0, 0)),
                  row_spec,
                  pl.BlockSpec((tm, d), lambda i: (i, 0)),
                  pl.BlockSpec((1, 6, d), lambda i: (i // tiles_per_seq, 0, 0))],
        out_specs=pl.BlockSpec((tm, d), lambda i: (i, 0)),
        out_shape=jax.ShapeDtypeStruct((n, d), F32),
        scratch_shapes=[pltpu.VMEM((tm + 2 * CONV_HALO, d), F32),
                        pltpu.VMEM((SUBLANES - 1, tm + 2 * CONV_HALO, d), F32), pltpu.VMEM((tm, d), F32)],
        compiler_params=_cparams("parallel"),
        name="conv_module",
    )(u, u, u, wdw, vec(b_dw), vec(ln_gain), vec(ln_bias), w_pw2.astype(BF16), vec(b_pw2), x, mod)


def _t5_bucket(rel):
    half = T5_BUCKETS // 2
    exact = half // 2
    n = jnp.abs(rel)
    log_ratio = jnp.log(jnp.maximum(n, 1).astype(F32) / exact) / math.log(T5_MAX_DIST / exact)
    large = jnp.minimum(exact + (log_ratio * (half - exact)).astype(jnp.int32), half - 1)
    return jnp.where(rel > 0, half, 0) + jnp.where(n < exact, n, large)


def _bias_tile_kernel(table_ref, idx_ref, o_ref):
    h = pl.program_id(0)
    idx = idx_ref[0]
    acc = jnp.zeros(idx.shape, F32)
    for b in range(T5_BUCKETS):
        acc = jnp.where(idx == b, table_ref[b, h], acc)
    o_ref[0, 0] = acc * LOG2_E


def _diff_kernel(q_ref, k_ref, v_ref, bias_ref, lam_ref, gain_ref, o_ref, s_scr, m_scr, acc_scr, *, n_chunks, out_scale,
                 lambda_init):
    tq = q_ref.shape[0]
    qi = pl.program_id(2)
    q = q_ref[...]
    lane = lax.broadcasted_iota(jnp.int32, (tq, LANES), 1)
    low = lane < DIFF_HEAD_DIM
    zero = jnp.zeros((tq, LANES), BF16)
    lhs = jnp.concatenate([jnp.where(low, q, zero), jnp.where(low, zero, q)], axis=0)

    def add_bias(s, j):
        tile = bias_ref[0, jnp.clip(j - qi, -2, 2) + 2]
        return s + jnp.concatenate([tile, tile], axis=0)

    o = _softmax_pv(lhs, k_ref, v_ref, s_scr, m_scr, acc_scr, n_chunks, add_bias)
    lam_v = lam_ref[...]
    lam = (jnp.exp(jnp.sum(lam_v[0:1] * lam_v[1:2], axis=-1, keepdims=True))
           - jnp.exp(jnp.sum(lam_v[2:3] * lam_v[3:4], axis=-1, keepdims=True)) + lambda_init)
    y = o[:tq] - lam * o[tq:]
    y = y * lax.rsqrt(jnp.mean(y * y, axis=-1, keepdims=True) + EPS) * gain_ref[...] * out_scale
    o_ref[...] = y.astype(BF16)


def _diff_mixer(x, mod, w_qkv, lam_q1, lam_k1, lam_q2, lam_k2, subln_gain, w_o, rel_bias, lambda_init,
                batch, seq):
    n, d = x.shape
    col_scale = jnp.concatenate([jnp.full((d,), DIFF_HEAD_DIM ** -0.5 * LOG2_E, F32), jnp.ones((2 * d,), F32)])
    w = (w_qkv * col_scale[None, :]).astype(BF16)
    tm = 1024
    qkv = _project(x, mod, 0, w, [], [], _plain_epilogue,
                   jax.ShapeDtypeStruct((n, 3 * d), BF16), pl.BlockSpec((tm, d), lambda i, j: (i, j)),
                   tm, d, seq, "diff_qkv")

    tq = KEY_CHUNK
    ar = jnp.arange(KEY_CHUNK, dtype=jnp.int32)
    offs = (jnp.arange(5, dtype=jnp.int32) - 2) * KEY_CHUNK
    rel = (ar[None, None, :] + offs[:, None, None]) - ar[None, :, None]
    bucket = _t5_bucket(rel).astype(jnp.int32)
    bias = pl.pallas_call(
        _bias_tile_kernel,
        grid=(DIFF_HEADS, 5),
        in_specs=[pl.BlockSpec(memory_space=pltpu.SMEM),
                  pl.BlockSpec((1, tq, KEY_CHUNK), lambda h, t: (t, 0, 0))],
        out_specs=pl.BlockSpec((1, 1, tq, KEY_CHUNK), lambda h, t: (h, t, 0, 0)),
        out_shape=jax.ShapeDtypeStruct((DIFF_HEADS, 5, tq, KEY_CHUNK), F32),
        compiler_params=_cparams("arbitrary", "arbitrary"),
        name="rel_bias_tiles",
    )(rel_bias.astype(F32), bucket)

    lam = jnp.stack([lam_q1, lam_k1, lam_q2, lam_k2]).astype(F32)
    n_chunks = seq // KEY_CHUNK
    q_tiles = seq // tq
    o = pl.pallas_call(
        functools.partial(_diff_kernel, n_chunks=n_chunks, out_scale=1.0 - lambda_init, lambda_init=lambda_init),
        grid=(batch, DIFF_HEADS, q_tiles),
        in_specs=[pl.BlockSpec((tq, LANES), lambda b, h, i: (b * q_tiles + i, h)),
                  pl.BlockSpec((seq, LANES), lambda b, h, i: (b, DIFF_HEADS + h)),
                  pl.BlockSpec((seq, LANES), lambda b, h, i: (b, 2 * DIFF_HEADS + h)),
                  pl.BlockSpec((1, 5, tq, KEY_CHUNK), lambda b, h, i: (h, 0, 0, 0)),
                  pl.BlockSpec((4, DIFF_HEAD_DIM), lambda b, h, i: (0, 0)),
                  pl.BlockSpec((1, LANES), lambda b, h, i: (0, 0))],
        out_specs=pl.BlockSpec((tq, LANES), lambda b, h, i: (b * q_tiles + i, h)),
        out_shape=jax.ShapeDtypeStruct((n, d), BF16),
        scratch_shapes=[pltpu.VMEM((n_chunks, 2 * tq, KEY_CHUNK), F32),
                        pltpu.VMEM((2 * tq, LANES), F32), pltpu.VMEM((2 * tq, 2 * LANES), F32)],
        compiler_params=_cparams("parallel", "parallel", "arbitrary"),
        name="diff_attention",
    )(qkv, qkv, qkv, bias, lam, subln_gain.reshape(1, LANES).astype(F32))
    return _out_project(o, 0, d, w_o.astype(BF16), x, mod, 0, seq, 1024, "diff_out")


def _final_norm_kernel(x_ref, g_ref, o_ref):
    x = x_ref[...]
    o_ref[...] = x * lax.rsqrt(jnp.mean(x * x, axis=-1, keepdims=True) + EPS) * g_ref[...]


def _final_norm(x, gain, tm):
    n, d = x.shape
    return pl.pallas_call(
        _final_norm_kernel,
        grid=(n // tm,),
        in_specs=[pl.BlockSpec((tm, d), lambda i: (i, 0)), pl.BlockSpec((1, d), lambda i: (0, 0))],
        out_specs=pl.BlockSpec((tm, d), lambda i: (i, 0)),
        out_shape=jax.ShapeDtypeStruct((n, d), F32),
        compiler_params=_cparams("parallel"),
        name="final_norm",
    )(x, gain.reshape(1, d).astype(F32))


def kernel(x, c, mod_w, mod_b, att_w_qkv, att_q_gain, att_k_gain, att_w_o, ret_w_in, ret_decay_logit, ret_gn_gain, ret_w_o, cnv_w_pw1, cnv_b_pw1, cnv_w_dw, cnv_b_dw, cnv_ln_gain, cnv_ln_bias, cnv_w_pw2, cnv_b_pw2, dif_w_qkv, dif_lam_q1, dif_lam_k1, dif_lam_q2, dif_lam_k2, dif_subln_gain, dif_w_o, rel_bias, ffn_w_gate, ffn_w_up, ffn_w_down, moe_w_router, moe_b_router, moe_w_gate, moe_w_up, moe_w_down, final_gain):
    batch, seq, d = x.shape
    depth = mod_w.shape[0]
    xf = x.reshape(batch * seq, d)
    mods = _modulation(c, mod_w, mod_b)
    for i in range(depth):
        mod = mods[i]
        kind, j = i % 4, i // 4
        if kind == 0:
            xf = _attention_mixer(xf, mod, att_w_qkv[j], att_q_gain[j], att_k_gain[j], att_w_o[j], batch, seq)
        elif kind == 1:
            xf = _retention_mixer(xf, mod, ret_w_in[j], ret_decay_logit[j], ret_gn_gain[j], ret_w_o[j], batch, seq)
        elif kind == 2:
            xf = _conv_mixer(xf, mod, cnv_w_pw1[j], cnv_b_pw1[j], cnv_w_dw[j], cnv_b_dw[j], cnv_ln_gain[j],
                             cnv_ln_bias[j], cnv_w_pw2[j], cnv_b_pw2[j], seq)
        else:
            lambda_init = 0.8 - 0.6 * math.exp(-0.3 * i)
            xf = _diff_mixer(xf, mod, dif_w_qkv[j], dif_lam_q1[j], dif_lam_k1[j], dif_lam_q2[j], dif_lam_k2[j],
                             dif_subln_gain[j], dif_w_o[j], rel_bias, lambda_init, batch, seq)
        if i % 2 == 0:
            m = i // 2
            xf = _ffn(xf, mod, ffn_w_gate[m].astype(BF16), ffn_w_up[m].astype(BF16), ffn_w_down[m].astype(BF16),
                      seq, 512, 1408)
        else:
            m = i // 2
            xf = _moe(xf, mod, moe_w_router[m], moe_b_router[m], moe_w_gate[m], moe_w_up[m], moe_w_down[m], seq)
    return _final_norm(xf, final_gain, 512).reshape(batch, seq, d)
```

```python
import functools
import math

import jax
import jax.numpy as jnp
from jax import lax
from jax.experimental import pallas as pl
from jax.experimental.pallas import tpu as pltpu

F32 = jnp.float32
BF16 = jnp.bfloat16

EPS = 1e-6
GRID_W = 64

ATT_HEADS = 16
ATT_KV_HEADS = 4
ATT_HEAD_DIM = 64
ROPE_THETA = 10000.0

RET_HEADS = 4
RET_QK_DIM = 256
RET_V_DIM = 512
RET_THETA = 10000.0
RET_CHUNK = 256

CONV_WIDTH = 31
CONV_PAD = CONV_WIDTH // 2
CONV_HALO = 16

DIFF_HEADS = 8
DIFF_HEAD_DIM = 64
T5_BUCKETS = 32
T5_MAX_DIST = 128

N_EXPERTS = 8
LANES = 128
SUBLANES = 8
KEY_CHUNK = 256
LOG2_E = 1.4426950408889634
VMEM_LIMIT_BYTES = 56 * 1024 * 1024


def _cparams(*sem):
    return pltpu.CompilerParams(dimension_semantics=sem, vmem_limit_bytes=VMEM_LIMIT_BYTES)


def _silu(v):
    return v * jax.nn.sigmoid(v)


def _norm_mod(x, mod, which):
    shift = mod[3 * which:3 * which + 1]
    scale = mod[3 * which + 1:3 * which + 2]
    y = x * lax.rsqrt(jnp.mean(x * x, axis=-1, keepdims=True) + EPS)
    return y * (1.0 + scale) + shift


def _split_dot(a, b):
    hi = a.astype(BF16)
    lo = (a - hi.astype(F32)).astype(BF16)
    return (jnp.dot(hi, b, preferred_element_type=F32) + jnp.dot(lo, b, preferred_element_type=F32))


def _mod_kernel(c_ref, w_ref, b_ref, o_ref):
    cond = _silu(c_ref[...])
    o_ref[0] = jnp.dot(cond, w_ref[0], preferred_element_type=F32, precision=lax.Precision.HIGHEST) + b_ref[0]


def _modulation(c, mod_w, mod_b):
    depth, d, d6 = mod_w.shape
    b = c.shape[0]
    tn = d
    out = pl.pallas_call(
        _mod_kernel,
        grid=(depth, d6 // tn),
        in_specs=[pl.BlockSpec((b, d), lambda i, j: (0, 0)),
                  pl.BlockSpec((1, d, tn), lambda i, j: (i, 0, j)),
                  pl.BlockSpec((1, 1, tn), lambda i, j: (i, 0, j))],
        out_specs=pl.BlockSpec((1, b, tn), lambda i, j: (i, 0, j)),
        out_shape=jax.ShapeDtypeStruct((depth, b, d6), F32),
        compiler_params=_cparams("arbitrary", "arbitrary"),
        name="modulation",
    )(c, mod_w, mod_b.reshape(depth, 1, d6))
    return out.reshape(depth, b, 6, d)


def _proj_kernel(x_ref, mod_ref, w_ref, *rest, which, n_extra, epilogue):
    extras = rest[:n_extra]
    outs = rest[n_extra:-1]
    h_scr = rest[-1]
    j = pl.program_id(1)

    @pl.when(j == 0)
    def _():
        h_scr[...] = _norm_mod(x_ref[...], mod_ref[0], which).astype(BF16)

    acc = jnp.dot(h_scr[...], w_ref[...], preferred_element_type=F32)
    epilogue(acc, j, extras, outs)


def _project(x, mod, which, w, extras, extra_specs, epilogue, out_shapes, out_specs, tm, tn, seq, name):
    n, d = x.shape
    c = w.shape[1]
    tiles_per_seq = seq // tm
    kern = functools.partial(_proj_kernel, which=which, n_extra=len(extras), epilogue=epilogue)
    return pl.pallas_call(
        kern,
        grid=(n // tm, c // tn),
        in_specs=[pl.BlockSpec((tm, d), lambda i, j: (i, 0)),
                  pl.BlockSpec((1, 6, d), lambda i, j: (i // tiles_per_seq, 0, 0)),
                  pl.BlockSpec((d, tn), lambda i, j: (0, j))] + extra_specs,
        out_specs=out_specs,
        out_shape=out_shapes,
        scratch_shapes=[pltpu.VMEM((tm, d), BF16)],
        compiler_params=_cparams("parallel", "arbitrary"),
        name=name,
    )(x, mod, w, *extras)


def _plain_epilogue(acc, j, extras, outs):
    outs[0][...] = acc.astype(outs[0].dtype)


def _oproj_kernel(*refs, which, gated, has_bias):
    refs = list(refs)
    y_ref = refs.pop(0)
    g_ref = refs.pop(0) if gated else None
    w_ref = refs.pop(0)
    b_ref = refs.pop(0) if has_bias else None
    x_ref, mod_ref, o_ref = refs
    y = y_ref[...]
    if gated:
        y = (y.astype(F32) * _silu(g_ref[...].astype(F32))).astype(BF16)
    z = jnp.dot(y, w_ref[...], preferred_element_type=F32)
    if has_bias:
        z = z + b_ref[...]
    gate = mod_ref[0][3 * which + 2:3 * which + 3]
    o_ref[...] = x_ref[...] + gate * z


def _out_project(y, y_col, k, w, x, mod, which, seq, tm, name, gate_src=None, gate_col=0, bias=None):
    n, d = x.shape
    tiles_per_seq = seq // tm
    args = [y]
    specs = [pl.BlockSpec((tm, k), lambda i: (i, y_col))]
    if gate_src is not None:
        args.append(gate_src)
        specs.append(pl.BlockSpec((tm, k), lambda i: (i, gate_col)))
    args.append(w)
    specs.append(pl.BlockSpec((k, d), lambda i: (0, 0)))
    if bias is not None:
        args.append(bias.reshape(1, d))
        specs.append(pl.BlockSpec((1, d), lambda i: (0, 0)))
    args += [x, mod]
    specs += [pl.BlockSpec((tm, d), lambda i: (i, 0)),
              pl.BlockSpec((1, 6, d), lambda i: (i // tiles_per_seq, 0, 0))]
    kern = functools.partial(_oproj_kernel, which=which, gated=gate_src is not None, has_bias=bias is not None)
    return pl.pallas_call(
        kern,
        grid=(n // tm,),
        in_specs=specs,
        out_specs=pl.BlockSpec((tm, d), lambda i: (i, 0)),
        out_shape=jax.ShapeDtypeStruct((n, d), F32),
        compiler_params=_cparams("parallel"),
        name=name,
    )(*args)


def _ffn_kernel(x_ref, mod_ref, wg_ref, wu_ref, wd_ref, o_ref, h_scr, acc_scr):
    f = pl.program_id(1)

    @pl.when(f == 0)
    def _():
        h_scr[...] = _norm_mod(x_ref[...], mod_ref[0], 1).astype(BF16)
        acc_scr[...] = jnp.zeros_like(acc_scr)

    h = h_scr[...]
    g = jnp.dot(h, wg_ref[...], preferred_element_type=F32)
    u = jnp.dot(h, wu_ref[...], preferred_element_type=F32)
    a = (_silu(g) * u).astype(BF16)
    acc_scr[...] += jnp.dot(a, wd_ref[...], preferred_element_type=F32)

    @pl.when(f == pl.num_programs(1) - 1)
    def _():
        o_ref[...] = x_ref[...] + mod_ref[0][5:6] * acc_scr[...]


def _ffn(x, mod, wg, wu, wd, seq, tm, tf):
    n, d = x.shape
    ff = wg.shape[1]
    tiles_per_seq = seq // tm
    return pl.pallas_call(
        _ffn_kernel,
        grid=(n // tm, ff // tf),
        in_specs=[pl.BlockSpec((tm, d), lambda i, f: (i, 0)),
                  pl.BlockSpec((1, 6, d), lambda i, f: (i // tiles_per_seq, 0, 0)),
                  pl.BlockSpec((d, tf), lambda i, f: (0, f)),
                  pl.BlockSpec((d, tf), lambda i, f: (0, f)),
                  pl.BlockSpec((tf, d), lambda i, f: (f, 0))],
        out_specs=pl.BlockSpec((tm, d), lambda i, f: (i, 0)),
        out_shape=jax.ShapeDtypeStruct((n, d), F32),
        scratch_shapes=[pltpu.VMEM((tm, d), BF16), pltpu.VMEM((tm, d), F32)],
        compiler_params=_cparams("parallel", "arbitrary"),
        name="dense_ffn",
    )(x, mod, wg, wu, wd)


def _router_kernel(x_ref, mod_ref, wr_ref, br_ref, sel_ref, route_ref):
    h = _norm_mod(x_ref[...], mod_ref[0], 1)
    logits = jnp.dot(h, wr_ref[...], preferred_element_type=F32, precision=lax.Precision.HIGHEST) + br_ref[...]
    lane = lax.broadcasted_iota(jnp.int32, logits.shape, 1)
    neg = jnp.float32(-jnp.inf)
    logits = jnp.where(lane < N_EXPERTS, logits, neg)
    m1 = jnp.max(logits, axis=-1, keepdims=True)
    i1 = jnp.min(jnp.where(logits == m1, lane, LANES), axis=-1, keepdims=True)
    rest = jnp.where(lane == i1, neg, logits)
    m2 = jnp.max(rest, axis=-1, keepdims=True)
    i2 = jnp.min(jnp.where(rest == m2, lane, LANES), axis=-1, keepdims=True)
    e2 = jnp.exp(m2 - m1)
    w1 = 1.0 / (1.0 + e2)
    w2 = e2 / (1.0 + e2)
    sel_ref[...] = jnp.where((lane == i1) | (lane == i2), 1.0, 0.0).astype(BF16)
    route = jnp.where(lane == 0, i1.astype(F32), jnp.where(lane == 1, i2.astype(F32), 0.0))
    route_ref[...] = jnp.where(lane == 2, w1, jnp.where(lane == 3, w2, route))


def _router(x, mod, w_router, b_router, seq, tm):
    n, d = x.shape
    tiles_per_seq = seq // tm
    wr = jnp.pad(w_router, ((0, 0), (0, LANES - N_EXPERTS)))
    br = jnp.pad(b_router, (0, LANES - N_EXPERTS)).reshape(1, LANES)
    return pl.pallas_call(
        _router_kernel,
        grid=(n // tm,),
        in_specs=[pl.BlockSpec((tm, d), lambda i: (i, 0)),
                  pl.BlockSpec((1, 6, d), lambda i: (i // tiles_per_seq, 0, 0)),
                  pl.BlockSpec((d, LANES), lambda i: (0, 0)),
                  pl.BlockSpec((1, LANES), lambda i: (0, 0))],
        out_specs=[pl.BlockSpec((tm, LANES), lambda i: (i, 0)), pl.BlockSpec((tm, LANES), lambda i: (i, 0))],
        out_shape=[jax.ShapeDtypeStruct((n, LANES), BF16), jax.ShapeDtypeStruct((n, LANES), F32)],
        compiler_params=_cparams("parallel"),
        name="moe_router",
    )(x, mod, wr, br)


def _rank_kernel(sel_ref, route_ref, tri_ref, rank_ref, count_ref, carry_scr):
    @pl.when(pl.program_id(0) == 0)
    def _():
        carry_scr[...] = jnp.zeros_like(carry_scr)

    sel = sel_ref[...]
    incl = jnp.dot(tri_ref[...], sel, preferred_element_type=F32)
    excl = incl - sel.astype(F32) + carry_scr[...]
    route = route_ref[...]
    lane = lax.broadcasted_iota(jnp.int32, excl.shape, 1)
    lane_f = lane.astype(F32)
    r1 = jnp.sum(jnp.where(lane_f == route[:, 0:1], excl, 0.0), axis=-1, keepdims=True)
    r2 = jnp.sum(jnp.where(lane_f == route[:, 1:2], excl, 0.0), axis=-1, keepdims=True)
    rank_ref[...] = jnp.where(lane == 0, r1, jnp.where(lane == 1, r2, 0.0))
    carry_scr[...] += incl[incl.shape[0] - 1:, :]
    count_ref[...] = carry_scr[...]


def _dispatch_kernel(d1_ref, d2_ref, x_ref, mod_ref, zeros_ref, xs_ref, h_scr, sems):
    del zeros_ref
    tm = x_ref.shape[0]
    h_scr[...] = _norm_mod(x_ref[...], mod_ref[0], 1)

    def row_copy(r, dest, which):
        return pltpu.make_async_copy(h_scr.at[pl.ds(r, 1), :], xs_ref.at[pl.ds(dest, 1), :], sems.at[which])

    def issue(r, carry):
        row_copy(r, d1_ref[0, 0, r], 0).start()
        row_copy(r, d2_ref[0, 0, r], 1).start()
        return carry

    lax.fori_loop(0, tm, issue, 0, unroll=4)
    for which in range(2):
        pltpu.make_async_copy(h_scr, xs_ref.at[pl.ds(0, tm), :], sems.at[which]).wait()


def _expert_kernel(tile_e_ref, n_act_ref, x_ref, wg_ref, wu_ref, wd_ref, o_ref, h_scr, acc_scr):
    del tile_e_ref
    s = pl.program_id(0)
    f = pl.program_id(1)

    @pl.when(s < n_act_ref[0])
    def _():
        @pl.when(f == 0)
        def _():
            h_scr[...] = x_ref[...].astype(BF16)
            acc_scr[...] = jnp.zeros_like(acc_scr)

        h = h_scr[...]
        g = jnp.dot(h, wg_ref[0], preferred_element_type=F32)
        u = jnp.dot(h, wu_ref[0], preferred_element_type=F32)
        a = (_silu(g) * u).astype(BF16)
        acc_scr[...] += jnp.dot(a, wd_ref[0], preferred_element_type=F32)

        @pl.when(f == pl.num_programs(1) - 1)
        def _():
            o_ref[...] = acc_scr[...]

    @pl.when(s >= n_act_ref[0])
    def _():
        o_ref[...] = jnp.zeros_like(o_ref)


def _combine_kernel(d1_ref, d2_ref, route_ref, x_ref, mod_ref, ys_ref, o_ref, buf, sems):
    tm = x_ref.shape[0]

    def row_copy(r, src, which):
        return pltpu.make_async_copy(ys_ref.at[pl.ds(src, 1), :], buf.at[which, pl.ds(r, 1), :], sems.at[which])

    def issue(r, carry):
        row_copy(r, d1_ref[0, 0, r], 0).start()
        row_copy(r, d2_ref[0, 0, r], 1).start()
        return carry

    lax.fori_loop(0, tm, issue, 0, unroll=4)
    for which in range(2):
        pltpu.make_async_copy(ys_ref.at[pl.ds(0, tm), :], buf.at[which], sems.at[which]).wait()
    route = route_ref[...]
    y = route[:, 2:3] * buf[0] + route[:, 3:4] * buf[1]
    o_ref[...] = x_ref[...] + mod_ref[0][5:6] * y


def _moe(x, mod, w_router, b_router, w_gate, w_up, w_down, seq):
    n, d = x.shape
    n_e, _, ff = w_gate.shape
    tm = 512
    te = 1024
    tf = 896
    tiles_per_seq = seq // tm
    n_tok_tiles = n // tm
    n_slots = 2 * n + n_e * te
    n_tiles = n_slots // te
    f_tiles = ff // tf

    sel, route = _router(x, mod, w_router, b_router, seq, tm)
    tri = (jnp.arange(tm)[:, None] >= jnp.arange(tm)[None, :]).astype(BF16)
    rank, counts = pl.pallas_call(
        _rank_kernel,
        grid=(n_tok_tiles,),
        in_specs=[pl.BlockSpec((tm, LANES), lambda i: (i, 0)),
                  pl.BlockSpec((tm, LANES), lambda i: (i, 0)),
                  pl.BlockSpec((tm, tm), lambda i: (0, 0))],
        out_specs=[pl.BlockSpec((tm, LANES), lambda i: (i, 0)), pl.BlockSpec((1, LANES), lambda i: (0, 0))],
        out_shape=[jax.ShapeDtypeStruct((n, LANES), F32), jax.ShapeDtypeStruct((1, LANES), F32)],
        scratch_shapes=[pltpu.VMEM((1, LANES), F32)],
        compiler_params=_cparams("arbitrary"),
        name="moe_rank",
    )(sel, route, tri)

    counts = counts[0, :n_e].astype(jnp.int32)
    padded = ((counts + te - 1) // te) * te
    ends = jnp.cumsum(padded)
    offsets = ends - padded
    n_act = (ends[-1] // te).astype(jnp.int32)
    tile_start = jnp.arange(n_tiles, dtype=jnp.int32) * te
    tile_e = jnp.sum((tile_start[:, None] >= ends[None, :]).astype(jnp.int32), axis=1)
    tile_e = jnp.minimum(tile_e, tile_e[jnp.maximum(n_act - 1, 0)]).astype(jnp.int32)
    e1 = route[:, 0].astype(jnp.int32)
    e2 = route[:, 1].astype(jnp.int32)
    dest1 = (offsets[e1] + rank[:, 0].astype(jnp.int32)).reshape(n_tok_tiles, 1, tm)
    dest2 = (offsets[e2] + rank[:, 1].astype(jnp.int32)).reshape(n_tok_tiles, 1, tm)
    n_act = n_act.reshape(1)

    slot_spec = pl.BlockSpec((1, 1, tm), lambda i: (i, 0, 0), memory_space=pltpu.SMEM)
    xs = pl.pallas_call(
        _dispatch_kernel,
        grid=(n_tok_tiles,),
        in_specs=[slot_spec, slot_spec,
                  pl.BlockSpec((tm, d), lambda i: (i, 0)),
                  pl.BlockSpec((1, 6, d), lambda i: (i // tiles_per_seq, 0, 0)),
                  pl.BlockSpec(memory_space=pl.ANY)],
        out_specs=pl.BlockSpec(memory_space=pl.ANY),
        out_shape=jax.ShapeDtypeStruct((n_slots, d), F32),
        scratch_shapes=[pltpu.VMEM((tm, d), F32), pltpu.SemaphoreType.DMA((2,))],
        input_output_aliases={4: 0},
        compiler_params=_cparams("arbitrary"),
        name="moe_dispatch",
    )(dest1, dest2, x, mod, jnp.zeros((n_slots, d), F32))

    def live(s, n_act_ref):
        return jnp.minimum(s, jnp.maximum(n_act_ref[0] - 1, 0))

    def f_idx(s, f, n_act_ref):
        return jnp.where(s < n_act_ref[0], f, f_tiles - 1)

    ys = pl.pallas_call(
        _expert_kernel,
        grid_spec=pltpu.PrefetchScalarGridSpec(
            num_scalar_prefetch=2,
            grid=(n_tiles, f_tiles),
            in_specs=[pl.BlockSpec((te, d), lambda s, f, te_ref, na_ref: (live(s, na_ref), 0)),
                      pl.BlockSpec((1, d, tf), lambda s, f, te_ref, na_ref: (te_ref[s], 0, f_idx(s, f, na_ref))),
                      pl.BlockSpec((1, d, tf), lambda s, f, te_ref, na_ref: (te_ref[s], 0, f_idx(s, f, na_ref))),
                      pl.BlockSpec((1, tf, d), lambda s, f, te_ref, na_ref: (te_ref[s], f_idx(s, f, na_ref), 0))],
            out_specs=pl.BlockSpec((te, d), lambda s, f, te_ref, na_ref: (s, 0)),
            scratch_shapes=[pltpu.VMEM((te, d), BF16), pltpu.VMEM((te, d), F32)]),
        out_shape=jax.ShapeDtypeStruct((n_slots, d), F32),
        compiler_params=_cparams("arbitrary", "arbitrary"),
        name="moe_experts",
    )(tile_e, n_act, xs, w_gate.astype(BF16), w_up.astype(BF16), w_down.astype(BF16))

    return pl.pallas_call(
        _combine_kernel,
        grid=(n_tok_tiles,),
        in_specs=[slot_spec, slot_spec,
                  pl.BlockSpec((tm, LANES), lambda i: (i, 0)),
                  pl.BlockSpec((tm, d), lambda i: (i, 0)),
                  pl.BlockSpec((1, 6, d), lambda i: (i // tiles_per_seq, 0, 0)),
                  pl.BlockSpec(memory_space=pl.ANY)],
        out_specs=pl.BlockSpec((tm, d), lambda i: (i, 0)),
        out_shape=jax.ShapeDtypeStruct((n, d), F32),
        scratch_shapes=[pltpu.VMEM((2, tm, d), F32), pltpu.SemaphoreType.DMA((2,))],
        compiler_params=_cparams("arbitrary"),
        name="moe_combine",
    )(dest1, dest2, route, x, mod, ys)


def _softmax_pv(lhs, k_ref, v_ref, s_scr, m_scr, acc_scr, n_chunks, bias_fn=None):
    rows = lhs.shape[0]
    group = min(8, n_chunks)
    m_scr[...] = jnp.full(m_scr.shape, -jnp.inf, F32)

    def logits_body(t, carry):
        for u in range(group):
            j = group * t + u
            start = pl.multiple_of(j * KEY_CHUNK, KEY_CHUNK)
            kc = k_ref[pl.ds(start, KEY_CHUNK), :]
            s = lax.dot_general(lhs, kc, (((1,), (1,)), ((), ())), preferred_element_type=F32)
            if bias_fn is not None:
                s = bias_fn(s, j)
            s_scr[j] = s
            m_scr[...] = jnp.maximum(m_scr[...], jnp.maximum(s[:, :LANES], s[:, LANES:]))
        return carry

    lax.fori_loop(0, n_chunks // group, logits_body, 0)
    m_scr[...] = jnp.broadcast_to(jnp.max(m_scr[...], axis=-1, keepdims=True), m_scr.shape)
    acc_scr[...] = jnp.zeros_like(acc_scr)
    ones = jnp.ones((group * KEY_CHUNK, LANES), BF16)

    def pv_body(t, carry):
        mb = m_scr[...]
        parts = []
        for u in range(group):
            s = s_scr[group * t + u]
            parts.append(jnp.exp2(s[:, :LANES] - mb).astype(BF16))
            parts.append(jnp.exp2(s[:, LANES:] - mb).astype(BF16))
        p = jnp.concatenate(parts, axis=1)
        start = pl.multiple_of(t * (group * KEY_CHUNK), group * KEY_CHUNK)
        v_aug = jnp.concatenate([v_ref[pl.ds(start, group * KEY_CHUNK), :], ones], axis=1)
        acc_scr[...] += jnp.dot(p, v_aug, preferred_element_type=F32)
        return carry

    lax.fori_loop(0, n_chunks // group, pv_body, 0)
    acc = acc_scr[...]
    return acc[:, :LANES] / acc[:, LANES:]


def _att_qkv_epilogue(acc, j, extras, outs, n_qk):
    gain_ref, cos_ref, sin_ref, g1_ref, g2_ref = extras
    q_ref, k_ref, v_ref = outs
    qk = acc[:, :n_qk]
    ms = _split_dot(qk * qk, g1_ref[...])
    r = lax.rsqrt(ms + EPS)
    y = qk * _split_dot(r, g2_ref[...]) * gain_ref[...]
    cos = cos_ref[...]
    sin = sin_ref[...]
    lane = lax.broadcasted_iota(jnp.int32, cos.shape, 1)
    first_half = (lane & (ATT_HEAD_DIM // 2)) == 0
    nq = q_ref.shape[1]
    for c in range(n_qk // LANES):
        yc = y[:, c * LANES:(c + 1) * LANES]
        partner = jnp.where(first_half, pltpu.roll(yc, LANES - ATT_HEAD_DIM // 2, 1),
                            pltpu.roll(yc, ATT_HEAD_DIM // 2, 1))
        out = (yc * cos + partner * sin).astype(BF16)
        if c * LANES < nq:
            q_ref[:, c * LANES:(c + 1) * LANES] = out
        else:
            k_ref[:, c * LANES - nq:(c + 1) * LANES - nq] = out
    v_ref[...] = acc[:, n_qk:].astype(BF16)


def _att_kernel(q_ref, k_ref, v_ref, o_ref, s_scr, m_scr, acc_scr, *, n_chunks):
    tq = q_ref.shape[0]
    q = q_ref[...]
    lane = lax.broadcasted_iota(jnp.int32, (tq, LANES), 1)
    low = lane < ATT_HEAD_DIM
    zero = jnp.zeros((tq, LANES), BF16)
    parts = []
    for pair in range(2):
        qp = q[:, pair * LANES:(pair + 1) * LANES]
        parts.append(jnp.where(low, qp, zero))
        parts.append(jnp.where(low, zero, qp))
    lhs = jnp.concatenate(parts, axis=0)
    o = _softmax_pv(lhs, k_ref, v_ref, s_scr, m_scr, acc_scr, n_chunks)
    for pair in range(2):
        o_ref[:, pair * LANES:(pair + 1) * LANES] = jnp.where(
            low, o[(2 * pair) * tq:(2 * pair + 1) * tq], o[(2 * pair + 1) * tq:(2 * pair + 2) * tq]).astype(BF16)


def _attention_mixer(x, mod, w_qkv, q_gain, k_gain, w_o, batch, seq):
    n, d = x.shape
    hd = ATT_HEAD_DIM
    half = hd // 2
    n_freq = half // 2
    groups = ATT_HEADS // ATT_KV_HEADS
    perm = jnp.concatenate([jnp.arange(n_freq), half + jnp.arange(n_freq),
                            n_freq + jnp.arange(n_freq), half + n_freq + jnp.arange(n_freq)])
    q_cols = (jnp.arange(ATT_HEADS)[:, None] * hd + perm[None, :]).reshape(-1)
    k_base = ATT_HEADS * hd
    v_base = k_base + ATT_KV_HEADS * hd
    kv_rep = jnp.repeat(jnp.arange(ATT_KV_HEADS), 2)
    k_cols = (k_base + kv_rep[:, None] * hd + perm[None, :]).reshape(-1)
    v_cols = (v_base + kv_rep[:, None] * hd + jnp.arange(hd)[None, :]).reshape(-1)
    w = w_qkv[:, jnp.concatenate([q_cols, k_cols, v_cols])].astype(BF16)
    n_q = ATT_HEADS * hd
    n_kv = 2 * ATT_KV_HEADS * hd
    n_qk = n_q + n_kv
    gain_row = jnp.concatenate([jnp.tile(q_gain[perm] * (hd ** -0.5 * LOG2_E), ATT_HEADS),
                                jnp.tile(k_gain[perm], 2 * ATT_KV_HEADS)]).reshape(1, n_qk).astype(F32)
    pos = jnp.arange(seq, dtype=jnp.int32)
    row = (pos // GRID_W).astype(F32)
    col = (pos % GRID_W).astype(F32)
    inv = ROPE_THETA ** (-jnp.arange(n_freq, dtype=F32) / n_freq)
    ang = jnp.concatenate([row[:, None] * inv, col[:, None] * inv], axis=-1)
    cos_t = jnp.tile(jnp.cos(ang), (1, 2 * LANES // hd))
    sin_t = jnp.tile(jnp.concatenate([-jnp.sin(ang), jnp.sin(ang)], axis=-1), (1, LANES // hd))
    grp = jnp.arange(n_qk) // hd
    g1 = (grp[:, None] == jnp.arange(LANES)[None, :]).astype(F32) / hd
    g2 = (jnp.arange(LANES)[:, None] == grp[None, :]).astype(F32)
    tm = 512
    tiles_per_seq = seq // tm
    extras = [gain_row, cos_t, sin_t, g1.astype(BF16), g2.astype(BF16)]
    extra_specs = [pl.BlockSpec((1, n_qk), lambda i, j: (0, 0)),
                   pl.BlockSpec((tm, LANES), lambda i, j: (i % tiles_per_seq, 0)),
                   pl.BlockSpec((tm, LANES), lambda i, j: (i % tiles_per_seq, 0)),
                   pl.BlockSpec((n_qk, LANES), lambda i, j: (0, 0)),
                   pl.BlockSpec((LANES, n_qk), lambda i, j: (0, 0))]
    q, k, v = _project(
        x, mod, 0, w, extras, extra_specs, functools.partial(_att_qkv_epilogue, n_qk=n_qk),
        [jax.ShapeDtypeStruct((n, n_q), BF16), jax.ShapeDtypeStruct((n, n_kv), BF16),
         jax.ShapeDtypeStruct((n, n_kv), BF16)],
        [pl.BlockSpec((tm, n_q), lambda i, j: (i, 0)), pl.BlockSpec((tm, n_kv), lambda i, j: (i, 0)),
         pl.BlockSpec((tm, n_kv), lambda i, j: (i, 0))],
        tm, w.shape[1], seq, "att_qkv")

    tq = 256
    n_chunks = seq // KEY_CHUNK
    q_tiles = seq // tq
    o = pl.pallas_call(
        functools.partial(_att_kernel, n_chunks=n_chunks),
        grid=(batch, ATT_KV_HEADS, q_tiles),
        in_specs=[pl.BlockSpec((tq, groups * hd), lambda b, g, i: (b * q_tiles + i, g)),
                  pl.BlockSpec((seq, LANES), lambda b, g, i: (b, g)),
                  pl.BlockSpec((seq, LANES), lambda b, g, i: (b, g))],
        out_specs=pl.BlockSpec((tq, groups * hd), lambda b, g, i: (b * q_tiles + i, g)),
        out_shape=jax.ShapeDtypeStruct((n, n_q), BF16),
        scratch_shapes=[pltpu.VMEM((n_chunks, groups * tq, KEY_CHUNK), F32),
                        pltpu.VMEM((groups * tq, LANES), F32), pltpu.VMEM((groups * tq, 2 * LANES), F32)],
        compiler_params=_cparams("parallel", "parallel", "arbitrary"),
        name="gqa_attention",
    )(q, k, v)
    return _out_project(o, 0, n_q, w_o.astype(BF16), x, mod, 0, seq, 1024, "att_out")


def _ret_proj_epilogue(acc, j, extras, outs):
    cos_ref, sin_ref = extras
    o_ref = outs[0]

    @pl.when(j < 2)
    def _():
        cos = cos_ref[...]
        sin = sin_ref[...]
        half = RET_QK_DIM // 2
        for hh in range(RET_HEADS):
            x1 = acc[:, hh * RET_QK_DIM:hh * RET_QK_DIM + half]
            x2 = acc[:, hh * RET_QK_DIM + half:(hh + 1) * RET_QK_DIM]
            o_ref[:, hh * RET_QK_DIM:hh * RET_QK_DIM + half] = (x1 * cos - x2 * sin).astype(BF16)
            o_ref[:, hh * RET_QK_DIM + half:(hh + 1) * RET_QK_DIM] = (x1 * sin + x2 * cos).astype(BF16)

    @pl.when(j >= 2)
    def _():
        o_ref[...] = acc.astype(BF16)


def _ret_kernel(logit_ref, q_ref, k_ref, v_ref, gain_ref, o_ref, acc_scr, state_f_scr, state_b_scr, *, n_chunks):
    h = pl.program_id(1)
    c_len = RET_CHUNK

    def log_gamma(direction, shape):
        z = jnp.full(shape, logit_ref[direction, h], F32)
        return jnp.minimum(z, 0.0) - jnp.log1p(jnp.exp(-jnp.abs(z)))

    ri = lax.broadcasted_iota(jnp.int32, (c_len, c_len), 0)
    ci = lax.broadcasted_iota(jnp.int32, (c_len, c_len), 1)
    dist = (ri - ci).astype(F32)
    lg_f = log_gamma(0, (c_len, c_len))
    lg_b = log_gamma(1, (c_len, c_len))
    decay = jnp.where(dist >= 0, jnp.exp(lg_f * jnp.maximum(dist, 0.0)), jnp.exp(lg_b * jnp.maximum(-dist, 0.0)))
    idx = lax.broadcasted_iota(jnp.int32, (c_len, 1), 0).astype(F32)
    lgf = log_gamma(0, (c_len, 1))
    lgb = log_gamma(1, (c_len, 1))
    xi_f = jnp.exp(lgf * (idx + 1.0))
    zeta_f = jnp.exp(lgf * (c_len - 1.0 - idx))
    cd_f = jnp.exp(lgf[0:1] * c_len)
    xi_b = jnp.exp(lgb * (c_len - idx))
    zeta_b = jnp.exp(lgb * idx)
    cd_b = jnp.exp(lgb[0:1] * c_len)

    def chunk(cidx):
        start = pl.multiple_of(cidx * c_len, c_len)
        return (q_ref[pl.ds(start, c_len), :], k_ref[pl.ds(start, c_len), :], v_ref[pl.ds(start, c_len), :], start)

    def update_state(state_scr, kc, vc, zeta, cd):
        zv = (vc.astype(F32) * zeta).astype(BF16)
        kv = lax.dot_general(kc, zv, (((0,), (0,)), ((), ())), preferred_element_type=F32)
        state_scr[...] = state_scr[...] * cd + kv

    def left_to_right(cidx):
        qc, kc, vc, start = chunk(cidx)
        scores = lax.dot_general(qc, kc, (((1,), (1,)), ((), ())), preferred_element_type=F32) * decay
        o = jnp.dot(scores.astype(BF16), vc, preferred_element_type=F32)
        o = o + xi_f * jnp.dot(qc, state_f_scr[...].astype(BF16), preferred_element_type=F32)
        update_state(state_f_scr, kc, vc, zeta_f, cd_f)
        return o, start

    def right_to_left(cidx):
        qc, kc, vc, start = chunk(cidx)
        o = xi_b * jnp.dot(qc, state_b_scr[...].astype(BF16), preferred_element_type=F32)
        update_state(state_b_scr, kc, vc, zeta_b, cd_b)
        return o, start

    def finish(y, start):
        y = y * lax.rsqrt(jnp.mean(y * y, axis=-1, keepdims=True) + EPS) * gain_ref[...]
        o_ref[pl.ds(start, c_len), :] = y.astype(BF16)

    state_f_scr[...] = jnp.zeros_like(state_f_scr)
    state_b_scr[...] = jnp.zeros_like(state_b_scr)
    half = n_chunks // 2

    def first_visits(t, carry):
        o_f, start_f = left_to_right(t)
        o_b, start_b = right_to_left(n_chunks - 1 - t)
        acc_scr[pl.ds(start_f, c_len), :] = o_f
        acc_scr[pl.ds(start_b, c_len), :] = o_b
        return carry

    def second_visits(t, carry):
        o_f, start_f = left_to_right(t)
        o_b, start_b = right_to_left(n_chunks - 1 - t)
        finish(acc_scr[pl.ds(start_f, c_len), :] + o_f, start_f)
        finish(acc_scr[pl.ds(start_b, c_len), :] + o_b, start_b)
        return carry

    lax.fori_loop(0, half, first_visits, 0)
    lax.fori_loop(half, n_chunks, second_visits, 0)


def _retention_mixer(x, mod, w_in, decay_logit, gn_gain, w_o, batch, seq):
    n, d = x.shape
    qk_cols = RET_HEADS * RET_QK_DIM
    v_cols = RET_HEADS * RET_V_DIM
    col_scale = jnp.concatenate([jnp.ones((qk_cols,), F32), jnp.full((qk_cols,), RET_QK_DIM ** -0.5, F32),
                                 jnp.ones((2 * v_cols,), F32)])
    w = (w_in * col_scale[None, :]).astype(BF16)
    pos = jnp.arange(seq, dtype=F32)
    inv = 1.0 / (RET_THETA ** jnp.linspace(0.0, 1.0, RET_QK_DIM // 2, dtype=F32))
    ang = pos[:, None] * inv
    tm, tn = 1024, qk_cols
    tiles_per_seq = seq // tm
    half = RET_QK_DIM // 2
    extras = [jnp.cos(ang), jnp.sin(ang)]
    extra_specs = [pl.BlockSpec((tm, half), lambda i, j: (i % tiles_per_seq, 0)),
                   pl.BlockSpec((tm, half), lambda i, j: (i % tiles_per_seq, 0))]
    proj = _project(x, mod, 0, w, extras, extra_specs, _ret_proj_epilogue,
                    jax.ShapeDtypeStruct((n, w.shape[1]), BF16), pl.BlockSpec((tm, tn), lambda i, j: (i, j)),
                    tm, tn, seq, "ret_proj")

    n_chunks = seq // RET_CHUNK
    k_off = qk_cols // RET_QK_DIM
    v_off = 2 * qk_cols // RET_V_DIM
    y = pl.pallas_call(
        functools.partial(_ret_kernel, n_chunks=n_chunks),
        grid=(batch, RET_HEADS),
        in_specs=[pl.BlockSpec(memory_space=pltpu.SMEM),
                  pl.BlockSpec((seq, RET_QK_DIM), lambda b, h: (b, h)),
                  pl.BlockSpec((seq, RET_QK_DIM), lambda b, h: (b, k_off + h)),
                  pl.BlockSpec((seq, RET_V_DIM), lambda b, h: (b, v_off + h)),
                  pl.BlockSpec((1, RET_V_DIM), lambda b, h: (0, 0))],
        out_specs=pl.BlockSpec((seq, RET_V_DIM), lambda b, h: (b, h)),
        out_shape=jax.ShapeDtypeStruct((n, v_cols), BF16),
        scratch_shapes=[pltpu.VMEM((seq, RET_V_DIM), F32), pltpu.VMEM((RET_QK_DIM, RET_V_DIM), F32),
                        pltpu.VMEM((RET_QK_DIM, RET_V_DIM), F32)],
        compiler_params=_cparams("parallel", "arbitrary"),
        name="retention",
    )(decay_logit.astype(F32), proj, proj, proj, gn_gain.reshape(1, RET_V_DIM).astype(F32))
    gate_col = (2 * qk_cols + v_cols) // v_cols
    return _out_project(y, 0, v_cols, w_o.astype(BF16), x, mod, 0, seq, 512, "ret_out",
                        gate_src=proj, gate_col=gate_col)


def _conv_pw1_epilogue(acc, j, extras, outs):
    b_ref = extras[0]
    u = acc + b_ref[...]
    dm = u.shape[1] // 2
    outs[0][...] = (u[:, :dm] * jax.nn.sigmoid(u[:, dm:])).astype(BF16)


def _conv_kernel(prev_ref, cur_ref, next_ref, wdw_ref, bdw_ref, lng_ref, lnb_ref, w2_ref, b2_ref, x_ref, mod_ref,
                 o_ref, ext_scr, shift_scr, c_scr, *, tiles_per_seq, row_block):
    tm = cur_ref.shape[0]
    t = pl.program_id(0) % tiles_per_seq
    keep_prev = jnp.where(t == 0, 0.0, 1.0)
    keep_next = jnp.where(t == tiles_per_seq - 1, 0.0, 1.0)
    ext_scr[0:CONV_HALO, :] = prev_ref[...].astype(F32) * keep_prev
    ext_scr[CONV_HALO:CONV_HALO + tm, :] = cur_ref[...].astype(F32)
    ext_scr[CONV_HALO + tm:, :] = next_ref[...].astype(F32) * keep_next
    base = CONV_HALO - CONV_PAD
    span = tm + 2 * CONV_HALO - SUBLANES
    for r in range(1, SUBLANES):
        shift_scr[r - 1, 0:span, :] = ext_scr[r:r + span, :]
    for rb in range(tm // row_block):
        r0 = rb * row_block
        acc = jnp.zeros((row_block, cur_ref.shape[1]), F32) + bdw_ref[...]
        for j in range(CONV_WIDTH):
            phase = (base + j) % SUBLANES
            start = r0 + base + j - phase
            if phase == 0:
                tap = ext_scr[start:start + row_block, :]
            else:
                tap = shift_scr[phase - 1, start:start + row_block, :]
            acc = acc + wdw_ref[j:j + 1, :] * tap
        c_scr[r0:r0 + row_block, :] = acc
    u = c_scr[...]
    mu = jnp.mean(u, axis=-1, keepdims=True)
    var = jnp.mean(jnp.square(u - mu), axis=-1, keepdims=True)
    u = (u - mu) * lax.rsqrt(var + EPS) * lng_ref[...] + lnb_ref[...]
    u = _silu(u).astype(BF16)
    z = jnp.dot(u, w2_ref[...], preferred_element_type=F32) + b2_ref[...]
    o_ref[...] = x_ref[...] + mod_ref[0][2:3] * z


def _conv_mixer(x, mod, w_pw1, b_pw1, w_dw, b_dw, ln_gain, ln_bias, w_pw2, b_pw2, seq):
    n, d = x.shape
    tm = 512
    u = _project(x, mod, 0, w_pw1.astype(BF16), [b_pw1.reshape(1, 2 * d).astype(F32)],
                 [pl.BlockSpec((1, 2 * d), lambda i, j: (0, 0))], _conv_pw1_epilogue,
                 jax.ShapeDtypeStruct((n, d), BF16), pl.BlockSpec((tm, d), lambda i, j: (i, 0)),
                 tm, 2 * d, seq, "conv_pw1")
    tm = 256
    tiles_per_seq = seq // tm
    halo_per_tile = tm // CONV_HALO
    n_halo = n // CONV_HALO
    wdw = jnp.pad(w_dw.reshape(CONV_WIDTH, d).astype(F32), ((0, 1), (0, 0)))
    vec = lambda a: a.reshape(1, d).astype(F32)
    row_spec = pl.BlockSpec((1, d), lambda i: (0, 0))
    return pl.pallas_call(
        functools.partial(_conv_kernel, tiles_per_seq=tiles_per_seq, row_block=32),
        grid=(n // tm,),
        in_specs=[pl.BlockSpec((CONV_HALO, d), lambda i: (jnp.maximum(i * halo_per_tile - 1, 0), 0)),
                  pl.BlockSpec((tm, d), lambda i: (i, 0)),
                  pl.BlockSpec((CONV_HALO, d), lambda i: (jnp.minimum((i + 1) * halo_per_tile, n_halo - 1), 0)),
                  pl.BlockSpec((CONV_WIDTH + 1, d), lambda i: (0, 0)),
                  row_spec, row_spec, row_spec,
                  pl.BlockSpec((d, d), lambda i: (0, 0)),
                  row_spec,
                  pl.BlockSpec((tm, d), lambda i: (i, 0)),
                  pl.BlockSpec((1, 6, d), lambda i: (i // tiles_per_seq, 0, 0))],
        out_specs=pl.BlockSpec((tm, d), lambda i: (i, 0)),
        out_shape=jax.ShapeDtypeStruct((n, d), F32),
        scratch_shapes=[pltpu.VMEM((tm + 2 * CONV_HALO, d), F32),
                        pltpu.VMEM((SUBLANES - 1, tm + 2 * CONV_HALO, d), F32), pltpu.VMEM((tm, d), F32)],
        compiler_params=_cparams("parallel"),
        name="conv_module",
    )(u, u, u, wdw, vec(b_dw), vec(ln_gain), vec(ln_bias), w_pw2.astype(BF16), vec(b_pw2), x, mod)


def _t5_bucket(rel):
    half = T5_BUCKETS // 2
    exact = half // 2
    n = jnp.abs(rel)
    log_ratio = jnp.log(jnp.maximum(n, 1).astype(F32) / exact) / math.log(T5_MAX_DIST / exact)
    large = jnp.minimum(exact + (log_ratio * (half - exact)).astype(jnp.int32), half - 1)
    return jnp.where(rel > 0, half, 0) + jnp.where(n < exact, n, large)


def _bias_tile_kernel(table_ref, idx_ref, o_ref):
    h = pl.program_id(0)
    idx = idx_ref[0]
    acc = jnp.zeros(idx.shape, F32)
    for b in range(T5_BUCKETS):
        acc = jnp.where(idx == b, table_ref[b, h], acc)
    o_ref[0, 0] = acc * LOG2_E


def _diff_kernel(q_ref, k_ref, v_ref, bias_ref, lam_ref, gain_ref, o_ref, s_scr, m_scr, acc_scr, *, n_chunks, out_scale,
                 lambda_init):
    tq = q_ref.shape[0]
    qi = pl.program_id(2)
    q = q_ref[...]
    lane = lax.broadcasted_iota(jnp.int32, (tq, LANES), 1)
    low = lane < DIFF_HEAD_DIM
    zero = jnp.zeros((tq, LANES), BF16)
    lhs = jnp.concatenate([jnp.where(low, q, zero), jnp.where(low, zero, q)], axis=0)

    q_sub = tq // KEY_CHUNK

    def add_bias(s, j):
        tiles = [bias_ref[0, jnp.clip(j - (q_sub * qi + a), -2, 2) + 2] for a in range(q_sub)]
        return s + jnp.concatenate(tiles + tiles, axis=0)

    o = _softmax_pv(lhs, k_ref, v_ref, s_scr, m_scr, acc_scr, n_chunks, add_bias)
    lam_v = lam_ref[...]
    lam = (jnp.exp(jnp.sum(lam_v[0:1] * lam_v[1:2], axis=-1, keepdims=True))
           - jnp.exp(jnp.sum(lam_v[2:3] * lam_v[3:4], axis=-1, keepdims=True)) + lambda_init)
    y = o[:tq] - lam * o[tq:]
    y = y * lax.rsqrt(jnp.mean(y * y, axis=-1, keepdims=True) + EPS) * gain_ref[...] * out_scale
    o_ref[...] = y.astype(BF16)


def _diff_mixer(x, mod, w_qkv, lam_q1, lam_k1, lam_q2, lam_k2, subln_gain, w_o, rel_bias, lambda_init,
                batch, seq):
    n, d = x.shape
    col_scale = jnp.concatenate([jnp.full((d,), DIFF_HEAD_DIM ** -0.5 * LOG2_E, F32), jnp.ones((2 * d,), F32)])
    w = (w_qkv * col_scale[None, :]).astype(BF16)
    tm = 1024
    qkv = _project(x, mod, 0, w, [], [], _plain_epilogue,
                   jax.ShapeDtypeStruct((n, 3 * d), BF16), pl.BlockSpec((tm, d), lambda i, j: (i, j)),
                   tm, d, seq, "diff_qkv")

    tb = KEY_CHUNK
    ar = jnp.arange(KEY_CHUNK, dtype=jnp.int32)
    offs = (jnp.arange(5, dtype=jnp.int32) - 2) * KEY_CHUNK
    rel = (ar[None, None, :] + offs[:, None, None]) - ar[None, :, None]
    bucket = _t5_bucket(rel).astype(jnp.int32)
    bias = pl.pallas_call(
        _bias_tile_kernel,
        grid=(DIFF_HEADS, 5),
        in_specs=[pl.BlockSpec(memory_space=pltpu.SMEM),
                  pl.BlockSpec((1, tb, KEY_CHUNK), lambda h, t: (t, 0, 0))],
        out_specs=pl.BlockSpec((1, 1, tb, KEY_CHUNK), lambda h, t: (h, t, 0, 0)),
        out_shape=jax.ShapeDtypeStruct((DIFF_HEADS, 5, tb, KEY_CHUNK), F32),
        compiler_params=_cparams("arbitrary", "arbitrary"),
        name="rel_bias_tiles",
    )(rel_bias.astype(F32), bucket)

    lam = jnp.stack([lam_q1, lam_k1, lam_q2, lam_k2]).astype(F32)
    n_chunks = seq // KEY_CHUNK
    tq = 2 * KEY_CHUNK
    q_tiles = seq // tq
    o = pl.pallas_call(
        functools.partial(_diff_kernel, n_chunks=n_chunks, out_scale=1.0 - lambda_init, lambda_init=lambda_init),
        grid=(batch, DIFF_HEADS, q_tiles),
        in_specs=[pl.BlockSpec((tq, LANES), lambda b, h, i: (b * q_tiles + i, h)),
                  pl.BlockSpec((seq, LANES), lambda b, h, i: (b, DIFF_HEADS + h)),
                  pl.BlockSpec((seq, LANES), lambda b, h, i: (b, 2 * DIFF_HEADS + h)),
                  pl.BlockSpec((1, 5, tb, KEY_CHUNK), lambda b, h, i: (h, 0, 0, 0)),
                  pl.BlockSpec((4, DIFF_HEAD_DIM), lambda b, h, i: (0, 0)),
                  pl.BlockSpec((1, LANES), lambda b, h, i: (0, 0))],
        out_specs=pl.BlockSpec((tq, LANES), lambda b, h, i: (b * q_tiles + i, h)),
        out_shape=jax.ShapeDtypeStruct((n, d), BF16),
        scratch_shapes=[pltpu.VMEM((n_chunks, 2 * tq, KEY_CHUNK), F32),
                        pltpu.VMEM((2 * tq, LANES), F32), pltpu.VMEM((2 * tq, 2 * LANES), F32)],
        compiler_params=_cparams("parallel", "parallel", "arbitrary"),
        name="diff_attention",
    )(qkv, qkv, qkv, bias, lam, subln_gain.reshape(1, LANES).astype(F32))
    return _out_project(o, 0, d, w_o.astype(BF16), x, mod, 0, seq, 1024, "diff_out")


def _final_norm_kernel(x_ref, g_ref, o_ref):
    x = x_ref[...]
    o_ref[...] = x * lax.rsqrt(jnp.mean(x * x, axis=-1, keepdims=True) + EPS) * g_ref[...]


def _final_norm(x, gain, tm):
    n, d = x.shape
    return pl.pallas_call(
        _final_norm_kernel,
        grid=(n // tm,),
        in_specs=[pl.BlockSpec((tm, d), lambda i: (i, 0)), pl.BlockSpec((1, d), lambda i: (0, 0))],
        out_specs=pl.BlockSpec((tm, d), lambda i: (i, 0)),
        out_shape=jax.ShapeDtypeStruct((n, d), F32),
        compiler_params=_cparams("parallel"),
        name="final_norm",
    )(x, gain.reshape(1, d).astype(F32))


def kernel(x, c, mod_w, mod_b, att_w_qkv, att_q_gain, att_k_gain, att_w_o, ret_w_in, ret_decay_logit, ret_gn_gain, ret_w_o, cnv_w_pw1, cnv_b_pw1, cnv_w_dw, cnv_b_dw, cnv_ln_gain, cnv_ln_bias, cnv_w_pw2, cnv_b_pw2, dif_w_qkv, dif_lam_q1, dif_lam_k1, dif_lam_q2, dif_lam_k2, dif_subln_gain, dif_w_o, rel_bias, ffn_w_gate, ffn_w_up, ffn_w_down, moe_w_router, moe_b_router, moe_w_gate, moe_w_up, moe_w_down, final_gain):
    batch, seq, d = x.shape
    depth = mod_w.shape[0]
    xf = x.reshape(batch * seq, d)
    mods = _modulation(c, mod_w, mod_b)
    for i in range(depth):
        mod = mods[i]
        kind, j = i % 4, i // 4
        if kind == 0:
            xf = _attention_mixer(xf, mod, att_w_qkv[j], att_q_gain[j], att_k_gain[j], att_w_o[j], batch, seq)
        elif kind == 1:
            xf = _retention_mixer(xf, mod, ret_w_in[j], ret_decay_logit[j], ret_gn_gain[j], ret_w_o[j], batch, seq)
        elif kind == 2:
            xf = _conv_mixer(xf, mod, cnv_w_pw1[j], cnv_b_pw1[j], cnv_w_dw[j], cnv_b_dw[j], cnv_ln_gain[j],
                             cnv_ln_bias[j], cnv_w_pw2[j], cnv_b_pw2[j], seq)
        else:
            lambda_init = 0.8 - 0.6 * math.exp(-0.3 * i)
            xf = _diff_mixer(xf, mod, dif_w_qkv[j], dif_lam_q1[j], dif_lam_k1[j], dif_lam_q2[j], dif_lam_k2[j],
                             dif_subln_gain[j], dif_w_o[j], rel_bias, lambda_init, batch, seq)
        if i % 2 == 0:
            m = i // 2
            xf = _ffn(xf, mod, ffn_w_gate[m].astype(BF16), ffn_w_up[m].astype(BF16), ffn_w_down[m].astype(BF16),
                      seq, 512, 1408)
        else:
            m = i // 2
            xf = _moe(xf, mod, moe_w_router[m], moe_b_router[m], moe_w_gate[m], moe_w_up[m], moe_w_down[m], seq)
    return _final_norm(xf, final_gain, 512).reshape(batch, seq, d)
```

```python
import functools
import math

import jax
import jax.numpy as jnp
from jax import lax
from jax.experimental import pallas as pl
from jax.experimental.pallas import tpu as pltpu

F32 = jnp.float32
BF16 = jnp.bfloat16

EPS = 1e-6
GRID_W = 64

ATT_HEADS = 16
ATT_KV_HEADS = 4
ATT_HEAD_DIM = 64
ROPE_THETA = 10000.0

RET_HEADS = 4
RET_QK_DIM = 256
RET_V_DIM = 512
RET_THETA = 10000.0
RET_CHUNK = 256

CONV_WIDTH = 31
CONV_PAD = CONV_WIDTH // 2
CONV_HALO = 16

DIFF_HEADS = 8
DIFF_HEAD_DIM = 64
T5_BUCKETS = 32
T5_MAX_DIST = 128

N_EXPERTS = 8
LANES = 128
SUBLANES = 8
KEY_CHUNK = 256
LOG2_E = 1.4426950408889634
VMEM_LIMIT_BYTES = 56 * 1024 * 1024


def _cparams(*sem):
    return pltpu.CompilerParams(dimension_semantics=sem, vmem_limit_bytes=VMEM_LIMIT_BYTES)


def _silu(v):
    return v * jax.nn.sigmoid(v)


def _norm_mod(x, mod, which):
    shift = mod[3 * which:3 * which + 1]
    scale = mod[3 * which + 1:3 * which + 2]
    y = x * lax.rsqrt(jnp.mean(x * x, axis=-1, keepdims=True) + EPS)
    return y * (1.0 + scale) + shift


def _split_dot(a, b):
    hi = a.astype(BF16)
    lo = (a - hi.astype(F32)).astype(BF16)
    return (jnp.dot(hi, b, preferred_element_type=F32) + jnp.dot(lo, b, preferred_element_type=F32))


def _mod_kernel(c_ref, w_ref, b_ref, o_ref):
    cond = _silu(c_ref[...])
    o_ref[0] = jnp.dot(cond, w_ref[0], preferred_element_type=F32, precision=lax.Precision.HIGHEST) + b_ref[0]


def _modulation(c, mod_w, mod_b):
    depth, d, d6 = mod_w.shape
    b = c.shape[0]
    tn = d
    out = pl.pallas_call(
        _mod_kernel,
        grid=(depth, d6 // tn),
        in_specs=[pl.BlockSpec((b, d), lambda i, j: (0, 0)),
                  pl.BlockSpec((1, d, tn), lambda i, j: (i, 0, j)),
                  pl.BlockSpec((1, 1, tn), lambda i, j: (i, 0, j))],
        out_specs=pl.BlockSpec((1, b, tn), lambda i, j: (i, 0, j)),
        out_shape=jax.ShapeDtypeStruct((depth, b, d6), F32),
        compiler_params=_cparams("arbitrary", "arbitrary"),
        name="modulation",
    )(c, mod_w, mod_b.reshape(depth, 1, d6))
    return out.reshape(depth, b, 6, d)


def _proj_kernel(x_ref, mod_ref, w_ref, *rest, which, n_extra, epilogue, tn):
    extras = rest[:n_extra]
    outs = rest[n_extra:]
    h = _norm_mod(x_ref[...], mod_ref[0], which).astype(BF16)
    for j in range(w_ref.shape[1] // tn):
        acc = jnp.dot(h, w_ref[:, j * tn:(j + 1) * tn], preferred_element_type=F32)
        epilogue(acc, j, extras, outs)


def _project(x, mod, which, w, extras, extra_specs, epilogue, out_shapes, out_specs, tm, tn, seq, name):
    n, d = x.shape
    c = w.shape[1]
    tiles_per_seq = seq // tm
    kern = functools.partial(_proj_kernel, which=which, n_extra=len(extras), epilogue=epilogue, tn=tn)
    return pl.pallas_call(
        kern,
        grid=(n // tm, 1),
        in_specs=[pl.BlockSpec((tm, d), lambda i, j: (i, 0)),
                  pl.BlockSpec((1, 6, d), lambda i, j: (i // tiles_per_seq, 0, 0)),
                  pl.BlockSpec((d, c), lambda i, j: (0, 0), pipeline_mode=pl.Buffered(1))] + extra_specs,
        out_specs=out_specs,
        out_shape=out_shapes,
        compiler_params=_cparams("parallel", "arbitrary"),
        name=name,
    )(x, mod, w, *extras)


def _plain_epilogue(acc, j, extras, outs):
    tn = acc.shape[1]
    outs[0][:, j * tn:(j + 1) * tn] = acc.astype(outs[0].dtype)


def _oproj_kernel(*refs, which, gated, has_bias):
    refs = list(refs)
    y_ref = refs.pop(0)
    g_ref = refs.pop(0) if gated else None
    w_ref = refs.pop(0)
    b_ref = refs.pop(0) if has_bias else None
    x_ref, mod_ref, o_ref = refs
    y = y_ref[...]
    if gated:
        y = (y.astype(F32) * _silu(g_ref[...].astype(F32))).astype(BF16)
    z = jnp.dot(y, w_ref[...], preferred_element_type=F32)
    if has_bias:
        z = z + b_ref[...]
    gate = mod_ref[0][3 * which + 2:3 * which + 3]
    o_ref[...] = x_ref[...] + gate * z


def _out_project(y, y_col, k, w, x, mod, which, seq, tm, name, gate_src=None, gate_col=0, bias=None):
    n, d = x.shape
    tiles_per_seq = seq // tm
    args = [y]
    specs = [pl.BlockSpec((tm, k), lambda i: (i, y_col))]
    if gate_src is not None:
        args.append(gate_src)
        specs.append(pl.BlockSpec((tm, k), lambda i: (i, gate_col)))
    args.append(w)
    specs.append(pl.BlockSpec((k, d), lambda i: (0, 0)))
    if bias is not None:
        args.append(bias.reshape(1, d))
        specs.append(pl.BlockSpec((1, d), lambda i: (0, 0)))
    args += [x, mod]
    specs += [pl.BlockSpec((tm, d), lambda i: (i, 0)),
              pl.BlockSpec((1, 6, d), lambda i: (i // tiles_per_seq, 0, 0))]
    kern = functools.partial(_oproj_kernel, which=which, gated=gate_src is not None, has_bias=bias is not None)
    return pl.pallas_call(
        kern,
        grid=(n // tm,),
        in_specs=specs,
        out_specs=pl.BlockSpec((tm, d), lambda i: (i, 0)),
        out_shape=jax.ShapeDtypeStruct((n, d), F32),
        compiler_params=_cparams("parallel"),
        name=name,
    )(*args)


def _ffn_kernel(x_ref, mod_ref, wg_ref, wu_ref, wd_ref, o_ref):
    x = x_ref[...]
    mod = mod_ref[0]
    h = _norm_mod(x, mod, 1).astype(BF16)
    g = jnp.dot(h, wg_ref[...], preferred_element_type=F32)
    u = jnp.dot(h, wu_ref[...], preferred_element_type=F32)
    a = (_silu(g) * u).astype(BF16)
    o_ref[...] = x + mod[5:6] * jnp.dot(a, wd_ref[...], preferred_element_type=F32)


def _ffn(x, mod, wg, wu, wd, seq, tm):
    n, d = x.shape
    ff = wg.shape[1]
    tiles_per_seq = seq // tm
    resident = pl.Buffered(1)
    return pl.pallas_call(
        _ffn_kernel,
        grid=(n // tm,),
        in_specs=[pl.BlockSpec((tm, d), lambda i: (i, 0)),
                  pl.BlockSpec((1, 6, d), lambda i: (i // tiles_per_seq, 0, 0)),
                  pl.BlockSpec((d, ff), lambda i: (0, 0), pipeline_mode=resident),
                  pl.BlockSpec((d, ff), lambda i: (0, 0), pipeline_mode=resident),
                  pl.BlockSpec((ff, d), lambda i: (0, 0), pipeline_mode=resident)],
        out_specs=pl.BlockSpec((tm, d), lambda i: (i, 0)),
        out_shape=jax.ShapeDtypeStruct((n, d), F32),
        compiler_params=_cparams("parallel"),
        name="dense_ffn",
    )(x, mod, wg, wu, wd)


def _router_kernel(x_ref, mod_ref, wr_ref, br_ref, sel_ref, route_ref):
    h = _norm_mod(x_ref[...], mod_ref[0], 1)
    logits = jnp.dot(h, wr_ref[...], preferred_element_type=F32, precision=lax.Precision.HIGHEST) + br_ref[...]
    lane = lax.broadcasted_iota(jnp.int32, logits.shape, 1)
    neg = jnp.float32(-jnp.inf)
    logits = jnp.where(lane < N_EXPERTS, logits, neg)
    m1 = jnp.max(logits, axis=-1, keepdims=True)
    i1 = jnp.min(jnp.where(logits == m1, lane, LANES), axis=-1, keepdims=True)
    rest = jnp.where(lane == i1, neg, logits)
    m2 = jnp.max(rest, axis=-1, keepdims=True)
    i2 = jnp.min(jnp.where(rest == m2, lane, LANES), axis=-1, keepdims=True)
    e2 = jnp.exp(m2 - m1)
    w1 = 1.0 / (1.0 + e2)
    w2 = e2 / (1.0 + e2)
    sel_ref[...] = jnp.where((lane == i1) | (lane == i2), 1.0, 0.0).astype(BF16)
    route = jnp.where(lane == 0, i1.astype(F32), jnp.where(lane == 1, i2.astype(F32), 0.0))
    route_ref[...] = jnp.where(lane == 2, w1, jnp.where(lane == 3, w2, route))


def _router(x, mod, w_router, b_router, seq, tm):
    n, d = x.shape
    tiles_per_seq = seq // tm
    wr = jnp.pad(w_router, ((0, 0), (0, LANES - N_EXPERTS)))
    br = jnp.pad(b_router, (0, LANES - N_EXPERTS)).reshape(1, LANES)
    return pl.pallas_call(
        _router_kernel,
        grid=(n // tm,),
        in_specs=[pl.BlockSpec((tm, d), lambda i: (i, 0)),
                  pl.BlockSpec((1, 6, d), lambda i: (i // tiles_per_seq, 0, 0)),
                  pl.BlockSpec((d, LANES), lambda i: (0, 0)),
                  pl.BlockSpec((1, LANES), lambda i: (0, 0))],
        out_specs=[pl.BlockSpec((tm, LANES), lambda i: (i, 0)), pl.BlockSpec((tm, LANES), lambda i: (i, 0))],
        out_shape=[jax.ShapeDtypeStruct((n, LANES), BF16), jax.ShapeDtypeStruct((n, LANES), F32)],
        compiler_params=_cparams("parallel"),
        name="moe_router",
    )(x, mod, wr, br)


def _rank_kernel(sel_ref, route_ref, tri_ref, rank_ref, count_ref, carry_scr):
    @pl.when(pl.program_id(0) == 0)
    def _():
        carry_scr[...] = jnp.zeros_like(carry_scr)

    sel = sel_ref[...]
    incl = jnp.dot(tri_ref[...], sel, preferred_element_type=F32)
    excl = incl - sel.astype(F32) + carry_scr[...]
    route = route_ref[...]
    lane = lax.broadcasted_iota(jnp.int32, excl.shape, 1)
    lane_f = lane.astype(F32)
    r1 = jnp.sum(jnp.where(lane_f == route[:, 0:1], excl, 0.0), axis=-1, keepdims=True)
    r2 = jnp.sum(jnp.where(lane_f == route[:, 1:2], excl, 0.0), axis=-1, keepdims=True)
    rank_ref[...] = jnp.where(lane == 0, r1, jnp.where(lane == 1, r2, 0.0))
    carry_scr[...] += incl[incl.shape[0] - 1:, :]
    count_ref[...] = carry_scr[...]


def _dispatch_kernel(d1_ref, d2_ref, x_ref, mod_ref, zeros_ref, xs_ref, h_scr, sems):
    del zeros_ref
    tm = x_ref.shape[0]
    h_scr[...] = _norm_mod(x_ref[...], mod_ref[0], 1)

    def row_copy(r, dest, which):
        return pltpu.make_async_copy(h_scr.at[pl.ds(r, 1), :], xs_ref.at[pl.ds(dest, 1), :], sems.at[which])

    def issue(r, carry):
        row_copy(r, d1_ref[0, 0, r], 0).start()
        row_copy(r, d2_ref[0, 0, r], 1).start()
        return carry

    lax.fori_loop(0, tm, issue, 0, unroll=4)
    for which in range(2):
        pltpu.make_async_copy(h_scr, xs_ref.at[pl.ds(0, tm), :], sems.at[which]).wait()


def _expert_kernel(tile_e_ref, n_act_ref, x_ref, wg_ref, wu_ref, wd_ref, o_ref, h_scr, acc_scr):
    del tile_e_ref
    s = pl.program_id(0)
    f = pl.program_id(1)

    @pl.when(s < n_act_ref[0])
    def _():
        @pl.when(f == 0)
        def _():
            h_scr[...] = x_ref[...].astype(BF16)
            acc_scr[...] = jnp.zeros_like(acc_scr)

        h = h_scr[...]
        g = jnp.dot(h, wg_ref[0], preferred_element_type=F32)
        u = jnp.dot(h, wu_ref[0], preferred_element_type=F32)
        a = (_silu(g) * u).astype(BF16)
        acc_scr[...] += jnp.dot(a, wd_ref[0], preferred_element_type=F32)

        @pl.when(f == pl.num_programs(1) - 1)
        def _():
            o_ref[...] = acc_scr[...]

    @pl.when(s >= n_act_ref[0])
    def _():
        o_ref[...] = jnp.zeros_like(o_ref)


def _combine_kernel(d1_ref, d2_ref, route_ref, x_ref, mod_ref, *rest, out_norm):
    if out_norm:
        gain_ref, ys_ref, o_ref, buf, sems = rest
    else:
        ys_ref, o_ref, buf, sems = rest
    tm = x_ref.shape[0]

    def row_copy(r, src, which):
        return pltpu.make_async_copy(ys_ref.at[pl.ds(src, 1), :], buf.at[which, pl.ds(r, 1), :], sems.at[which])

    def issue(r, carry):
        row_copy(r, d1_ref[0, 0, r], 0).start()
        row_copy(r, d2_ref[0, 0, r], 1).start()
        return carry

    lax.fori_loop(0, tm, issue, 0, unroll=4)
    for which in range(2):
        pltpu.make_async_copy(ys_ref.at[pl.ds(0, tm), :], buf.at[which], sems.at[which]).wait()
    route = route_ref[...]
    y = route[:, 2:3] * buf[0] + route[:, 3:4] * buf[1]
    out = x_ref[...] + mod_ref[0][5:6] * y
    if out_norm:
        out = out * lax.rsqrt(jnp.mean(out * out, axis=-1, keepdims=True) + EPS) * gain_ref[...]
    o_ref[...] = out


def _moe(x, mod, w_router, b_router, w_gate, w_up, w_down, seq, out_gain=None):
    n, d = x.shape
    n_e, _, ff = w_gate.shape
    tm = 512
    te = 512
    tf = 1792
    tiles_per_seq = seq // tm
    n_tok_tiles = n // tm
    n_slots = 2 * n + n_e * te
    n_tiles = n_slots // te
    f_tiles = ff // tf

    sel, route = _router(x, mod, w_router, b_router, seq, tm)
    tri = (jnp.arange(tm)[:, None] >= jnp.arange(tm)[None, :]).astype(BF16)
    rank, counts = pl.pallas_call(
        _rank_kernel,
        grid=(n_tok_tiles,),
        in_specs=[pl.BlockSpec((tm, LANES), lambda i: (i, 0)),
                  pl.BlockSpec((tm, LANES), lambda i: (i, 0)),
                  pl.BlockSpec((tm, tm), lambda i: (0, 0))],
        out_specs=[pl.BlockSpec((tm, LANES), lambda i: (i, 0)), pl.BlockSpec((1, LANES), lambda i: (0, 0))],
        out_shape=[jax.ShapeDtypeStruct((n, LANES), F32), jax.ShapeDtypeStruct((1, LANES), F32)],
        scratch_shapes=[pltpu.VMEM((1, LANES), F32)],
        compiler_params=_cparams("arbitrary"),
        name="moe_rank",
    )(sel, route, tri)

    counts = counts[0, :n_e].astype(jnp.int32)
    padded = ((counts + te - 1) // te) * te
    ends = jnp.cumsum(padded)
    offsets = ends - padded
    n_act = (ends[-1] // te).astype(jnp.int32)
    tile_start = jnp.arange(n_tiles, dtype=jnp.int32) * te
    tile_e = jnp.sum((tile_start[:, None] >= ends[None, :]).astype(jnp.int32), axis=1)
    tile_e = jnp.minimum(tile_e, tile_e[jnp.maximum(n_act - 1, 0)]).astype(jnp.int32)
    e1 = route[:, 0].astype(jnp.int32)
    e2 = route[:, 1].astype(jnp.int32)
    dest1 = (offsets[e1] + rank[:, 0].astype(jnp.int32)).reshape(n_tok_tiles, 1, tm)
    dest2 = (offsets[e2] + rank[:, 1].astype(jnp.int32)).reshape(n_tok_tiles, 1, tm)
    n_act = n_act.reshape(1)

    slot_spec = pl.BlockSpec((1, 1, tm), lambda i: (i, 0, 0), memory_space=pltpu.SMEM)
    xs = pl.pallas_call(
        _dispatch_kernel,
        grid=(n_tok_tiles,),
        in_specs=[slot_spec, slot_spec,
                  pl.BlockSpec((tm, d), lambda i: (i, 0)),
                  pl.BlockSpec((1, 6, d), lambda i: (i // tiles_per_seq, 0, 0)),
                  pl.BlockSpec(memory_space=pl.ANY)],
        out_specs=pl.BlockSpec(memory_space=pl.ANY),
        out_shape=jax.ShapeDtypeStruct((n_slots, d), F32),
        scratch_shapes=[pltpu.VMEM((tm, d), F32), pltpu.SemaphoreType.DMA((2,))],
        input_output_aliases={4: 0},
        compiler_params=_cparams("arbitrary"),
        name="moe_dispatch",
    )(dest1, dest2, x, mod, jnp.zeros((n_slots, d), F32))

    def live(s, n_act_ref):
        return jnp.minimum(s, jnp.maximum(n_act_ref[0] - 1, 0))

    def f_idx(s, f, n_act_ref):
        return jnp.where(s < n_act_ref[0], f, f_tiles - 1)

    ys = pl.pallas_call(
        _expert_kernel,
        grid_spec=pltpu.PrefetchScalarGridSpec(
            num_scalar_prefetch=2,
            grid=(n_tiles, f_tiles),
            in_specs=[pl.BlockSpec((te, d), lambda s, f, te_ref, na_ref: (live(s, na_ref), 0)),
                      pl.BlockSpec((1, d, tf), lambda s, f, te_ref, na_ref: (te_ref[s], 0, f_idx(s, f, na_ref))),
                      pl.BlockSpec((1, d, tf), lambda s, f, te_ref, na_ref: (te_ref[s], 0, f_idx(s, f, na_ref))),
                      pl.BlockSpec((1, tf, d), lambda s, f, te_ref, na_ref: (te_ref[s], f_idx(s, f, na_ref), 0))],
            out_specs=pl.BlockSpec((te, d), lambda s, f, te_ref, na_ref: (s, 0)),
            scratch_shapes=[pltpu.VMEM((te, d), BF16), pltpu.VMEM((te, d), F32)]),
        out_shape=jax.ShapeDtypeStruct((n_slots, d), F32),
        compiler_params=_cparams("arbitrary", "arbitrary"),
        name="moe_experts",
    )(tile_e, n_act, xs, w_gate.astype(BF16), w_up.astype(BF16), w_down.astype(BF16))

    out_norm = out_gain is not None
    gain_args = [out_gain.reshape(1, d).astype(F32)] if out_norm else []
    gain_specs = [pl.BlockSpec((1, d), lambda i: (0, 0))] if out_norm else []
    return pl.pallas_call(
        functools.partial(_combine_kernel, out_norm=out_norm),
        grid=(n_tok_tiles,),
        in_specs=[slot_spec, slot_spec,
                  pl.BlockSpec((tm, LANES), lambda i: (i, 0)),
                  pl.BlockSpec((tm, d), lambda i: (i, 0)),
                  pl.BlockSpec((1, 6, d), lambda i: (i // tiles_per_seq, 0, 0))] + gain_specs
                 + [pl.BlockSpec(memory_space=pl.ANY)],
        out_specs=pl.BlockSpec((tm, d), lambda i: (i, 0)),
        out_shape=jax.ShapeDtypeStruct((n, d), F32),
        scratch_shapes=[pltpu.VMEM((2, tm, d), F32), pltpu.SemaphoreType.DMA((2,))],
        compiler_params=_cparams("arbitrary"),
        name="moe_combine",
    )(dest1, dest2, route, x, mod, *gain_args, ys)


def _softmax_pv(lhs, k_ref, v_ref, s_scr, m_scr, acc_scr, n_chunks, bias_fn=None):
    rows = lhs.shape[0]
    group = min(8, n_chunks)
    m_scr[...] = jnp.full(m_scr.shape, -jnp.inf, F32)

    def logits_body(t, carry):
        for u in range(group):
            j = group * t + u
            start = pl.multiple_of(j * KEY_CHUNK, KEY_CHUNK)
            kc = k_ref[pl.ds(start, KEY_CHUNK), :]
            s = lax.dot_general(lhs, kc, (((1,), (1,)), ((), ())), preferred_element_type=F32)
            if bias_fn is not None:
                s = bias_fn(s, j)
            s_scr[j] = s
            m_scr[...] = jnp.maximum(m_scr[...], jnp.maximum(s[:, :LANES], s[:, LANES:]))
        return carry

    lax.fori_loop(0, n_chunks // group, logits_body, 0)
    m_scr[...] = jnp.broadcast_to(jnp.max(m_scr[...], axis=-1, keepdims=True), m_scr.shape)
    acc_scr[...] = jnp.zeros_like(acc_scr)
    ones = jnp.ones((group * KEY_CHUNK, LANES), BF16)

    def pv_body(t, carry):
        mb = m_scr[...]
        parts = []
        for u in range(group):
            s = s_scr[group * t + u]
            parts.append(jnp.exp2(s[:, :LANES] - mb).astype(BF16))
            parts.append(jnp.exp2(s[:, LANES:] - mb).astype(BF16))
        p = jnp.concatenate(parts, axis=1)
        start = pl.multiple_of(t * (group * KEY_CHUNK), group * KEY_CHUNK)
        v_aug = jnp.concatenate([v_ref[pl.ds(start, group * KEY_CHUNK), :], ones], axis=1)
        acc_scr[...] += jnp.dot(p, v_aug, preferred_element_type=F32)
        return carry

    lax.fori_loop(0, n_chunks // group, pv_body, 0)
    acc = acc_scr[...]
    return acc[:, :LANES] / acc[:, LANES:]


def _att_qkv_epilogue(acc, j, extras, outs, n_qk):
    gain_ref, cos_ref, sin_ref, g1_ref, g2_ref = extras
    q_ref, k_ref, v_ref = outs
    qk = acc[:, :n_qk]
    ms = _split_dot(qk * qk, g1_ref[...])
    r = lax.rsqrt(ms + EPS)
    y = qk * _split_dot(r, g2_ref[...]) * gain_ref[...]
    cos = cos_ref[...]
    sin = sin_ref[...]
    lane = lax.broadcasted_iota(jnp.int32, cos.shape, 1)
    first_half = (lane & (ATT_HEAD_DIM // 2)) == 0
    nq = q_ref.shape[1]
    for c in range(n_qk // LANES):
        yc = y[:, c * LANES:(c + 1) * LANES]
        partner = jnp.where(first_half, pltpu.roll(yc, LANES - ATT_HEAD_DIM // 2, 1),
                            pltpu.roll(yc, ATT_HEAD_DIM // 2, 1))
        out = (yc * cos + partner * sin).astype(BF16)
        if c * LANES < nq:
            q_ref[:, c * LANES:(c + 1) * LANES] = out
        else:
            k_ref[:, c * LANES - nq:(c + 1) * LANES - nq] = out
    v_ref[...] = acc[:, n_qk:].astype(BF16)


def _att_kernel(q_ref, k_ref, v_ref, o_ref, s_scr, m_scr, acc_scr, *, n_chunks):
    tq = q_ref.shape[0]
    q = q_ref[...]
    lane = lax.broadcasted_iota(jnp.int32, (tq, LANES), 1)
    low = lane < ATT_HEAD_DIM
    zero = jnp.zeros((tq, LANES), BF16)
    parts = []
    for pair in range(2):
        qp = q[:, pair * LANES:(pair + 1) * LANES]
        parts.append(jnp.where(low, qp, zero))
        parts.append(jnp.where(low, zero, qp))
    lhs = jnp.concatenate(parts, axis=0)
    o = _softmax_pv(lhs, k_ref, v_ref, s_scr, m_scr, acc_scr, n_chunks)
    for pair in range(2):
        o_ref[:, pair * LANES:(pair + 1) * LANES] = jnp.where(
            low, o[(2 * pair) * tq:(2 * pair + 1) * tq], o[(2 * pair + 1) * tq:(2 * pair + 2) * tq]).astype(BF16)


def _attention_mixer(x, mod, w_qkv, q_gain, k_gain, w_o, batch, seq):
    n, d = x.shape
    hd = ATT_HEAD_DIM
    half = hd // 2
    n_freq = half // 2
    groups = ATT_HEADS // ATT_KV_HEADS
    perm = jnp.concatenate([jnp.arange(n_freq), half + jnp.arange(n_freq),
                            n_freq + jnp.arange(n_freq), half + n_freq + jnp.arange(n_freq)])
    q_cols = (jnp.arange(ATT_HEADS)[:, None] * hd + perm[None, :]).reshape(-1)
    k_base = ATT_HEADS * hd
    v_base = k_base + ATT_KV_HEADS * hd
    kv_rep = jnp.repeat(jnp.arange(ATT_KV_HEADS), 2)
    k_cols = (k_base + kv_rep[:, None] * hd + perm[None, :]).reshape(-1)
    v_cols = (v_base + kv_rep[:, None] * hd + jnp.arange(hd)[None, :]).reshape(-1)
    w = w_qkv[:, jnp.concatenate([q_cols, k_cols, v_cols])].astype(BF16)
    n_q = ATT_HEADS * hd
    n_kv = 2 * ATT_KV_HEADS * hd
    n_qk = n_q + n_kv
    gain_row = jnp.concatenate([jnp.tile(q_gain[perm] * (hd ** -0.5 * LOG2_E), ATT_HEADS),
                                jnp.tile(k_gain[perm], 2 * ATT_KV_HEADS)]).reshape(1, n_qk).astype(F32)
    pos = jnp.arange(seq, dtype=jnp.int32)
    row = (pos // GRID_W).astype(F32)
    col = (pos % GRID_W).astype(F32)
    inv = ROPE_THETA ** (-jnp.arange(n_freq, dtype=F32) / n_freq)
    ang = jnp.concatenate([row[:, None] * inv, col[:, None] * inv], axis=-1)
    cos_t = jnp.tile(jnp.cos(ang), (1, 2 * LANES // hd))
    sin_t = jnp.tile(jnp.concatenate([-jnp.sin(ang), jnp.sin(ang)], axis=-1), (1, LANES // hd))
    grp = jnp.arange(n_qk) // hd
    g1 = (grp[:, None] == jnp.arange(LANES)[None, :]).astype(F32) / hd
    g2 = (jnp.arange(LANES)[:, None] == grp[None, :]).astype(F32)
    tm = 512
    tiles_per_seq = seq // tm
    extras = [gain_row, cos_t, sin_t, g1.astype(BF16), g2.astype(BF16)]
    extra_specs = [pl.BlockSpec((1, n_qk), lambda i, j: (0, 0)),
                   pl.BlockSpec((tm, LANES), lambda i, j: (i % tiles_per_seq, 0)),
                   pl.BlockSpec((tm, LANES), lambda i, j: (i % tiles_per_seq, 0)),
                   pl.BlockSpec((n_qk, LANES), lambda i, j: (0, 0)),
                   pl.BlockSpec((LANES, n_qk), lambda i, j: (0, 0))]
    q, k, v = _project(
        x, mod, 0, w, extras, extra_specs, functools.partial(_att_qkv_epilogue, n_qk=n_qk),
        [jax.ShapeDtypeStruct((n, n_q), BF16), jax.ShapeDtypeStruct((n, n_kv), BF16),
         jax.ShapeDtypeStruct((n, n_kv), BF16)],
        [pl.BlockSpec((tm, n_q), lambda i, j: (i, 0)), pl.BlockSpec((tm, n_kv), lambda i, j: (i, 0)),
         pl.BlockSpec((tm, n_kv), lambda i, j: (i, 0))],
        tm, w.shape[1], seq, "att_qkv")

    tq = 256
    n_chunks = seq // KEY_CHUNK
    q_tiles = seq // tq
    o = pl.pallas_call(
        functools.partial(_att_kernel, n_chunks=n_chunks),
        grid=(batch, ATT_KV_HEADS, q_tiles),
        in_specs=[pl.BlockSpec((tq, groups * hd), lambda b, g, i: (b * q_tiles + i, g)),
                  pl.BlockSpec((seq, LANES), lambda b, g, i: (b, g)),
                  pl.BlockSpec((seq, LANES), lambda b, g, i: (b, g))],
        out_specs=pl.BlockSpec((tq, groups * hd), lambda b, g, i: (b * q_tiles + i, g)),
        out_shape=jax.ShapeDtypeStruct((n, n_q), BF16),
        scratch_shapes=[pltpu.VMEM((n_chunks, groups * tq, KEY_CHUNK), F32),
                        pltpu.VMEM((groups * tq, LANES), F32), pltpu.VMEM((groups * tq, 2 * LANES), F32)],
        compiler_params=_cparams("parallel", "parallel", "arbitrary"),
        name="gqa_attention",
    )(q, k, v)
    return _out_project(o, 0, n_q, w_o.astype(BF16), x, mod, 0, seq, 1024, "att_out")


def _ret_proj_epilogue(acc, j, extras, outs):
    cos_ref, sin_ref = extras
    o_ref = outs[0]
    tn = acc.shape[1]
    base = j * tn
    if j < 2:
        cos = cos_ref[...]
        sin = sin_ref[...]
        half = RET_QK_DIM // 2
        for hh in range(RET_HEADS):
            lo = hh * RET_QK_DIM
            x1 = acc[:, lo:lo + half]
            x2 = acc[:, lo + half:lo + RET_QK_DIM]
            o_ref[:, base + lo:base + lo + half] = (x1 * cos - x2 * sin).astype(BF16)
            o_ref[:, base + lo + half:base + lo + RET_QK_DIM] = (x1 * sin + x2 * cos).astype(BF16)
    else:
        o_ref[:, base:base + tn] = acc.astype(BF16)


def _ret_kernel(logit_ref, q_ref, k_ref, v_ref, gain_ref, o_ref, acc_scr, state_f_scr, state_b_scr, *, n_chunks):
    h = pl.program_id(1)
    c_len = RET_CHUNK

    def log_gamma(direction, shape):
        z = jnp.full(shape, logit_ref[direction, h], F32)
        return jnp.minimum(z, 0.0) - jnp.log1p(jnp.exp(-jnp.abs(z)))

    ri = lax.broadcasted_iota(jnp.int32, (c_len, c_len), 0)
    ci = lax.broadcasted_iota(jnp.int32, (c_len, c_len), 1)
    dist = (ri - ci).astype(F32)
    lg_f = log_gamma(0, (c_len, c_len))
    lg_b = log_gamma(1, (c_len, c_len))
    decay = jnp.where(dist >= 0, jnp.exp(lg_f * jnp.maximum(dist, 0.0)), jnp.exp(lg_b * jnp.maximum(-dist, 0.0)))
    idx = lax.broadcasted_iota(jnp.int32, (c_len, 1), 0).astype(F32)
    lgf = log_gamma(0, (c_len, 1))
    lgb = log_gamma(1, (c_len, 1))
    xi_f = jnp.exp(lgf * (idx + 1.0))
    zeta_f = jnp.exp(lgf * (c_len - 1.0 - idx))
    cd_f = jnp.exp(lgf[0:1] * c_len)
    xi_b = jnp.exp(lgb * (c_len - idx))
    zeta_b = jnp.exp(lgb * idx)
    cd_b = jnp.exp(lgb[0:1] * c_len)

    def chunk(cidx):
        start = pl.multiple_of(cidx * c_len, c_len)
        return (q_ref[pl.ds(start, c_len), :], k_ref[pl.ds(start, c_len), :], v_ref[pl.ds(start, c_len), :], start)

    def update_state(state_scr, kc, vc, zeta, cd):
        zv = (vc.astype(F32) * zeta).astype(BF16)
        kv = lax.dot_general(kc, zv, (((0,), (0,)), ((), ())), preferred_element_type=F32)
        state_scr[...] = state_scr[...] * cd + kv

    def left_to_right(cidx):
        qc, kc, vc, start = chunk(cidx)
        scores = lax.dot_general(qc, kc, (((1,), (1,)), ((), ())), preferred_element_type=F32) * decay
        o = jnp.dot(scores.astype(BF16), vc, preferred_element_type=F32)
        o = o + xi_f * jnp.dot(qc, state_f_scr[...].astype(BF16), preferred_element_type=F32)
        update_state(state_f_scr, kc, vc, zeta_f, cd_f)
        return o, start

    def right_to_left(cidx):
        qc, kc, vc, start = chunk(cidx)
        o = xi_b * jnp.dot(qc, state_b_scr[...].astype(BF16), preferred_element_type=F32)
        update_state(state_b_scr, kc, vc, zeta_b, cd_b)
        return o, start

    def finish(y, start):
        y = y * lax.rsqrt(jnp.mean(y * y, axis=-1, keepdims=True) + EPS) * gain_ref[...]
        o_ref[pl.ds(start, c_len), :] = y.astype(BF16)

    state_f_scr[...] = jnp.zeros_like(state_f_scr)
    state_b_scr[...] = jnp.zeros_like(state_b_scr)
    half = n_chunks // 2

    def first_visits(t, carry):
        o_f, start_f = left_to_right(t)
        o_b, start_b = right_to_left(n_chunks - 1 - t)
        acc_scr[pl.ds(start_f, c_len), :] = o_f
        acc_scr[pl.ds(start_b, c_len), :] = o_b
        return carry

    def second_visits(t, carry):
        o_f, start_f = left_to_right(t)
        o_b, start_b = right_to_left(n_chunks - 1 - t)
        finish(acc_scr[pl.ds(start_f, c_len), :] + o_f, start_f)
        finish(acc_scr[pl.ds(start_b, c_len), :] + o_b, start_b)
        return carry

    lax.fori_loop(0, half, first_visits, 0)
    lax.fori_loop(half, n_chunks, second_visits, 0)


def _retention_mixer(x, mod, w_in, decay_logit, gn_gain, w_o, batch, seq):
    n, d = x.shape
    qk_cols = RET_HEADS * RET_QK_DIM
    v_cols = RET_HEADS * RET_V_DIM
    col_scale = jnp.concatenate([jnp.ones((qk_cols,), F32), jnp.full((qk_cols,), RET_QK_DIM ** -0.5, F32),
                                 jnp.ones((2 * v_cols,), F32)])
    w = (w_in * col_scale[None, :]).astype(BF16)
    pos = jnp.arange(seq, dtype=F32)
    inv = 1.0 / (RET_THETA ** jnp.linspace(0.0, 1.0, RET_QK_DIM // 2, dtype=F32))
    ang = pos[:, None] * inv
    tm, tn = 512, qk_cols
    tiles_per_seq = seq // tm
    half = RET_QK_DIM // 2
    extras = [jnp.cos(ang), jnp.sin(ang)]
    extra_specs = [pl.BlockSpec((tm, half), lambda i, j: (i % tiles_per_seq, 0)),
                   pl.BlockSpec((tm, half), lambda i, j: (i % tiles_per_seq, 0))]
    proj = _project(x, mod, 0, w, extras, extra_specs, _ret_proj_epilogue,
                    jax.ShapeDtypeStruct((n, w.shape[1]), BF16), pl.BlockSpec((tm, w.shape[1]), lambda i, j: (i, 0)),
                    tm, tn, seq, "ret_proj")

    n_chunks = seq // RET_CHUNK
    k_off = qk_cols // RET_QK_DIM
    v_off = 2 * qk_cols // RET_V_DIM
    y = pl.pallas_call(
        functools.partial(_ret_kernel, n_chunks=n_chunks),
        grid=(batch, RET_HEADS),
        in_specs=[pl.BlockSpec(memory_space=pltpu.SMEM),
                  pl.BlockSpec((seq, RET_QK_DIM), lambda b, h: (b, h)),
                  pl.BlockSpec((seq, RET_QK_DIM), lambda b, h: (b, k_off + h)),
                  pl.BlockSpec((seq, RET_V_DIM), lambda b, h: (b, v_off + h)),
                  pl.BlockSpec((1, RET_V_DIM), lambda b, h: (0, 0))],
        out_specs=pl.BlockSpec((seq, RET_V_DIM), lambda b, h: (b, h)),
        out_shape=jax.ShapeDtypeStruct((n, v_cols), BF16),
        scratch_shapes=[pltpu.VMEM((seq, RET_V_DIM), F32), pltpu.VMEM((RET_QK_DIM, RET_V_DIM), F32),
                        pltpu.VMEM((RET_QK_DIM, RET_V_DIM), F32)],
        compiler_params=_cparams("parallel", "arbitrary"),
        name="retention",
    )(decay_logit.astype(F32), proj, proj, proj, gn_gain.reshape(1, RET_V_DIM).astype(F32))
    gate_col = (2 * qk_cols + v_cols) // v_cols
    return _out_project(y, 0, v_cols, w_o.astype(BF16), x, mod, 0, seq, 512, "ret_out",
                        gate_src=proj, gate_col=gate_col)


def _conv_pw1_epilogue(acc, j, extras, outs):
    b_ref = extras[0]
    u = acc + b_ref[...]
    dm = u.shape[1] // 2
    outs[0][...] = (u[:, :dm] * jax.nn.sigmoid(u[:, dm:])).astype(BF16)


def _conv_kernel(prev_ref, cur_ref, next_ref, wdw_ref, bdw_ref, lng_ref, lnb_ref, w2_ref, b2_ref, x_ref, mod_ref,
                 o_ref, ext_scr, shift_scr, c_scr, *, tiles_per_seq, row_block):
    tm = cur_ref.shape[0]
    t = pl.program_id(0) % tiles_per_seq
    keep_prev = jnp.where(t == 0, 0.0, 1.0)
    keep_next = jnp.where(t == tiles_per_seq - 1, 0.0, 1.0)
    ext_scr[0:CONV_HALO, :] = prev_ref[...].astype(F32) * keep_prev
    ext_scr[CONV_HALO:CONV_HALO + tm, :] = cur_ref[...].astype(F32)
    ext_scr[CONV_HALO + tm:, :] = next_ref[...].astype(F32) * keep_next
    base = CONV_HALO - CONV_PAD
    span = tm + 2 * CONV_HALO - SUBLANES
    for r in range(1, SUBLANES):
        shift_scr[r - 1, 0:span, :] = ext_scr[r:r + span, :]
    for rb in range(tm // row_block):
        r0 = rb * row_block
        acc = jnp.zeros((row_block, cur_ref.shape[1]), F32) + bdw_ref[...]
        for j in range(CONV_WIDTH):
            phase = (base + j) % SUBLANES
            start = r0 + base + j - phase
            if phase == 0:
                tap = ext_scr[start:start + row_block, :]
            else:
                tap = shift_scr[phase - 1, start:start + row_block, :]
            acc = acc + wdw_ref[j:j + 1, :] * tap
        c_scr[r0:r0 + row_block, :] = acc
    u = c_scr[...]
    mu = jnp.mean(u, axis=-1, keepdims=True)
    var = jnp.mean(jnp.square(u - mu), axis=-1, keepdims=True)
    u = (u - mu) * lax.rsqrt(var + EPS) * lng_ref[...] + lnb_ref[...]
    u = _silu(u).astype(BF16)
    z = jnp.dot(u, w2_ref[...], preferred_element_type=F32) + b2_ref[...]
    o_ref[...] = x_ref[...] + mod_ref[0][2:3] * z


def _conv_mixer(x, mod, w_pw1, b_pw1, w_dw, b_dw, ln_gain, ln_bias, w_pw2, b_pw2, seq):
    n, d = x.shape
    tm = 512
    u = _project(x, mod, 0, w_pw1.astype(BF16), [b_pw1.reshape(1, 2 * d).astype(F32)],
                 [pl.BlockSpec((1, 2 * d), lambda i, j: (0, 0))], _conv_pw1_epilogue,
                 jax.ShapeDtypeStruct((n, d), BF16), pl.BlockSpec((tm, d), lambda i, j: (i, 0)),
                 tm, 2 * d, seq, "conv_pw1")
    tm = 256
    tiles_per_seq = seq // tm
    halo_per_tile = tm // CONV_HALO
    n_halo = n // CONV_HALO
    wdw = jnp.pad(w_dw.reshape(CONV_WIDTH, d).astype(F32), ((0, 1), (0, 0)))
    vec = lambda a: a.reshape(1, d).astype(F32)
    row_spec = pl.BlockSpec((1, d), lambda i: (0, 0))
    return pl.pallas_call(
        functools.partial(_conv_kernel, tiles_per_seq=tiles_per_seq, row_block=32),
        grid=(n // tm,),
        in_specs=[pl.BlockSpec((CONV_HALO, d), lambda i: (jnp.maximum(i * halo_per_tile - 1, 0), 0)),
                  pl.BlockSpec((tm, d), lambda i: (i, 0)),
                  pl.BlockSpec((CONV_HALO, d), lambda i: (jnp.minimum((i + 1) * halo_per_tile, n_halo - 1), 0)),
                  pl.BlockSpec((CONV_WIDTH + 1, d), lambda i: (0, 0)),
                  row_spec, row_spec, row_spec,
                  pl.BlockSpec((d, d), lambda i: (0, 0)),
                  row_spec,
                  pl.BlockSpec((tm, d), lambda i: (i, 0)),
                  pl.BlockSpec((1, 6, d), lambda i: (i // tiles_per_seq, 0, 0))],
        out_specs=pl.BlockSpec((tm, d), lambda i: (i, 0)),
        out_shape=jax.ShapeDtypeStruct((n, d), F32),
        scratch_shapes=[pltpu.VMEM((tm + 2 * CONV_HALO, d), F32),
                        pltpu.VMEM((SUBLANES - 1, tm + 2 * CONV_HALO, d), F32), pltpu.VMEM((tm, d), F32)],
        compiler_params=_cparams("parallel"),
        name="conv_module",
    )(u, u, u, wdw, vec(b_dw), vec(ln_gain), vec(ln_bias), w_pw2.astype(BF16), vec(b_pw2), x, mod)


def _t5_bucket(rel):
    half = T5_BUCKETS // 2
    exact = half // 2
    n = jnp.abs(rel)
    log_ratio = jnp.log(jnp.maximum(n, 1).astype(F32) / exact) / math.log(T5_MAX_DIST / exact)
    large = jnp.minimum(exact + (log_ratio * (half - exact)).astype(jnp.int32), half - 1)
    return jnp.where(rel > 0, half, 0) + jnp.where(n < exact, n, large)


def _bias_tile_kernel(table_ref, idx_ref, o_ref):
    h = pl.program_id(0)
    idx = idx_ref[0]
    acc = jnp.zeros(idx.shape, F32)
    for b in range(T5_BUCKETS):
        acc = jnp.where(idx == b, table_ref[b, h], acc)
    o_ref[0, 0] = acc * LOG2_E


def _diff_kernel(q_ref, k_ref, v_ref, bias_ref, lam_ref, gain_ref, o_ref, s_scr, m_scr, acc_scr, *, n_chunks, out_scale,
                 lambda_init):
    tq = q_ref.shape[0]
    qi = pl.program_id(2)
    q = q_ref[...]
    lane = lax.broadcasted_iota(jnp.int32, (tq, LANES), 1)
    low = lane < DIFF_HEAD_DIM
    zero = jnp.zeros((tq, LANES), BF16)
    lhs = jnp.concatenate([jnp.where(low, q, zero), jnp.where(low, zero, q)], axis=0)

    q_sub = tq // KEY_CHUNK

    def add_bias(s, j):
        tiles = [bias_ref[0, jnp.clip(j - (q_sub * qi + a), -2, 2) + 2] for a in range(q_sub)]
        return s + jnp.concatenate(tiles + tiles, axis=0)

    o = _softmax_pv(lhs, k_ref, v_ref, s_scr, m_scr, acc_scr, n_chunks, add_bias)
    lam_v = lam_ref[...]
    lam = (jnp.exp(jnp.sum(lam_v[0:1] * lam_v[1:2], axis=-1, keepdims=True))
           - jnp.exp(jnp.sum(lam_v[2:3] * lam_v[3:4], axis=-1, keepdims=True)) + lambda_init)
    y = o[:tq] - lam * o[tq:]
    y = y * lax.rsqrt(jnp.mean(y * y, axis=-1, keepdims=True) + EPS) * gain_ref[...] * out_scale
    o_ref[...] = y.astype(BF16)


def _diff_mixer(x, mod, w_qkv, lam_q1, lam_k1, lam_q2, lam_k2, subln_gain, w_o, rel_bias, lambda_init,
                batch, seq):
    n, d = x.shape
    col_scale = jnp.concatenate([jnp.full((d,), DIFF_HEAD_DIM ** -0.5 * LOG2_E, F32), jnp.ones((2 * d,), F32)])
    w = (w_qkv * col_scale[None, :]).astype(BF16)
    tm = 1024
    qkv = _project(x, mod, 0, w, [], [], _plain_epilogue,
                   jax.ShapeDtypeStruct((n, 3 * d), BF16), pl.BlockSpec((tm, 3 * d), lambda i, j: (i, 0)),
                   tm, d, seq, "diff_qkv")

    tb = KEY_CHUNK
    ar = jnp.arange(KEY_CHUNK, dtype=jnp.int32)
    offs = (jnp.arange(5, dtype=jnp.int32) - 2) * KEY_CHUNK
    rel = (ar[None, None, :] + offs[:, None, None]) - ar[None, :, None]
    bucket = _t5_bucket(rel).astype(jnp.int32)
    bias = pl.pallas_call(
        _bias_tile_kernel,
        grid=(DIFF_HEADS, 5),
        in_specs=[pl.BlockSpec(memory_space=pltpu.SMEM),
                  pl.BlockSpec((1, tb, KEY_CHUNK), lambda h, t: (t, 0, 0))],
        out_specs=pl.BlockSpec((1, 1, tb, KEY_CHUNK), lambda h, t: (h, t, 0, 0)),
        out_shape=jax.ShapeDtypeStruct((DIFF_HEADS, 5, tb, KEY_CHUNK), F32),
        compiler_params=_cparams("arbitrary", "arbitrary"),
        name="rel_bias_tiles",
    )(rel_bias.astype(F32), bucket)

    lam = jnp.stack([lam_q1, lam_k1, lam_q2, lam_k2]).astype(F32)
    n_chunks = seq // KEY_CHUNK
    tq = 2 * KEY_CHUNK
    q_tiles = seq // tq
    o = pl.pallas_call(
        functools.partial(_diff_kernel, n_chunks=n_chunks, out_scale=1.0 - lambda_init, lambda_init=lambda_init),
        grid=(batch, DIFF_HEADS, q_tiles),
        in_specs=[pl.BlockSpec((tq, LANES), lambda b, h, i: (b * q_tiles + i, h)),
                  pl.BlockSpec((seq, LANES), lambda b, h, i: (b, DIFF_HEADS + h)),
                  pl.BlockSpec((seq, LANES), lambda b, h, i: (b, 2 * DIFF_HEADS + h)),
                  pl.BlockSpec((1, 5, tb, KEY_CHUNK), lambda b, h, i: (h, 0, 0, 0)),
                  pl.BlockSpec((4, DIFF_HEAD_DIM), lambda b, h, i: (0, 0)),
                  pl.BlockSpec((1, LANES), lambda b, h, i: (0, 0))],
        out_specs=pl.BlockSpec((tq, LANES), lambda b, h, i: (b * q_tiles + i, h)),
        out_shape=jax.ShapeDtypeStruct((n, d), BF16),
        scratch_shapes=[pltpu.VMEM((n_chunks, 2 * tq, KEY_CHUNK), F32),
                        pltpu.VMEM((2 * tq, LANES), F32), pltpu.VMEM((2 * tq, 2 * LANES), F32)],
        compiler_params=_cparams("parallel", "parallel", "arbitrary"),
        name="diff_attention",
    )(qkv, qkv, qkv, bias, lam, subln_gain.reshape(1, LANES).astype(F32))
    return _out_project(o, 0, d, w_o.astype(BF16), x, mod, 0, seq, 1024, "diff_out")


def _final_norm_kernel(x_ref, g_ref, o_ref):
    x = x_ref[...]
    o_ref[...] = x * lax.rsqrt(jnp.mean(x * x, axis=-1, keepdims=True) + EPS) * g_ref[...]


def _final_norm(x, gain, tm):
    n, d = x.shape
    return pl.pallas_call(
        _final_norm_kernel,
        grid=(n // tm,),
        in_specs=[pl.BlockSpec((tm, d), lambda i: (i, 0)), pl.BlockSpec((1, d), lambda i: (0, 0))],
        out_specs=pl.BlockSpec((tm, d), lambda i: (i, 0)),
        out_shape=jax.ShapeDtypeStruct((n, d), F32),
        compiler_params=_cparams("parallel"),
        name="final_norm",
    )(x, gain.reshape(1, d).astype(F32))


def kernel(x, c, mod_w, mod_b, att_w_qkv, att_q_gain, att_k_gain, att_w_o, ret_w_in, ret_decay_logit, ret_gn_gain, ret_w_o, cnv_w_pw1, cnv_b_pw1, cnv_w_dw, cnv_b_dw, cnv_ln_gain, cnv_ln_bias, cnv_w_pw2, cnv_b_pw2, dif_w_qkv, dif_lam_q1, dif_lam_k1, dif_lam_q2, dif_lam_k2, dif_subln_gain, dif_w_o, rel_bias, ffn_w_gate, ffn_w_up, ffn_w_down, moe_w_router, moe_b_router, moe_w_gate, moe_w_up, moe_w_down, final_gain):
    batch, seq, d = x.shape
    depth = mod_w.shape[0]
    xf = x.reshape(batch * seq, d)
    mods = _modulation(c, mod_w, mod_b)
    for i in range(depth):
        mod = mods[i]
        kind, j = i % 4, i // 4
        if kind == 0:
            xf = _attention_mixer(xf, mod, att_w_qkv[j], att_q_gain[j], att_k_gain[j], att_w_o[j], batch, seq)
        elif kind == 1:
            xf = _retention_mixer(xf, mod, ret_w_in[j], ret_decay_logit[j], ret_gn_gain[j], ret_w_o[j], batch, seq)
        elif kind == 2:
            xf = _conv_mixer(xf, mod, cnv_w_pw1[j], cnv_b_pw1[j], cnv_w_dw[j], cnv_b_dw[j], cnv_ln_gain[j],
                             cnv_ln_bias[j], cnv_w_pw2[j], cnv_b_pw2[j], seq)
        else:
            lambda_init = 0.8 - 0.6 * math.exp(-0.3 * i)
            xf = _diff_mixer(xf, mod, dif_w_qkv[j], dif_lam_q1[j], dif_lam_k1[j], dif_lam_q2[j], dif_lam_k2[j],
                             dif_subln_gain[j], dif_w_o[j], rel_bias, lambda_init, batch, seq)
        if i % 2 == 0:
            m = i // 2
            xf = _ffn(xf, mod, ffn_w_gate[m].astype(BF16), ffn_w_up[m].astype(BF16), ffn_w_down[m].astype(BF16),
                      seq, 512)
        else:
            m = i // 2
            xf = _moe(xf, mod, moe_w_router[m], moe_b_router[m], moe_w_gate[m], moe_w_up[m], moe_w_down[m], seq,
                      out_gain=final_gain if i == depth - 1 else None)
    if depth % 2 == 1:
        xf = _final_norm(xf, final_gain, 512)
    return xf.reshape(batch, seq, d)
```

```python
import functools
import math

import jax
import jax.numpy as jnp
from jax import lax
from jax.experimental import pallas as pl
from jax.experimental.pallas import tpu as pltpu

F32 = jnp.float32
BF16 = jnp.bfloat16

EPS = 1e-6
GRID_W = 64

ATT_HEADS = 16
ATT_KV_HEADS = 4
ATT_HEAD_DIM = 64
ROPE_THETA = 10000.0

RET_HEADS = 4
RET_QK_DIM = 256
RET_V_DIM = 512
RET_THETA = 10000.0
RET_CHUNK = 256

CONV_WIDTH = 31
CONV_PAD = CONV_WIDTH // 2
CONV_HALO = 16

DIFF_HEADS = 8
DIFF_HEAD_DIM = 64
T5_BUCKETS = 32
T5_MAX_DIST = 128

N_EXPERTS = 8
LANES = 128
SUBLANES = 8
KEY_CHUNK = 256
LOG2_E = 1.4426950408889634
VMEM_LIMIT_BYTES = 56 * 1024 * 1024


def _cparams(*sem):
    return pltpu.CompilerParams(dimension_semantics=sem, vmem_limit_bytes=VMEM_LIMIT_BYTES)


def _silu(v):
    return v * jax.nn.sigmoid(v)


def _norm_mod(x, mod, which):
    shift = mod[3 * which:3 * which + 1]
    scale = mod[3 * which + 1:3 * which + 2]
    y = x * lax.rsqrt(jnp.mean(x * x, axis=-1, keepdims=True) + EPS)
    return y * (1.0 + scale) + shift


def _split_dot(a, b):
    hi = a.astype(BF16)
    lo = (a - hi.astype(F32)).astype(BF16)
    return (jnp.dot(hi, b, preferred_element_type=F32) + jnp.dot(lo, b, preferred_element_type=F32))


def _mod_kernel(c_ref, w_ref, b_ref, o_ref):
    cond = _silu(c_ref[...])
    o_ref[0] = jnp.dot(cond, w_ref[0], preferred_element_type=F32, precision=lax.Precision.HIGHEST) + b_ref[0]


def _modulation(c, mod_w, mod_b):
    depth, d, d6 = mod_w.shape
    b = c.shape[0]
    tn = d
    out = pl.pallas_call(
        _mod_kernel,
        grid=(depth, d6 // tn),
        in_specs=[pl.BlockSpec((b, d), lambda i, j: (0, 0)),
                  pl.BlockSpec((1, d, tn), lambda i, j: (i, 0, j)),
                  pl.BlockSpec((1, 1, tn), lambda i, j: (i, 0, j))],
        out_specs=pl.BlockSpec((1, b, tn), lambda i, j: (i, 0, j)),
        out_shape=jax.ShapeDtypeStruct((depth, b, d6), F32),
        compiler_params=_cparams("arbitrary", "arbitrary"),
        name="modulation",
    )(c, mod_w, mod_b.reshape(depth, 1, d6))
    return out.reshape(depth, b, 6, d)


def _proj_kernel(x_ref, mod_ref, w_ref, *rest, which, n_extra, epilogue, tn):
    extras = rest[:n_extra]
    outs = rest[n_extra:]
    h = _norm_mod(x_ref[...], mod_ref[0], which).astype(BF16)
    for j in range(w_ref.shape[1] // tn):
        acc = jnp.dot(h, w_ref[:, j * tn:(j + 1) * tn], preferred_element_type=F32)
        epilogue(acc, j, extras, outs)


def _project(x, mod, which, w, extras, extra_specs, epilogue, out_shapes, out_specs, tm, tn, seq, name):
    n, d = x.shape
    c = w.shape[1]
    tiles_per_seq = seq // tm
    kern = functools.partial(_proj_kernel, which=which, n_extra=len(extras), epilogue=epilogue, tn=tn)
    return pl.pallas_call(
        kern,
        grid=(n // tm, 1),
        in_specs=[pl.BlockSpec((tm, d), lambda i, j: (i, 0)),
                  pl.BlockSpec((1, 6, d), lambda i, j: (i // tiles_per_seq, 0, 0)),
                  pl.BlockSpec((d, c), lambda i, j: (0, 0), pipeline_mode=pl.Buffered(1))] + extra_specs,
        out_specs=out_specs,
        out_shape=out_shapes,
        compiler_params=_cparams("parallel", "arbitrary"),
        name=name,
    )(x, mod, w, *extras)


def _plain_epilogue(acc, j, extras, outs):
    tn = acc.shape[1]
    outs[0][:, j * tn:(j + 1) * tn] = acc.astype(outs[0].dtype)


def _oproj_kernel(*refs, which, gated, has_bias):
    refs = list(refs)
    y_ref = refs.pop(0)
    g_ref = refs.pop(0) if gated else None
    w_ref = refs.pop(0)
    b_ref = refs.pop(0) if has_bias else None
    x_ref, mod_ref, o_ref = refs
    y = y_ref[...]
    if gated:
        y = (y.astype(F32) * _silu(g_ref[...].astype(F32))).astype(BF16)
    z = jnp.dot(y, w_ref[...], preferred_element_type=F32)
    if has_bias:
        z = z + b_ref[...]
    gate = mod_ref[0][3 * which + 2:3 * which + 3]
    o_ref[...] = x_ref[...] + gate * z


def _out_project(y, y_col, k, w, x, mod, which, seq, tm, name, gate_src=None, gate_col=0, bias=None):
    n, d = x.shape
    tiles_per_seq = seq // tm
    args = [y]
    specs = [pl.BlockSpec((tm, k), lambda i: (i, y_col))]
    if gate_src is not None:
        args.append(gate_src)
        specs.append(pl.BlockSpec((tm, k), lambda i: (i, gate_col)))
    args.append(w)
    specs.append(pl.BlockSpec((k, d), lambda i: (0, 0)))
    if bias is not None:
        args.append(bias.reshape(1, d))
        specs.append(pl.BlockSpec((1, d), lambda i: (0, 0)))
    args += [x, mod]
    specs += [pl.BlockSpec((tm, d), lambda i: (i, 0)),
              pl.BlockSpec((1, 6, d), lambda i: (i // tiles_per_seq, 0, 0))]
    kern = functools.partial(_oproj_kernel, which=which, gated=gate_src is not None, has_bias=bias is not None)
    return pl.pallas_call(
        kern,
        grid=(n // tm,),
        in_specs=specs,
        out_specs=pl.BlockSpec((tm, d), lambda i: (i, 0)),
        out_shape=jax.ShapeDtypeStruct((n, d), F32),
        compiler_params=_cparams("parallel"),
        name=name,
    )(*args)


def _ffn_kernel(x_ref, mod_ref, wg_ref, wu_ref, wd_ref, o_ref):
    x = x_ref[...]
    mod = mod_ref[0]
    h = _norm_mod(x, mod, 1).astype(BF16)
    g = jnp.dot(h, wg_ref[...], preferred_element_type=F32)
    u = jnp.dot(h, wu_ref[...], preferred_element_type=F32)
    a = (_silu(g) * u).astype(BF16)
    o_ref[...] = x + mod[5:6] * jnp.dot(a, wd_ref[...], preferred_element_type=F32)


def _ffn(x, mod, wg, wu, wd, seq, tm):
    n, d = x.shape
    ff = wg.shape[1]
    tiles_per_seq = seq // tm
    resident = pl.Buffered(1)
    return pl.pallas_call(
        _ffn_kernel,
        grid=(n // tm,),
        in_specs=[pl.BlockSpec((tm, d), lambda i: (i, 0)),
                  pl.BlockSpec((1, 6, d), lambda i: (i // tiles_per_seq, 0, 0)),
                  pl.BlockSpec((d, ff), lambda i: (0, 0), pipeline_mode=resident),
                  pl.BlockSpec((d, ff), lambda i: (0, 0), pipeline_mode=resident),
                  pl.BlockSpec((ff, d), lambda i: (0, 0), pipeline_mode=resident)],
        out_specs=pl.BlockSpec((tm, d), lambda i: (i, 0)),
        out_shape=jax.ShapeDtypeStruct((n, d), F32),
        compiler_params=_cparams("parallel"),
        name="dense_ffn",
    )(x, mod, wg, wu, wd)


def _router_kernel(x_ref, mod_ref, wr_ref, br_ref, sel_ref, route_ref):
    h = _norm_mod(x_ref[...], mod_ref[0], 1)
    logits = jnp.dot(h, wr_ref[...], preferred_element_type=F32, precision=lax.Precision.HIGHEST) + br_ref[...]
    lane = lax.broadcasted_iota(jnp.int32, logits.shape, 1)
    neg = jnp.float32(-jnp.inf)
    logits = jnp.where(lane < N_EXPERTS, logits, neg)
    m1 = jnp.max(logits, axis=-1, keepdims=True)
    i1 = jnp.min(jnp.where(logits == m1, lane, LANES), axis=-1, keepdims=True)
    rest = jnp.where(lane == i1, neg, logits)
    m2 = jnp.max(rest, axis=-1, keepdims=True)
    i2 = jnp.min(jnp.where(rest == m2, lane, LANES), axis=-1, keepdims=True)
    e2 = jnp.exp(m2 - m1)
    w1 = 1.0 / (1.0 + e2)
    w2 = e2 / (1.0 + e2)
    sel_ref[...] = jnp.where((lane == i1) | (lane == i2), 1.0, 0.0).astype(BF16)
    route = jnp.where(lane == 0, i1.astype(F32), jnp.where(lane == 1, i2.astype(F32), 0.0))
    route_ref[...] = jnp.where(lane == 2, w1, jnp.where(lane == 3, w2, route))


def _router(x, mod, w_router, b_router, seq, tm):
    n, d = x.shape
    tiles_per_seq = seq // tm
    wr = jnp.pad(w_router, ((0, 0), (0, LANES - N_EXPERTS)))
    br = jnp.pad(b_router, (0, LANES - N_EXPERTS)).reshape(1, LANES)
    return pl.pallas_call(
        _router_kernel,
        grid=(n // tm,),
        in_specs=[pl.BlockSpec((tm, d), lambda i: (i, 0)),
                  pl.BlockSpec((1, 6, d), lambda i: (i // tiles_per_seq, 0, 0)),
                  pl.BlockSpec((d, LANES), lambda i: (0, 0)),
                  pl.BlockSpec((1, LANES), lambda i: (0, 0))],
        out_specs=[pl.BlockSpec((tm, LANES), lambda i: (i, 0)), pl.BlockSpec((tm, LANES), lambda i: (i, 0))],
        out_shape=[jax.ShapeDtypeStruct((n, LANES), BF16), jax.ShapeDtypeStruct((n, LANES), F32)],
        compiler_params=_cparams("parallel"),
        name="moe_router",
    )(x, mod, wr, br)


def _rank_kernel(sel_ref, route_ref, tri_ref, rank_ref, count_ref, carry_scr):
    @pl.when(pl.program_id(0) == 0)
    def _():
        carry_scr[...] = jnp.zeros_like(carry_scr)

    sel = sel_ref[...]
    incl = jnp.dot(tri_ref[...], sel, preferred_element_type=F32)
    excl = incl - sel.astype(F32) + carry_scr[...]
    route = route_ref[...]
    lane = lax.broadcasted_iota(jnp.int32, excl.shape, 1)
    lane_f = lane.astype(F32)
    r1 = jnp.sum(jnp.where(lane_f == route[:, 0:1], excl, 0.0), axis=-1, keepdims=True)
    r2 = jnp.sum(jnp.where(lane_f == route[:, 1:2], excl, 0.0), axis=-1, keepdims=True)
    rank_ref[...] = jnp.where(lane == 0, r1, jnp.where(lane == 1, r2, 0.0))
    carry_scr[...] += incl[incl.shape[0] - 1:, :]
    count_ref[...] = carry_scr[...]


def _dispatch_kernel(zero_rows_ref, d1_ref, d2_ref, x_ref, mod_ref, xs_ref, h_scr, zero_scr, sems):
    tm = x_ref.shape[0]

    @pl.when(pl.program_id(0) == 0)
    def _():
        zero_scr[...] = jnp.zeros_like(zero_scr)
        te = zero_scr.shape[0]
        for k in range(zero_rows_ref.shape[0]):
            start = pl.multiple_of(zero_rows_ref[k], te)
            copy = pltpu.make_async_copy(zero_scr, xs_ref.at[pl.ds(start, te), :], sems.at[2])
            copy.start()
            copy.wait()

    h_scr[...] = _norm_mod(x_ref[...], mod_ref[0], 1)

    def row_copy(r, dest, which):
        return pltpu.make_async_copy(h_scr.at[pl.ds(r, 1), :], xs_ref.at[pl.ds(dest, 1), :], sems.at[which])

    def issue(r, carry):
        row_copy(r, d1_ref[0, 0, r], 0).start()
        row_copy(r, d2_ref[0, 0, r], 1).start()
        return carry

    lax.fori_loop(0, tm, issue, 0, unroll=4)
    for which in range(2):
        pltpu.make_async_copy(h_scr, xs_ref.at[pl.ds(0, tm), :], sems.at[which]).wait()


def _expert_kernel(tile_e_ref, n_act_ref, x_ref, wg_ref, wu_ref, wd_ref, o_ref, h_scr, acc_scr):
    del tile_e_ref
    s = pl.program_id(0)
    f = pl.program_id(1)

    @pl.when(s < n_act_ref[0])
    def _():
        @pl.when(f == 0)
        def _():
            h_scr[...] = x_ref[...].astype(BF16)
            acc_scr[...] = jnp.zeros_like(acc_scr)

        h = h_scr[...]
        g = jnp.dot(h, wg_ref[0, 0], preferred_element_type=F32)
        u = jnp.dot(h, wu_ref[0, 0], preferred_element_type=F32)
        a = (_silu(g) * u).astype(BF16)
        acc_scr[...] += jnp.dot(a, wd_ref[0, 0], preferred_element_type=F32)

        @pl.when(f == pl.num_programs(1) - 1)
        def _():
            o_ref[...] = acc_scr[...]

    @pl.when(s >= n_act_ref[0])
    def _():
        o_ref[...] = jnp.zeros_like(o_ref)


def _combine_kernel(d1_ref, d2_ref, route_ref, x_ref, mod_ref, *rest, out_norm):
    if out_norm:
        gain_ref, ys_ref, o_ref, buf, sems = rest
    else:
        ys_ref, o_ref, buf, sems = rest
    tm = x_ref.shape[0]

    def row_copy(r, src, which):
        return pltpu.make_async_copy(ys_ref.at[pl.ds(src, 1), :], buf.at[which, pl.ds(r, 1), :], sems.at[which])

    def issue(r, carry):
        row_copy(r, d1_ref[0, 0, r], 0).start()
        row_copy(r, d2_ref[0, 0, r], 1).start()
        return carry

    lax.fori_loop(0, tm, issue, 0, unroll=4)
    for which in range(2):
        pltpu.make_async_copy(ys_ref.at[pl.ds(0, tm), :], buf.at[which], sems.at[which]).wait()
    route = route_ref[...]
    y = route[:, 2:3] * buf[0] + route[:, 3:4] * buf[1]
    out = x_ref[...] + mod_ref[0][5:6] * y
    if out_norm:
        out = out * lax.rsqrt(jnp.mean(out * out, axis=-1, keepdims=True) + EPS) * gain_ref[...]
    o_ref[...] = out


def _moe(x, mod, w_router, b_router, w_gate, w_up, w_down, layer, seq, out_gain=None):
    n, d = x.shape
    _, n_e, _, ff = w_gate.shape
    tm = 512
    te = 512
    tf = 1792
    tiles_per_seq = seq // tm
    n_tok_tiles = n // tm
    n_slots = 2 * n + n_e * te
    n_tiles = n_slots // te
    f_tiles = ff // tf

    sel, route = _router(x, mod, w_router, b_router, seq, tm)
    tri = (jnp.arange(tm)[:, None] >= jnp.arange(tm)[None, :]).astype(BF16)
    rank, counts = pl.pallas_call(
        _rank_kernel,
        grid=(n_tok_tiles,),
        in_specs=[pl.BlockSpec((tm, LANES), lambda i: (i, 0)),
                  pl.BlockSpec((tm, LANES), lambda i: (i, 0)),
                  pl.BlockSpec((tm, tm), lambda i: (0, 0))],
        out_specs=[pl.BlockSpec((tm, LANES), lambda i: (i, 0)), pl.BlockSpec((1, LANES), lambda i: (0, 0))],
        out_shape=[jax.ShapeDtypeStruct((n, LANES), F32), jax.ShapeDtypeStruct((1, LANES), F32)],
        scratch_shapes=[pltpu.VMEM((1, LANES), F32)],
        compiler_params=_cparams("arbitrary"),
        name="moe_rank",
    )(sel, route, tri)

    counts = counts[0, :n_e].astype(jnp.int32)
    padded = ((counts + te - 1) // te) * te
    ends = jnp.cumsum(padded)
    offsets = ends - padded
    n_act = (ends[-1] // te).astype(jnp.int32)
    tile_start = jnp.arange(n_tiles, dtype=jnp.int32) * te
    tile_e = jnp.sum((tile_start[:, None] >= ends[None, :]).astype(jnp.int32), axis=1)
    tile_e = jnp.minimum(tile_e, tile_e[jnp.maximum(n_act - 1, 0)]).astype(jnp.int32)
    e1 = route[:, 0].astype(jnp.int32)
    e2 = route[:, 1].astype(jnp.int32)
    dest1 = (offsets[e1] + rank[:, 0].astype(jnp.int32)).reshape(n_tok_tiles, 1, tm)
    dest2 = (offsets[e2] + rank[:, 1].astype(jnp.int32)).reshape(n_tok_tiles, 1, tm)
    n_act = n_act.reshape(1)

    slot_spec = pl.BlockSpec((1, 1, tm), lambda i: (i, 0, 0), memory_space=pltpu.SMEM)
    zero_rows = jnp.concatenate([jnp.maximum(ends - te, 0),
                                 (n_tiles - n_e + jnp.arange(n_e, dtype=jnp.int32)) * te]).astype(jnp.int32)
    xs = pl.pallas_call(
        _dispatch_kernel,
        grid=(n_tok_tiles,),
        in_specs=[pl.BlockSpec(memory_space=pltpu.SMEM), slot_spec, slot_spec,
                  pl.BlockSpec((tm, d), lambda i: (i, 0)),
                  pl.BlockSpec((1, 6, d), lambda i: (i // tiles_per_seq, 0, 0))],
        out_specs=pl.BlockSpec(memory_space=pl.ANY),
        out_shape=jax.ShapeDtypeStruct((n_slots, d), F32),
        scratch_shapes=[pltpu.VMEM((tm, d), F32), pltpu.VMEM((te, d), F32), pltpu.SemaphoreType.DMA((3,))],
        compiler_params=_cparams("arbitrary"),
        name="moe_dispatch",
    )(zero_rows, dest1, dest2, x, mod)

    def live(s, n_act_ref):
        return jnp.minimum(s, jnp.maximum(n_act_ref[0] - 1, 0))

    def f_idx(s, f, n_act_ref):
        return jnp.where(s < n_act_ref[0], f, f_tiles - 1)

    ys = pl.pallas_call(
        _expert_kernel,
        grid_spec=pltpu.PrefetchScalarGridSpec(
            num_scalar_prefetch=2,
            grid=(n_tiles, f_tiles),
            in_specs=[pl.BlockSpec((te, d), lambda s, f, te_ref, na_ref: (live(s, na_ref), 0)),
                      pl.BlockSpec((1, 1, d, tf),
                                   lambda s, f, te_ref, na_ref: (layer, te_ref[s], 0, f_idx(s, f, na_ref))),
                      pl.BlockSpec((1, 1, d, tf),
                                   lambda s, f, te_ref, na_ref: (layer, te_ref[s], 0, f_idx(s, f, na_ref))),
                      pl.BlockSpec((1, 1, tf, d),
                                   lambda s, f, te_ref, na_ref: (layer, te_ref[s], f_idx(s, f, na_ref), 0))],
            out_specs=pl.BlockSpec((te, d), lambda s, f, te_ref, na_ref: (s, 0)),
            scratch_shapes=[pltpu.VMEM((te, d), BF16), pltpu.VMEM((te, d), F32)]),
        out_shape=jax.ShapeDtypeStruct((n_slots, d), F32),
        compiler_params=_cparams("arbitrary", "arbitrary"),
        name="moe_experts",
    )(tile_e, n_act, xs, w_gate, w_up, w_down)

    out_norm = out_gain is not None
    gain_args = [out_gain.reshape(1, d).astype(F32)] if out_norm else []
    gain_specs = [pl.BlockSpec((1, d), lambda i: (0, 0))] if out_norm else []
    return pl.pallas_call(
        functools.partial(_combine_kernel, out_norm=out_norm),
        grid=(n_tok_tiles,),
        in_specs=[slot_spec, slot_spec,
                  pl.BlockSpec((tm, LANES), lambda i: (i, 0)),
                  pl.BlockSpec((tm, d), lambda i: (i, 0)),
                  pl.BlockSpec((1, 6, d), lambda i: (i // tiles_per_seq, 0, 0))] + gain_specs
                 + [pl.BlockSpec(memory_space=pl.ANY)],
        out_specs=pl.BlockSpec((tm, d), lambda i: (i, 0)),
        out_shape=jax.ShapeDtypeStruct((n, d), F32),
        scratch_shapes=[pltpu.VMEM((2, tm, d), F32), pltpu.SemaphoreType.DMA((2,))],
        compiler_params=_cparams("arbitrary"),
        name="moe_combine",
    )(dest1, dest2, route, x, mod, *gain_args, ys)


def _softmax_pv(lhs, k_ref, v_ref, s_scr, m_scr, acc_scr, n_chunks, bias_fn=None):
    rows = lhs.shape[0]
    group = min(8, n_chunks)
    m_scr[...] = jnp.full(m_scr.shape, -jnp.inf, F32)

    for j in range(n_chunks):
        kc = k_ref[j * KEY_CHUNK:(j + 1) * KEY_CHUNK, :]
        s = lax.dot_general(lhs, kc, (((1,), (1,)), ((), ())), preferred_element_type=F32)
        if bias_fn is not None:
            s = bias_fn(s, j)
        s_scr[j] = s
        m_scr[...] = jnp.maximum(m_scr[...], jnp.maximum(s[:, :LANES], s[:, LANES:]))
    m_scr[...] = jnp.broadcast_to(jnp.max(m_scr[...], axis=-1, keepdims=True), m_scr.shape)
    acc_scr[...] = jnp.zeros_like(acc_scr)
    ones = jnp.ones((group * KEY_CHUNK, LANES), BF16)

    def pv_body(t, carry):
        mb = m_scr[...]
        parts = []
        for u in range(group):
            s = s_scr[group * t + u]
            parts.append(jnp.exp2(s[:, :LANES] - mb).astype(BF16))
            parts.append(jnp.exp2(s[:, LANES:] - mb).astype(BF16))
        p = jnp.concatenate(parts, axis=1)
        start = pl.multiple_of(t * (group * KEY_CHUNK), group * KEY_CHUNK)
        v_aug = jnp.concatenate([v_ref[pl.ds(start, group * KEY_CHUNK), :], ones], axis=1)
        acc_scr[...] += jnp.dot(p, v_aug, preferred_element_type=F32)
        return carry

    lax.fori_loop(0, n_chunks // group, pv_body, 0)
    acc = acc_scr[...]
    return acc[:, :LANES] / acc[:, LANES:]


def _att_qkv_epilogue(acc, j, extras, outs, n_qk):
    gain_ref, cos_ref, sin_ref, g1_ref, g2_ref = extras
    q_ref, k_ref, v_ref = outs
    qk = acc[:, :n_qk]
    ms = _split_dot(qk * qk, g1_ref[...])
    r = lax.rsqrt(ms + EPS)
    y = qk * _split_dot(r, g2_ref[...]) * gain_ref[...]
    cos = cos_ref[...]
    sin = sin_ref[...]
    lane = lax.broadcasted_iota(jnp.int32, cos.shape, 1)
    first_half = (lane & (ATT_HEAD_DIM // 2)) == 0
    nq = q_ref.shape[1]
    for c in range(n_qk // LANES):
        yc = y[:, c * LANES:(c + 1) * LANES]
        partner = jnp.where(first_half, pltpu.roll(yc, LANES - ATT_HEAD_DIM // 2, 1),
                            pltpu.roll(yc, ATT_HEAD_DIM // 2, 1))
        out = (yc * cos + partner * sin).astype(BF16)
        if c * LANES < nq:
            q_ref[:, c * LANES:(c + 1) * LANES] = out
        else:
            k_ref[:, c * LANES - nq:(c + 1) * LANES - nq] = out
    v_ref[...] = acc[:, n_qk:].astype(BF16)


def _att_kernel(q_ref, k_ref, v_ref, o_ref, s_scr, m_scr, acc_scr, *, n_chunks):
    tq = q_ref.shape[0]
    q = q_ref[...]
    lane = lax.broadcasted_iota(jnp.int32, (tq, LANES), 1)
    low = lane < ATT_HEAD_DIM
    zero = jnp.zeros((tq, LANES), BF16)
    parts = []
    for pair in range(2):
        qp = q[:, pair * LANES:(pair + 1) * LANES]
        parts.append(jnp.where(low, qp, zero))
        parts.append(jnp.where(low, zero, qp))
    lhs = jnp.concatenate(parts, axis=0)
    o = _softmax_pv(lhs, k_ref, v_ref, s_scr, m_scr, acc_scr, n_chunks)
    for pair in range(2):
        o_ref[:, pair * LANES:(pair + 1) * LANES] = jnp.where(
            low, o[(2 * pair) * tq:(2 * pair + 1) * tq], o[(2 * pair + 1) * tq:(2 * pair + 2) * tq]).astype(BF16)


def _attention_mixer(x, mod, w_qkv, q_gain, k_gain, w_o, batch, seq):
    n, d = x.shape
    hd = ATT_HEAD_DIM
    half = hd // 2
    n_freq = half // 2
    groups = ATT_HEADS // ATT_KV_HEADS
    perm = jnp.concatenate([jnp.arange(n_freq), half + jnp.arange(n_freq),
                            n_freq + jnp.arange(n_freq), half + n_freq + jnp.arange(n_freq)])
    q_cols = (jnp.arange(ATT_HEADS)[:, None] * hd + perm[None, :]).reshape(-1)
    k_base = ATT_HEADS * hd
    v_base = k_base + ATT_KV_HEADS * hd
    kv_rep = jnp.repeat(jnp.arange(ATT_KV_HEADS), 2)
    k_cols = (k_base + kv_rep[:, None] * hd + perm[None, :]).reshape(-1)
    v_cols = (v_base + kv_rep[:, None] * hd + jnp.arange(hd)[None, :]).reshape(-1)
    w = w_qkv[:, jnp.concatenate([q_cols, k_cols, v_cols])].astype(BF16)
    n_q = ATT_HEADS * hd
    n_kv = 2 * ATT_KV_HEADS * hd
    n_qk = n_q + n_kv
    gain_row = jnp.concatenate([jnp.tile(q_gain[perm] * (hd ** -0.5 * LOG2_E), ATT_HEADS),
                                jnp.tile(k_gain[perm], 2 * ATT_KV_HEADS)]).reshape(1, n_qk).astype(F32)
    pos = jnp.arange(seq, dtype=jnp.int32)
    row = (pos // GRID_W).astype(F32)
    col = (pos % GRID_W).astype(F32)
    inv = ROPE_THETA ** (-jnp.arange(n_freq, dtype=F32) / n_freq)
    ang = jnp.concatenate([row[:, None] * inv, col[:, None] * inv], axis=-1)
    cos_t = jnp.tile(jnp.cos(ang), (1, 2 * LANES // hd))
    sin_t = jnp.tile(jnp.concatenate([-jnp.sin(ang), jnp.sin(ang)], axis=-1), (1, LANES // hd))
    grp = jnp.arange(n_qk) // hd
    g1 = (grp[:, None] == jnp.arange(LANES)[None, :]).astype(F32) / hd
    g2 = (jnp.arange(LANES)[:, None] == grp[None, :]).astype(F32)
    tm = 512
    tiles_per_seq = seq // tm
    extras = [gain_row, cos_t, sin_t, g1.astype(BF16), g2.astype(BF16)]
    extra_specs = [pl.BlockSpec((1, n_qk), lambda i, j: (0, 0)),
                   pl.BlockSpec((tm, LANES), lambda i, j: (i % tiles_per_seq, 0)),
                   pl.BlockSpec((tm, LANES), lambda i, j: (i % tiles_per_seq, 0)),
                   pl.BlockSpec((n_qk, LANES), lambda i, j: (0, 0)),
                   pl.BlockSpec((LANES, n_qk), lambda i, j: (0, 0))]
    q, k, v = _project(
        x, mod, 0, w, extras, extra_specs, functools.partial(_att_qkv_epilogue, n_qk=n_qk),
        [jax.ShapeDtypeStruct((n, n_q), BF16), jax.ShapeDtypeStruct((n, n_kv), BF16),
         jax.ShapeDtypeStruct((n, n_kv), BF16)],
        [pl.BlockSpec((tm, n_q), lambda i, j: (i, 0)), pl.BlockSpec((tm, n_kv), lambda i, j: (i, 0)),
         pl.BlockSpec((tm, n_kv), lambda i, j: (i, 0))],
        tm, w.shape[1], seq, "att_qkv")

    tq = 256
    n_chunks = seq // KEY_CHUNK
    q_tiles = seq // tq
    o = pl.pallas_call(
        functools.partial(_att_kernel, n_chunks=n_chunks),
        grid=(batch, ATT_KV_HEADS, q_tiles),
        in_specs=[pl.BlockSpec((tq, groups * hd), lambda b, g, i: (b * q_tiles + i, g)),
                  pl.BlockSpec((seq, LANES), lambda b, g, i: (b, g)),
                  pl.BlockSpec((seq, LANES), lambda b, g, i: (b, g))],
        out_specs=pl.BlockSpec((tq, groups * hd), lambda b, g, i: (b * q_tiles + i, g)),
        out_shape=jax.ShapeDtypeStruct((n, n_q), BF16),
        scratch_shapes=[pltpu.VMEM((n_chunks, groups * tq, KEY_CHUNK), F32),
                        pltpu.VMEM((groups * tq, LANES), F32), pltpu.VMEM((groups * tq, 2 * LANES), F32)],
        compiler_params=_cparams("parallel", "parallel", "arbitrary"),
        name="gqa_attention",
    )(q, k, v)
    return _out_project(o, 0, n_q, w_o.astype(BF16), x, mod, 0, seq, 1024, "att_out")


def _ret_proj_epilogue(acc, j, extras, outs):
    cos_ref, sin_ref = extras
    o_ref = outs[0]
    tn = acc.shape[1]
    base = j * tn
    if j < 2:
        cos = cos_ref[...]
        sin = sin_ref[...]
        half = RET_QK_DIM // 2
        for hh in range(RET_HEADS):
            lo = hh * RET_QK_DIM
            x1 = acc[:, lo:lo + half]
            x2 = acc[:, lo + half:lo + RET_QK_DIM]
            o_ref[:, base + lo:base + lo + half] = (x1 * cos - x2 * sin).astype(BF16)
            o_ref[:, base + lo + half:base + lo + RET_QK_DIM] = (x1 * sin + x2 * cos).astype(BF16)
    else:
        o_ref[:, base:base + tn] = acc.astype(BF16)


def _ret_kernel(logit_ref, q_ref, k_ref, v_ref, gain_ref, o_ref, acc_scr, state_f_scr, state_b_scr, *, n_chunks):
    h = pl.program_id(1)
    c_len = RET_CHUNK

    def log_gamma(direction, shape):
        z = jnp.full(shape, logit_ref[direction, h], F32)
        return jnp.minimum(z, 0.0) - jnp.log1p(jnp.exp(-jnp.abs(z)))

    ri = lax.broadcasted_iota(jnp.int32, (c_len, c_len), 0)
    ci = lax.broadcasted_iota(jnp.int32, (c_len, c_len), 1)
    dist = (ri - ci).astype(F32)
    lg_f = log_gamma(0, (c_len, c_len))
    lg_b = log_gamma(1, (c_len, c_len))
    decay = jnp.where(dist >= 0, jnp.exp(lg_f * jnp.maximum(dist, 0.0)), jnp.exp(lg_b * jnp.maximum(-dist, 0.0)))
    idx = lax.broadcasted_iota(jnp.int32, (c_len, 1), 0).astype(F32)
    lgf = log_gamma(0, (c_len, 1))
    lgb = log_gamma(1, (c_len, 1))
    xi_f = jnp.exp(lgf * (idx + 1.0))
    zeta_f = jnp.exp(lgf * (c_len - 1.0 - idx))
    cd_f = jnp.exp(lgf[0:1] * c_len)
    xi_b = jnp.exp(lgb * (c_len - idx))
    zeta_b = jnp.exp(lgb * idx)
    cd_b = jnp.exp(lgb[0:1] * c_len)

    def chunk(cidx):
        start = pl.multiple_of(cidx * c_len, c_len)
        return (q_ref[pl.ds(start, c_len), :], k_ref[pl.ds(start, c_len), :], v_ref[pl.ds(start, c_len), :], start)

    def update_state(state_scr, kc, vc, zeta, cd):
        zv = (vc.astype(F32) * zeta).astype(BF16)
        kv = lax.dot_general(kc, zv, (((0,), (0,)), ((), ())), preferred_element_type=F32)
        state_scr[...] = state_scr[...] * cd + kv

    def left_to_right(cidx):
        qc, kc, vc, start = chunk(cidx)
        scores = lax.dot_general(qc, kc, (((1,), (1,)), ((), ())), preferred_element_type=F32) * decay
        o = jnp.dot(scores.astype(BF16), vc, preferred_element_type=F32)
        o = o + xi_f * jnp.dot(qc, state_f_scr[...].astype(BF16), preferred_element_type=F32)
        update_state(state_f_scr, kc, vc, zeta_f, cd_f)
        return o, start

    def right_to_left(cidx):
        qc, kc, vc, start = chunk(cidx)
        o = xi_b * jnp.dot(qc, state_b_scr[...].astype(BF16), preferred_element_type=F32)
        update_state(state_b_scr, kc, vc, zeta_b, cd_b)
        return o, start

    def finish(y, start):
        y = y * lax.rsqrt(jnp.mean(y * y, axis=-1, keepdims=True) + EPS) * gain_ref[...]
        o_ref[pl.ds(start, c_len), :] = y.astype(BF16)

    state_f_scr[...] = jnp.zeros_like(state_f_scr)
    state_b_scr[...] = jnp.zeros_like(state_b_scr)
    half = n_chunks // 2

    def first_visits(t, carry):
        o_f, start_f = left_to_right(t)
        o_b, start_b = right_to_left(n_chunks - 1 - t)
        acc_scr[pl.ds(start_f, c_len), :] = o_f
        acc_scr[pl.ds(start_b, c_len), :] = o_b
        return carry

    def second_visits(t, carry):
        o_f, start_f = left_to_right(t)
        o_b, start_b = right_to_left(n_chunks - 1 - t)
        finish(acc_scr[pl.ds(start_f, c_len), :] + o_f, start_f)
        finish(acc_scr[pl.ds(start_b, c_len), :] + o_b, start_b)
        return carry

    lax.fori_loop(0, half, first_visits, 0)
    lax.fori_loop(half, n_chunks, second_visits, 0)


def _retention_mixer(x, mod, w_in, decay_logit, gn_gain, w_o, batch, seq):
    n, d = x.shape
    qk_cols = RET_HEADS * RET_QK_DIM
    v_cols = RET_HEADS * RET_V_DIM
    col_scale = jnp.concatenate([jnp.ones((qk_cols,), F32), jnp.full((qk_cols,), RET_QK_DIM ** -0.5, F32),
                                 jnp.ones((2 * v_cols,), F32)])
    w = (w_in * col_scale[None, :]).astype(BF16)
    pos = jnp.arange(seq, dtype=F32)
    inv = 1.0 / (RET_THETA ** jnp.linspace(0.0, 1.0, RET_QK_DIM // 2, dtype=F32))
    ang = pos[:, None] * inv
    tm, tn = 512, qk_cols
    tiles_per_seq = seq // tm
    half = RET_QK_DIM // 2
    extras = [jnp.cos(ang), jnp.sin(ang)]
    extra_specs = [pl.BlockSpec((tm, half), lambda i, j: (i % tiles_per_seq, 0)),
                   pl.BlockSpec((tm, half), lambda i, j: (i % tiles_per_seq, 0))]
    proj = _project(x, mod, 0, w, extras, extra_specs, _ret_proj_epilogue,
                    jax.ShapeDtypeStruct((n, w.shape[1]), BF16), pl.BlockSpec((tm, w.shape[1]), lambda i, j: (i, 0)),
                    tm, tn, seq, "ret_proj")

    n_chunks = seq // RET_CHUNK
    k_off = qk_cols // RET_QK_DIM
    v_off = 2 * qk_cols // RET_V_DIM
    y = pl.pallas_call(
        functools.partial(_ret_kernel, n_chunks=n_chunks),
        grid=(batch, RET_HEADS),
        in_specs=[pl.BlockSpec(memory_space=pltpu.SMEM),
                  pl.BlockSpec((seq, RET_QK_DIM), lambda b, h: (b, h)),
                  pl.BlockSpec((seq, RET_QK_DIM), lambda b, h: (b, k_off + h)),
                  pl.BlockSpec((seq, RET_V_DIM), lambda b, h: (b, v_off + h)),
                  pl.BlockSpec((1, RET_V_DIM), lambda b, h: (0, 0))],
        out_specs=pl.BlockSpec((seq, RET_V_DIM), lambda b, h: (b, h)),
        out_shape=jax.ShapeDtypeStruct((n, v_cols), BF16),
        scratch_shapes=[pltpu.VMEM((seq, RET_V_DIM), F32), pltpu.VMEM((RET_QK_DIM, RET_V_DIM), F32),
                        pltpu.VMEM((RET_QK_DIM, RET_V_DIM), F32)],
        compiler_params=_cparams("parallel", "arbitrary"),
        name="retention",
    )(decay_logit.astype(F32), proj, proj, proj, gn_gain.reshape(1, RET_V_DIM).astype(F32))
    gate_col = (2 * qk_cols + v_cols) // v_cols
    return _out_project(y, 0, v_cols, w_o.astype(BF16), x, mod, 0, seq, 512, "ret_out",
                        gate_src=proj, gate_col=gate_col)


def _conv_pw1_epilogue(acc, j, extras, outs):
    b_ref = extras[0]
    u = acc + b_ref[...]
    dm = u.shape[1] // 2
    outs[0][...] = (u[:, :dm] * jax.nn.sigmoid(u[:, dm:])).astype(BF16)


def _conv_kernel(prev_ref, cur_ref, next_ref, wdw_ref, bdw_ref, lng_ref, lnb_ref, w2_ref, b2_ref, x_ref, mod_ref,
                 o_ref, ext_scr, shift_scr, c_scr, *, tiles_per_seq, row_block):
    tm = cur_ref.shape[0]
    t = pl.program_id(0) % tiles_per_seq
    keep_prev = jnp.where(t == 0, 0.0, 1.0)
    keep_next = jnp.where(t == tiles_per_seq - 1, 0.0, 1.0)
    ext_scr[0:CONV_HALO, :] = prev_ref[...].astype(F32) * keep_prev
    ext_scr[CONV_HALO:CONV_HALO + tm, :] = cur_ref[...].astype(F32)
    ext_scr[CONV_HALO + tm:, :] = next_ref[...].astype(F32) * keep_next
    base = CONV_HALO - CONV_PAD
    span = tm + 2 * CONV_HALO - SUBLANES
    for r in range(1, SUBLANES):
        shift_scr[r - 1, 0:span, :] = ext_scr[r:r + span, :]
    for rb in range(tm // row_block):
        r0 = rb * row_block
        acc = jnp.zeros((row_block, cur_ref.shape[1]), F32) + bdw_ref[...]
        for j in range(CONV_WIDTH):
            phase = (base + j) % SUBLANES
            start = r0 + base + j - phase
            if phase == 0:
                tap = ext_scr[start:start + row_block, :]
            else:
                tap = shift_scr[phase - 1, start:start + row_block, :]
            acc = acc + wdw_ref[j:j + 1, :] * tap
        c_scr[r0:r0 + row_block, :] = acc
    u = c_scr[...]
    mu = jnp.mean(u, axis=-1, keepdims=True)
    var = jnp.mean(jnp.square(u - mu), axis=-1, keepdims=True)
    u = (u - mu) * lax.rsqrt(var + EPS) * lng_ref[...] + lnb_ref[...]
    u = _silu(u).astype(BF16)
    z = jnp.dot(u, w2_ref[...], preferred_element_type=F32) + b2_ref[...]
    o_ref[...] = x_ref[...] + mod_ref[0][2:3] * z


def _conv_mixer(x, mod, w_pw1, b_pw1, w_dw, b_dw, ln_gain, ln_bias, w_pw2, b_pw2, seq):
    n, d = x.shape
    tm = 512
    u = _project(x, mod, 0, w_pw1.astype(BF16), [b_pw1.reshape(1, 2 * d).astype(F32)],
                 [pl.BlockSpec((1, 2 * d), lambda i, j: (0, 0))], _conv_pw1_epilogue,
                 jax.ShapeDtypeStruct((n, d), BF16), pl.BlockSpec((tm, d), lambda i, j: (i, 0)),
                 tm, 2 * d, seq, "conv_pw1")
    tm = 256
    tiles_per_seq = seq // tm
    halo_per_tile = tm // CONV_HALO
    n_halo = n // CONV_HALO
    wdw = jnp.pad(w_dw.reshape(CONV_WIDTH, d).astype(F32), ((0, 1), (0, 0)))
    vec = lambda a: a.reshape(1, d).astype(F32)
    row_spec = pl.BlockSpec((1, d), lambda i: (0, 0))
    return pl.pallas_call(
        functools.partial(_conv_kernel, tiles_per_seq=tiles_per_seq, row_block=32),
        grid=(n // tm,),
        in_specs=[pl.BlockSpec((CONV_HALO, d), lambda i: (jnp.maximum(i * halo_per_tile - 1, 0), 0)),
                  pl.BlockSpec((tm, d), lambda i: (i, 0)),
                  pl.BlockSpec((CONV_HALO, d), lambda i: (jnp.minimum((i + 1) * halo_per_tile, n_halo - 1), 0)),
                  pl.BlockSpec((CONV_WIDTH + 1, d), lambda i: (0, 0)),
                  row_spec, row_spec, row_spec,
                  pl.BlockSpec((d, d), lambda i: (0, 0)),
                  row_spec,
                  pl.BlockSpec((tm, d), lambda i: (i, 0)),
                  pl.BlockSpec((1, 6, d), lambda i: (i // tiles_per_seq, 0, 0))],
        out_specs=pl.BlockSpec((tm, d), lambda i: (i, 0)),
        out_shape=jax.ShapeDtypeStruct((n, d), F32),
        scratch_shapes=[pltpu.VMEM((tm + 2 * CONV_HALO, d), F32),
                        pltpu.VMEM((SUBLANES - 1, tm + 2 * CONV_HALO, d), F32), pltpu.VMEM((tm, d), F32)],
        compiler_params=_cparams("parallel"),
        name="conv_module",
    )(u, u, u, wdw, vec(b_dw), vec(ln_gain), vec(ln_bias), w_pw2.astype(BF16), vec(b_pw2), x, mod)


def _t5_bucket(rel):
    half = T5_BUCKETS // 2
    exact = half // 2
    n = jnp.abs(rel)
    log_ratio = jnp.log(jnp.maximum(n, 1).astype(F32) / exact) / math.log(T5_MAX_DIST / exact)
    large = jnp.minimum(exact + (log_ratio * (half - exact)).astype(jnp.int32), half - 1)
    return jnp.where(rel > 0, half, 0) + jnp.where(n < exact, n, large)


def _bias_tile_kernel(table_ref, idx_ref, o_ref):
    h = pl.program_id(0)
    idx = idx_ref[0]
    acc = jnp.zeros(idx.shape, F32)
    for b in range(T5_BUCKETS):
        acc = jnp.where(idx == b, table_ref[b, h], acc)
    o_ref[0, 0] = acc * LOG2_E


def _diff_kernel(q_ref, k_ref, v_ref, bias_ref, lam_ref, gain_ref, o_ref, s_scr, m_scr, acc_scr, *, n_chunks, out_scale,
                 lambda_init):
    tq = q_ref.shape[0]
    qi = pl.program_id(2)
    q = q_ref[...]
    lane = lax.broadcasted_iota(jnp.int32, (tq, LANES), 1)
    low = lane < DIFF_HEAD_DIM
    zero = jnp.zeros((tq, LANES), BF16)
    lhs = jnp.concatenate([jnp.where(low, q, zero), jnp.where(low, zero, q)], axis=0)

    q_sub = tq // KEY_CHUNK

    def add_bias(s, j):
        tiles = [bias_ref[0, jnp.clip(j - (q_sub * qi + a), -2, 2) + 2] for a in range(q_sub)]
        return s + jnp.concatenate(tiles + tiles, axis=0)

    o = _softmax_pv(lhs, k_ref, v_ref, s_scr, m_scr, acc_scr, n_chunks, add_bias)
    lam_v = lam_ref[...]
    lam = (jnp.exp(jnp.sum(lam_v[0:1] * lam_v[1:2], axis=-1, keepdims=True))
           - jnp.exp(jnp.sum(lam_v[2:3] * lam_v[3:4], axis=-1, keepdims=True)) + lambda_init)
    y = o[:tq] - lam * o[tq:]
    y = y * lax.rsqrt(jnp.mean(y * y, axis=-1, keepdims=True) + EPS) * gain_ref[...] * out_scale
    o_ref[...] = y.astype(BF16)


def _diff_mixer(x, mod, w_qkv, lam_q1, lam_k1, lam_q2, lam_k2, subln_gain, w_o, rel_bias, lambda_init,
                batch, seq):
    n, d = x.shape
    col_scale = jnp.concatenate([jnp.full((d,), DIFF_HEAD_DIM ** -0.5 * LOG2_E, F32), jnp.ones((2 * d,), F32)])
    w = (w_qkv * col_scale[None, :]).astype(BF16)
    tm = 1024
    qkv = _project(x, mod, 0, w, [], [], _plain_epilogue,
                   jax.ShapeDtypeStruct((n, 3 * d), BF16), pl.BlockSpec((tm, 3 * d), lambda i, j: (i, 0)),
                   tm, d, seq, "diff_qkv")

    tb = KEY_CHUNK
    ar = jnp.arange(KEY_CHUNK, dtype=jnp.int32)
    offs = (jnp.arange(5, dtype=jnp.int32) - 2) * KEY_CHUNK
    rel = (ar[None, None, :] + offs[:, None, None]) - ar[None, :, None]
    bucket = _t5_bucket(rel).astype(jnp.int32)
    bias = pl.pallas_call(
        _bias_tile_kernel,
        grid=(DIFF_HEADS, 5),
        in_specs=[pl.BlockSpec(memory_space=pltpu.SMEM),
                  pl.BlockSpec((1, tb, KEY_CHUNK), lambda h, t: (t, 0, 0))],
        out_specs=pl.BlockSpec((1, 1, tb, KEY_CHUNK), lambda h, t: (h, t, 0, 0)),
        out_shape=jax.ShapeDtypeStruct((DIFF_HEADS, 5, tb, KEY_CHUNK), F32),
        compiler_params=_cparams("arbitrary", "arbitrary"),
        name="rel_bias_tiles",
    )(rel_bias.astype(F32), bucket)

    lam = jnp.stack([lam_q1, lam_k1, lam_q2, lam_k2]).astype(F32)
    n_chunks = seq // KEY_CHUNK
    tq = 2 * KEY_CHUNK
    q_tiles = seq // tq
    o = pl.pallas_call(
        functools.partial(_diff_kernel, n_chunks=n_chunks, out_scale=1.0 - lambda_init, lambda_init=lambda_init),
        grid=(batch, DIFF_HEADS, q_tiles),
        in_specs=[pl.BlockSpec((tq, LANES), lambda b, h, i: (b * q_tiles + i, h)),
                  pl.BlockSpec((seq, LANES), lambda b, h, i: (b, DIFF_HEADS + h)),
                  pl.BlockSpec((seq, LANES), lambda b, h, i: (b, 2 * DIFF_HEADS + h)),
                  pl.BlockSpec((1, 5, tb, KEY_CHUNK), lambda b, h, i: (h, 0, 0, 0)),
                  pl.BlockSpec((4, DIFF_HEAD_DIM), lambda b, h, i: (0, 0)),
                  pl.BlockSpec((1, LANES), lambda b, h, i: (0, 0))],
        out_specs=pl.BlockSpec((tq, LANES), lambda b, h, i: (b * q_tiles + i, h)),
        out_shape=jax.ShapeDtypeStruct((n, d), BF16),
        scratch_shapes=[pltpu.VMEM((n_chunks, 2 * tq, KEY_CHUNK), F32),
                        pltpu.VMEM((2 * tq, LANES), F32), pltpu.VMEM((2 * tq, 2 * LANES), F32)],
        compiler_params=_cparams("parallel", "parallel", "arbitrary"),
        name="diff_attention",
    )(qkv, qkv, qkv, bias, lam, subln_gain.reshape(1, LANES).astype(F32))
    return _out_project(o, 0, d, w_o.astype(BF16), x, mod, 0, seq, 1024, "diff_out")


def _final_norm_kernel(x_ref, g_ref, o_ref):
    x = x_ref[...]
    o_ref[...] = x * lax.rsqrt(jnp.mean(x * x, axis=-1, keepdims=True) + EPS) * g_ref[...]


def _final_norm(x, gain, tm):
    n, d = x.shape
    return pl.pallas_call(
        _final_norm_kernel,
        grid=(n // tm,),
        in_specs=[pl.BlockSpec((tm, d), lambda i: (i, 0)), pl.BlockSpec((1, d), lambda i: (0, 0))],
        out_specs=pl.BlockSpec((tm, d), lambda i: (i, 0)),
        out_shape=jax.ShapeDtypeStruct((n, d), F32),
        compiler_params=_cparams("parallel"),
        name="final_norm",
    )(x, gain.reshape(1, d).astype(F32))


def kernel(x, c, mod_w, mod_b, att_w_qkv, att_q_gain, att_k_gain, att_w_o, ret_w_in, ret_decay_logit, ret_gn_gain, ret_w_o, cnv_w_pw1, cnv_b_pw1, cnv_w_dw, cnv_b_dw, cnv_ln_gain, cnv_ln_bias, cnv_w_pw2, cnv_b_pw2, dif_w_qkv, dif_lam_q1, dif_lam_k1, dif_lam_q2, dif_lam_k2, dif_subln_gain, dif_w_o, rel_bias, ffn_w_gate, ffn_w_up, ffn_w_down, moe_w_router, moe_b_router, moe_w_gate, moe_w_up, moe_w_down, final_gain):
    batch, seq, d = x.shape
    depth = mod_w.shape[0]
    xf = x.reshape(batch * seq, d)
    mods = _modulation(c, mod_w, mod_b)
    moe_gate, moe_up, moe_down = moe_w_gate.astype(BF16), moe_w_up.astype(BF16), moe_w_down.astype(BF16)
    for i in range(depth):
        mod = mods[i]
        kind, j = i % 4, i // 4
        if kind == 0:
            xf = _attention_mixer(xf, mod, att_w_qkv[j], att_q_gain[j], att_k_gain[j], att_w_o[j], batch, seq)
        elif kind == 1:
            xf = _retention_mixer(xf, mod, ret_w_in[j], ret_decay_logit[j], ret_gn_gain[j], ret_w_o[j], batch, seq)
        elif kind == 2:
            xf = _conv_mixer(xf, mod, cnv_w_pw1[j], cnv_b_pw1[j], cnv_w_dw[j], cnv_b_dw[j], cnv_ln_gain[j],
                             cnv_ln_bias[j], cnv_w_pw2[j], cnv_b_pw2[j], seq)
        else:
            lambda_init = 0.8 - 0.6 * math.exp(-0.3 * i)
            xf = _diff_mixer(xf, mod, dif_w_qkv[j], dif_lam_q1[j], dif_lam_k1[j], dif_lam_q2[j], dif_lam_k2[j],
                             dif_subln_gain[j], dif_w_o[j], rel_bias, lambda_init, batch, seq)
        if i % 2 == 0:
            m = i // 2
            xf = _ffn(xf, mod, ffn_w_gate[m].astype(BF16), ffn_w_up[m].astype(BF16), ffn_w_down[m].astype(BF16),
                      seq, 512)
        else:
            m = i // 2
            xf = _moe(xf, mod, moe_w_router[m], moe_b_router[m], moe_gate, moe_up, moe_down, m, seq,
                      out_gain=final_gain if i == depth - 1 else None)
    if depth % 2 == 1:
        xf = _final_norm(xf, final_gain, 512)
    return xf.reshape(batch, seq, d)
```

```python
import functools
import math

import jax
import jax.numpy as jnp
from jax import lax
from jax.experimental import pallas as pl
from jax.experimental.pallas import tpu as pltpu

F32 = jnp.float32
BF16 = jnp.bfloat16

EPS = 1e-6
GRID_W = 64

ATT_HEADS = 16
ATT_KV_HEADS = 4
ATT_HEAD_DIM = 64
ROPE_THETA = 10000.0

RET_HEADS = 4
RET_QK_DIM = 256
RET_V_DIM = 512
RET_THETA = 10000.0
RET_CHUNK = 256

CONV_WIDTH = 31
CONV_PAD = CONV_WIDTH // 2
CONV_HALO = 16

DIFF_HEADS = 8
DIFF_HEAD_DIM = 64
T5_BUCKETS = 32
T5_MAX_DIST = 128

N_EXPERTS = 8
LANES = 128
SUBLANES = 8
KEY_CHUNK = 256
LOG2_E = 1.4426950408889634
VMEM_LIMIT_BYTES = 56 * 1024 * 1024


def _cparams(*sem):
    return pltpu.CompilerParams(dimension_semantics=sem, vmem_limit_bytes=VMEM_LIMIT_BYTES)


def _silu(v):
    return v * jax.nn.sigmoid(v)


def _norm_mod(x, mod, which):
    shift = mod[3 * which:3 * which + 1]
    scale = mod[3 * which + 1:3 * which + 2]
    y = x * lax.rsqrt(jnp.mean(x * x, axis=-1, keepdims=True) + EPS)
    return y * (1.0 + scale) + shift


def _split_dot(a, b):
    hi = a.astype(BF16)
    lo = (a - hi.astype(F32)).astype(BF16)
    return (jnp.dot(hi, b, preferred_element_type=F32) + jnp.dot(lo, b, preferred_element_type=F32))


def _mod_kernel(c_ref, w_ref, b_ref, o_ref):
    cond = _silu(c_ref[...])
    o_ref[0] = jnp.dot(cond, w_ref[0], preferred_element_type=F32, precision=lax.Precision.HIGHEST) + b_ref[0]


def _modulation(c, mod_w, mod_b):
    depth, d, d6 = mod_w.shape
    b = c.shape[0]
    tn = d
    out = pl.pallas_call(
        _mod_kernel,
        grid=(depth, d6 // tn),
        in_specs=[pl.BlockSpec((b, d), lambda i, j: (0, 0)),
                  pl.BlockSpec((1, d, tn), lambda i, j: (i, 0, j)),
                  pl.BlockSpec((1, 1, tn), lambda i, j: (i, 0, j))],
        out_specs=pl.BlockSpec((1, b, tn), lambda i, j: (i, 0, j)),
        out_shape=jax.ShapeDtypeStruct((depth, b, d6), F32),
        compiler_params=_cparams("arbitrary", "arbitrary"),
        name="modulation",
    )(c, mod_w, mod_b.reshape(depth, 1, d6))
    return out.reshape(depth, b, 6, d)


def _proj_kernel(x_ref, mod_ref, w_ref, *rest, which, n_extra, epilogue, tn):
    extras = rest[:n_extra]
    outs = rest[n_extra:]
    h = _norm_mod(x_ref[...], mod_ref[0], which).astype(BF16)
    for j in range(w_ref.shape[1] // tn):
        acc = jnp.dot(h, w_ref[:, j * tn:(j + 1) * tn], preferred_element_type=F32)
        epilogue(acc, j, extras, outs)


def _project(x, mod, which, w, extras, extra_specs, epilogue, out_shapes, out_specs, tm, tn, seq, name):
    n, d = x.shape
    c = w.shape[1]
    tiles_per_seq = seq // tm
    kern = functools.partial(_proj_kernel, which=which, n_extra=len(extras), epilogue=epilogue, tn=tn)
    return pl.pallas_call(
        kern,
        grid=(n // tm, 1),
        in_specs=[pl.BlockSpec((tm, d), lambda i, j: (i, 0)),
                  pl.BlockSpec((1, 6, d), lambda i, j: (i // tiles_per_seq, 0, 0)),
                  pl.BlockSpec((d, c), lambda i, j: (0, 0), pipeline_mode=pl.Buffered(1))] + extra_specs,
        out_specs=out_specs,
        out_shape=out_shapes,
        compiler_params=_cparams("parallel", "arbitrary"),
        name=name,
    )(x, mod, w, *extras)


def _plain_epilogue(acc, j, extras, outs):
    tn = acc.shape[1]
    outs[0][:, j * tn:(j + 1) * tn] = acc.astype(outs[0].dtype)


def _oproj_kernel(*refs, which, gated, has_bias):
    refs = list(refs)
    y_ref = refs.pop(0)
    g_ref = refs.pop(0) if gated else None
    w_ref = refs.pop(0)
    b_ref = refs.pop(0) if has_bias else None
    x_ref, mod_ref, o_ref = refs
    y = y_ref[...]
    if gated:
        y = (y.astype(F32) * _silu(g_ref[...].astype(F32))).astype(BF16)
    z = jnp.dot(y, w_ref[...], preferred_element_type=F32)
    if has_bias:
        z = z + b_ref[...]
    gate = mod_ref[0][3 * which + 2:3 * which + 3]
    o_ref[...] = x_ref[...] + gate * z


def _out_project(y, y_col, k, w, x, mod, which, seq, tm, name, gate_src=None, gate_col=0, bias=None):
    n, d = x.shape
    tiles_per_seq = seq // tm
    args = [y]
    specs = [pl.BlockSpec((tm, k), lambda i: (i, y_col))]
    if gate_src is not None:
        args.append(gate_src)
        specs.append(pl.BlockSpec((tm, k), lambda i: (i, gate_col)))
    args.append(w)
    specs.append(pl.BlockSpec((k, d), lambda i: (0, 0)))
    if bias is not None:
        args.append(bias.reshape(1, d))
        specs.append(pl.BlockSpec((1, d), lambda i: (0, 0)))
    args += [x, mod]
    specs += [pl.BlockSpec((tm, d), lambda i: (i, 0)),
              pl.BlockSpec((1, 6, d), lambda i: (i // tiles_per_seq, 0, 0))]
    kern = functools.partial(_oproj_kernel, which=which, gated=gate_src is not None, has_bias=bias is not None)
    return pl.pallas_call(
        kern,
        grid=(n // tm,),
        in_specs=specs,
        out_specs=pl.BlockSpec((tm, d), lambda i: (i, 0)),
        out_shape=jax.ShapeDtypeStruct((n, d), F32),
        compiler_params=_cparams("parallel"),
        name=name,
    )(*args)


def _ffn_kernel(x_ref, mod_ref, wg_ref, wu_ref, wd_ref, o_ref):
    x = x_ref[...]
    mod = mod_ref[0]
    h = _norm_mod(x, mod, 1).astype(BF16)
    g = jnp.dot(h, wg_ref[...], preferred_element_type=F32)
    u = jnp.dot(h, wu_ref[...], preferred_element_type=F32)
    a = (_silu(g) * u).astype(BF16)
    o_ref[...] = x + mod[5:6] * jnp.dot(a, wd_ref[...], preferred_element_type=F32)


def _ffn(x, mod, wg, wu, wd, seq, tm):
    n, d = x.shape
    ff = wg.shape[1]
    tiles_per_seq = seq // tm
    resident = pl.Buffered(1)
    return pl.pallas_call(
        _ffn_kernel,
        grid=(n // tm,),
        in_specs=[pl.BlockSpec((tm, d), lambda i: (i, 0)),
                  pl.BlockSpec((1, 6, d), lambda i: (i // tiles_per_seq, 0, 0)),
                  pl.BlockSpec((d, ff), lambda i: (0, 0), pipeline_mode=resident),
                  pl.BlockSpec((d, ff), lambda i: (0, 0), pipeline_mode=resident),
                  pl.BlockSpec((ff, d), lambda i: (0, 0), pipeline_mode=resident)],
        out_specs=pl.BlockSpec((tm, d), lambda i: (i, 0)),
        out_shape=jax.ShapeDtypeStruct((n, d), F32),
        compiler_params=_cparams("parallel"),
        name="dense_ffn",
    )(x, mod, wg, wu, wd)


def _router_kernel(x_ref, mod_ref, wr_ref, br_ref, tri_ref, route_ref, count_ref, carry_scr):
    @pl.when(pl.program_id(0) == 0)
    def _():
        carry_scr[...] = jnp.zeros_like(carry_scr)

    h = _norm_mod(x_ref[...], mod_ref[0], 1)
    logits = jnp.dot(h, wr_ref[...], preferred_element_type=F32, precision=lax.Precision.HIGHEST) + br_ref[...]
    lane = lax.broadcasted_iota(jnp.int32, logits.shape, 1)
    neg = jnp.float32(-jnp.inf)
    logits = jnp.where(lane < N_EXPERTS, logits, neg)
    m1 = jnp.max(logits, axis=-1, keepdims=True)
    i1 = jnp.min(jnp.where(logits == m1, lane, LANES), axis=-1, keepdims=True)
    rest = jnp.where(lane == i1, neg, logits)
    m2 = jnp.max(rest, axis=-1, keepdims=True)
    i2 = jnp.min(jnp.where(rest == m2, lane, LANES), axis=-1, keepdims=True)
    e2 = jnp.exp(m2 - m1)
    w1 = 1.0 / (1.0 + e2)
    w2 = e2 / (1.0 + e2)
    sel = jnp.where((lane == i1) | (lane == i2), 1.0, 0.0)
    incl = jnp.dot(tri_ref[...], sel.astype(BF16), preferred_element_type=F32)
    excl = incl - sel + carry_scr[...]
    r1 = jnp.sum(jnp.where(lane == i1, excl, 0.0), axis=-1, keepdims=True)
    r2 = jnp.sum(jnp.where(lane == i2, excl, 0.0), axis=-1, keepdims=True)
    carry_scr[...] += incl[incl.shape[0] - 1:, :]
    count_ref[...] = carry_scr[...]
    fields = (i1.astype(F32), i2.astype(F32), w1, w2, r1, r2)
    route = jnp.zeros(logits.shape, F32)
    for k, field in enumerate(fields):
        route = jnp.where(lane == k, field, route)
    route_ref[...] = route


def _router(x, mod, w_router, b_router, seq, tm):
    n, d = x.shape
    tiles_per_seq = seq // tm
    wr = jnp.pad(w_router, ((0, 0), (0, LANES - N_EXPERTS)))
    br = jnp.pad(b_router, (0, LANES - N_EXPERTS)).reshape(1, LANES)
    tri = (jnp.arange(tm)[:, None] >= jnp.arange(tm)[None, :]).astype(BF16)
    return pl.pallas_call(
        _router_kernel,
        grid=(n // tm,),
        in_specs=[pl.BlockSpec((tm, d), lambda i: (i, 0)),
                  pl.BlockSpec((1, 6, d), lambda i: (i // tiles_per_seq, 0, 0)),
                  pl.BlockSpec((d, LANES), lambda i: (0, 0)),
                  pl.BlockSpec((1, LANES), lambda i: (0, 0)),
                  pl.BlockSpec((tm, tm), lambda i: (0, 0))],
        out_specs=[pl.BlockSpec((tm, LANES), lambda i: (i, 0)), pl.BlockSpec((1, LANES), lambda i: (0, 0))],
        out_shape=[jax.ShapeDtypeStruct((n, LANES), F32), jax.ShapeDtypeStruct((1, LANES), F32)],
        scratch_shapes=[pltpu.VMEM((1, LANES), F32)],
        compiler_params=_cparams("arbitrary"),
        name="moe_router",
    )(x, mod, wr, br, tri)


def _dispatch_kernel(zero_rows_ref, d1_ref, d2_ref, x_ref, mod_ref, xs_ref, h_scr, zero_scr, sems):
    tm = x_ref.shape[0]

    @pl.when(pl.program_id(0) == 0)
    def _():
        zero_scr[...] = jnp.zeros_like(zero_scr)
        te = zero_scr.shape[0]
        for k in range(zero_rows_ref.shape[0]):
            start = pl.multiple_of(zero_rows_ref[k], te)
            copy = pltpu.make_async_copy(zero_scr, xs_ref.at[pl.ds(start, te), :], sems.at[2])
            copy.start()
            copy.wait()

    h_scr[...] = _norm_mod(x_ref[...], mod_ref[0], 1)

    def row_copy(r, dest, which):
        return pltpu.make_async_copy(h_scr.at[pl.ds(r, 1), :], xs_ref.at[pl.ds(dest, 1), :], sems.at[which])

    def issue(r, carry):
        row_copy(r, d1_ref[0, 0, r], 0).start()
        row_copy(r, d2_ref[0, 0, r], 1).start()
        return carry

    lax.fori_loop(0, tm, issue, 0, unroll=4)
    for which in range(2):
        pltpu.make_async_copy(h_scr, xs_ref.at[pl.ds(0, tm), :], sems.at[which]).wait()


def _expert_kernel(tile_e_ref, n_act_ref, x_ref, wg_ref, wu_ref, wd_ref, o_ref, *, tf):
    del tile_e_ref
    s = pl.program_id(0)

    @pl.when(s < n_act_ref[0])
    def _():
        h = x_ref[...].astype(BF16)
        out = None
        for f in range(wg_ref.shape[3] // tf):
            cols = slice(f * tf, (f + 1) * tf)
            g = jnp.dot(h, wg_ref[0, 0, :, cols], preferred_element_type=F32)
            u = jnp.dot(h, wu_ref[0, 0, :, cols], preferred_element_type=F32)
            a = (_silu(g) * u).astype(BF16)
            part = jnp.dot(a, wd_ref[0, 0, cols, :], preferred_element_type=F32)
            out = part if out is None else out + part
        o_ref[...] = out

    @pl.when(s >= n_act_ref[0])
    def _():
        o_ref[...] = jnp.zeros_like(o_ref)


def _combine_kernel(d1_ref, d2_ref, route_ref, x_ref, mod_ref, *rest, out_norm):
    if out_norm:
        gain_ref, ys_ref, o_ref, buf, sems = rest
    else:
        ys_ref, o_ref, buf, sems = rest
    tm = x_ref.shape[0]

    def row_copy(r, src, which):
        return pltpu.make_async_copy(ys_ref.at[pl.ds(src, 1), :], buf.at[which, pl.ds(r, 1), :], sems.at[which])

    def issue(r, carry):
        row_copy(r, d1_ref[0, 0, r], 0).start()
        row_copy(r, d2_ref[0, 0, r], 1).start()
        return carry

    lax.fori_loop(0, tm, issue, 0, unroll=4)
    for which in range(2):
        pltpu.make_async_copy(ys_ref.at[pl.ds(0, tm), :], buf.at[which], sems.at[which]).wait()
    route = route_ref[...]
    y = route[:, 2:3] * buf[0] + route[:, 3:4] * buf[1]
    out = x_ref[...] + mod_ref[0][5:6] * y
    if out_norm:
        out = out * lax.rsqrt(jnp.mean(out * out, axis=-1, keepdims=True) + EPS) * gain_ref[...]
    o_ref[...] = out


def _moe(x, mod, w_router, b_router, w_gate, w_up, w_down, layer, seq, out_gain=None):
    n, d = x.shape
    _, n_e, _, ff = w_gate.shape
    tm = 512
    te = 512
    tf = 1792
    tiles_per_seq = seq // tm
    n_tok_tiles = n // tm
    n_slots = 2 * n + n_e * te
    n_tiles = n_slots // te

    route, counts = _router(x, mod, w_router, b_router, seq, tm)

    counts = counts[0, :n_e].astype(jnp.int32)
    padded = ((counts + te - 1) // te) * te
    ends = jnp.cumsum(padded)
    offsets = ends - padded
    n_act = (ends[-1] // te).astype(jnp.int32)
    tile_start = jnp.arange(n_tiles, dtype=jnp.int32) * te
    tile_e = jnp.sum((tile_start[:, None] >= ends[None, :]).astype(jnp.int32), axis=1)
    tile_e = jnp.minimum(tile_e, tile_e[jnp.maximum(n_act - 1, 0)]).astype(jnp.int32)
    e1 = route[:, 0].astype(jnp.int32)
    e2 = route[:, 1].astype(jnp.int32)
    dest1 = (offsets[e1] + route[:, 4].astype(jnp.int32)).reshape(n_tok_tiles, 1, tm)
    dest2 = (offsets[e2] + route[:, 5].astype(jnp.int32)).reshape(n_tok_tiles, 1, tm)
    n_act = n_act.reshape(1)

    slot_spec = pl.BlockSpec((1, 1, tm), lambda i: (i, 0, 0), memory_space=pltpu.SMEM)
    zero_rows = jnp.concatenate([jnp.maximum(ends - te, 0),
                                 (n_tiles - n_e + jnp.arange(n_e, dtype=jnp.int32)) * te]).astype(jnp.int32)
    xs = pl.pallas_call(
        _dispatch_kernel,
        grid=(n_tok_tiles,),
        in_specs=[pl.BlockSpec(memory_space=pltpu.SMEM), slot_spec, slot_spec,
                  pl.BlockSpec((tm, d), lambda i: (i, 0)),
                  pl.BlockSpec((1, 6, d), lambda i: (i // tiles_per_seq, 0, 0))],
        out_specs=pl.BlockSpec(memory_space=pl.ANY),
        out_shape=jax.ShapeDtypeStruct((n_slots, d), F32),
        scratch_shapes=[pltpu.VMEM((tm, d), F32), pltpu.VMEM((te, d), F32), pltpu.SemaphoreType.DMA((3,))],
        compiler_params=_cparams("arbitrary"),
        name="moe_dispatch",
    )(zero_rows, dest1, dest2, x, mod)

    def live(s, n_act_ref):
        return jnp.minimum(s, jnp.maximum(n_act_ref[0] - 1, 0))

    def expert_weights(shape):
        return pl.BlockSpec(shape, lambda s, te_ref, na_ref: (layer, te_ref[s], 0, 0), pipeline_mode=pl.Buffered(1))

    ys = pl.pallas_call(
        functools.partial(_expert_kernel, tf=tf),
        grid_spec=pltpu.PrefetchScalarGridSpec(
            num_scalar_prefetch=2,
            grid=(n_tiles,),
            in_specs=[pl.BlockSpec((te, d), lambda s, te_ref, na_ref: (live(s, na_ref), 0)),
                      expert_weights((1, 1, d, ff)), expert_weights((1, 1, d, ff)), expert_weights((1, 1, ff, d))],
            out_specs=pl.BlockSpec((te, d), lambda s, te_ref, na_ref: (s, 0))),
        out_shape=jax.ShapeDtypeStruct((n_slots, d), F32),
        compiler_params=_cparams("arbitrary"),
        name="moe_experts",
    )(tile_e, n_act, xs, w_gate, w_up, w_down)

    out_norm = out_gain is not None
    gain_args = [out_gain.reshape(1, d).astype(F32)] if out_norm else []
    gain_specs = [pl.BlockSpec((1, d), lambda i: (0, 0))] if out_norm else []
    return pl.pallas_call(
        functools.partial(_combine_kernel, out_norm=out_norm),
        grid=(n_tok_tiles,),
        in_specs=[slot_spec, slot_spec,
                  pl.BlockSpec((tm, LANES), lambda i: (i, 0)),
                  pl.BlockSpec((tm, d), lambda i: (i, 0)),
                  pl.BlockSpec((1, 6, d), lambda i: (i // tiles_per_seq, 0, 0))] + gain_specs
                 + [pl.BlockSpec(memory_space=pl.ANY)],
        out_specs=pl.BlockSpec((tm, d), lambda i: (i, 0)),
        out_shape=jax.ShapeDtypeStruct((n, d), F32),
        scratch_shapes=[pltpu.VMEM((2, tm, d), F32), pltpu.SemaphoreType.DMA((2,))],
        compiler_params=_cparams("arbitrary"),
        name="moe_combine",
    )(dest1, dest2, route, x, mod, *gain_args, ys)


def _softmax_pv(lhs, k_ref, v_ref, s_scr, m_scr, acc_scr, n_chunks, bias_fn=None):
    rows = lhs.shape[0]
    group = min(8, n_chunks)
    m_scr[...] = jnp.full(m_scr.shape, -jnp.inf, F32)

    for j in range(n_chunks):
        kc = k_ref[j * KEY_CHUNK:(j + 1) * KEY_CHUNK, :]
        s = lax.dot_general(lhs, kc, (((1,), (1,)), ((), ())), preferred_element_type=F32)
        if bias_fn is not None:
            s = bias_fn(s, j)
        s_scr[j] = s
        m_scr[...] = jnp.maximum(m_scr[...], jnp.maximum(s[:, :LANES], s[:, LANES:]))
    m_scr[...] = jnp.broadcast_to(jnp.max(m_scr[...], axis=-1, keepdims=True), m_scr.shape)
    acc_scr[...] = jnp.zeros_like(acc_scr)
    ones = jnp.ones((group * KEY_CHUNK, LANES), BF16)

    def pv_body(t, carry):
        mb = m_scr[...]
        parts = []
        for u in range(group):
            s = s_scr[group * t + u]
            parts.append(jnp.exp2(s[:, :LANES] - mb).astype(BF16))
            parts.append(jnp.exp2(s[:, LANES:] - mb).astype(BF16))
        p = jnp.concatenate(parts, axis=1)
        start = pl.multiple_of(t * (group * KEY_CHUNK), group * KEY_CHUNK)
        v_aug = jnp.concatenate([v_ref[pl.ds(start, group * KEY_CHUNK), :], ones], axis=1)
        acc_scr[...] += jnp.dot(p, v_aug, preferred_element_type=F32)
        return carry

    lax.fori_loop(0, n_chunks // group, pv_body, 0)
    acc = acc_scr[...]
    return acc[:, :LANES] / acc[:, LANES:]


def _att_qkv_epilogue(acc, j, extras, outs, n_qk):
    gain_ref, cos_ref, sin_ref, g1_ref, g2_ref = extras
    q_ref, k_ref, v_ref = outs
    qk = acc[:, :n_qk]
    ms = _split_dot(qk * qk, g1_ref[...])
    r = lax.rsqrt(ms + EPS)
    y = qk * _split_dot(r, g2_ref[...]) * gain_ref[...]
    cos = cos_ref[...]
    sin = sin_ref[...]
    lane = lax.broadcasted_iota(jnp.int32, cos.shape, 1)
    first_half = (lane & (ATT_HEAD_DIM // 2)) == 0
    nq = q_ref.shape[1]
    for c in range(n_qk // LANES):
        yc = y[:, c * LANES:(c + 1) * LANES]
        partner = jnp.where(first_half, pltpu.roll(yc, LANES - ATT_HEAD_DIM // 2, 1),
                            pltpu.roll(yc, ATT_HEAD_DIM // 2, 1))
        out = (yc * cos + partner * sin).astype(BF16)
        if c * LANES < nq:
            q_ref[:, c * LANES:(c + 1) * LANES] = out
        else:
            k_ref[:, c * LANES - nq:(c + 1) * LANES - nq] = out
    v_ref[...] = acc[:, n_qk:].astype(BF16)


def _att_kernel(q_ref, k_ref, v_ref, o_ref, s_scr, m_scr, acc_scr, *, n_chunks):
    tq = q_ref.shape[0]
    q = q_ref[...]
    lane = lax.broadcasted_iota(jnp.int32, (tq, LANES), 1)
    low = lane < ATT_HEAD_DIM
    zero = jnp.zeros((tq, LANES), BF16)
    parts = []
    for pair in range(2):
        qp = q[:, pair * LANES:(pair + 1) * LANES]
        parts.append(jnp.where(low, qp, zero))
        parts.append(jnp.where(low, zero, qp))
    lhs = jnp.concatenate(parts, axis=0)
    o = _softmax_pv(lhs, k_ref, v_ref, s_scr, m_scr, acc_scr, n_chunks)
    for pair in range(2):
        o_ref[:, pair * LANES:(pair + 1) * LANES] = jnp.where(
            low, o[(2 * pair) * tq:(2 * pair + 1) * tq], o[(2 * pair + 1) * tq:(2 * pair + 2) * tq]).astype(BF16)


def _attention_mixer(x, mod, w_qkv, q_gain, k_gain, w_o, batch, seq):
    n, d = x.shape
    hd = ATT_HEAD_DIM
    half = hd // 2
    n_freq = half // 2
    groups = ATT_HEADS // ATT_KV_HEADS
    perm = jnp.concatenate([jnp.arange(n_freq), half + jnp.arange(n_freq),
                            n_freq + jnp.arange(n_freq), half + n_freq + jnp.arange(n_freq)])
    q_cols = (jnp.arange(ATT_HEADS)[:, None] * hd + perm[None, :]).reshape(-1)
    k_base = ATT_HEADS * hd
    v_base = k_base + ATT_KV_HEADS * hd
    kv_rep = jnp.repeat(jnp.arange(ATT_KV_HEADS), 2)
    k_cols = (k_base + kv_rep[:, None] * hd + perm[None, :]).reshape(-1)
    v_cols = (v_base + kv_rep[:, None] * hd + jnp.arange(hd)[None, :]).reshape(-1)
    w = w_qkv[:, jnp.concatenate([q_cols, k_cols, v_cols])].astype(BF16)
    n_q = ATT_HEADS * hd
    n_kv = 2 * ATT_KV_HEADS * hd
    n_qk = n_q + n_kv
    gain_row = jnp.concatenate([jnp.tile(q_gain[perm] * (hd ** -0.5 * LOG2_E), ATT_HEADS),
                                jnp.tile(k_gain[perm], 2 * ATT_KV_HEADS)]).reshape(1, n_qk).astype(F32)
    pos = jnp.arange(seq, dtype=jnp.int32)
    row = (pos // GRID_W).astype(F32)
    col = (pos % GRID_W).astype(F32)
    inv = ROPE_THETA ** (-jnp.arange(n_freq, dtype=F32) / n_freq)
    ang = jnp.concatenate([row[:, None] * inv, col[:, None] * inv], axis=-1)
    cos_t = jnp.tile(jnp.cos(ang), (1, 2 * LANES // hd))
    sin_t = jnp.tile(jnp.concatenate([-jnp.sin(ang), jnp.sin(ang)], axis=-1), (1, LANES // hd))
    grp = jnp.arange(n_qk) // hd
    g1 = (grp[:, None] == jnp.arange(LANES)[None, :]).astype(F32) / hd
    g2 = (jnp.arange(LANES)[:, None] == grp[None, :]).astype(F32)
    tm = 512
    tiles_per_seq = seq // tm
    extras = [gain_row, cos_t, sin_t, g1.astype(BF16), g2.astype(BF16)]
    extra_specs = [pl.BlockSpec((1, n_qk), lambda i, j: (0, 0)),
                   pl.BlockSpec((tm, LANES), lambda i, j: (i % tiles_per_seq, 0)),
                   pl.BlockSpec((tm, LANES), lambda i, j: (i % tiles_per_seq, 0)),
                   pl.BlockSpec((n_qk, LANES), lambda i, j: (0, 0)),
                   pl.BlockSpec((LANES, n_qk), lambda i, j: (0, 0))]
    q, k, v = _project(
        x, mod, 0, w, extras, extra_specs, functools.partial(_att_qkv_epilogue, n_qk=n_qk),
        [jax.ShapeDtypeStruct((n, n_q), BF16), jax.ShapeDtypeStruct((n, n_kv), BF16),
         jax.ShapeDtypeStruct((n, n_kv), BF16)],
        [pl.BlockSpec((tm, n_q), lambda i, j: (i, 0)), pl.BlockSpec((tm, n_kv), lambda i, j: (i, 0)),
         pl.BlockSpec((tm, n_kv), lambda i, j: (i, 0))],
        tm, w.shape[1], seq, "att_qkv")

    tq = 256
    n_chunks = seq // KEY_CHUNK
    q_tiles = seq // tq
    o = pl.pallas_call(
        functools.partial(_att_kernel, n_chunks=n_chunks),
        grid=(batch, ATT_KV_HEADS, q_tiles),
        in_specs=[pl.BlockSpec((tq, groups * hd), lambda b, g, i: (b * q_tiles + i, g)),
                  pl.BlockSpec((seq, LANES), lambda b, g, i: (b, g)),
                  pl.BlockSpec((seq, LANES), lambda b, g, i: (b, g))],
        out_specs=pl.BlockSpec((tq, groups * hd), lambda b, g, i: (b * q_tiles + i, g)),
        out_shape=jax.ShapeDtypeStruct((n, n_q), BF16),
        scratch_shapes=[pltpu.VMEM((n_chunks, groups * tq, KEY_CHUNK), F32),
                        pltpu.VMEM((groups * tq, LANES), F32), pltpu.VMEM((groups * tq, 2 * LANES), F32)],
        compiler_params=_cparams("parallel", "parallel", "arbitrary"),
        name="gqa_attention",
    )(q, k, v)
    return _out_project(o, 0, n_q, w_o.astype(BF16), x, mod, 0, seq, 1024, "att_out")


def _ret_proj_epilogue(acc, j, extras, outs):
    cos_ref, sin_ref = extras
    o_ref = outs[0]
    tn = acc.shape[1]
    base = j * tn
    if j < 2:
        cos = cos_ref[...]
        sin = sin_ref[...]
        half = RET_QK_DIM // 2
        for hh in range(RET_HEADS):
            lo = hh * RET_QK_DIM
            x1 = acc[:, lo:lo + half]
            x2 = acc[:, lo + half:lo + RET_QK_DIM]
            o_ref[:, base + lo:base + lo + half] = (x1 * cos - x2 * sin).astype(BF16)
            o_ref[:, base + lo + half:base + lo + RET_QK_DIM] = (x1 * sin + x2 * cos).astype(BF16)
    else:
        o_ref[:, base:base + tn] = acc.astype(BF16)


def _ret_kernel(logit_ref, q_ref, k_ref, v_ref, gain_ref, o_ref, acc_scr, state_f_scr, state_b_scr, *, n_chunks):
    h = pl.program_id(1)
    c_len = RET_CHUNK

    def log_gamma(direction, shape):
        z = jnp.full(shape, logit_ref[direction, h], F32)
        return jnp.minimum(z, 0.0) - jnp.log1p(jnp.exp(-jnp.abs(z)))

    ri = lax.broadcasted_iota(jnp.int32, (c_len, c_len), 0)
    ci = lax.broadcasted_iota(jnp.int32, (c_len, c_len), 1)
    dist = (ri - ci).astype(F32)
    lg_f = log_gamma(0, (c_len, c_len))
    lg_b = log_gamma(1, (c_len, c_len))
    decay = jnp.where(dist >= 0, jnp.exp(lg_f * jnp.maximum(dist, 0.0)), jnp.exp(lg_b * jnp.maximum(-dist, 0.0)))
    idx = lax.broadcasted_iota(jnp.int32, (c_len, 1), 0).astype(F32)
    lgf = log_gamma(0, (c_len, 1))
    lgb = log_gamma(1, (c_len, 1))
    xi_f = jnp.exp(lgf * (idx + 1.0))
    zeta_f = jnp.exp(lgf * (c_len - 1.0 - idx))
    cd_f = jnp.exp(lgf[0:1] * c_len)
    xi_b = jnp.exp(lgb * (c_len - idx))
    zeta_b = jnp.exp(lgb * idx)
    cd_b = jnp.exp(lgb[0:1] * c_len)

    def chunk(cidx):
        start = pl.multiple_of(cidx * c_len, c_len)
        return (q_ref[pl.ds(start, c_len), :], k_ref[pl.ds(start, c_len), :], v_ref[pl.ds(start, c_len), :], start)

    def update_state(state_scr, kc, vc, zeta, cd):
        zv = (vc.astype(F32) * zeta).astype(BF16)
        kv = lax.dot_general(kc, zv, (((0,), (0,)), ((), ())), preferred_element_type=F32)
        state_scr[...] = state_scr[...] * cd + kv

    def left_to_right(cidx):
        qc, kc, vc, start = chunk(cidx)
        scores = lax.dot_general(qc, kc, (((1,), (1,)), ((), ())), preferred_element_type=F32) * decay
        o = jnp.dot(scores.astype(BF16), vc, preferred_element_type=F32)
        o = o + xi_f * jnp.dot(qc, state_f_scr[...].astype(BF16), preferred_element_type=F32)
        update_state(state_f_scr, kc, vc, zeta_f, cd_f)
        return o, start

    def right_to_left(cidx):
        qc, kc, vc, start = chunk(cidx)
        o = xi_b * jnp.dot(qc, state_b_scr[...].astype(BF16), preferred_element_type=F32)
        update_state(state_b_scr, kc, vc, zeta_b, cd_b)
        return o, start

    def finish(y, start):
        y = y * lax.rsqrt(jnp.mean(y * y, axis=-1, keepdims=True) + EPS) * gain_ref[...]
        o_ref[pl.ds(start, c_len), :] = y.astype(BF16)

    state_f_scr[...] = jnp.zeros_like(state_f_scr)
    state_b_scr[...] = jnp.zeros_like(state_b_scr)
    half = n_chunks // 2

    def first_visits(t, carry):
        o_f, start_f = left_to_right(t)
        o_b, start_b = right_to_left(n_chunks - 1 - t)
        acc_scr[pl.ds(start_f, c_len), :] = o_f
        acc_scr[pl.ds(start_b, c_len), :] = o_b
        return carry

    def second_visits(t, carry):
        o_f, start_f = left_to_right(t)
        o_b, start_b = right_to_left(n_chunks - 1 - t)
        finish(acc_scr[pl.ds(start_f, c_len), :] + o_f, start_f)
        finish(acc_scr[pl.ds(start_b, c_len), :] + o_b, start_b)
        return carry

    lax.fori_loop(0, half, first_visits, 0)
    lax.fori_loop(half, n_chunks, second_visits, 0)


def _retention_mixer(x, mod, w_in, decay_logit, gn_gain, w_o, batch, seq):
    n, d = x.shape
    qk_cols = RET_HEADS * RET_QK_DIM
    v_cols = RET_HEADS * RET_V_DIM
    col_scale = jnp.concatenate([jnp.ones((qk_cols,), F32), jnp.full((qk_cols,), RET_QK_DIM ** -0.5, F32),
                                 jnp.ones((2 * v_cols,), F32)])
    w = (w_in * col_scale[None, :]).astype(BF16)
    pos = jnp.arange(seq, dtype=F32)
    inv = 1.0 / (RET_THETA ** jnp.linspace(0.0, 1.0, RET_QK_DIM // 2, dtype=F32))
    ang = pos[:, None] * inv
    tm, tn = 512, qk_cols
    tiles_per_seq = seq // tm
    half = RET_QK_DIM // 2
    extras = [jnp.cos(ang), jnp.sin(ang)]
    extra_specs = [pl.BlockSpec((tm, half), lambda i, j: (i % tiles_per_seq, 0)),
                   pl.BlockSpec((tm, half), lambda i, j: (i % tiles_per_seq, 0))]
    proj = _project(x, mod, 0, w, extras, extra_specs, _ret_proj_epilogue,
                    jax.ShapeDtypeStruct((n, w.shape[1]), BF16), pl.BlockSpec((tm, w.shape[1]), lambda i, j: (i, 0)),
                    tm, tn, seq, "ret_proj")

    n_chunks = seq // RET_CHUNK
    k_off = qk_cols // RET_QK_DIM
    v_off = 2 * qk_cols // RET_V_DIM
    y = pl.pallas_call(
        functools.partial(_ret_kernel, n_chunks=n_chunks),
        grid=(batch, RET_HEADS),
        in_specs=[pl.BlockSpec(memory_space=pltpu.SMEM),
                  pl.BlockSpec((seq, RET_QK_DIM), lambda b, h: (b, h)),
                  pl.BlockSpec((seq, RET_QK_DIM), lambda b, h: (b, k_off + h)),
                  pl.BlockSpec((seq, RET_V_DIM), lambda b, h: (b, v_off + h)),
                  pl.BlockSpec((1, RET_V_DIM), lambda b, h: (0, 0))],
        out_specs=pl.BlockSpec((seq, RET_V_DIM), lambda b, h: (b, h)),
        out_shape=jax.ShapeDtypeStruct((n, v_cols), BF16),
        scratch_shapes=[pltpu.VMEM((seq, RET_V_DIM), F32), pltpu.VMEM((RET_QK_DIM, RET_V_DIM), F32),
                        pltpu.VMEM((RET_QK_DIM, RET_V_DIM), F32)],
        compiler_params=_cparams("parallel", "arbitrary"),
        name="retention",
    )(decay_logit.astype(F32), proj, proj, proj, gn_gain.reshape(1, RET_V_DIM).astype(F32))
    gate_col = (2 * qk_cols + v_cols) // v_cols
    return _out_project(y, 0, v_cols, w_o.astype(BF16), x, mod, 0, seq, 512, "ret_out",
                        gate_src=proj, gate_col=gate_col)


def _conv_pw1_epilogue(acc, j, extras, outs):
    b_ref = extras[0]
    u = acc + b_ref[...]
    dm = u.shape[1] // 2
    outs[0][...] = (u[:, :dm] * jax.nn.sigmoid(u[:, dm:])).astype(BF16)


def _conv_kernel(prev_ref, cur_ref, next_ref, wdw_ref, bdw_ref, lng_ref, lnb_ref, w2_ref, b2_ref, x_ref, mod_ref,
                 o_ref, ext_scr, shift_scr, c_scr, *, tiles_per_seq, row_block):
    tm = cur_ref.shape[0]
    t = pl.program_id(0) % tiles_per_seq
    keep_prev = jnp.where(t == 0, 0.0, 1.0)
    keep_next = jnp.where(t == tiles_per_seq - 1, 0.0, 1.0)
    ext_scr[0:CONV_HALO, :] = prev_ref[...].astype(F32) * keep_prev
    ext_scr[CONV_HALO:CONV_HALO + tm, :] = cur_ref[...].astype(F32)
    ext_scr[CONV_HALO + tm:, :] = next_ref[...].astype(F32) * keep_next
    base = CONV_HALO - CONV_PAD
    span = tm + 2 * CONV_HALO - SUBLANES
    for r in range(1, SUBLANES):
        shift_scr[r - 1, 0:span, :] = ext_scr[r:r + span, :]
    for rb in range(tm // row_block):
        r0 = rb * row_block
        acc = jnp.zeros((row_block, cur_ref.shape[1]), F32) + bdw_ref[...]
        for j in range(CONV_WIDTH):
            phase = (base + j) % SUBLANES
            start = r0 + base + j - phase
            if phase == 0:
                tap = ext_scr[start:start + row_block, :]
            else:
                tap = shift_scr[phase - 1, start:start + row_block, :]
            acc = acc + wdw_ref[j:j + 1, :] * tap
        c_scr[r0:r0 + row_block, :] = acc
    u = c_scr[...]
    mu = jnp.mean(u, axis=-1, keepdims=True)
    var = jnp.mean(jnp.square(u - mu), axis=-1, keepdims=True)
    u = (u - mu) * lax.rsqrt(var + EPS) * lng_ref[...] + lnb_ref[...]
    u = _silu(u).astype(BF16)
    z = jnp.dot(u, w2_ref[...], preferred_element_type=F32) + b2_ref[...]
    o_ref[...] = x_ref[...] + mod_ref[0][2:3] * z


def _conv_mixer(x, mod, w_pw1, b_pw1, w_dw, b_dw, ln_gain, ln_bias, w_pw2, b_pw2, seq):
    n, d = x.shape
    tm = 512
    u = _project(x, mod, 0, w_pw1.astype(BF16), [b_pw1.reshape(1, 2 * d).astype(F32)],
                 [pl.BlockSpec((1, 2 * d), lambda i, j: (0, 0))], _conv_pw1_epilogue,
                 jax.ShapeDtypeStruct((n, d), BF16), pl.BlockSpec((tm, d), lambda i, j: (i, 0)),
                 tm, 2 * d, seq, "conv_pw1")
    tm = 256
    tiles_per_seq = seq // tm
    halo_per_tile = tm // CONV_HALO
    n_halo = n // CONV_HALO
    wdw = jnp.pad(w_dw.reshape(CONV_WIDTH, d).astype(F32), ((0, 1), (0, 0)))
    vec = lambda a: a.reshape(1, d).astype(F32)
    row_spec = pl.BlockSpec((1, d), lambda i: (0, 0))
    return pl.pallas_call(
        functools.partial(_conv_kernel, tiles_per_seq=tiles_per_seq, row_block=32),
        grid=(n // tm,),
        in_specs=[pl.BlockSpec((CONV_HALO, d), lambda i: (jnp.maximum(i * halo_per_tile - 1, 0), 0)),
                  pl.BlockSpec((tm, d), lambda i: (i, 0)),
                  pl.BlockSpec((CONV_HALO, d), lambda i: (jnp.minimum((i + 1) * halo_per_tile, n_halo - 1), 0)),
                  pl.BlockSpec((CONV_WIDTH + 1, d), lambda i: (0, 0)),
                  row_spec, row_spec, row_spec,
                  pl.BlockSpec((d, d), lambda i: (0, 0)),
                  row_spec,
                  pl.BlockSpec((tm, d), lambda i: (i, 0)),
                  pl.BlockSpec((1, 6, d), lambda i: (i // tiles_per_seq, 0, 0))],
        out_specs=pl.BlockSpec((tm, d), lambda i: (i, 0)),
        out_shape=jax.ShapeDtypeStruct((n, d), F32),
        scratch_shapes=[pltpu.VMEM((tm + 2 * CONV_HALO, d), F32),
                        pltpu.VMEM((SUBLANES - 1, tm + 2 * CONV_HALO, d), F32), pltpu.VMEM((tm, d), F32)],
        compiler_params=_cparams("parallel"),
        name="conv_module",
    )(u, u, u, wdw, vec(b_dw), vec(ln_gain), vec(ln_bias), w_pw2.astype(BF16), vec(b_pw2), x, mod)


def _t5_bucket(rel):
    half = T5_BUCKETS // 2
    exact = half // 2
    n = jnp.abs(rel)
    log_ratio = jnp.log(jnp.maximum(n, 1).astype(F32) / exact) / math.log(T5_MAX_DIST / exact)
    large = jnp.minimum(exact + (log_ratio * (half - exact)).astype(jnp.int32), half - 1)
    return jnp.where(rel > 0, half, 0) + jnp.where(n < exact, n, large)


def _bias_tile_kernel(table_ref, idx_ref, o_ref):
    h = pl.program_id(0)
    idx = idx_ref[0]
    acc = jnp.zeros(idx.shape, F32)
    for b in range(T5_BUCKETS):
        acc = jnp.where(idx == b, table_ref[b, h], acc)
    o_ref[0, 0] = acc * LOG2_E


def _diff_kernel(q_ref, k_ref, v_ref, bias_ref, lam_ref, gain_ref, o_ref, s_scr, m_scr, acc_scr, *, n_chunks, out_scale,
                 lambda_init):
    tq = q_ref.shape[0]
    qi = pl.program_id(2)
    q = q_ref[...]
    lane = lax.broadcasted_iota(jnp.int32, (tq, LANES), 1)
    low = lane < DIFF_HEAD_DIM
    zero = jnp.zeros((tq, LANES), BF16)
    lhs = jnp.concatenate([jnp.where(low, q, zero), jnp.where(low, zero, q)], axis=0)

    q_sub = tq // KEY_CHUNK

    def add_bias(s, j):
        tiles = [bias_ref[0, jnp.clip(j - (q_sub * qi + a), -2, 2) + 2] for a in range(q_sub)]
        return s + jnp.concatenate(tiles + tiles, axis=0)

    o = _softmax_pv(lhs, k_ref, v_ref, s_scr, m_scr, acc_scr, n_chunks, add_bias)
    lam_v = lam_ref[...]
    lam = (jnp.exp(jnp.sum(lam_v[0:1] * lam_v[1:2], axis=-1, keepdims=True))
           - jnp.exp(jnp.sum(lam_v[2:3] * lam_v[3:4], axis=-1, keepdims=True)) + lambda_init)
    y = o[:tq] - lam * o[tq:]
    y = y * lax.rsqrt(jnp.mean(y * y, axis=-1, keepdims=True) + EPS) * gain_ref[...] * out_scale
    o_ref[...] = y.astype(BF16)


def _diff_mixer(x, mod, w_qkv, lam_q1, lam_k1, lam_q2, lam_k2, subln_gain, w_o, rel_bias, lambda_init,
                batch, seq):
    n, d = x.shape
    col_scale = jnp.concatenate([jnp.full((d,), DIFF_HEAD_DIM ** -0.5 * LOG2_E, F32), jnp.ones((2 * d,), F32)])
    w = (w_qkv * col_scale[None, :]).astype(BF16)
    tm = 1024
    qkv = _project(x, mod, 0, w, [], [], _plain_epilogue,
                   jax.ShapeDtypeStruct((n, 3 * d), BF16), pl.BlockSpec((tm, 3 * d), lambda i, j: (i, 0)),
                   tm, d, seq, "diff_qkv")

    tb = KEY_CHUNK
    ar = jnp.arange(KEY_CHUNK, dtype=jnp.int32)
    offs = (jnp.arange(5, dtype=jnp.int32) - 2) * KEY_CHUNK
    rel = (ar[None, None, :] + offs[:, None, None]) - ar[None, :, None]
    bucket = _t5_bucket(rel).astype(jnp.int32)
    bias = pl.pallas_call(
        _bias_tile_kernel,
        grid=(DIFF_HEADS, 5),
        in_specs=[pl.BlockSpec(memory_space=pltpu.SMEM),
                  pl.BlockSpec((1, tb, KEY_CHUNK), lambda h, t: (t, 0, 0))],
        out_specs=pl.BlockSpec((1, 1, tb, KEY_CHUNK), lambda h, t: (h, t, 0, 0)),
        out_shape=jax.ShapeDtypeStruct((DIFF_HEADS, 5, tb, KEY_CHUNK), F32),
        compiler_params=_cparams("arbitrary", "arbitrary"),
        name="rel_bias_tiles",
    )(rel_bias.astype(F32), bucket)

    lam = jnp.stack([lam_q1, lam_k1, lam_q2, lam_k2]).astype(F32)
    n_chunks = seq // KEY_CHUNK
    tq = 2 * KEY_CHUNK
    q_tiles = seq // tq
    o = pl.pallas_call(
        functools.partial(_diff_kernel, n_chunks=n_chunks, out_scale=1.0 - lambda_init, lambda_init=lambda_init),
        grid=(batch, DIFF_HEADS, q_tiles),
        in_specs=[pl.BlockSpec((tq, LANES), lambda b, h, i: (b * q_tiles + i, h)),
                  pl.BlockSpec((seq, LANES), lambda b, h, i: (b, DIFF_HEADS + h)),
                  pl.BlockSpec((seq, LANES), lambda b, h, i: (b, 2 * DIFF_HEADS + h)),
                  pl.BlockSpec((1, 5, tb, KEY_CHUNK), lambda b, h, i: (h, 0, 0, 0)),
                  pl.BlockSpec((4, DIFF_HEAD_DIM), lambda b, h, i: (0, 0)),
                  pl.BlockSpec((1, LANES), lambda b, h, i: (0, 0))],
        out_specs=pl.BlockSpec((tq, LANES), lambda b, h, i: (b * q_tiles + i, h)),
        out_shape=jax.ShapeDtypeStruct((n, d), BF16),
        scratch_shapes=[pltpu.VMEM((n_chunks, 2 * tq, KEY_CHUNK), F32),
                        pltpu.VMEM((2 * tq, LANES), F32), pltpu.VMEM((2 * tq, 2 * LANES), F32)],
        compiler_params=_cparams("parallel", "parallel", "arbitrary"),
        name="diff_attention",
    )(qkv, qkv, qkv, bias, lam, subln_gain.reshape(1, LANES).astype(F32))
    return _out_project(o, 0, d, w_o.astype(BF16), x, mod, 0, seq, 1024, "diff_out")


def _final_norm_kernel(x_ref, g_ref, o_ref):
    x = x_ref[...]
    o_ref[...] = x * lax.rsqrt(jnp.mean(x * x, axis=-1, keepdims=True) + EPS) * g_ref[...]


def _final_norm(x, gain, tm):
    n, d = x.shape
    return pl.pallas_call(
        _final_norm_kernel,
        grid=(n // tm,),
        in_specs=[pl.BlockSpec((tm, d), lambda i: (i, 0)), pl.BlockSpec((1, d), lambda i: (0, 0))],
        out_specs=pl.BlockSpec((tm, d), lambda i: (i, 0)),
        out_shape=jax.ShapeDtypeStruct((n, d), F32),
        compiler_params=_cparams("parallel"),
        name="final_norm",
    )(x, gain.reshape(1, d).astype(F32))


def kernel(x, c, mod_w, mod_b, att_w_qkv, att_q_gain, att_k_gain, att_w_o, ret_w_in, ret_decay_logit, ret_gn_gain, ret_w_o, cnv_w_pw1, cnv_b_pw1, cnv_w_dw, cnv_b_dw, cnv_ln_gain, cnv_ln_bias, cnv_w_pw2, cnv_b_pw2, dif_w_qkv, dif_lam_q1, dif_lam_k1, dif_lam_q2, dif_lam_k2, dif_subln_gain, dif_w_o, rel_bias, ffn_w_gate, ffn_w_up, ffn_w_down, moe_w_router, moe_b_router, moe_w_gate, moe_w_up, moe_w_down, final_gain):
    batch, seq, d = x.shape
    depth = mod_w.shape[0]
    xf = x.reshape(batch * seq, d)
    mods = _modulation(c, mod_w, mod_b)
    moe_gate, moe_up, moe_down = moe_w_gate.astype(BF16), moe_w_up.astype(BF16), moe_w_down.astype(BF16)
    for i in range(depth):
        mod = mods[i]
        kind, j = i % 4, i // 4
        if kind == 0:
            xf = _attention_mixer(xf, mod, att_w_qkv[j], att_q_gain[j], att_k_gain[j], att_w_o[j], batch, seq)
        elif kind == 1:
            xf = _retention_mixer(xf, mod, ret_w_in[j], ret_decay_logit[j], ret_gn_gain[j], ret_w_o[j], batch, seq)
        elif kind == 2:
            xf = _conv_mixer(xf, mod, cnv_w_pw1[j], cnv_b_pw1[j], cnv_w_dw[j], cnv_b_dw[j], cnv_ln_gain[j],
                             cnv_ln_bias[j], cnv_w_pw2[j], cnv_b_pw2[j], seq)
        else:
            lambda_init = 0.8 - 0.6 * math.exp(-0.3 * i)
            xf = _diff_mixer(xf, mod, dif_w_qkv[j], dif_lam_q1[j], dif_lam_k1[j], dif_lam_q2[j], dif_lam_k2[j],
                             dif_subln_gain[j], dif_w_o[j], rel_bias, lambda_init, batch, seq)
        if i % 2 == 0:
            m = i // 2
            xf = _ffn(xf, mod, ffn_w_gate[m].astype(BF16), ffn_w_up[m].astype(BF16), ffn_w_down[m].astype(BF16),
                      seq, 512)
        else:
            m = i // 2
            xf = _moe(xf, mod, moe_w_router[m], moe_b_router[m], moe_gate, moe_up, moe_down, m, seq,
                      out_gain=final_gain if i == depth - 1 else None)
    if depth % 2 == 1:
        xf = _final_norm(xf, final_gain, 512)
    return xf.reshape(batch, seq, d)
```

```python
import functools
import math

import jax
import jax.numpy as jnp
from jax import lax
from jax.experimental import pallas as pl
from jax.experimental.pallas import tpu as pltpu

F32 = jnp.float32
BF16 = jnp.bfloat16

EPS = 1e-6
GRID_W = 64

ATT_HEADS = 16
ATT_KV_HEADS = 4
ATT_HEAD_DIM = 64
ROPE_THETA = 10000.0

RET_HEADS = 4
RET_QK_DIM = 256
RET_V_DIM = 512
RET_THETA = 10000.0
RET_CHUNK = 256

CONV_WIDTH = 31
CONV_PAD = CONV_WIDTH // 2
CONV_HALO = 16

DIFF_HEADS = 8
DIFF_HEAD_DIM = 64
T5_BUCKETS = 32
T5_MAX_DIST = 128

N_EXPERTS = 8
LANES = 128
SUBLANES = 8
KEY_CHUNK = 256
LOG2_E = 1.4426950408889634
VMEM_LIMIT_BYTES = 56 * 1024 * 1024


def _cparams(*sem):
    return pltpu.CompilerParams(dimension_semantics=sem, vmem_limit_bytes=VMEM_LIMIT_BYTES)


def _silu(v):
    return v * jax.nn.sigmoid(v)


def _norm_mod(x, mod, which):
    shift = mod[3 * which:3 * which + 1]
    scale = mod[3 * which + 1:3 * which + 2]
    y = x * lax.rsqrt(jnp.mean(x * x, axis=-1, keepdims=True) + EPS)
    return y * (1.0 + scale) + shift


def _split_dot(a, b):
    hi = a.astype(BF16)
    lo = (a - hi.astype(F32)).astype(BF16)
    return (jnp.dot(hi, b, preferred_element_type=F32) + jnp.dot(lo, b, preferred_element_type=F32))


def _mod_kernel(c_ref, w_ref, b_ref, o_ref):
    cond = _silu(c_ref[...])
    o_ref[0] = jnp.dot(cond, w_ref[0], preferred_element_type=F32, precision=lax.Precision.HIGHEST) + b_ref[0]


def _modulation(c, mod_w, mod_b):
    depth, d, d6 = mod_w.shape
    b = c.shape[0]
    tn = d
    out = pl.pallas_call(
        _mod_kernel,
        grid=(depth, d6 // tn),
        in_specs=[pl.BlockSpec((b, d), lambda i, j: (0, 0)),
                  pl.BlockSpec((1, d, tn), lambda i, j: (i, 0, j)),
                  pl.BlockSpec((1, 1, tn), lambda i, j: (i, 0, j))],
        out_specs=pl.BlockSpec((1, b, tn), lambda i, j: (i, 0, j)),
        out_shape=jax.ShapeDtypeStruct((depth, b, d6), F32),
        compiler_params=_cparams("arbitrary", "arbitrary"),
        name="modulation",
    )(c, mod_w, mod_b.reshape(depth, 1, d6))
    return out.reshape(depth, b, 6, d)


def _proj_kernel(x_ref, mod_ref, w_ref, *rest, which, n_extra, epilogue, tn):
    extras = rest[:n_extra]
    outs = rest[n_extra:]
    h = _norm_mod(x_ref[...], mod_ref[0], which).astype(BF16)
    for j in range(w_ref.shape[1] // tn):
        acc = jnp.dot(h, w_ref[:, j * tn:(j + 1) * tn], preferred_element_type=F32)
        epilogue(acc, j, extras, outs)


def _project(x, mod, which, w, extras, extra_specs, epilogue, out_shapes, out_specs, tm, tn, seq, name):
    n, d = x.shape
    c = w.shape[1]
    tiles_per_seq = seq // tm
    kern = functools.partial(_proj_kernel, which=which, n_extra=len(extras), epilogue=epilogue, tn=tn)
    return pl.pallas_call(
        kern,
        grid=(n // tm, 1),
        in_specs=[pl.BlockSpec((tm, d), lambda i, j: (i, 0)),
                  pl.BlockSpec((1, 6, d), lambda i, j: (i // tiles_per_seq, 0, 0)),
                  pl.BlockSpec((d, c), lambda i, j: (0, 0), pipeline_mode=pl.Buffered(1))] + extra_specs,
        out_specs=out_specs,
        out_shape=out_shapes,
        compiler_params=_cparams("parallel", "arbitrary"),
        name=name,
    )(x, mod, w, *extras)


def _plain_epilogue(acc, j, extras, outs):
    tn = acc.shape[1]
    outs[0][:, j * tn:(j + 1) * tn] = acc.astype(outs[0].dtype)


def _oproj_kernel(*refs, which, gated, has_bias):
    refs = list(refs)
    y_ref = refs.pop(0)
    g_ref = refs.pop(0) if gated else None
    w_ref = refs.pop(0)
    b_ref = refs.pop(0) if has_bias else None
    x_ref, mod_ref, o_ref = refs
    y = y_ref[...]
    if gated:
        y = (y.astype(F32) * _silu(g_ref[...].astype(F32))).astype(BF16)
    z = jnp.dot(y, w_ref[...], preferred_element_type=F32)
    if has_bias:
        z = z + b_ref[...]
    gate = mod_ref[0][3 * which + 2:3 * which + 3]
    o_ref[...] = x_ref[...] + gate * z


def _out_project(y, y_col, k, w, x, mod, which, seq, tm, name, gate_src=None, gate_col=0, bias=None):
    n, d = x.shape
    tiles_per_seq = seq // tm
    args = [y]
    specs = [pl.BlockSpec((tm, k), lambda i: (i, y_col))]
    if gate_src is not None:
        args.append(gate_src)
        specs.append(pl.BlockSpec((tm, k), lambda i: (i, gate_col)))
    args.append(w)
    specs.append(pl.BlockSpec((k, d), lambda i: (0, 0)))
    if bias is not None:
        args.append(bias.reshape(1, d))
        specs.append(pl.BlockSpec((1, d), lambda i: (0, 0)))
    args += [x, mod]
    specs += [pl.BlockSpec((tm, d), lambda i: (i, 0)),
              pl.BlockSpec((1, 6, d), lambda i: (i // tiles_per_seq, 0, 0))]
    kern = functools.partial(_oproj_kernel, which=which, gated=gate_src is not None, has_bias=bias is not None)
    return pl.pallas_call(
        kern,
        grid=(n // tm,),
        in_specs=specs,
        out_specs=pl.BlockSpec((tm, d), lambda i: (i, 0)),
        out_shape=jax.ShapeDtypeStruct((n, d), F32),
        compiler_params=_cparams("parallel"),
        name=name,
    )(*args)


def _ffn_kernel(x_ref, mod_ref, wg_ref, wu_ref, wd_ref, o_ref):
    x = x_ref[...]
    mod = mod_ref[0]
    h = _norm_mod(x, mod, 1).astype(BF16)
    g = jnp.dot(h, wg_ref[...], preferred_element_type=F32)
    u = jnp.dot(h, wu_ref[...], preferred_element_type=F32)
    a = (_silu(g) * u).astype(BF16)
    o_ref[...] = x + mod[5:6] * jnp.dot(a, wd_ref[...], preferred_element_type=F32)


def _ffn(x, mod, wg, wu, wd, seq, tm):
    n, d = x.shape
    ff = wg.shape[1]
    tiles_per_seq = seq // tm
    resident = pl.Buffered(1)
    return pl.pallas_call(
        _ffn_kernel,
        grid=(n // tm,),
        in_specs=[pl.BlockSpec((tm, d), lambda i: (i, 0)),
                  pl.BlockSpec((1, 6, d), lambda i: (i // tiles_per_seq, 0, 0)),
                  pl.BlockSpec((d, ff), lambda i: (0, 0), pipeline_mode=resident),
                  pl.BlockSpec((d, ff), lambda i: (0, 0), pipeline_mode=resident),
                  pl.BlockSpec((ff, d), lambda i: (0, 0), pipeline_mode=resident)],
        out_specs=pl.BlockSpec((tm, d), lambda i: (i, 0)),
        out_shape=jax.ShapeDtypeStruct((n, d), F32),
        compiler_params=_cparams("parallel"),
        name="dense_ffn",
    )(x, mod, wg, wu, wd)


def _router_kernel(x_ref, mod_ref, wr_ref, br_ref, tri_ref, route_ref, count_ref, carry_scr):
    @pl.when(pl.program_id(0) == 0)
    def _():
        carry_scr[...] = jnp.zeros_like(carry_scr)

    h = _norm_mod(x_ref[...], mod_ref[0], 1)
    h_hi = h.astype(BF16)
    h_lo = (h - h_hi.astype(F32)).astype(BF16)
    logits = (jnp.dot(h_hi, wr_ref[0], preferred_element_type=F32)
              + (jnp.dot(h_hi, wr_ref[1], preferred_element_type=F32)
                 + jnp.dot(h_lo, wr_ref[0], preferred_element_type=F32))) + br_ref[...]
    lane = lax.broadcasted_iota(jnp.int32, logits.shape, 1)
    neg = jnp.float32(-jnp.inf)
    logits = jnp.where(lane < N_EXPERTS, logits, neg)
    m1 = jnp.max(logits, axis=-1, keepdims=True)
    i1 = jnp.min(jnp.where(logits == m1, lane, LANES), axis=-1, keepdims=True)
    rest = jnp.where(lane == i1, neg, logits)
    m2 = jnp.max(rest, axis=-1, keepdims=True)
    i2 = jnp.min(jnp.where(rest == m2, lane, LANES), axis=-1, keepdims=True)
    e2 = jnp.exp(m2 - m1)
    w1 = 1.0 / (1.0 + e2)
    w2 = e2 / (1.0 + e2)
    sel = jnp.where((lane == i1) | (lane == i2), 1.0, 0.0)
    incl = jnp.dot(tri_ref[...], sel.astype(BF16), preferred_element_type=F32)
    excl = incl - sel + carry_scr[...]
    r1 = jnp.sum(jnp.where(lane == i1, excl, 0.0), axis=-1, keepdims=True)
    r2 = jnp.sum(jnp.where(lane == i2, excl, 0.0), axis=-1, keepdims=True)
    carry_scr[...] += incl[incl.shape[0] - 1:, :]
    count_ref[...] = carry_scr[...]
    fields = (i1.astype(F32), i2.astype(F32), w1, w2, r1, r2)
    route = jnp.zeros(logits.shape, F32)
    for k, field in enumerate(fields):
        route = jnp.where(lane == k, field, route)
    route_ref[...] = route


def _router(x, mod, w_router, b_router, seq, tm):
    n, d = x.shape
    tiles_per_seq = seq // tm
    wr = jnp.pad(w_router, ((0, 0), (0, LANES - N_EXPERTS))).astype(F32)
    wr_hi = wr.astype(BF16)
    wr = jnp.stack([wr_hi, (wr - wr_hi.astype(F32)).astype(BF16)])
    br = jnp.pad(b_router, (0, LANES - N_EXPERTS)).reshape(1, LANES)
    tri = (jnp.arange(tm)[:, None] >= jnp.arange(tm)[None, :]).astype(BF16)
    return pl.pallas_call(
        _router_kernel,
        grid=(n // tm,),
        in_specs=[pl.BlockSpec((tm, d), lambda i: (i, 0)),
                  pl.BlockSpec((1, 6, d), lambda i: (i // tiles_per_seq, 0, 0)),
                  pl.BlockSpec((2, d, LANES), lambda i: (0, 0, 0)),
                  pl.BlockSpec((1, LANES), lambda i: (0, 0)),
                  pl.BlockSpec((tm, tm), lambda i: (0, 0))],
        out_specs=[pl.BlockSpec((tm, LANES), lambda i: (i, 0)), pl.BlockSpec((1, LANES), lambda i: (0, 0))],
        out_shape=[jax.ShapeDtypeStruct((n, LANES), F32), jax.ShapeDtypeStruct((1, LANES), F32)],
        scratch_shapes=[pltpu.VMEM((1, LANES), F32)],
        compiler_params=_cparams("arbitrary"),
        name="moe_router",
    )(x, mod, wr, br, tri)


def _dispatch_kernel(zero_rows_ref, d1_ref, d2_ref, x_ref, mod_ref, xs_ref, h_scr, zero_scr, sems):
    tm = x_ref.shape[0]

    @pl.when(pl.program_id(0) == 0)
    def _():
        zero_scr[...] = jnp.zeros_like(zero_scr)
        te = zero_scr.shape[0]
        for k in range(zero_rows_ref.shape[0]):
            start = pl.multiple_of(zero_rows_ref[k], te)
            copy = pltpu.make_async_copy(zero_scr, xs_ref.at[pl.ds(start, te), :], sems.at[2])
            copy.start()
            copy.wait()

    h_scr[...] = _norm_mod(x_ref[...], mod_ref[0], 1)

    def row_copy(r, dest, which):
        return pltpu.make_async_copy(h_scr.at[pl.ds(r, 1), :], xs_ref.at[pl.ds(dest, 1), :], sems.at[which])

    def issue(r, carry):
        row_copy(r, d1_ref[0, 0, r], 0).start()
        row_copy(r, d2_ref[0, 0, r], 1).start()
        return carry

    lax.fori_loop(0, tm, issue, 0, unroll=4)
    for which in range(2):
        pltpu.make_async_copy(h_scr, xs_ref.at[pl.ds(0, tm), :], sems.at[which]).wait()


def _expert_kernel(tile_e_ref, n_act_ref, x_ref, wg_ref, wu_ref, wd_ref, o_ref, *, tf):
    del tile_e_ref
    s = pl.program_id(0)

    @pl.when(s < n_act_ref[0])
    def _():
        h = x_ref[...].astype(BF16)
        out = None
        for f in range(wg_ref.shape[3] // tf):
            cols = slice(f * tf, (f + 1) * tf)
            g = jnp.dot(h, wg_ref[0, 0, :, cols], preferred_element_type=F32)
            u = jnp.dot(h, wu_ref[0, 0, :, cols], preferred_element_type=F32)
            a = (_silu(g) * u).astype(BF16)
            part = jnp.dot(a, wd_ref[0, 0, cols, :], preferred_element_type=F32)
            out = part if out is None else out + part
        o_ref[...] = out

    @pl.when(s >= n_act_ref[0])
    def _():
        o_ref[...] = jnp.zeros_like(o_ref)


def _combine_kernel(d1_ref, d2_ref, route_ref, x_ref, mod_ref, *rest, out_norm):
    if out_norm:
        gain_ref, ys_ref, o_ref, buf, sems = rest
    else:
        ys_ref, o_ref, buf, sems = rest
    tm = x_ref.shape[0]

    def row_copy(r, src, which):
        return pltpu.make_async_copy(ys_ref.at[pl.ds(src, 1), :], buf.at[which, pl.ds(r, 1), :], sems.at[which])

    def issue(r, carry):
        row_copy(r, d1_ref[0, 0, r], 0).start()
        row_copy(r, d2_ref[0, 0, r], 1).start()
        return carry

    lax.fori_loop(0, tm, issue, 0, unroll=4)
    for which in range(2):
        pltpu.make_async_copy(ys_ref.at[pl.ds(0, tm), :], buf.at[which], sems.at[which]).wait()
    route = route_ref[...]
    y = route[:, 2:3] * buf[0] + route[:, 3:4] * buf[1]
    out = x_ref[...] + mod_ref[0][5:6] * y
    if out_norm:
        out = out * lax.rsqrt(jnp.mean(out * out, axis=-1, keepdims=True) + EPS) * gain_ref[...]
    o_ref[...] = out


def _moe(x, mod, w_router, b_router, w_gate, w_up, w_down, layer, seq, out_gain=None):
    n, d = x.shape
    _, n_e, _, ff = w_gate.shape
    tm = 512
    te = 512
    tf = 1792
    tiles_per_seq = seq // tm
    n_tok_tiles = n // tm
    n_slots = 2 * n + n_e * te
    n_tiles = n_slots // te

    route, counts = _router(x, mod, w_router, b_router, seq, tm)

    counts = counts[0, :n_e].astype(jnp.int32)
    padded = ((counts + te - 1) // te) * te
    ends = jnp.cumsum(padded)
    offsets = ends - padded
    n_act = (ends[-1] // te).astype(jnp.int32)
    tile_start = jnp.arange(n_tiles, dtype=jnp.int32) * te
    tile_e = jnp.sum((tile_start[:, None] >= ends[None, :]).astype(jnp.int32), axis=1)
    tile_e = jnp.minimum(tile_e, tile_e[jnp.maximum(n_act - 1, 0)]).astype(jnp.int32)
    e1 = route[:, 0].astype(jnp.int32)
    e2 = route[:, 1].astype(jnp.int32)
    dest1 = (offsets[e1] + route[:, 4].astype(jnp.int32)).reshape(n_tok_tiles, 1, tm)
    dest2 = (offsets[e2] + route[:, 5].astype(jnp.int32)).reshape(n_tok_tiles, 1, tm)
    n_act = n_act.reshape(1)

    slot_spec = pl.BlockSpec((1, 1, tm), lambda i: (i, 0, 0), memory_space=pltpu.SMEM)
    zero_rows = jnp.concatenate([jnp.maximum(ends - te, 0),
                                 (n_tiles - n_e + jnp.arange(n_e, dtype=jnp.int32)) * te]).astype(jnp.int32)
    xs = pl.pallas_call(
        _dispatch_kernel,
        grid=(n_tok_tiles,),
        in_specs=[pl.BlockSpec(memory_space=pltpu.SMEM), slot_spec, slot_spec,
                  pl.BlockSpec((tm, d), lambda i: (i, 0)),
                  pl.BlockSpec((1, 6, d), lambda i: (i // tiles_per_seq, 0, 0))],
        out_specs=pl.BlockSpec(memory_space=pl.ANY),
        out_shape=jax.ShapeDtypeStruct((n_slots, d), F32),
        scratch_shapes=[pltpu.VMEM((tm, d), F32), pltpu.VMEM((te, d), F32), pltpu.SemaphoreType.DMA((3,))],
        compiler_params=_cparams("arbitrary"),
        name="moe_dispatch",
    )(zero_rows, dest1, dest2, x, mod)

    def live(s, n_act_ref):
        return jnp.minimum(s, jnp.maximum(n_act_ref[0] - 1, 0))

    def expert_weights(shape):
        return pl.BlockSpec(shape, lambda s, te_ref, na_ref: (layer, te_ref[s], 0, 0), pipeline_mode=pl.Buffered(1))

    ys = pl.pallas_call(
        functools.partial(_expert_kernel, tf=tf),
        grid_spec=pltpu.PrefetchScalarGridSpec(
            num_scalar_prefetch=2,
            grid=(n_tiles,),
            in_specs=[pl.BlockSpec((te, d), lambda s, te_ref, na_ref: (live(s, na_ref), 0)),
                      expert_weights((1, 1, d, ff)), expert_weights((1, 1, d, ff)), expert_weights((1, 1, ff, d))],
            out_specs=pl.BlockSpec((te, d), lambda s, te_ref, na_ref: (s, 0))),
        out_shape=jax.ShapeDtypeStruct((n_slots, d), F32),
        compiler_params=_cparams("arbitrary"),
        name="moe_experts",
    )(tile_e, n_act, xs, w_gate, w_up, w_down)

    out_norm = out_gain is not None
    gain_args = [out_gain.reshape(1, d).astype(F32)] if out_norm else []
    gain_specs = [pl.BlockSpec((1, d), lambda i: (0, 0))] if out_norm else []
    return pl.pallas_call(
        functools.partial(_combine_kernel, out_norm=out_norm),
        grid=(n_tok_tiles,),
        in_specs=[slot_spec, slot_spec,
                  pl.BlockSpec((tm, LANES), lambda i: (i, 0)),
                  pl.BlockSpec((tm, d), lambda i: (i, 0)),
                  pl.BlockSpec((1, 6, d), lambda i: (i // tiles_per_seq, 0, 0))] + gain_specs
                 + [pl.BlockSpec(memory_space=pl.ANY)],
        out_specs=pl.BlockSpec((tm, d), lambda i: (i, 0)),
        out_shape=jax.ShapeDtypeStruct((n, d), F32),
        scratch_shapes=[pltpu.VMEM((2, tm, d), F32), pltpu.SemaphoreType.DMA((2,))],
        compiler_params=_cparams("arbitrary"),
        name="moe_combine",
    )(dest1, dest2, route, x, mod, *gain_args, ys)


def _softmax_pv(lhs, k_ref, v_ref, s_scr, m_scr, acc_scr, n_chunks, bias_fn=None):
    rows = lhs.shape[0]
    group = min(8, n_chunks)
    m_scr[...] = jnp.full(m_scr.shape, -jnp.inf, F32)

    for j in range(n_chunks):
        kc = k_ref[j * KEY_CHUNK:(j + 1) * KEY_CHUNK, :]
        s = lax.dot_general(lhs, kc, (((1,), (1,)), ((), ())), preferred_element_type=F32)
        if bias_fn is not None:
            s = bias_fn(s, j)
        s_scr[j] = s
        m_scr[...] = jnp.maximum(m_scr[...], jnp.maximum(s[:, :LANES], s[:, LANES:]))
    m_scr[...] = jnp.broadcast_to(jnp.max(m_scr[...], axis=-1, keepdims=True), m_scr.shape)
    acc_scr[...] = jnp.zeros_like(acc_scr)
    ones = jnp.ones((group * KEY_CHUNK, LANES), BF16)

    def pv_body(t, carry):
        mb = m_scr[...]
        parts = []
        for u in range(group):
            s = s_scr[group * t + u]
            parts.append(jnp.exp2(s[:, :LANES] - mb).astype(BF16))
            parts.append(jnp.exp2(s[:, LANES:] - mb).astype(BF16))
        p = jnp.concatenate(parts, axis=1)
        start = pl.multiple_of(t * (group * KEY_CHUNK), group * KEY_CHUNK)
        v_aug = jnp.concatenate([v_ref[pl.ds(start, group * KEY_CHUNK), :], ones], axis=1)
        acc_scr[...] += jnp.dot(p, v_aug, preferred_element_type=F32)
        return carry

    lax.fori_loop(0, n_chunks // group, pv_body, 0)
    acc = acc_scr[...]
    return acc[:, :LANES] / acc[:, LANES:]


def _att_qkv_epilogue(acc, j, extras, outs, n_qk):
    gain_ref, cos_ref, sin_ref, g1_ref, g2_ref = extras
    q_ref, k_ref, v_ref = outs
    qk = acc[:, :n_qk]
    ms = _split_dot(qk * qk, g1_ref[...])
    r = lax.rsqrt(ms + EPS)
    y = qk * _split_dot(r, g2_ref[...]) * gain_ref[...]
    cos = cos_ref[...]
    sin = sin_ref[...]
    lane = lax.broadcasted_iota(jnp.int32, cos.shape, 1)
    first_half = (lane & (ATT_HEAD_DIM // 2)) == 0
    nq = q_ref.shape[1]
    for c in range(n_qk // LANES):
        yc = y[:, c * LANES:(c + 1) * LANES]
        partner = jnp.where(first_half, pltpu.roll(yc, LANES - ATT_HEAD_DIM // 2, 1),
                            pltpu.roll(yc, ATT_HEAD_DIM // 2, 1))
        out = (yc * cos + partner * sin).astype(BF16)
        if c * LANES < nq:
            q_ref[:, c * LANES:(c + 1) * LANES] = out
        else:
            k_ref[:, c * LANES - nq:(c + 1) * LANES - nq] = out
    v_ref[...] = acc[:, n_qk:].astype(BF16)


def _att_kernel(q_ref, k_ref, v_ref, o_ref, s_scr, m_scr, acc_scr, *, n_chunks):
    tq = q_ref.shape[0]
    q = q_ref[...]
    lane = lax.broadcasted_iota(jnp.int32, (tq, LANES), 1)
    low = lane < ATT_HEAD_DIM
    zero = jnp.zeros((tq, LANES), BF16)
    parts = []
    for pair in range(2):
        qp = q[:, pair * LANES:(pair + 1) * LANES]
        parts.append(jnp.where(low, qp, zero))
        parts.append(jnp.where(low, zero, qp))
    lhs = jnp.concatenate(parts, axis=0)
    o = _softmax_pv(lhs, k_ref, v_ref, s_scr, m_scr, acc_scr, n_chunks)
    for pair in range(2):
        o_ref[:, pair * LANES:(pair + 1) * LANES] = jnp.where(
            low, o[(2 * pair) * tq:(2 * pair + 1) * tq], o[(2 * pair + 1) * tq:(2 * pair + 2) * tq]).astype(BF16)


def _attention_mixer(x, mod, w_qkv, q_gain, k_gain, w_o, batch, seq):
    n, d = x.shape
    hd = ATT_HEAD_DIM
    half = hd // 2
    n_freq = half // 2
    groups = ATT_HEADS // ATT_KV_HEADS
    perm = jnp.concatenate([jnp.arange(n_freq), half + jnp.arange(n_freq),
                            n_freq + jnp.arange(n_freq), half + n_freq + jnp.arange(n_freq)])
    q_cols = (jnp.arange(ATT_HEADS)[:, None] * hd + perm[None, :]).reshape(-1)
    k_base = ATT_HEADS * hd
    v_base = k_base + ATT_KV_HEADS * hd
    kv_rep = jnp.repeat(jnp.arange(ATT_KV_HEADS), 2)
    k_cols = (k_base + kv_rep[:, None] * hd + perm[None, :]).reshape(-1)
    v_cols = (v_base + kv_rep[:, None] * hd + jnp.arange(hd)[None, :]).reshape(-1)
    w = w_qkv[:, jnp.concatenate([q_cols, k_cols, v_cols])].astype(BF16)
    n_q = ATT_HEADS * hd
    n_kv = 2 * ATT_KV_HEADS * hd
    n_qk = n_q + n_kv
    gain_row = jnp.concatenate([jnp.tile(q_gain[perm] * (hd ** -0.5 * LOG2_E), ATT_HEADS),
                                jnp.tile(k_gain[perm], 2 * ATT_KV_HEADS)]).reshape(1, n_qk).astype(F32)
    pos = jnp.arange(seq, dtype=jnp.int32)
    row = (pos // GRID_W).astype(F32)
    col = (pos % GRID_W).astype(F32)
    inv = ROPE_THETA ** (-jnp.arange(n_freq, dtype=F32) / n_freq)
    ang = jnp.concatenate([row[:, None] * inv, col[:, None] * inv], axis=-1)
    cos_t = jnp.tile(jnp.cos(ang), (1, 2 * LANES // hd))
    sin_t = jnp.tile(jnp.concatenate([-jnp.sin(ang), jnp.sin(ang)], axis=-1), (1, LANES // hd))
    grp = jnp.arange(n_qk) // hd
    g1 = (grp[:, None] == jnp.arange(LANES)[None, :]).astype(F32) / hd
    g2 = (jnp.arange(LANES)[:, None] == grp[None, :]).astype(F32)
    tm = 512
    tiles_per_seq = seq // tm
    extras = [gain_row, cos_t, sin_t, g1.astype(BF16), g2.astype(BF16)]
    extra_specs = [pl.BlockSpec((1, n_qk), lambda i, j: (0, 0)),
                   pl.BlockSpec((tm, LANES), lambda i, j: (i % tiles_per_seq, 0)),
                   pl.BlockSpec((tm, LANES), lambda i, j: (i % tiles_per_seq, 0)),
                   pl.BlockSpec((n_qk, LANES), lambda i, j: (0, 0)),
                   pl.BlockSpec((LANES, n_qk), lambda i, j: (0, 0))]
    q, k, v = _project(
        x, mod, 0, w, extras, extra_specs, functools.partial(_att_qkv_epilogue, n_qk=n_qk),
        [jax.ShapeDtypeStruct((n, n_q), BF16), jax.ShapeDtypeStruct((n, n_kv), BF16),
         jax.ShapeDtypeStruct((n, n_kv), BF16)],
        [pl.BlockSpec((tm, n_q), lambda i, j: (i, 0)), pl.BlockSpec((tm, n_kv), lambda i, j: (i, 0)),
         pl.BlockSpec((tm, n_kv), lambda i, j: (i, 0))],
        tm, w.shape[1], seq, "att_qkv")

    tq = 256
    n_chunks = seq // KEY_CHUNK
    q_tiles = seq // tq
    o = pl.pallas_call(
        functools.partial(_att_kernel, n_chunks=n_chunks),
        grid=(batch, ATT_KV_HEADS, q_tiles),
        in_specs=[pl.BlockSpec((tq, groups * hd), lambda b, g, i: (b * q_tiles + i, g)),
                  pl.BlockSpec((seq, LANES), lambda b, g, i: (b, g)),
                  pl.BlockSpec((seq, LANES), lambda b, g, i: (b, g))],
        out_specs=pl.BlockSpec((tq, groups * hd), lambda b, g, i: (b * q_tiles + i, g)),
        out_shape=jax.ShapeDtypeStruct((n, n_q), BF16),
        scratch_shapes=[pltpu.VMEM((n_chunks, groups * tq, KEY_CHUNK), F32),
                        pltpu.VMEM((groups * tq, LANES), F32), pltpu.VMEM((groups * tq, 2 * LANES), F32)],
        compiler_params=_cparams("parallel", "parallel", "arbitrary"),
        name="gqa_attention",
    )(q, k, v)
    return _out_project(o, 0, n_q, w_o.astype(BF16), x, mod, 0, seq, 1024, "att_out")


def _ret_proj_epilogue(acc, j, extras, outs):
    cos_ref, sin_ref = extras
    o_ref = outs[0]
    tn = acc.shape[1]
    base = j * tn
    if j < 2:
        cos = cos_ref[...]
        sin = sin_ref[...]
        half = RET_QK_DIM // 2
        for hh in range(RET_HEADS):
            lo = hh * RET_QK_DIM
            x1 = acc[:, lo:lo + half]
            x2 = acc[:, lo + half:lo + RET_QK_DIM]
            o_ref[:, base + lo:base + lo + half] = (x1 * cos - x2 * sin).astype(BF16)
            o_ref[:, base + lo + half:base + lo + RET_QK_DIM] = (x1 * sin + x2 * cos).astype(BF16)
    else:
        o_ref[:, base:base + tn] = acc.astype(BF16)


def _ret_kernel(logit_ref, q_ref, k_ref, v_ref, gain_ref, o_ref, acc_scr, state_f_scr, state_b_scr, *, n_chunks):
    h = pl.program_id(1)
    c_len = RET_CHUNK

    def log_gamma(direction, shape):
        z = jnp.full(shape, logit_ref[direction, h], F32)
        return jnp.minimum(z, 0.0) - jnp.log1p(jnp.exp(-jnp.abs(z)))

    ri = lax.broadcasted_iota(jnp.int32, (c_len, c_len), 0)
    ci = lax.broadcasted_iota(jnp.int32, (c_len, c_len), 1)
    dist = (ri - ci).astype(F32)
    lg_f = log_gamma(0, (c_len, c_len))
    lg_b = log_gamma(1, (c_len, c_len))
    decay = jnp.where(dist >= 0, jnp.exp(lg_f * jnp.maximum(dist, 0.0)), jnp.exp(lg_b * jnp.maximum(-dist, 0.0)))
    idx = lax.broadcasted_iota(jnp.int32, (c_len, 1), 0).astype(F32)
    lgf = log_gamma(0, (c_len, 1))
    lgb = log_gamma(1, (c_len, 1))
    xi_f = jnp.exp(lgf * (idx + 1.0))
    zeta_f = jnp.exp(lgf * (c_len - 1.0 - idx))
    cd_f = jnp.exp(lgf[0:1] * c_len)
    xi_b = jnp.exp(lgb * (c_len - idx))
    zeta_b = jnp.exp(lgb * idx)
    cd_b = jnp.exp(lgb[0:1] * c_len)

    def chunk(cidx):
        start = pl.multiple_of(cidx * c_len, c_len)
        return (q_ref[pl.ds(start, c_len), :], k_ref[pl.ds(start, c_len), :], v_ref[pl.ds(start, c_len), :], start)

    def update_state(state_scr, kc, vc, zeta, cd):
        zv = (vc.astype(F32) * zeta).astype(BF16)
        kv = lax.dot_general(kc, zv, (((0,), (0,)), ((), ())), preferred_element_type=F32)
        state_scr[...] = state_scr[...] * cd + kv

    def left_to_right(cidx):
        qc, kc, vc, start = chunk(cidx)
        scores = lax.dot_general(qc, kc, (((1,), (1,)), ((), ())), preferred_element_type=F32) * decay
        o = jnp.dot(scores.astype(BF16), vc, preferred_element_type=F32)
        o = o + xi_f * jnp.dot(qc, state_f_scr[...].astype(BF16), preferred_element_type=F32)
        update_state(state_f_scr, kc, vc, zeta_f, cd_f)
        return o, start

    def right_to_left(cidx):
        qc, kc, vc, start = chunk(cidx)
        o = xi_b * jnp.dot(qc, state_b_scr[...].astype(BF16), preferred_element_type=F32)
        update_state(state_b_scr, kc, vc, zeta_b, cd_b)
        return o, start

    def finish(y, start):
        y = y * lax.rsqrt(jnp.mean(y * y, axis=-1, keepdims=True) + EPS) * gain_ref[...]
        o_ref[pl.ds(start, c_len), :] = y.astype(BF16)

    state_f_scr[...] = jnp.zeros_like(state_f_scr)
    state_b_scr[...] = jnp.zeros_like(state_b_scr)
    half = n_chunks // 2

    def first_visits(t, carry):
        o_f, start_f = left_to_right(t)
        o_b, start_b = right_to_left(n_chunks - 1 - t)
        acc_scr[pl.ds(start_f, c_len), :] = o_f
        acc_scr[pl.ds(start_b, c_len), :] = o_b
        return carry

    def second_visits(t, carry):
        o_f, start_f = left_to_right(t)
        o_b, start_b = right_to_left(n_chunks - 1 - t)
        finish(acc_scr[pl.ds(start_f, c_len), :] + o_f, start_f)
        finish(acc_scr[pl.ds(start_b, c_len), :] + o_b, start_b)
        return carry

    lax.fori_loop(0, half, first_visits, 0)
    lax.fori_loop(half, n_chunks, second_visits, 0)


def _retention_mixer(x, mod, w_in, decay_logit, gn_gain, w_o, batch, seq):
    n, d = x.shape
    qk_cols = RET_HEADS * RET_QK_DIM
    v_cols = RET_HEADS * RET_V_DIM
    col_scale = jnp.concatenate([jnp.ones((qk_cols,), F32), jnp.full((qk_cols,), RET_QK_DIM ** -0.5, F32),
                                 jnp.ones((2 * v_cols,), F32)])
    w = (w_in * col_scale[None, :]).astype(BF16)
    pos = jnp.arange(seq, dtype=F32)
    inv = 1.0 / (RET_THETA ** jnp.linspace(0.0, 1.0, RET_QK_DIM // 2, dtype=F32))
    ang = pos[:, None] * inv
    tm, tn = 512, qk_cols
    tiles_per_seq = seq // tm
    half = RET_QK_DIM // 2
    extras = [jnp.cos(ang), jnp.sin(ang)]
    extra_specs = [pl.BlockSpec((tm, half), lambda i, j: (i % tiles_per_seq, 0)),
                   pl.BlockSpec((tm, half), lambda i, j: (i % tiles_per_seq, 0))]
    proj = _project(x, mod, 0, w, extras, extra_specs, _ret_proj_epilogue,
                    jax.ShapeDtypeStruct((n, w.shape[1]), BF16), pl.BlockSpec((tm, w.shape[1]), lambda i, j: (i, 0)),
                    tm, tn, seq, "ret_proj")

    n_chunks = seq // RET_CHUNK
    k_off = qk_cols // RET_QK_DIM
    v_off = 2 * qk_cols // RET_V_DIM
    y = pl.pallas_call(
        functools.partial(_ret_kernel, n_chunks=n_chunks),
        grid=(batch, RET_HEADS),
        in_specs=[pl.BlockSpec(memory_space=pltpu.SMEM),
                  pl.BlockSpec((seq, RET_QK_DIM), lambda b, h: (b, h)),
                  pl.BlockSpec((seq, RET_QK_DIM), lambda b, h: (b, k_off + h)),
                  pl.BlockSpec((seq, RET_V_DIM), lambda b, h: (b, v_off + h)),
                  pl.BlockSpec((1, RET_V_DIM), lambda b, h: (0, 0))],
        out_specs=pl.BlockSpec((seq, RET_V_DIM), lambda b, h: (b, h)),
        out_shape=jax.ShapeDtypeStruct((n, v_cols), BF16),
        scratch_shapes=[pltpu.VMEM((seq, RET_V_DIM), F32), pltpu.VMEM((RET_QK_DIM, RET_V_DIM), F32),
                        pltpu.VMEM((RET_QK_DIM, RET_V_DIM), F32)],
        compiler_params=_cparams("parallel", "arbitrary"),
        name="retention",
    )(decay_logit.astype(F32), proj, proj, proj, gn_gain.reshape(1, RET_V_DIM).astype(F32))
    gate_col = (2 * qk_cols + v_cols) // v_cols
    return _out_project(y, 0, v_cols, w_o.astype(BF16), x, mod, 0, seq, 512, "ret_out",
                        gate_src=proj, gate_col=gate_col)


def _conv_pw1_epilogue(acc, j, extras, outs):
    b_ref = extras[0]
    u = acc + b_ref[...]
    dm = u.shape[1] // 2
    outs[0][...] = (u[:, :dm] * jax.nn.sigmoid(u[:, dm:])).astype(BF16)


def _conv_kernel(prev_ref, cur_ref, next_ref, wdw_ref, bdw_ref, lng_ref, lnb_ref, w2_ref, b2_ref, x_ref, mod_ref,
                 o_ref, ext_scr, shift_scr, c_scr, *, tiles_per_seq, row_block):
    tm = cur_ref.shape[0]
    t = pl.program_id(0) % tiles_per_seq
    keep_prev = jnp.where(t == 0, 0.0, 1.0)
    keep_next = jnp.where(t == tiles_per_seq - 1, 0.0, 1.0)
    ext_scr[0:CONV_HALO, :] = prev_ref[...].astype(F32) * keep_prev
    ext_scr[CONV_HALO:CONV_HALO + tm, :] = cur_ref[...].astype(F32)
    ext_scr[CONV_HALO + tm:, :] = next_ref[...].astype(F32) * keep_next
    base = CONV_HALO - CONV_PAD
    span = tm + 2 * CONV_HALO - SUBLANES
    for r in range(1, SUBLANES):
        shift_scr[r - 1, 0:span, :] = ext_scr[r:r + span, :]
    for rb in range(tm // row_block):
        r0 = rb * row_block
        acc = jnp.zeros((row_block, cur_ref.shape[1]), F32) + bdw_ref[...]
        for j in range(CONV_WIDTH):
            phase = (base + j) % SUBLANES
            start = r0 + base + j - phase
            if phase == 0:
                tap = ext_scr[start:start + row_block, :]
            else:
                tap = shift_scr[phase - 1, start:start + row_block, :]
            acc = acc + jnp.tile(wdw_ref[j], (row_block // SUBLANES, 1)) * tap
        c_scr[r0:r0 + row_block, :] = acc
    u = c_scr[...]
    mu = jnp.mean(u, axis=-1, keepdims=True)
    var = jnp.mean(jnp.square(u - mu), axis=-1, keepdims=True)
    u = (u - mu) * lax.rsqrt(var + EPS) * lng_ref[...] + lnb_ref[...]
    u = _silu(u).astype(BF16)
    z = jnp.dot(u, w2_ref[...], preferred_element_type=F32) + b2_ref[...]
    o_ref[...] = x_ref[...] + mod_ref[0][2:3] * z


def _conv_mixer(x, mod, w_pw1, b_pw1, w_dw, b_dw, ln_gain, ln_bias, w_pw2, b_pw2, seq):
    n, d = x.shape
    tm = 512
    u = _project(x, mod, 0, w_pw1.astype(BF16), [b_pw1.reshape(1, 2 * d).astype(F32)],
                 [pl.BlockSpec((1, 2 * d), lambda i, j: (0, 0))], _conv_pw1_epilogue,
                 jax.ShapeDtypeStruct((n, d), BF16), pl.BlockSpec((tm, d), lambda i, j: (i, 0)),
                 tm, 2 * d, seq, "conv_pw1")
    tm = 256
    tiles_per_seq = seq // tm
    halo_per_tile = tm // CONV_HALO
    n_halo = n // CONV_HALO
    wdw = jnp.broadcast_to(w_dw.reshape(CONV_WIDTH, 1, d).astype(F32), (CONV_WIDTH, SUBLANES, d))
    vec = lambda a: a.reshape(1, d).astype(F32)
    row_spec = pl.BlockSpec((1, d), lambda i: (0, 0))
    return pl.pallas_call(
        functools.partial(_conv_kernel, tiles_per_seq=tiles_per_seq, row_block=32),
        grid=(n // tm,),
        in_specs=[pl.BlockSpec((CONV_HALO, d), lambda i: (jnp.maximum(i * halo_per_tile - 1, 0), 0)),
                  pl.BlockSpec((tm, d), lambda i: (i, 0)),
                  pl.BlockSpec((CONV_HALO, d), lambda i: (jnp.minimum((i + 1) * halo_per_tile, n_halo - 1), 0)),
                  pl.BlockSpec((CONV_WIDTH, SUBLANES, d), lambda i: (0, 0, 0)),
                  row_spec, row_spec, row_spec,
                  pl.BlockSpec((d, d), lambda i: (0, 0)),
                  row_spec,
                  pl.BlockSpec((tm, d), lambda i: (i, 0)),
                  pl.BlockSpec((1, 6, d), lambda i: (i // tiles_per_seq, 0, 0))],
        out_specs=pl.BlockSpec((tm, d), lambda i: (i, 0)),
        out_shape=jax.ShapeDtypeStruct((n, d), F32),
        scratch_shapes=[pltpu.VMEM((tm + 2 * CONV_HALO, d), F32),
                        pltpu.VMEM((SUBLANES - 1, tm + 2 * CONV_HALO, d), F32), pltpu.VMEM((tm, d), F32)],
        compiler_params=_cparams("parallel"),
        name="conv_module",
    )(u, u, u, wdw, vec(b_dw), vec(ln_gain), vec(ln_bias), w_pw2.astype(BF16), vec(b_pw2), x, mod)


def _t5_bucket(rel):
    half = T5_BUCKETS // 2
    exact = half // 2
    n = jnp.abs(rel)
    log_ratio = jnp.log(jnp.maximum(n, 1).astype(F32) / exact) / math.log(T5_MAX_DIST / exact)
    large = jnp.minimum(exact + (log_ratio * (half - exact)).astype(jnp.int32), half - 1)
    return jnp.where(rel > 0, half, 0) + jnp.where(n < exact, n, large)


def _bias_tile_kernel(table_ref, idx_ref, o_ref):
    h = pl.program_id(0)
    idx = idx_ref[0]
    acc = jnp.zeros(idx.shape, F32)
    for b in range(T5_BUCKETS):
        acc = jnp.where(idx == b, table_ref[b, h], acc)
    o_ref[0, 0] = acc * LOG2_E


def _diff_kernel(q_ref, k_ref, v_ref, bias_ref, lam_ref, gain_ref, o_ref, s_scr, m_scr, acc_scr, *, n_chunks, out_scale,
                 lambda_init):
    tq = q_ref.shape[0]
    qi = pl.program_id(2)
    q = q_ref[...]
    lane = lax.broadcasted_iota(jnp.int32, (tq, LANES), 1)
    low = lane < DIFF_HEAD_DIM
    zero = jnp.zeros((tq, LANES), BF16)
    lhs = jnp.concatenate([jnp.where(low, q, zero), jnp.where(low, zero, q)], axis=0)

    q_sub = tq // KEY_CHUNK

    def add_bias(s, j):
        tiles = [bias_ref[0, jnp.clip(j - (q_sub * qi + a), -2, 2) + 2] for a in range(q_sub)]
        return s + jnp.concatenate(tiles + tiles, axis=0)

    o = _softmax_pv(lhs, k_ref, v_ref, s_scr, m_scr, acc_scr, n_chunks, add_bias)
    lam_v = lam_ref[...]
    lam = (jnp.exp(jnp.sum(lam_v[0:1] * lam_v[1:2], axis=-1, keepdims=True))
           - jnp.exp(jnp.sum(lam_v[2:3] * lam_v[3:4], axis=-1, keepdims=True)) + lambda_init)
    y = o[:tq] - lam * o[tq:]
    y = y * lax.rsqrt(jnp.mean(y * y, axis=-1, keepdims=True) + EPS) * gain_ref[...] * out_scale
    o_ref[...] = y.astype(BF16)


def _diff_mixer(x, mod, w_qkv, lam_q1, lam_k1, lam_q2, lam_k2, subln_gain, w_o, rel_bias, lambda_init,
                batch, seq):
    n, d = x.shape
    col_scale = jnp.concatenate([jnp.full((d,), DIFF_HEAD_DIM ** -0.5 * LOG2_E, F32), jnp.ones((2 * d,), F32)])
    w = (w_qkv * col_scale[None, :]).astype(BF16)
    tm = 1024
    qkv = _project(x, mod, 0, w, [], [], _plain_epilogue,
                   jax.ShapeDtypeStruct((n, 3 * d), BF16), pl.BlockSpec((tm, 3 * d), lambda i, j: (i, 0)),
                   tm, d, seq, "diff_qkv")

    tb = KEY_CHUNK
    ar = jnp.arange(KEY_CHUNK, dtype=jnp.int32)
    offs = (jnp.arange(5, dtype=jnp.int32) - 2) * KEY_CHUNK
    rel = (ar[None, None, :] + offs[:, None, None]) - ar[None, :, None]
    bucket = _t5_bucket(rel).astype(jnp.int32)
    bias = pl.pallas_call(
        _bias_tile_kernel,
        grid=(DIFF_HEADS, 5),
        in_specs=[pl.BlockSpec(memory_space=pltpu.SMEM),
                  pl.BlockSpec((1, tb, KEY_CHUNK), lambda h, t: (t, 0, 0))],
        out_specs=pl.BlockSpec((1, 1, tb, KEY_CHUNK), lambda h, t: (h, t, 0, 0)),
        out_shape=jax.ShapeDtypeStruct((DIFF_HEADS, 5, tb, KEY_CHUNK), F32),
        compiler_params=_cparams("arbitrary", "arbitrary"),
        name="rel_bias_tiles",
    )(rel_bias.astype(F32), bucket)

    lam = jnp.stack([lam_q1, lam_k1, lam_q2, lam_k2]).astype(F32)
    n_chunks = seq // KEY_CHUNK
    tq = 2 * KEY_CHUNK
    q_tiles = seq // tq
    o = pl.pallas_call(
        functools.partial(_diff_kernel, n_chunks=n_chunks, out_scale=1.0 - lambda_init, lambda_init=lambda_init),
        grid=(batch, DIFF_HEADS, q_tiles),
        in_specs=[pl.BlockSpec((tq, LANES), lambda b, h, i: (b * q_tiles + i, h)),
                  pl.BlockSpec((seq, LANES), lambda b, h, i: (b, DIFF_HEADS + h)),
                  pl.BlockSpec((seq, LANES), lambda b, h, i: (b, 2 * DIFF_HEADS + h)),
                  pl.BlockSpec((1, 5, tb, KEY_CHUNK), lambda b, h, i: (h, 0, 0, 0)),
                  pl.BlockSpec((4, DIFF_HEAD_DIM), lambda b, h, i: (0, 0)),
                  pl.BlockSpec((1, LANES), lambda b, h, i: (0, 0))],
        out_specs=pl.BlockSpec((tq, LANES), lambda b, h, i: (b * q_tiles + i, h)),
        out_shape=jax.ShapeDtypeStruct((n, d), BF16),
        scratch_shapes=[pltpu.VMEM((n_chunks, 2 * tq, KEY_CHUNK), F32),
                        pltpu.VMEM((2 * tq, LANES), F32), pltpu.VMEM((2 * tq, 2 * LANES), F32)],
        compiler_params=_cparams("parallel", "parallel", "arbitrary"),
        name="diff_attention",
    )(qkv, qkv, qkv, bias, lam, subln_gain.reshape(1, LANES).astype(F32))
    return _out_project(o, 0, d, w_o.astype(BF16), x, mod, 0, seq, 1024, "diff_out")


def _final_norm_kernel(x_ref, g_ref, o_ref):
    x = x_ref[...]
    o_ref[...] = x * lax.rsqrt(jnp.mean(x * x, axis=-1, keepdims=True) + EPS) * g_ref[...]


def _final_norm(x, gain, tm):
    n, d = x.shape
    return pl.pallas_call(
        _final_norm_kernel,
        grid=(n // tm,),
        in_specs=[pl.BlockSpec((tm, d), lambda i: (i, 0)), pl.BlockSpec((1, d), lambda i: (0, 0))],
        out_specs=pl.BlockSpec((tm, d), lambda i: (i, 0)),
        out_shape=jax.ShapeDtypeStruct((n, d), F32),
        compiler_params=_cparams("parallel"),
        name="final_norm",
    )(x, gain.reshape(1, d).astype(F32))


def kernel(x, c, mod_w, mod_b, att_w_qkv, att_q_gain, att_k_gain, att_w_o, ret_w_in, ret_decay_logit, ret_gn_gain, ret_w_o, cnv_w_pw1, cnv_b_pw1, cnv_w_dw, cnv_b_dw, cnv_ln_gain, cnv_ln_bias, cnv_w_pw2, cnv_b_pw2, dif_w_qkv, dif_lam_q1, dif_lam_k1, dif_lam_q2, dif_lam_k2, dif_subln_gain, dif_w_o, rel_bias, ffn_w_gate, ffn_w_up, ffn_w_down, moe_w_router, moe_b_router, moe_w_gate, moe_w_up, moe_w_down, final_gain):
    batch, seq, d = x.shape
    depth = mod_w.shape[0]
    xf = x.reshape(batch * seq, d)
    mods = _modulation(c, mod_w, mod_b)
    moe_gate, moe_up, moe_down = moe_w_gate.astype(BF16), moe_w_up.astype(BF16), moe_w_down.astype(BF16)
    for i in range(depth):
        mod = mods[i]
        kind, j = i % 4, i // 4
        if kind == 0:
            xf = _attention_mixer(xf, mod, att_w_qkv[j], att_q_gain[j], att_k_gain[j], att_w_o[j], batch, seq)
        elif kind == 1:
            xf = _retention_mixer(xf, mod, ret_w_in[j], ret_decay_logit[j], ret_gn_gain[j], ret_w_o[j], batch, seq)
        elif kind == 2:
            xf = _conv_mixer(xf, mod, cnv_w_pw1[j], cnv_b_pw1[j], cnv_w_dw[j], cnv_b_dw[j], cnv_ln_gain[j],
                             cnv_ln_bias[j], cnv_w_pw2[j], cnv_b_pw2[j], seq)
        else:
            lambda_init = 0.8 - 0.6 * math.exp(-0.3 * i)
            xf = _diff_mixer(xf, mod, dif_w_qkv[j], dif_lam_q1[j], dif_lam_k1[j], dif_lam_q2[j], dif_lam_k2[j],
                             dif_subln_gain[j], dif_w_o[j], rel_bias, lambda_init, batch, seq)
        if i % 2 == 0:
            m = i // 2
            xf = _ffn(xf, mod, ffn_w_gate[m].astype(BF16), ffn_w_up[m].astype(BF16), ffn_w_down[m].astype(BF16),
                      seq, 512)
        else:
            m = i // 2
            xf = _moe(xf, mod, moe_w_router[m], moe_b_router[m], moe_gate, moe_up, moe_down, m, seq,
                      out_gain=final_gain if i == depth - 1 else None)
    if depth % 2 == 1:
        xf = _final_norm(xf, final_gain, 512)
    return xf.reshape(batch, seq, d)
```

```python
import functools
import math

import jax
import jax.numpy as jnp
from jax import lax
from jax.experimental import pallas as pl
from jax.experimental.pallas import tpu as pltpu

F32 = jnp.float32
BF16 = jnp.bfloat16

EPS = 1e-6
GRID_W = 64

ATT_HEADS = 16
ATT_KV_HEADS = 4
ATT_HEAD_DIM = 64
ROPE_THETA = 10000.0

RET_HEADS = 4
RET_QK_DIM = 256
RET_V_DIM = 512
RET_THETA = 10000.0
RET_CHUNK = 256

CONV_WIDTH = 31
CONV_PAD = CONV_WIDTH // 2
CONV_HALO = 16

DIFF_HEADS = 8
DIFF_HEAD_DIM = 64
T5_BUCKETS = 32
T5_MAX_DIST = 128

N_EXPERTS = 8
LANES = 128
SUBLANES = 8
KEY_CHUNK = 256
LOG2_E = 1.4426950408889634
VMEM_LIMIT_BYTES = 56 * 1024 * 1024

TILE_PROJ = 512
TILE_PROJ_NARROW = 1024
TILE_ROUTE = 512
TILE_SLOT = 512
TILE_EXPERT_HIDDEN = 1792
TILE_CONV = 256
CONV_ROW_BLOCK = 32
TILE_ATT_Q = 256
TILE_DIFF_Q = 512


def _cparams(*sem):
    return pltpu.CompilerParams(dimension_semantics=sem, vmem_limit_bytes=VMEM_LIMIT_BYTES)


def _silu(v):
    return v * jax.nn.sigmoid(v)


def _norm_mod(x, mod, which):
    shift = mod[3 * which:3 * which + 1]
    scale = mod[3 * which + 1:3 * which + 2]
    y = x * lax.rsqrt(jnp.mean(x * x, axis=-1, keepdims=True) + EPS)
    return y * (1.0 + scale) + shift


def _split_dot(a, b):
    hi = a.astype(BF16)
    lo = (a - hi.astype(F32)).astype(BF16)
    return (jnp.dot(hi, b, preferred_element_type=F32) + jnp.dot(lo, b, preferred_element_type=F32))


def _mod_kernel(c_ref, w_ref, b_ref, o_ref):
    cond = _silu(c_ref[...])
    o_ref[0] = jnp.dot(cond, w_ref[0], preferred_element_type=F32, precision=lax.Precision.HIGHEST) + b_ref[0]


def _modulation(c, mod_w, mod_b):
    depth, d, d6 = mod_w.shape
    b = c.shape[0]
    tn = d
    out = pl.pallas_call(
        _mod_kernel,
        grid=(depth, d6 // tn),
        in_specs=[pl.BlockSpec((b, d), lambda i, j: (0, 0)),
                  pl.BlockSpec((1, d, tn), lambda i, j: (i, 0, j)),
                  pl.BlockSpec((1, 1, tn), lambda i, j: (i, 0, j))],
        out_specs=pl.BlockSpec((1, b, tn), lambda i, j: (i, 0, j)),
        out_shape=jax.ShapeDtypeStruct((depth, b, d6), F32),
        compiler_params=_cparams("arbitrary", "arbitrary"),
        name="modulation",
    )(c, mod_w, mod_b.reshape(depth, 1, d6))
    return out.reshape(depth, b, 6, d)


def _proj_kernel(x_ref, mod_ref, w_ref, *rest, which, n_extra, epilogue, tn):
    extras = rest[:n_extra]
    outs = rest[n_extra:]
    h = _norm_mod(x_ref[...], mod_ref[0], which).astype(BF16)
    for j in range(w_ref.shape[1] // tn):
        acc = jnp.dot(h, w_ref[:, j * tn:(j + 1) * tn], preferred_element_type=F32)
        epilogue(acc, j, extras, outs)


def _project(x, mod, which, w, extras, extra_specs, epilogue, out_shapes, out_specs, tm, tn, seq, name):
    n, d = x.shape
    c = w.shape[1]
    tiles_per_seq = seq // tm
    kern = functools.partial(_proj_kernel, which=which, n_extra=len(extras), epilogue=epilogue, tn=tn)
    return pl.pallas_call(
        kern,
        grid=(n // tm, 1),
        in_specs=[pl.BlockSpec((tm, d), lambda i, j: (i, 0)),
                  pl.BlockSpec((1, 6, d), lambda i, j: (i // tiles_per_seq, 0, 0)),
                  pl.BlockSpec((d, c), lambda i, j: (0, 0), pipeline_mode=pl.Buffered(1))] + extra_specs,
        out_specs=out_specs,
        out_shape=out_shapes,
        compiler_params=_cparams("parallel", "arbitrary"),
        name=name,
    )(x, mod, w, *extras)


def _plain_epilogue(acc, j, extras, outs):
    tn = acc.shape[1]
    outs[0][:, j * tn:(j + 1) * tn] = acc.astype(outs[0].dtype)


def _oproj_kernel(*refs, which, gated, has_bias):
    refs = list(refs)
    y_ref = refs.pop(0)
    g_ref = refs.pop(0) if gated else None
    w_ref = refs.pop(0)
    b_ref = refs.pop(0) if has_bias else None
    x_ref, mod_ref, o_ref = refs
    y = y_ref[...]
    if gated:
        y = (y.astype(F32) * _silu(g_ref[...].astype(F32))).astype(BF16)
    z = jnp.dot(y, w_ref[...], preferred_element_type=F32)
    if has_bias:
        z = z + b_ref[...]
    gate = mod_ref[0][3 * which + 2:3 * which + 3]
    o_ref[...] = x_ref[...] + gate * z


def _out_project(y, y_col, k, w, x, mod, which, seq, tm, name, gate_src=None, gate_col=0, bias=None):
    n, d = x.shape
    tiles_per_seq = seq // tm
    args = [y]
    specs = [pl.BlockSpec((tm, k), lambda i: (i, y_col))]
    if gate_src is not None:
        args.append(gate_src)
        specs.append(pl.BlockSpec((tm, k), lambda i: (i, gate_col)))
    args.append(w)
    specs.append(pl.BlockSpec((k, d), lambda i: (0, 0)))
    if bias is not None:
        args.append(bias.reshape(1, d))
        specs.append(pl.BlockSpec((1, d), lambda i: (0, 0)))
    args += [x, mod]
    specs += [pl.BlockSpec((tm, d), lambda i: (i, 0)),
              pl.BlockSpec((1, 6, d), lambda i: (i // tiles_per_seq, 0, 0))]
    kern = functools.partial(_oproj_kernel, which=which, gated=gate_src is not None, has_bias=bias is not None)
    return pl.pallas_call(
        kern,
        grid=(n // tm,),
        in_specs=specs,
        out_specs=pl.BlockSpec((tm, d), lambda i: (i, 0)),
        out_shape=jax.ShapeDtypeStruct((n, d), F32),
        compiler_params=_cparams("parallel"),
        name=name,
    )(*args)


def _ffn_kernel(x_ref, mod_ref, wg_ref, wu_ref, wd_ref, o_ref):
    x = x_ref[...]
    mod = mod_ref[0]
    h = _norm_mod(x, mod, 1).astype(BF16)
    g = jnp.dot(h, wg_ref[...], preferred_element_type=F32)
    u = jnp.dot(h, wu_ref[...], preferred_element_type=F32)
    a = (_silu(g) * u).astype(BF16)
    o_ref[...] = x + mod[5:6] * jnp.dot(a, wd_ref[...], preferred_element_type=F32)


def _ffn(x, mod, wg, wu, wd, seq, tm):
    n, d = x.shape
    ff = wg.shape[1]
    tiles_per_seq = seq // tm
    resident = pl.Buffered(1)
    return pl.pallas_call(
        _ffn_kernel,
        grid=(n // tm,),
        in_specs=[pl.BlockSpec((tm, d), lambda i: (i, 0)),
                  pl.BlockSpec((1, 6, d), lambda i: (i // tiles_per_seq, 0, 0)),
                  pl.BlockSpec((d, ff), lambda i: (0, 0), pipeline_mode=resident),
                  pl.BlockSpec((d, ff), lambda i: (0, 0), pipeline_mode=resident),
                  pl.BlockSpec((ff, d), lambda i: (0, 0), pipeline_mode=resident)],
        out_specs=pl.BlockSpec((tm, d), lambda i: (i, 0)),
        out_shape=jax.ShapeDtypeStruct((n, d), F32),
        compiler_params=_cparams("parallel"),
        name="dense_ffn",
    )(x, mod, wg, wu, wd)


def _router_kernel(x_ref, mod_ref, wr_ref, br_ref, tri_ref, route_ref, count_ref, carry_scr):
    @pl.when(pl.program_id(0) == 0)
    def _():
        carry_scr[...] = jnp.zeros_like(carry_scr)

    h = _norm_mod(x_ref[...], mod_ref[0], 1)
    h_hi = h.astype(BF16)
    h_lo = (h - h_hi.astype(F32)).astype(BF16)
    logits = (jnp.dot(h_hi, wr_ref[0], preferred_element_type=F32)
              + (jnp.dot(h_hi, wr_ref[1], preferred_element_type=F32)
                 + jnp.dot(h_lo, wr_ref[0], preferred_element_type=F32))) + br_ref[...]
    lane = lax.broadcasted_iota(jnp.int32, logits.shape, 1)
    neg = jnp.float32(-jnp.inf)
    logits = jnp.where(lane < N_EXPERTS, logits, neg)
    m1 = jnp.max(logits, axis=-1, keepdims=True)
    i1 = jnp.min(jnp.where(logits == m1, lane, LANES), axis=-1, keepdims=True)
    rest = jnp.where(lane == i1, neg, logits)
    m2 = jnp.max(rest, axis=-1, keepdims=True)
    i2 = jnp.min(jnp.where(rest == m2, lane, LANES), axis=-1, keepdims=True)
    e2 = jnp.exp(m2 - m1)
    w1 = 1.0 / (1.0 + e2)
    w2 = e2 / (1.0 + e2)
    sel = jnp.where((lane == i1) | (lane == i2), 1.0, 0.0)
    incl = jnp.dot(tri_ref[...], sel.astype(BF16), preferred_element_type=F32)
    excl = incl - sel + carry_scr[...]
    r1 = jnp.sum(jnp.where(lane == i1, excl, 0.0), axis=-1, keepdims=True)
    r2 = jnp.sum(jnp.where(lane == i2, excl, 0.0), axis=-1, keepdims=True)
    carry_scr[...] += incl[incl.shape[0] - 1:, :]
    count_ref[...] = carry_scr[...]
    fields = (i1.astype(F32), i2.astype(F32), w1, w2, r1, r2)
    route = jnp.zeros(logits.shape, F32)
    for k, field in enumerate(fields):
        route = jnp.where(lane == k, field, route)
    route_ref[...] = route


def _router(x, mod, w_router, b_router, seq, tm):
    n, d = x.shape
    tiles_per_seq = seq // tm
    wr = jnp.pad(w_router, ((0, 0), (0, LANES - N_EXPERTS))).astype(F32)
    wr_hi = wr.astype(BF16)
    wr = jnp.stack([wr_hi, (wr - wr_hi.astype(F32)).astype(BF16)])
    br = jnp.pad(b_router, (0, LANES - N_EXPERTS)).reshape(1, LANES)
    tri = (jnp.arange(tm)[:, None] >= jnp.arange(tm)[None, :]).astype(BF16)
    return pl.pallas_call(
        _router_kernel,
        grid=(n // tm,),
        in_specs=[pl.BlockSpec((tm, d), lambda i: (i, 0)),
                  pl.BlockSpec((1, 6, d), lambda i: (i // tiles_per_seq, 0, 0)),
                  pl.BlockSpec((2, d, LANES), lambda i: (0, 0, 0)),
                  pl.BlockSpec((1, LANES), lambda i: (0, 0)),
                  pl.BlockSpec((tm, tm), lambda i: (0, 0))],
        out_specs=[pl.BlockSpec((tm, LANES), lambda i: (i, 0)), pl.BlockSpec((1, LANES), lambda i: (0, 0))],
        out_shape=[jax.ShapeDtypeStruct((n, LANES), F32), jax.ShapeDtypeStruct((1, LANES), F32)],
        scratch_shapes=[pltpu.VMEM((1, LANES), F32)],
        compiler_params=_cparams("arbitrary"),
        name="moe_router",
    )(x, mod, wr, br, tri)


def _dispatch_kernel(zero_rows_ref, d1_ref, d2_ref, x_ref, mod_ref, xs_ref, h_scr, zero_scr, sems):
    tm = x_ref.shape[0]

    @pl.when(pl.program_id(0) == 0)
    def _():
        zero_scr[...] = jnp.zeros_like(zero_scr)
        te = zero_scr.shape[0]
        for k in range(zero_rows_ref.shape[0]):
            start = pl.multiple_of(zero_rows_ref[k], te)
            copy = pltpu.make_async_copy(zero_scr, xs_ref.at[pl.ds(start, te), :], sems.at[2])
            copy.start()
            copy.wait()

    h_scr[...] = _norm_mod(x_ref[...], mod_ref[0], 1)

    def row_copy(r, dest, which):
        return pltpu.make_async_copy(h_scr.at[pl.ds(r, 1), :], xs_ref.at[pl.ds(dest, 1), :], sems.at[which])

    def issue(r, carry):
        row_copy(r, d1_ref[0, 0, r], 0).start()
        row_copy(r, d2_ref[0, 0, r], 1).start()
        return carry

    lax.fori_loop(0, tm, issue, 0, unroll=4)
    for which in range(2):
        pltpu.make_async_copy(h_scr, xs_ref.at[pl.ds(0, tm), :], sems.at[which]).wait()


def _expert_kernel(tile_e_ref, n_act_ref, x_ref, wg_ref, wu_ref, wd_ref, o_ref, *, tf):
    del tile_e_ref
    s = pl.program_id(0)

    @pl.when(s < n_act_ref[0])
    def _():
        h = x_ref[...].astype(BF16)
        out = None
        for f in range(wg_ref.shape[3] // tf):
            cols = slice(f * tf, (f + 1) * tf)
            g = jnp.dot(h, wg_ref[0, 0, :, cols], preferred_element_type=F32)
            u = jnp.dot(h, wu_ref[0, 0, :, cols], preferred_element_type=F32)
            a = (_silu(g) * u).astype(BF16)
            part = jnp.dot(a, wd_ref[0, 0, cols, :], preferred_element_type=F32)
            out = part if out is None else out + part
        o_ref[...] = out

    @pl.when(s >= n_act_ref[0])
    def _():
        o_ref[...] = jnp.zeros_like(o_ref)


def _combine_kernel(d1_ref, d2_ref, route_ref, x_ref, mod_ref, *rest, out_norm):
    if out_norm:
        gain_ref, ys_ref, o_ref, buf, sems = rest
    else:
        ys_ref, o_ref, buf, sems = rest
    tm = x_ref.shape[0]

    def row_copy(r, src, which):
        return pltpu.make_async_copy(ys_ref.at[pl.ds(src, 1), :], buf.at[which, pl.ds(r, 1), :], sems.at[which])

    def issue(r, carry):
        row_copy(r, d1_ref[0, 0, r], 0).start()
        row_copy(r, d2_ref[0, 0, r], 1).start()
        return carry

    lax.fori_loop(0, tm, issue, 0, unroll=4)
    for which in range(2):
        pltpu.make_async_copy(ys_ref.at[pl.ds(0, tm), :], buf.at[which], sems.at[which]).wait()
    route = route_ref[...]
    y = route[:, 2:3] * buf[0] + route[:, 3:4] * buf[1]
    out = x_ref[...] + mod_ref[0][5:6] * y
    if out_norm:
        out = out * lax.rsqrt(jnp.mean(out * out, axis=-1, keepdims=True) + EPS) * gain_ref[...]
    o_ref[...] = out


def _moe(x, mod, w_router, b_router, w_gate, w_up, w_down, layer, seq, out_gain=None):
    n, d = x.shape
    _, n_e, _, ff = w_gate.shape
    tm, te, tf = TILE_ROUTE, TILE_SLOT, TILE_EXPERT_HIDDEN
    tiles_per_seq = seq // tm
    n_tok_tiles = n // tm
    n_slots = 2 * n + n_e * te
    n_tiles = n_slots // te

    route, counts = _router(x, mod, w_router, b_router, seq, tm)

    counts = counts[0, :n_e].astype(jnp.int32)
    padded = ((counts + te - 1) // te) * te
    ends = jnp.cumsum(padded)
    offsets = ends - padded
    n_act = (ends[-1] // te).astype(jnp.int32)
    tile_start = jnp.arange(n_tiles, dtype=jnp.int32) * te
    tile_e = jnp.sum((tile_start[:, None] >= ends[None, :]).astype(jnp.int32), axis=1)
    tile_e = jnp.minimum(tile_e, tile_e[jnp.maximum(n_act - 1, 0)]).astype(jnp.int32)
    e1 = route[:, 0].astype(jnp.int32)
    e2 = route[:, 1].astype(jnp.int32)
    dest1 = (offsets[e1] + route[:, 4].astype(jnp.int32)).reshape(n_tok_tiles, 1, tm)
    dest2 = (offsets[e2] + route[:, 5].astype(jnp.int32)).reshape(n_tok_tiles, 1, tm)
    n_act = n_act.reshape(1)

    slot_spec = pl.BlockSpec((1, 1, tm), lambda i: (i, 0, 0), memory_space=pltpu.SMEM)
    zero_rows = jnp.concatenate([jnp.maximum(ends - te, 0),
                                 (n_tiles - n_e + jnp.arange(n_e, dtype=jnp.int32)) * te]).astype(jnp.int32)
    xs = pl.pallas_call(
        _dispatch_kernel,
        grid=(n_tok_tiles,),
        in_specs=[pl.BlockSpec(memory_space=pltpu.SMEM), slot_spec, slot_spec,
                  pl.BlockSpec((tm, d), lambda i: (i, 0)),
                  pl.BlockSpec((1, 6, d), lambda i: (i // tiles_per_seq, 0, 0))],
        out_specs=pl.BlockSpec(memory_space=pl.ANY),
        out_shape=jax.ShapeDtypeStruct((n_slots, d), F32),
        scratch_shapes=[pltpu.VMEM((tm, d), F32), pltpu.VMEM((te, d), F32), pltpu.SemaphoreType.DMA((3,))],
        compiler_params=_cparams("arbitrary"),
        name="moe_dispatch",
    )(zero_rows, dest1, dest2, x, mod)

    def live(s, n_act_ref):
        return jnp.minimum(s, jnp.maximum(n_act_ref[0] - 1, 0))

    def expert_weights(shape):
        return pl.BlockSpec(shape, lambda s, te_ref, na_ref: (layer, te_ref[s], 0, 0), pipeline_mode=pl.Buffered(1))

    ys = pl.pallas_call(
        functools.partial(_expert_kernel, tf=tf),
        grid_spec=pltpu.PrefetchScalarGridSpec(
            num_scalar_prefetch=2,
            grid=(n_tiles,),
            in_specs=[pl.BlockSpec((te, d), lambda s, te_ref, na_ref: (live(s, na_ref), 0)),
                      expert_weights((1, 1, d, ff)), expert_weights((1, 1, d, ff)), expert_weights((1, 1, ff, d))],
            out_specs=pl.BlockSpec((te, d), lambda s, te_ref, na_ref: (s, 0))),
        out_shape=jax.ShapeDtypeStruct((n_slots, d), F32),
        compiler_params=_cparams("arbitrary"),
        name="moe_experts",
    )(tile_e, n_act, xs, w_gate, w_up, w_down)

    out_norm = out_gain is not None
    gain_args = [out_gain.reshape(1, d).astype(F32)] if out_norm else []
    gain_specs = [pl.BlockSpec((1, d), lambda i: (0, 0))] if out_norm else []
    return pl.pallas_call(
        functools.partial(_combine_kernel, out_norm=out_norm),
        grid=(n_tok_tiles,),
        in_specs=[slot_spec, slot_spec,
                  pl.BlockSpec((tm, LANES), lambda i: (i, 0)),
                  pl.BlockSpec((tm, d), lambda i: (i, 0)),
                  pl.BlockSpec((1, 6, d), lambda i: (i // tiles_per_seq, 0, 0))] + gain_specs
                 + [pl.BlockSpec(memory_space=pl.ANY)],
        out_specs=pl.BlockSpec((tm, d), lambda i: (i, 0)),
        out_shape=jax.ShapeDtypeStruct((n, d), F32),
        scratch_shapes=[pltpu.VMEM((2, tm, d), F32), pltpu.SemaphoreType.DMA((2,))],
        compiler_params=_cparams("arbitrary"),
        name="moe_combine",
    )(dest1, dest2, route, x, mod, *gain_args, ys)


def _softmax_pv(lhs, k_ref, v_ref, s_scr, m_scr, acc_scr, n_chunks, bias_fn=None):
    rows = lhs.shape[0]
    group = min(8, n_chunks)
    m_scr[...] = jnp.full(m_scr.shape, -jnp.inf, F32)

    for j in range(n_chunks):
        kc = k_ref[j * KEY_CHUNK:(j + 1) * KEY_CHUNK, :]
        s = lax.dot_general(lhs, kc, (((1,), (1,)), ((), ())), preferred_element_type=F32)
        if bias_fn is not None:
            s = bias_fn(s, j)
        s_scr[j] = s
        m_scr[...] = jnp.maximum(m_scr[...], jnp.maximum(s[:, :LANES], s[:, LANES:]))
    m_scr[...] = jnp.broadcast_to(jnp.max(m_scr[...], axis=-1, keepdims=True), m_scr.shape)
    acc_scr[...] = jnp.zeros_like(acc_scr)
    ones = jnp.ones((group * KEY_CHUNK, LANES), BF16)

    def pv_body(t, carry):
        mb = m_scr[...]
        parts = []
        for u in range(group):
            s = s_scr[group * t + u]
            parts.append(jnp.exp2(s[:, :LANES] - mb).astype(BF16))
            parts.append(jnp.exp2(s[:, LANES:] - mb).astype(BF16))
        p = jnp.concatenate(parts, axis=1)
        start = pl.multiple_of(t * (group * KEY_CHUNK), group * KEY_CHUNK)
        v_aug = jnp.concatenate([v_ref[pl.ds(start, group * KEY_CHUNK), :], ones], axis=1)
        acc_scr[...] += jnp.dot(p, v_aug, preferred_element_type=F32)
        return carry

    lax.fori_loop(0, n_chunks // group, pv_body, 0)
    acc = acc_scr[...]
    return acc[:, :LANES] / acc[:, LANES:]


def _att_qkv_epilogue(acc, j, extras, outs, n_qk):
    gain_ref, cos_ref, sin_ref, g1_ref, g2_ref = extras
    q_ref, k_ref, v_ref = outs
    qk = acc[:, :n_qk]
    ms = _split_dot(qk * qk, g1_ref[...])
    r = lax.rsqrt(ms + EPS)
    y = qk * _split_dot(r, g2_ref[...]) * gain_ref[...]
    cos = cos_ref[...]
    sin = sin_ref[...]
    lane = lax.broadcasted_iota(jnp.int32, cos.shape, 1)
    first_half = (lane & (ATT_HEAD_DIM // 2)) == 0
    nq = q_ref.shape[1]
    for c in range(n_qk // LANES):
        yc = y[:, c * LANES:(c + 1) * LANES]
        partner = jnp.where(first_half, pltpu.roll(yc, LANES - ATT_HEAD_DIM // 2, 1),
                            pltpu.roll(yc, ATT_HEAD_DIM // 2, 1))
        out = (yc * cos + partner * sin).astype(BF16)
        if c * LANES < nq:
            q_ref[:, c * LANES:(c + 1) * LANES] = out
        else:
            k_ref[:, c * LANES - nq:(c + 1) * LANES - nq] = out
    v_ref[...] = acc[:, n_qk:].astype(BF16)


def _att_kernel(q_ref, k_ref, v_ref, o_ref, s_scr, m_scr, acc_scr, *, n_chunks):
    tq = q_ref.shape[0]
    q = q_ref[...]
    lane = lax.broadcasted_iota(jnp.int32, (tq, LANES), 1)
    low = lane < ATT_HEAD_DIM
    zero = jnp.zeros((tq, LANES), BF16)
    parts = []
    for pair in range(2):
        qp = q[:, pair * LANES:(pair + 1) * LANES]
        parts.append(jnp.where(low, qp, zero))
        parts.append(jnp.where(low, zero, qp))
    lhs = jnp.concatenate(parts, axis=0)
    o = _softmax_pv(lhs, k_ref, v_ref, s_scr, m_scr, acc_scr, n_chunks)
    for pair in range(2):
        o_ref[:, pair * LANES:(pair + 1) * LANES] = jnp.where(
            low, o[(2 * pair) * tq:(2 * pair + 1) * tq], o[(2 * pair + 1) * tq:(2 * pair + 2) * tq]).astype(BF16)


def _attention_mixer(x, mod, w_qkv, q_gain, k_gain, w_o, batch, seq):
    n, d = x.shape
    hd = ATT_HEAD_DIM
    half = hd // 2
    n_freq = half // 2
    groups = ATT_HEADS // ATT_KV_HEADS
    perm = jnp.concatenate([jnp.arange(n_freq), half + jnp.arange(n_freq),
                            n_freq + jnp.arange(n_freq), half + n_freq + jnp.arange(n_freq)])
    q_cols = (jnp.arange(ATT_HEADS)[:, None] * hd + perm[None, :]).reshape(-1)
    k_base = ATT_HEADS * hd
    v_base = k_base + ATT_KV_HEADS * hd
    kv_rep = jnp.repeat(jnp.arange(ATT_KV_HEADS), 2)
    k_cols = (k_base + kv_rep[:, None] * hd + perm[None, :]).reshape(-1)
    v_cols = (v_base + kv_rep[:, None] * hd + jnp.arange(hd)[None, :]).reshape(-1)
    w = w_qkv[:, jnp.concatenate([q_cols, k_cols, v_cols])].astype(BF16)
    n_q = ATT_HEADS * hd
    n_kv = 2 * ATT_KV_HEADS * hd
    n_qk = n_q + n_kv
    gain_row = jnp.concatenate([jnp.tile(q_gain[perm] * (hd ** -0.5 * LOG2_E), ATT_HEADS),
                                jnp.tile(k_gain[perm], 2 * ATT_KV_HEADS)]).reshape(1, n_qk).astype(F32)
    pos = jnp.arange(seq, dtype=jnp.int32)
    row = (pos // GRID_W).astype(F32)
    col = (pos % GRID_W).astype(F32)
    inv = ROPE_THETA ** (-jnp.arange(n_freq, dtype=F32) / n_freq)
    ang = jnp.concatenate([row[:, None] * inv, col[:, None] * inv], axis=-1)
    cos_t = jnp.tile(jnp.cos(ang), (1, 2 * LANES // hd))
    sin_t = jnp.tile(jnp.concatenate([-jnp.sin(ang), jnp.sin(ang)], axis=-1), (1, LANES // hd))
    grp = jnp.arange(n_qk) // hd
    g1 = (grp[:, None] == jnp.arange(LANES)[None, :]).astype(F32) / hd
    g2 = (jnp.arange(LANES)[:, None] == grp[None, :]).astype(F32)
    tm = TILE_PROJ
    tiles_per_seq = seq // tm
    extras = [gain_row, cos_t, sin_t, g1.astype(BF16), g2.astype(BF16)]
    extra_specs = [pl.BlockSpec((1, n_qk), lambda i, j: (0, 0)),
                   pl.BlockSpec((tm, LANES), lambda i, j: (i % tiles_per_seq, 0)),
                   pl.BlockSpec((tm, LANES), lambda i, j: (i % tiles_per_seq, 0)),
                   pl.BlockSpec((n_qk, LANES), lambda i, j: (0, 0)),
                   pl.BlockSpec((LANES, n_qk), lambda i, j: (0, 0))]
    q, k, v = _project(
        x, mod, 0, w, extras, extra_specs, functools.partial(_att_qkv_epilogue, n_qk=n_qk),
        [jax.ShapeDtypeStruct((n, n_q), BF16), jax.ShapeDtypeStruct((n, n_kv), BF16),
         jax.ShapeDtypeStruct((n, n_kv), BF16)],
        [pl.BlockSpec((tm, n_q), lambda i, j: (i, 0)), pl.BlockSpec((tm, n_kv), lambda i, j: (i, 0)),
         pl.BlockSpec((tm, n_kv), lambda i, j: (i, 0))],
        tm, w.shape[1], seq, "att_qkv")

    tq = TILE_ATT_Q
    n_chunks = seq // KEY_CHUNK
    q_tiles = seq // tq
    o = pl.pallas_call(
        functools.partial(_att_kernel, n_chunks=n_chunks),
        grid=(batch, ATT_KV_HEADS, q_tiles),
        in_specs=[pl.BlockSpec((tq, groups * hd), lambda b, g, i: (b * q_tiles + i, g)),
                  pl.BlockSpec((seq, LANES), lambda b, g, i: (b, g)),
                  pl.BlockSpec((seq, LANES), lambda b, g, i: (b, g))],
        out_specs=pl.BlockSpec((tq, groups * hd), lambda b, g, i: (b * q_tiles + i, g)),
        out_shape=jax.ShapeDtypeStruct((n, n_q), BF16),
        scratch_shapes=[pltpu.VMEM((n_chunks, groups * tq, KEY_CHUNK), F32),
                        pltpu.VMEM((groups * tq, LANES), F32), pltpu.VMEM((groups * tq, 2 * LANES), F32)],
        compiler_params=_cparams("parallel", "parallel", "arbitrary"),
        name="gqa_attention",
    )(q, k, v)
    return _out_project(o, 0, n_q, w_o.astype(BF16), x, mod, 0, seq, TILE_PROJ_NARROW, "att_out")


def _ret_proj_epilogue(acc, j, extras, outs):
    cos_ref, sin_ref = extras
    o_ref = outs[0]
    tn = acc.shape[1]
    base = j * tn
    if j < 2:
        cos = cos_ref[...]
        sin = sin_ref[...]
        half = RET_QK_DIM // 2
        for hh in range(RET_HEADS):
            lo = hh * RET_QK_DIM
            x1 = acc[:, lo:lo + half]
            x2 = acc[:, lo + half:lo + RET_QK_DIM]
            o_ref[:, base + lo:base + lo + half] = (x1 * cos - x2 * sin).astype(BF16)
            o_ref[:, base + lo + half:base + lo + RET_QK_DIM] = (x1 * sin + x2 * cos).astype(BF16)
    else:
        o_ref[:, base:base + tn] = acc.astype(BF16)


def _ret_kernel(logit_ref, q_ref, k_ref, v_ref, gain_ref, o_ref, acc_scr, state_f_scr, state_b_scr, *, n_chunks):
    h = pl.program_id(1)
    c_len = RET_CHUNK

    def log_gamma(direction, shape):
        z = jnp.full(shape, logit_ref[direction, h], F32)
        return jnp.minimum(z, 0.0) - jnp.log1p(jnp.exp(-jnp.abs(z)))

    ri = lax.broadcasted_iota(jnp.int32, (c_len, c_len), 0)
    ci = lax.broadcasted_iota(jnp.int32, (c_len, c_len), 1)
    dist = (ri - ci).astype(F32)
    lg_f = log_gamma(0, (c_len, c_len))
    lg_b = log_gamma(1, (c_len, c_len))
    decay = jnp.where(dist >= 0, jnp.exp(lg_f * jnp.maximum(dist, 0.0)), jnp.exp(lg_b * jnp.maximum(-dist, 0.0)))
    idx = lax.broadcasted_iota(jnp.int32, (c_len, 1), 0).astype(F32)
    lgf = log_gamma(0, (c_len, 1))
    lgb = log_gamma(1, (c_len, 1))
    xi_f = jnp.exp(lgf * (idx + 1.0))
    zeta_f = jnp.exp(lgf * (c_len - 1.0 - idx))
    cd_f = jnp.exp(lgf[0:1] * c_len)
    xi_b = jnp.exp(lgb * (c_len - idx))
    zeta_b = jnp.exp(lgb * idx)
    cd_b = jnp.exp(lgb[0:1] * c_len)

    def chunk(cidx):
        start = pl.multiple_of(cidx * c_len, c_len)
        return (q_ref[pl.ds(start, c_len), :], k_ref[pl.ds(start, c_len), :], v_ref[pl.ds(start, c_len), :], start)

    def update_state(state_scr, kc, vc, zeta, cd):
        zv = (vc.astype(F32) * zeta).astype(BF16)
        kv = lax.dot_general(kc, zv, (((0,), (0,)), ((), ())), preferred_element_type=F32)
        state_scr[...] = state_scr[...] * cd + kv

    def left_to_right(cidx):
        qc, kc, vc, start = chunk(cidx)
        scores = lax.dot_general(qc, kc, (((1,), (1,)), ((), ())), preferred_element_type=F32) * decay
        o = jnp.dot(scores.astype(BF16), vc, preferred_element_type=F32)
        o = o + xi_f * jnp.dot(qc, state_f_scr[...].astype(BF16), preferred_element_type=F32)
        update_state(state_f_scr, kc, vc, zeta_f, cd_f)
        return o, start

    def right_to_left(cidx):
        qc, kc, vc, start = chunk(cidx)
        o = xi_b * jnp.dot(qc, state_b_scr[...].astype(BF16), preferred_element_type=F32)
        update_state(state_b_scr, kc, vc, zeta_b, cd_b)
        return o, start

    def finish(y, start):
        y = y * lax.rsqrt(jnp.mean(y * y, axis=-1, keepdims=True) + EPS) * gain_ref[...]
        o_ref[pl.ds(start, c_len), :] = y.astype(BF16)

    state_f_scr[...] = jnp.zeros_like(state_f_scr)
    state_b_scr[...] = jnp.zeros_like(state_b_scr)
    half = n_chunks // 2

    def first_visits(t, carry):
        o_f, start_f = left_to_right(t)
        o_b, start_b = right_to_left(n_chunks - 1 - t)
        acc_scr[pl.ds(start_f, c_len), :] = o_f
        acc_scr[pl.ds(start_b, c_len), :] = o_b
        return carry

    def second_visits(t, carry):
        o_f, start_f = left_to_right(t)
        o_b, start_b = right_to_left(n_chunks - 1 - t)
        finish(acc_scr[pl.ds(start_f, c_len), :] + o_f, start_f)
        finish(acc_scr[pl.ds(start_b, c_len), :] + o_b, start_b)
        return carry

    lax.fori_loop(0, half, first_visits, 0, unroll=4)
    lax.fori_loop(half, n_chunks, second_visits, 0, unroll=4)


def _retention_mixer(x, mod, w_in, decay_logit, gn_gain, w_o, batch, seq):
    n, d = x.shape
    qk_cols = RET_HEADS * RET_QK_DIM
    v_cols = RET_HEADS * RET_V_DIM
    col_scale = jnp.concatenate([jnp.ones((qk_cols,), F32), jnp.full((qk_cols,), RET_QK_DIM ** -0.5, F32),
                                 jnp.ones((2 * v_cols,), F32)])
    w = (w_in * col_scale[None, :]).astype(BF16)
    pos = jnp.arange(seq, dtype=F32)
    inv = 1.0 / (RET_THETA ** jnp.linspace(0.0, 1.0, RET_QK_DIM // 2, dtype=F32))
    ang = pos[:, None] * inv
    tm, tn = TILE_PROJ, qk_cols
    tiles_per_seq = seq // tm
    half = RET_QK_DIM // 2
    extras = [jnp.cos(ang), jnp.sin(ang)]
    extra_specs = [pl.BlockSpec((tm, half), lambda i, j: (i % tiles_per_seq, 0)),
                   pl.BlockSpec((tm, half), lambda i, j: (i % tiles_per_seq, 0))]
    proj = _project(x, mod, 0, w, extras, extra_specs, _ret_proj_epilogue,
                    jax.ShapeDtypeStruct((n, w.shape[1]), BF16), pl.BlockSpec((tm, w.shape[1]), lambda i, j: (i, 0)),
                    tm, tn, seq, "ret_proj")

    n_chunks = seq // RET_CHUNK
    k_off = qk_cols // RET_QK_DIM
    v_off = 2 * qk_cols // RET_V_DIM
    y = pl.pallas_call(
        functools.partial(_ret_kernel, n_chunks=n_chunks),
        grid=(batch, RET_HEADS),
        in_specs=[pl.BlockSpec(memory_space=pltpu.SMEM),
                  pl.BlockSpec((seq, RET_QK_DIM), lambda b, h: (b, h)),
                  pl.BlockSpec((seq, RET_QK_DIM), lambda b, h: (b, k_off + h)),
                  pl.BlockSpec((seq, RET_V_DIM), lambda b, h: (b, v_off + h)),
                  pl.BlockSpec((1, RET_V_DIM), lambda b, h: (0, 0))],
        out_specs=pl.BlockSpec((seq, RET_V_DIM), lambda b, h: (b, h)),
        out_shape=jax.ShapeDtypeStruct((n, v_cols), BF16),
        scratch_shapes=[pltpu.VMEM((seq, RET_V_DIM), F32), pltpu.VMEM((RET_QK_DIM, RET_V_DIM), F32),
                        pltpu.VMEM((RET_QK_DIM, RET_V_DIM), F32)],
        compiler_params=_cparams("parallel", "arbitrary"),
        name="retention",
    )(decay_logit.astype(F32), proj, proj, proj, gn_gain.reshape(1, RET_V_DIM).astype(F32))
    gate_col = (2 * qk_cols + v_cols) // v_cols
    return _out_project(y, 0, v_cols, w_o.astype(BF16), x, mod, 0, seq, TILE_PROJ, "ret_out",
                        gate_src=proj, gate_col=gate_col)


def _conv_pw1_epilogue(acc, j, extras, outs):
    b_ref = extras[0]
    u = acc + b_ref[...]
    dm = u.shape[1] // 2
    outs[0][...] = (u[:, :dm] * jax.nn.sigmoid(u[:, dm:])).astype(BF16)


def _conv_kernel(prev_ref, cur_ref, next_ref, wdw_ref, bdw_ref, lng_ref, lnb_ref, w2_ref, b2_ref, x_ref, mod_ref,
                 o_ref, ext_scr, shift_scr, c_scr, *, tiles_per_seq, row_block):
    tm = cur_ref.shape[0]
    t = pl.program_id(0) % tiles_per_seq
    keep_prev = jnp.where(t == 0, 0.0, 1.0)
    keep_next = jnp.where(t == tiles_per_seq - 1, 0.0, 1.0)
    ext_scr[0:CONV_HALO, :] = prev_ref[...].astype(F32) * keep_prev
    ext_scr[CONV_HALO:CONV_HALO + tm, :] = cur_ref[...].astype(F32)
    ext_scr[CONV_HALO + tm:, :] = next_ref[...].astype(F32) * keep_next
    base = CONV_HALO - CONV_PAD
    span = tm + 2 * CONV_HALO - SUBLANES
    for r in range(1, SUBLANES):
        shift_scr[r - 1, 0:span, :] = ext_scr[r:r + span, :]
    for rb in range(tm // row_block):
        r0 = rb * row_block
        acc = jnp.zeros((row_block, cur_ref.shape[1]), F32) + bdw_ref[...]
        for j in range(CONV_WIDTH):
            phase = (base + j) % SUBLANES
            start = r0 + base + j - phase
            if phase == 0:
                tap = ext_scr[start:start + row_block, :]
            else:
                tap = shift_scr[phase - 1, start:start + row_block, :]
            acc = acc + jnp.tile(wdw_ref[j], (row_block // SUBLANES, 1)) * tap
        c_scr[r0:r0 + row_block, :] = acc
    u = c_scr[...]
    mu = jnp.mean(u, axis=-1, keepdims=True)
    var = jnp.mean(jnp.square(u - mu), axis=-1, keepdims=True)
    u = (u - mu) * lax.rsqrt(var + EPS) * lng_ref[...] + lnb_ref[...]
    u = _silu(u).astype(BF16)
    z = jnp.dot(u, w2_ref[...], preferred_element_type=F32) + b2_ref[...]
    o_ref[...] = x_ref[...] + mod_ref[0][2:3] * z


def _conv_mixer(x, mod, w_pw1, b_pw1, w_dw, b_dw, ln_gain, ln_bias, w_pw2, b_pw2, seq):
    n, d = x.shape
    tm = TILE_PROJ
    u = _project(x, mod, 0, w_pw1.astype(BF16), [b_pw1.reshape(1, 2 * d).astype(F32)],
                 [pl.BlockSpec((1, 2 * d), lambda i, j: (0, 0))], _conv_pw1_epilogue,
                 jax.ShapeDtypeStruct((n, d), BF16), pl.BlockSpec((tm, d), lambda i, j: (i, 0)),
                 tm, 2 * d, seq, "conv_pw1")
    tm = TILE_CONV
    tiles_per_seq = seq // tm
    halo_per_tile = tm // CONV_HALO
    n_halo = n // CONV_HALO
    wdw = jnp.broadcast_to(w_dw.reshape(CONV_WIDTH, 1, d).astype(F32), (CONV_WIDTH, SUBLANES, d))
    vec = lambda a: a.reshape(1, d).astype(F32)
    row_spec = pl.BlockSpec((1, d), lambda i: (0, 0))
    return pl.pallas_call(
        functools.partial(_conv_kernel, tiles_per_seq=tiles_per_seq, row_block=CONV_ROW_BLOCK),
        grid=(n // tm,),
        in_specs=[pl.BlockSpec((CONV_HALO, d), lambda i: (jnp.maximum(i * halo_per_tile - 1, 0), 0)),
                  pl.BlockSpec((tm, d), lambda i: (i, 0)),
                  pl.BlockSpec((CONV_HALO, d), lambda i: (jnp.minimum((i + 1) * halo_per_tile, n_halo - 1), 0)),
                  pl.BlockSpec((CONV_WIDTH, SUBLANES, d), lambda i: (0, 0, 0)),
                  row_spec, row_spec, row_spec,
                  pl.BlockSpec((d, d), lambda i: (0, 0)),
                  row_spec,
                  pl.BlockSpec((tm, d), lambda i: (i, 0)),
                  pl.BlockSpec((1, 6, d), lambda i: (i // tiles_per_seq, 0, 0))],
        out_specs=pl.BlockSpec((tm, d), lambda i: (i, 0)),
        out_shape=jax.ShapeDtypeStruct((n, d), F32),
        scratch_shapes=[pltpu.VMEM((tm + 2 * CONV_HALO, d), F32),
                        pltpu.VMEM((SUBLANES - 1, tm + 2 * CONV_HALO, d), F32), pltpu.VMEM((tm, d), F32)],
        compiler_params=_cparams("parallel"),
        name="conv_module",
    )(u, u, u, wdw, vec(b_dw), vec(ln_gain), vec(ln_bias), w_pw2.astype(BF16), vec(b_pw2), x, mod)


def _t5_bucket(rel):
    half = T5_BUCKETS // 2
    exact = half // 2
    n = jnp.abs(rel)
    log_ratio = jnp.log(jnp.maximum(n, 1).astype(F32) / exact) / math.log(T5_MAX_DIST / exact)
    large = jnp.minimum(exact + (log_ratio * (half - exact)).astype(jnp.int32), half - 1)
    return jnp.where(rel > 0, half, 0) + jnp.where(n < exact, n, large)


def _bias_tile_kernel(table_ref, idx_ref, o_ref):
    h = pl.program_id(0)
    idx = idx_ref[0]
    acc = jnp.zeros(idx.shape, F32)
    for b in range(T5_BUCKETS):
        acc = jnp.where(idx == b, table_ref[b, h], acc)
    o_ref[0, 0] = acc * LOG2_E


def _diff_kernel(q_ref, k_ref, v_ref, bias_ref, lam_ref, gain_ref, o_ref, s_scr, m_scr, acc_scr, *, n_chunks, out_scale,
                 lambda_init):
    tq = q_ref.shape[0]
    qi = pl.program_id(2)
    q = q_ref[...]
    lane = lax.broadcasted_iota(jnp.int32, (tq, LANES), 1)
    low = lane < DIFF_HEAD_DIM
    zero = jnp.zeros((tq, LANES), BF16)
    lhs = jnp.concatenate([jnp.where(low, q, zero), jnp.where(low, zero, q)], axis=0)

    q_sub = tq // KEY_CHUNK

    def add_bias(s, j):
        tiles = [bias_ref[0, jnp.clip(j - (q_sub * qi + a), -2, 2) + 2] for a in range(q_sub)]
        return s + jnp.concatenate(tiles + tiles, axis=0)

    o = _softmax_pv(lhs, k_ref, v_ref, s_scr, m_scr, acc_scr, n_chunks, add_bias)
    lam_v = lam_ref[...]
    lam = (jnp.exp(jnp.sum(lam_v[0:1] * lam_v[1:2], axis=-1, keepdims=True))
           - jnp.exp(jnp.sum(lam_v[2:3] * lam_v[3:4], axis=-1, keepdims=True)) + lambda_init)
    y = o[:tq] - lam * o[tq:]
    y = y * lax.rsqrt(jnp.mean(y * y, axis=-1, keepdims=True) + EPS) * gain_ref[...] * out_scale
    o_ref[...] = y.astype(BF16)


def _diff_mixer(x, mod, w_qkv, lam_q1, lam_k1, lam_q2, lam_k2, subln_gain, w_o, rel_bias, lambda_init,
                batch, seq):
    n, d = x.shape
    col_scale = jnp.concatenate([jnp.full((d,), DIFF_HEAD_DIM ** -0.5 * LOG2_E, F32), jnp.ones((2 * d,), F32)])
    w = (w_qkv * col_scale[None, :]).astype(BF16)
    tm = TILE_PROJ_NARROW
    qkv = _project(x, mod, 0, w, [], [], _plain_epilogue,
                   jax.ShapeDtypeStruct((n, 3 * d), BF16), pl.BlockSpec((tm, 3 * d), lambda i, j: (i, 0)),
                   tm, d, seq, "diff_qkv")

    tb = KEY_CHUNK
    ar = jnp.arange(KEY_CHUNK, dtype=jnp.int32)
    offs = (jnp.arange(5, dtype=jnp.int32) - 2) * KEY_CHUNK
    rel = (ar[None, None, :] + offs[:, None, None]) - ar[None, :, None]
    bucket = _t5_bucket(rel).astype(jnp.int32)
    bias = pl.pallas_call(
        _bias_tile_kernel,
        grid=(DIFF_HEADS, 5),
        in_specs=[pl.BlockSpec(memory_space=pltpu.SMEM),
                  pl.BlockSpec((1, tb, KEY_CHUNK), lambda h, t: (t, 0, 0))],
        out_specs=pl.BlockSpec((1, 1, tb, KEY_CHUNK), lambda h, t: (h, t, 0, 0)),
        out_shape=jax.ShapeDtypeStruct((DIFF_HEADS, 5, tb, KEY_CHUNK), F32),
        compiler_params=_cparams("arbitrary", "arbitrary"),
        name="rel_bias_tiles",
    )(rel_bias.astype(F32), bucket)

    lam = jnp.stack([lam_q1, lam_k1, lam_q2, lam_k2]).astype(F32)
    n_chunks = seq // KEY_CHUNK
    tq = TILE_DIFF_Q
    q_tiles = seq // tq
    o = pl.pallas_call(
        functools.partial(_diff_kernel, n_chunks=n_chunks, out_scale=1.0 - lambda_init, lambda_init=lambda_init),
        grid=(batch, DIFF_HEADS, q_tiles),
        in_specs=[pl.BlockSpec((tq, LANES), lambda b, h, i: (b * q_tiles + i, h)),
                  pl.BlockSpec((seq, LANES), lambda b, h, i: (b, DIFF_HEADS + h)),
                  pl.BlockSpec((seq, LANES), lambda b, h, i: (b, 2 * DIFF_HEADS + h)),
                  pl.BlockSpec((1, 5, tb, KEY_CHUNK), lambda b, h, i: (h, 0, 0, 0)),
                  pl.BlockSpec((4, DIFF_HEAD_DIM), lambda b, h, i: (0, 0)),
                  pl.BlockSpec((1, LANES), lambda b, h, i: (0, 0))],
        out_specs=pl.BlockSpec((tq, LANES), lambda b, h, i: (b * q_tiles + i, h)),
        out_shape=jax.ShapeDtypeStruct((n, d), BF16),
        scratch_shapes=[pltpu.VMEM((n_chunks, 2 * tq, KEY_CHUNK), F32),
                        pltpu.VMEM((2 * tq, LANES), F32), pltpu.VMEM((2 * tq, 2 * LANES), F32)],
        compiler_params=_cparams("parallel", "parallel", "arbitrary"),
        name="diff_attention",
    )(qkv, qkv, qkv, bias, lam, subln_gain.reshape(1, LANES).astype(F32))
    return _out_project(o, 0, d, w_o.astype(BF16), x, mod, 0, seq, TILE_PROJ_NARROW, "diff_out")


def _final_norm_kernel(x_ref, g_ref, o_ref):
    x = x_ref[...]
    o_ref[...] = x * lax.rsqrt(jnp.mean(x * x, axis=-1, keepdims=True) + EPS) * g_ref[...]


def _final_norm(x, gain, tm):
    n, d = x.shape
    return pl.pallas_call(
        _final_norm_kernel,
        grid=(n // tm,),
        in_specs=[pl.BlockSpec((tm, d), lambda i: (i, 0)), pl.BlockSpec((1, d), lambda i: (0, 0))],
        out_specs=pl.BlockSpec((tm, d), lambda i: (i, 0)),
        out_shape=jax.ShapeDtypeStruct((n, d), F32),
        compiler_params=_cparams("parallel"),
        name="final_norm",
    )(x, gain.reshape(1, d).astype(F32))


def kernel(x, c, mod_w, mod_b, att_w_qkv, att_q_gain, att_k_gain, att_w_o, ret_w_in, ret_decay_logit, ret_gn_gain, ret_w_o, cnv_w_pw1, cnv_b_pw1, cnv_w_dw, cnv_b_dw, cnv_ln_gain, cnv_ln_bias, cnv_w_pw2, cnv_b_pw2, dif_w_qkv, dif_lam_q1, dif_lam_k1, dif_lam_q2, dif_lam_k2, dif_subln_gain, dif_w_o, rel_bias, ffn_w_gate, ffn_w_up, ffn_w_down, moe_w_router, moe_b_router, moe_w_gate, moe_w_up, moe_w_down, final_gain):
    batch, seq, d = x.shape
    depth = mod_w.shape[0]
    xf = x.reshape(batch * seq, d)
    mods = _modulation(c, mod_w, mod_b)
    moe_gate, moe_up, moe_down = moe_w_gate.astype(BF16), moe_w_up.astype(BF16), moe_w_down.astype(BF16)
    for i in range(depth):
        mod = mods[i]
        kind, j = i % 4, i // 4
        if kind == 0:
            xf = _attention_mixer(xf, mod, att_w_qkv[j], att_q_gain[j], att_k_gain[j], att_w_o[j], batch, seq)
        elif kind == 1:
            xf = _retention_mixer(xf, mod, ret_w_in[j], ret_decay_logit[j], ret_gn_gain[j], ret_w_o[j], batch, seq)
        elif kind == 2:
            xf = _conv_mixer(xf, mod, cnv_w_pw1[j], cnv_b_pw1[j], cnv_w_dw[j], cnv_b_dw[j], cnv_ln_gain[j],
                             cnv_ln_bias[j], cnv_w_pw2[j], cnv_b_pw2[j], seq)
        else:
            lambda_init = 0.8 - 0.6 * math.exp(-0.3 * i)
            xf = _diff_mixer(xf, mod, dif_w_qkv[j], dif_lam_q1[j], dif_lam_k1[j], dif_lam_q2[j], dif_lam_k2[j],
                             dif_subln_gain[j], dif_w_o[j], rel_bias, lambda_init, batch, seq)
        if i % 2 == 0:
            m = i // 2
            xf = _ffn(xf, mod, ffn_w_gate[m].astype(BF16), ffn_w_up[m].astype(BF16), ffn_w_down[m].astype(BF16),
                      seq, TILE_PROJ)
        else:
            m = i // 2
            xf = _moe(xf, mod, moe_w_router[m], moe_b_router[m], moe_gate, moe_up, moe_down, m, seq,
                      out_gain=final_gain if i == depth - 1 else None)
    if depth % 2 == 1:
        xf = _final_norm(xf, final_gain, TILE_PROJ)
    return xf.reshape(batch, seq, d)
```

```python
import functools
import math

import jax
import jax.numpy as jnp
from jax import lax
from jax.experimental import pallas as pl
from jax.experimental.pallas import tpu as pltpu

F32 = jnp.float32
BF16 = jnp.bfloat16

EPS = 1e-6
GRID_W = 64

ATT_HEADS = 16
ATT_KV_HEADS = 4
ATT_HEAD_DIM = 64
ROPE_THETA = 10000.0

RET_HEADS = 4
RET_QK_DIM = 256
RET_V_DIM = 512
RET_THETA = 10000.0
RET_CHUNK = 256

CONV_WIDTH = 31
CONV_PAD = CONV_WIDTH // 2
CONV_HALO = 16

DIFF_HEADS = 8
DIFF_HEAD_DIM = 64
T5_BUCKETS = 32
T5_MAX_DIST = 128

N_EXPERTS = 8
LANES = 128
SUBLANES = 8
KEY_CHUNK = 256
LOG2_E = 1.4426950408889634
VMEM_LIMIT_BYTES = 56 * 1024 * 1024

TILE_PROJ = 512
TILE_PROJ_NARROW = 1024
TILE_ROUTE = 512
TILE_SLOT = 512
TILE_EXPERT_HIDDEN = 1792
TILE_CONV = 256
CONV_ROW_BLOCK = 32
TILE_ATT_Q = 256
TILE_DIFF_Q = 512


def _cparams(*sem):
    return pltpu.CompilerParams(dimension_semantics=sem, vmem_limit_bytes=VMEM_LIMIT_BYTES)


def _silu(v):
    return v * jax.nn.sigmoid(v)


def _norm_mod(x, mod, which):
    shift = mod[3 * which:3 * which + 1]
    scale = mod[3 * which + 1:3 * which + 2]
    y = x * lax.rsqrt(jnp.mean(x * x, axis=-1, keepdims=True) + EPS)
    return y * (1.0 + scale) + shift


def _split_dot(a, b):
    hi = a.astype(BF16)
    lo = (a - hi.astype(F32)).astype(BF16)
    return (jnp.dot(hi, b, preferred_element_type=F32) + jnp.dot(lo, b, preferred_element_type=F32))


def _mod_kernel(c_ref, w_ref, b_ref, o_ref):
    cond = _silu(c_ref[...])
    o_ref[0] = jnp.dot(cond, w_ref[0], preferred_element_type=F32, precision=lax.Precision.HIGHEST) + b_ref[0]


def _modulation(c, mod_w, mod_b):
    depth, d, d6 = mod_w.shape
    b = c.shape[0]
    tn = d
    out = pl.pallas_call(
        _mod_kernel,
        grid=(depth, d6 // tn),
        in_specs=[pl.BlockSpec((b, d), lambda i, j: (0, 0)),
                  pl.BlockSpec((1, d, tn), lambda i, j: (i, 0, j)),
                  pl.BlockSpec((1, 1, tn), lambda i, j: (i, 0, j))],
        out_specs=pl.BlockSpec((1, b, tn), lambda i, j: (i, 0, j)),
        out_shape=jax.ShapeDtypeStruct((depth, b, d6), F32),
        compiler_params=_cparams("arbitrary", "arbitrary"),
        name="modulation",
    )(c, mod_w, mod_b.reshape(depth, 1, d6))
    return out.reshape(depth, b, 6, d)


def _proj_kernel(x_ref, mod_ref, w_ref, *rest, which, n_extra, epilogue, tn):
    extras = rest[:n_extra]
    outs = rest[n_extra:]
    h = _norm_mod(x_ref[...], mod_ref[0], which).astype(BF16)
    for j in range(w_ref.shape[1] // tn):
        acc = jnp.dot(h, w_ref[:, j * tn:(j + 1) * tn], preferred_element_type=F32)
        epilogue(acc, j, extras, outs)


def _project(x, mod, which, w, extras, extra_specs, epilogue, out_shapes, out_specs, tm, tn, seq, name):
    n, d = x.shape
    c = w.shape[1]
    tiles_per_seq = seq // tm
    kern = functools.partial(_proj_kernel, which=which, n_extra=len(extras), epilogue=epilogue, tn=tn)
    return pl.pallas_call(
        kern,
        grid=(n // tm, 1),
        in_specs=[pl.BlockSpec((tm, d), lambda i, j: (i, 0)),
                  pl.BlockSpec((1, 6, d), lambda i, j: (i // tiles_per_seq, 0, 0)),
                  pl.BlockSpec((d, c), lambda i, j: (0, 0), pipeline_mode=pl.Buffered(1))] + extra_specs,
        out_specs=out_specs,
        out_shape=out_shapes,
        compiler_params=_cparams("parallel", "arbitrary"),
        name=name,
    )(x, mod, w, *extras)


def _plain_epilogue(acc, j, extras, outs):
    tn = acc.shape[1]
    outs[0][:, j * tn:(j + 1) * tn] = acc.astype(outs[0].dtype)


def _oproj_kernel(*refs, which, gated, has_bias):
    refs = list(refs)
    y_ref = refs.pop(0)
    g_ref = refs.pop(0) if gated else None
    w_ref = refs.pop(0)
    b_ref = refs.pop(0) if has_bias else None
    x_ref, mod_ref, o_ref = refs
    y = y_ref[...]
    if gated:
        y = (y.astype(F32) * _silu(g_ref[...].astype(F32))).astype(BF16)
    z = jnp.dot(y, w_ref[...], preferred_element_type=F32)
    if has_bias:
        z = z + b_ref[...]
    gate = mod_ref[0][3 * which + 2:3 * which + 3]
    o_ref[...] = x_ref[...] + gate * z


def _out_project(y, y_col, k, w, x, mod, which, seq, tm, name, gate_src=None, gate_col=0, bias=None):
    n, d = x.shape
    tiles_per_seq = seq // tm
    args = [y]
    specs = [pl.BlockSpec((tm, k), lambda i: (i, y_col))]
    if gate_src is not None:
        args.append(gate_src)
        specs.append(pl.BlockSpec((tm, k), lambda i: (i, gate_col)))
    args.append(w)
    specs.append(pl.BlockSpec((k, d), lambda i: (0, 0)))
    if bias is not None:
        args.append(bias.reshape(1, d))
        specs.append(pl.BlockSpec((1, d), lambda i: (0, 0)))
    args += [x, mod]
    specs += [pl.BlockSpec((tm, d), lambda i: (i, 0)),
              pl.BlockSpec((1, 6, d), lambda i: (i // tiles_per_seq, 0, 0))]
    kern = functools.partial(_oproj_kernel, which=which, gated=gate_src is not None, has_bias=bias is not None)
    return pl.pallas_call(
        kern,
        grid=(n // tm,),
        in_specs=specs,
        out_specs=pl.BlockSpec((tm, d), lambda i: (i, 0)),
        out_shape=jax.ShapeDtypeStruct((n, d), F32),
        compiler_params=_cparams("parallel"),
        name=name,
    )(*args)


def _ffn_kernel(x_ref, mod_ref, wg_ref, wu_ref, wd_ref, o_ref):
    x = x_ref[...]
    mod = mod_ref[0]
    h = _norm_mod(x, mod, 1).astype(BF16)
    g = jnp.dot(h, wg_ref[...], preferred_element_type=F32)
    u = jnp.dot(h, wu_ref[...], preferred_element_type=F32)
    a = (_silu(g) * u).astype(BF16)
    o_ref[...] = x + mod[5:6] * jnp.dot(a, wd_ref[...], preferred_element_type=F32)


def _ffn(x, mod, wg, wu, wd, seq, tm):
    n, d = x.shape
    ff = wg.shape[1]
    tiles_per_seq = seq // tm
    resident = pl.Buffered(1)
    return pl.pallas_call(
        _ffn_kernel,
        grid=(n // tm,),
        in_specs=[pl.BlockSpec((tm, d), lambda i: (i, 0)),
                  pl.BlockSpec((1, 6, d), lambda i: (i // tiles_per_seq, 0, 0)),
                  pl.BlockSpec((d, ff), lambda i: (0, 0), pipeline_mode=resident),
                  pl.BlockSpec((d, ff), lambda i: (0, 0), pipeline_mode=resident),
                  pl.BlockSpec((ff, d), lambda i: (0, 0), pipeline_mode=resident)],
        out_specs=pl.BlockSpec((tm, d), lambda i: (i, 0)),
        out_shape=jax.ShapeDtypeStruct((n, d), F32),
        compiler_params=_cparams("parallel"),
        name="dense_ffn",
    )(x, mod, wg, wu, wd)


def _router_kernel(x_ref, mod_ref, wr_ref, br_ref, tri_ref, route_ref, route_t_ref, count_ref, carry_scr):
    @pl.when(pl.program_id(0) == 0)
    def _():
        carry_scr[...] = jnp.zeros_like(carry_scr)

    h = _norm_mod(x_ref[...], mod_ref[0], 1)
    h_hi = h.astype(BF16)
    h_lo = (h - h_hi.astype(F32)).astype(BF16)
    logits = (jnp.dot(h_hi, wr_ref[0], preferred_element_type=F32)
              + (jnp.dot(h_hi, wr_ref[1], preferred_element_type=F32)
                 + jnp.dot(h_lo, wr_ref[0], preferred_element_type=F32))) + br_ref[...]
    lane = lax.broadcasted_iota(jnp.int32, logits.shape, 1)
    neg = jnp.float32(-jnp.inf)
    logits = jnp.where(lane < N_EXPERTS, logits, neg)
    m1 = jnp.max(logits, axis=-1, keepdims=True)
    i1 = jnp.min(jnp.where(logits == m1, lane, LANES), axis=-1, keepdims=True)
    rest = jnp.where(lane == i1, neg, logits)
    m2 = jnp.max(rest, axis=-1, keepdims=True)
    i2 = jnp.min(jnp.where(rest == m2, lane, LANES), axis=-1, keepdims=True)
    e2 = jnp.exp(m2 - m1)
    w1 = 1.0 / (1.0 + e2)
    w2 = e2 / (1.0 + e2)
    sel = jnp.where((lane == i1) | (lane == i2), 1.0, 0.0)
    incl = jnp.dot(tri_ref[...], sel.astype(BF16), preferred_element_type=F32)
    excl = incl - sel + carry_scr[...]
    r1 = jnp.sum(jnp.where(lane == i1, excl, 0.0), axis=-1, keepdims=True)
    r2 = jnp.sum(jnp.where(lane == i2, excl, 0.0), axis=-1, keepdims=True)
    carry_scr[...] += incl[incl.shape[0] - 1:, :]
    count_ref[...] = carry_scr[...]
    fields = (i1.astype(F32), i2.astype(F32), w1, w2, r1, r2)
    route = jnp.zeros(logits.shape, F32)
    for k, field in enumerate(fields):
        route = jnp.where(lane == k, field, route)
    route_ref[...] = route
    route_t_ref[...] = route.T[:SUBLANES]


def _router(x, mod, w_router, b_router, seq, tm):
    n, d = x.shape
    tiles_per_seq = seq // tm
    wr = jnp.pad(w_router, ((0, 0), (0, LANES - N_EXPERTS))).astype(F32)
    wr_hi = wr.astype(BF16)
    wr = jnp.stack([wr_hi, (wr - wr_hi.astype(F32)).astype(BF16)])
    br = jnp.pad(b_router, (0, LANES - N_EXPERTS)).reshape(1, LANES)
    tri = (jnp.arange(tm)[:, None] >= jnp.arange(tm)[None, :]).astype(BF16)
    return pl.pallas_call(
        _router_kernel,
        grid=(n // tm,),
        in_specs=[pl.BlockSpec((tm, d), lambda i: (i, 0)),
                  pl.BlockSpec((1, 6, d), lambda i: (i // tiles_per_seq, 0, 0)),
                  pl.BlockSpec((2, d, LANES), lambda i: (0, 0, 0)),
                  pl.BlockSpec((1, LANES), lambda i: (0, 0)),
                  pl.BlockSpec((tm, tm), lambda i: (0, 0))],
        out_specs=[pl.BlockSpec((tm, LANES), lambda i: (i, 0)), pl.BlockSpec((SUBLANES, tm), lambda i: (0, i)),
                   pl.BlockSpec((1, LANES), lambda i: (0, 0))],
        out_shape=[jax.ShapeDtypeStruct((n, LANES), F32), jax.ShapeDtypeStruct((SUBLANES, n), F32),
                   jax.ShapeDtypeStruct((1, LANES), F32)],
        scratch_shapes=[pltpu.VMEM((1, LANES), F32)],
        compiler_params=_cparams("arbitrary"),
        name="moe_router",
    )(x, mod, wr, br, tri)


def _dispatch_kernel(zero_rows_ref, d1_ref, d2_ref, x_ref, mod_ref, xs_ref, h_scr, zero_scr, sems):
    tm = x_ref.shape[0]

    @pl.when(pl.program_id(0) == 0)
    def _():
        zero_scr[...] = jnp.zeros_like(zero_scr)
        te = zero_scr.shape[0]
        for k in range(zero_rows_ref.shape[0]):
            start = pl.multiple_of(zero_rows_ref[k], te)
            copy = pltpu.make_async_copy(zero_scr, xs_ref.at[pl.ds(start, te), :], sems.at[2])
            copy.start()
            copy.wait()

    h_scr[...] = _norm_mod(x_ref[...], mod_ref[0], 1)

    def row_copy(r, dest, which):
        return pltpu.make_async_copy(h_scr.at[pl.ds(r, 1), :], xs_ref.at[pl.ds(dest, 1), :], sems.at[which])

    def issue(r, carry):
        row_copy(r, d1_ref[0, 0, r], 0).start()
        row_copy(r, d2_ref[0, 0, r], 1).start()
        return carry

    lax.fori_loop(0, tm, issue, 0, unroll=4)
    for which in range(2):
        pltpu.make_async_copy(h_scr, xs_ref.at[pl.ds(0, tm), :], sems.at[which]).wait()


def _expert_kernel(tile_e_ref, n_act_ref, x_ref, wg_ref, wu_ref, wd_ref, o_ref, *, tf):
    del tile_e_ref
    s = pl.program_id(0)

    @pl.when(s < n_act_ref[0])
    def _():
        h = x_ref[...].astype(BF16)
        out = None
        for f in range(wg_ref.shape[3] // tf):
            cols = slice(f * tf, (f + 1) * tf)
            g = jnp.dot(h, wg_ref[0, 0, :, cols], preferred_element_type=F32)
            u = jnp.dot(h, wu_ref[0, 0, :, cols], preferred_element_type=F32)
            a = (_silu(g) * u).astype(BF16)
            part = jnp.dot(a, wd_ref[0, 0, cols, :], preferred_element_type=F32)
            out = part if out is None else out + part
        o_ref[...] = out

    @pl.when(s >= n_act_ref[0])
    def _():
        o_ref[...] = jnp.zeros_like(o_ref)


def _combine_kernel(d1_ref, d2_ref, route_ref, x_ref, mod_ref, *rest, out_norm):
    if out_norm:
        gain_ref, ys_ref, o_ref, buf, sems = rest
    else:
        ys_ref, o_ref, buf, sems = rest
    tm = x_ref.shape[0]

    def row_copy(r, src, which):
        return pltpu.make_async_copy(ys_ref.at[pl.ds(src, 1), :], buf.at[which, pl.ds(r, 1), :], sems.at[which])

    def issue(r, carry):
        row_copy(r, d1_ref[0, 0, r], 0).start()
        row_copy(r, d2_ref[0, 0, r], 1).start()
        return carry

    lax.fori_loop(0, tm, issue, 0, unroll=4)
    for which in range(2):
        pltpu.make_async_copy(ys_ref.at[pl.ds(0, tm), :], buf.at[which], sems.at[which]).wait()
    route = route_ref[...]
    y = route[:, 2:3] * buf[0] + route[:, 3:4] * buf[1]
    out = x_ref[...] + mod_ref[0][5:6] * y
    if out_norm:
        out = out * lax.rsqrt(jnp.mean(out * out, axis=-1, keepdims=True) + EPS) * gain_ref[...]
    o_ref[...] = out


def _moe(x, mod, w_router, b_router, w_gate, w_up, w_down, layer, seq, out_gain=None):
    n, d = x.shape
    _, n_e, _, ff = w_gate.shape
    tm, te, tf = TILE_ROUTE, TILE_SLOT, TILE_EXPERT_HIDDEN
    tiles_per_seq = seq // tm
    n_tok_tiles = n // tm
    n_slots = 2 * n + n_e * te
    n_tiles = n_slots // te

    route, route_t, counts = _router(x, mod, w_router, b_router, seq, tm)

    counts = counts[0, :n_e].astype(jnp.int32)
    padded = ((counts + te - 1) // te) * te
    ends = jnp.cumsum(padded)
    offsets = ends - padded
    n_act = (ends[-1] // te).astype(jnp.int32)
    tile_start = jnp.arange(n_tiles, dtype=jnp.int32) * te
    tile_e = jnp.sum((tile_start[:, None] >= ends[None, :]).astype(jnp.int32), axis=1)
    tile_e = jnp.minimum(tile_e, tile_e[jnp.maximum(n_act - 1, 0)]).astype(jnp.int32)
    e1 = route_t[0].astype(jnp.int32)
    e2 = route_t[1].astype(jnp.int32)
    dest1 = (offsets[e1] + route_t[4].astype(jnp.int32)).reshape(n_tok_tiles, 1, tm)
    dest2 = (offsets[e2] + route_t[5].astype(jnp.int32)).reshape(n_tok_tiles, 1, tm)
    n_act = n_act.reshape(1)

    slot_spec = pl.BlockSpec((1, 1, tm), lambda i: (i, 0, 0), memory_space=pltpu.SMEM)
    zero_rows = jnp.concatenate([jnp.maximum(ends - te, 0),
                                 (n_tiles - n_e + jnp.arange(n_e, dtype=jnp.int32)) * te]).astype(jnp.int32)
    xs = pl.pallas_call(
        _dispatch_kernel,
        grid=(n_tok_tiles,),
        in_specs=[pl.BlockSpec(memory_space=pltpu.SMEM), slot_spec, slot_spec,
                  pl.BlockSpec((tm, d), lambda i: (i, 0)),
                  pl.BlockSpec((1, 6, d), lambda i: (i // tiles_per_seq, 0, 0))],
        out_specs=pl.BlockSpec(memory_space=pl.ANY),
        out_shape=jax.ShapeDtypeStruct((n_slots, d), F32),
        scratch_shapes=[pltpu.VMEM((tm, d), F32), pltpu.VMEM((te, d), F32), pltpu.SemaphoreType.DMA((3,))],
        compiler_params=_cparams("arbitrary"),
        name="moe_dispatch",
    )(zero_rows, dest1, dest2, x, mod)

    def live(s, n_act_ref):
        return jnp.minimum(s, jnp.maximum(n_act_ref[0] - 1, 0))

    def expert_weights(shape):
        return pl.BlockSpec(shape, lambda s, te_ref, na_ref: (layer, te_ref[s], 0, 0), pipeline_mode=pl.Buffered(1))

    ys = pl.pallas_call(
        functools.partial(_expert_kernel, tf=tf),
        grid_spec=pltpu.PrefetchScalarGridSpec(
            num_scalar_prefetch=2,
            grid=(n_tiles,),
            in_specs=[pl.BlockSpec((te, d), lambda s, te_ref, na_ref: (live(s, na_ref), 0)),
                      expert_weights((1, 1, d, ff)), expert_weights((1, 1, d, ff)), expert_weights((1, 1, ff, d))],
            out_specs=pl.BlockSpec((te, d), lambda s, te_ref, na_ref: (s, 0))),
        out_shape=jax.ShapeDtypeStruct((n_slots, d), F32),
        compiler_params=_cparams("arbitrary"),
        name="moe_experts",
    )(tile_e, n_act, xs, w_gate, w_up, w_down)

    out_norm = out_gain is not None
    gain_args = [out_gain.reshape(1, d).astype(F32)] if out_norm else []
    gain_specs = [pl.BlockSpec((1, d), lambda i: (0, 0))] if out_norm else []
    return pl.pallas_call(
        functools.partial(_combine_kernel, out_norm=out_norm),
        grid=(n_tok_tiles,),
        in_specs=[slot_spec, slot_spec,
                  pl.BlockSpec((tm, LANES), lambda i: (i, 0)),
                  pl.BlockSpec((tm, d), lambda i: (i, 0)),
                  pl.BlockSpec((1, 6, d), lambda i: (i // tiles_per_seq, 0, 0))] + gain_specs
                 + [pl.BlockSpec(memory_space=pl.ANY)],
        out_specs=pl.BlockSpec((tm, d), lambda i: (i, 0)),
        out_shape=jax.ShapeDtypeStruct((n, d), F32),
        scratch_shapes=[pltpu.VMEM((2, tm, d), F32), pltpu.SemaphoreType.DMA((2,))],
        compiler_params=_cparams("arbitrary"),
        name="moe_combine",
    )(dest1, dest2, route, x, mod, *gain_args, ys)


def _softmax_pv(lhs, k_ref, v_ref, s_scr, m_scr, acc_scr, n_chunks, bias_fn=None):
    rows = lhs.shape[0]
    group = min(8, n_chunks)
    m_scr[...] = jnp.full(m_scr.shape, -jnp.inf, F32)

    for j in range(n_chunks):
        kc = k_ref[j * KEY_CHUNK:(j + 1) * KEY_CHUNK, :]
        s = lax.dot_general(lhs, kc, (((1,), (1,)), ((), ())), preferred_element_type=F32)
        if bias_fn is not None:
            s = bias_fn(s, j)
        s_scr[j] = s
        m_scr[...] = jnp.maximum(m_scr[...], jnp.maximum(s[:, :LANES], s[:, LANES:]))
    m_scr[...] = jnp.broadcast_to(jnp.max(m_scr[...], axis=-1, keepdims=True), m_scr.shape)
    acc_scr[...] = jnp.zeros_like(acc_scr)
    ones = jnp.ones((group * KEY_CHUNK, LANES), BF16)

    def pv_body(t, carry):
        mb = m_scr[...]
        parts = []
        for u in range(group):
            s = s_scr[group * t + u]
            parts.append(jnp.exp2(s[:, :LANES] - mb).astype(BF16))
            parts.append(jnp.exp2(s[:, LANES:] - mb).astype(BF16))
        p = jnp.concatenate(parts, axis=1)
        start = pl.multiple_of(t * (group * KEY_CHUNK), group * KEY_CHUNK)
        v_aug = jnp.concatenate([v_ref[pl.ds(start, group * KEY_CHUNK), :], ones], axis=1)
        acc_scr[...] += jnp.dot(p, v_aug, preferred_element_type=F32)
        return carry

    lax.fori_loop(0, n_chunks // group, pv_body, 0)
    acc = acc_scr[...]
    return acc[:, :LANES] / acc[:, LANES:]


def _att_qkv_epilogue(acc, j, extras, outs, n_qk):
    gain_ref, cos_ref, sin_ref, g1_ref, g2_ref = extras
    q_ref, k_ref, v_ref = outs
    qk = acc[:, :n_qk]
    ms = _split_dot(qk * qk, g1_ref[...])
    r = lax.rsqrt(ms + EPS)
    y = qk * _split_dot(r, g2_ref[...]) * gain_ref[...]
    cos = cos_ref[...]
    sin = sin_ref[...]
    lane = lax.broadcasted_iota(jnp.int32, cos.shape, 1)
    first_half = (lane & (ATT_HEAD_DIM // 2)) == 0
    nq = q_ref.shape[1]
    for c in range(n_qk // LANES):
        yc = y[:, c * LANES:(c + 1) * LANES]
        partner = jnp.where(first_half, pltpu.roll(yc, LANES - ATT_HEAD_DIM // 2, 1),
                            pltpu.roll(yc, ATT_HEAD_DIM // 2, 1))
        out = (yc * cos + partner * sin).astype(BF16)
        if c * LANES < nq:
            q_ref[:, c * LANES:(c + 1) * LANES] = out
        else:
            k_ref[:, c * LANES - nq:(c + 1) * LANES - nq] = out
    v_ref[...] = acc[:, n_qk:].astype(BF16)


def _att_kernel(q_ref, k_ref, v_ref, o_ref, s_scr, m_scr, acc_scr, *, n_chunks):
    tq = q_ref.shape[0]
    q = q_ref[...]
    lane = lax.broadcasted_iota(jnp.int32, (tq, LANES), 1)
    low = lane < ATT_HEAD_DIM
    zero = jnp.zeros((tq, LANES), BF16)
    parts = []
    for pair in range(2):
        qp = q[:, pair * LANES:(pair + 1) * LANES]
        parts.append(jnp.where(low, qp, zero))
        parts.append(jnp.where(low, zero, qp))
    lhs = jnp.concatenate(parts, axis=0)
    o = _softmax_pv(lhs, k_ref, v_ref, s_scr, m_scr, acc_scr, n_chunks)
    for pair in range(2):
        o_ref[:, pair * LANES:(pair + 1) * LANES] = jnp.where(
            low, o[(2 * pair) * tq:(2 * pair + 1) * tq], o[(2 * pair + 1) * tq:(2 * pair + 2) * tq]).astype(BF16)


def _attention_mixer(x, mod, w_qkv, q_gain, k_gain, w_o, batch, seq):
    n, d = x.shape
    hd = ATT_HEAD_DIM
    half = hd // 2
    n_freq = half // 2
    groups = ATT_HEADS // ATT_KV_HEADS
    perm = jnp.concatenate([jnp.arange(n_freq), half + jnp.arange(n_freq),
                            n_freq + jnp.arange(n_freq), half + n_freq + jnp.arange(n_freq)])
    q_cols = (jnp.arange(ATT_HEADS)[:, None] * hd + perm[None, :]).reshape(-1)
    k_base = ATT_HEADS * hd
    v_base = k_base + ATT_KV_HEADS * hd
    kv_rep = jnp.repeat(jnp.arange(ATT_KV_HEADS), 2)
    k_cols = (k_base + kv_rep[:, None] * hd + perm[None, :]).reshape(-1)
    v_cols = (v_base + kv_rep[:, None] * hd + jnp.arange(hd)[None, :]).reshape(-1)
    w = w_qkv[:, jnp.concatenate([q_cols, k_cols, v_cols])].astype(BF16)
    n_q = ATT_HEADS * hd
    n_kv = 2 * ATT_KV_HEADS * hd
    n_qk = n_q + n_kv
    gain_row = jnp.concatenate([jnp.tile(q_gain[perm] * (hd ** -0.5 * LOG2_E), ATT_HEADS),
                                jnp.tile(k_gain[perm], 2 * ATT_KV_HEADS)]).reshape(1, n_qk).astype(F32)
    pos = jnp.arange(seq, dtype=jnp.int32)
    row = (pos // GRID_W).astype(F32)
    col = (pos % GRID_W).astype(F32)
    inv = ROPE_THETA ** (-jnp.arange(n_freq, dtype=F32) / n_freq)
    ang = jnp.concatenate([row[:, None] * inv, col[:, None] * inv], axis=-1)
    cos_t = jnp.tile(jnp.cos(ang), (1, 2 * LANES // hd))
    sin_t = jnp.tile(jnp.concatenate([-jnp.sin(ang), jnp.sin(ang)], axis=-1), (1, LANES // hd))
    grp = jnp.arange(n_qk) // hd
    g1 = (grp[:, None] == jnp.arange(LANES)[None, :]).astype(F32) / hd
    g2 = (jnp.arange(LANES)[:, None] == grp[None, :]).astype(F32)
    tm = TILE_PROJ
    tiles_per_seq = seq // tm
    extras = [gain_row, cos_t, sin_t, g1.astype(BF16), g2.astype(BF16)]
    extra_specs = [pl.BlockSpec((1, n_qk), lambda i, j: (0, 0)),
                   pl.BlockSpec((tm, LANES), lambda i, j: (i % tiles_per_seq, 0)),
                   pl.BlockSpec((tm, LANES), lambda i, j: (i % tiles_per_seq, 0)),
                   pl.BlockSpec((n_qk, LANES), lambda i, j: (0, 0)),
                   pl.BlockSpec((LANES, n_qk), lambda i, j: (0, 0))]
    q, k, v = _project(
        x, mod, 0, w, extras, extra_specs, functools.partial(_att_qkv_epilogue, n_qk=n_qk),
        [jax.ShapeDtypeStruct((n, n_q), BF16), jax.ShapeDtypeStruct((n, n_kv), BF16),
         jax.ShapeDtypeStruct((n, n_kv), BF16)],
        [pl.BlockSpec((tm, n_q), lambda i, j: (i, 0)), pl.BlockSpec((tm, n_kv), lambda i, j: (i, 0)),
         pl.BlockSpec((tm, n_kv), lambda i, j: (i, 0))],
        tm, w.shape[1], seq, "att_qkv")

    tq = TILE_ATT_Q
    n_chunks = seq // KEY_CHUNK
    q_tiles = seq // tq
    o = pl.pallas_call(
        functools.partial(_att_kernel, n_chunks=n_chunks),
        grid=(batch, ATT_KV_HEADS, q_tiles),
        in_specs=[pl.BlockSpec((tq, groups * hd), lambda b, g, i: (b * q_tiles + i, g)),
                  pl.BlockSpec((seq, LANES), lambda b, g, i: (b, g)),
                  pl.BlockSpec((seq, LANES), lambda b, g, i: (b, g))],
        out_specs=pl.BlockSpec((tq, groups * hd), lambda b, g, i: (b * q_tiles + i, g)),
        out_shape=jax.ShapeDtypeStruct((n, n_q), BF16),
        scratch_shapes=[pltpu.VMEM((n_chunks, groups * tq, KEY_CHUNK), F32),
                        pltpu.VMEM((groups * tq, LANES), F32), pltpu.VMEM((groups * tq, 2 * LANES), F32)],
        compiler_params=_cparams("parallel", "parallel", "arbitrary"),
        name="gqa_attention",
    )(q, k, v)
    return _out_project(o, 0, n_q, w_o.astype(BF16), x, mod, 0, seq, TILE_PROJ_NARROW, "att_out")


def _ret_proj_epilogue(acc, j, extras, outs):
    cos_ref, sin_ref = extras
    o_ref = outs[0]
    tn = acc.shape[1]
    base = j * tn
    if j < 2:
        cos = cos_ref[...]
        sin = sin_ref[...]
        half = RET_QK_DIM // 2
        for hh in range(RET_HEADS):
            lo = hh * RET_QK_DIM
            x1 = acc[:, lo:lo + half]
            x2 = acc[:, lo + half:lo + RET_QK_DIM]
            o_ref[:, base + lo:base + lo + half] = (x1 * cos - x2 * sin).astype(BF16)
            o_ref[:, base + lo + half:base + lo + RET_QK_DIM] = (x1 * sin + x2 * cos).astype(BF16)
    else:
        o_ref[:, base:base + tn] = acc.astype(BF16)


def _ret_kernel(logit_ref, q_ref, k_ref, v_ref, gain_ref, o_ref, acc_scr, state_f_scr, state_b_scr, *, n_chunks):
    h = pl.program_id(1)
    c_len = RET_CHUNK

    def log_gamma(direction, shape):
        z = jnp.full(shape, logit_ref[direction, h], F32)
        return jnp.minimum(z, 0.0) - jnp.log1p(jnp.exp(-jnp.abs(z)))

    ri = lax.broadcasted_iota(jnp.int32, (c_len, c_len), 0)
    ci = lax.broadcasted_iota(jnp.int32, (c_len, c_len), 1)
    dist = (ri - ci).astype(F32)
    lg_f = log_gamma(0, (c_len, c_len))
    lg_b = log_gamma(1, (c_len, c_len))
    decay = jnp.where(dist >= 0, jnp.exp(lg_f * jnp.maximum(dist, 0.0)), jnp.exp(lg_b * jnp.maximum(-dist, 0.0)))
    idx = lax.broadcasted_iota(jnp.int32, (c_len, 1), 0).astype(F32)
    lgf = log_gamma(0, (c_len, 1))
    lgb = log_gamma(1, (c_len, 1))
    xi_f = jnp.exp(lgf * (idx + 1.0))
    zeta_f = jnp.exp(lgf * (c_len - 1.0 - idx))
    cd_f = jnp.exp(lgf[0:1] * c_len)
    xi_b = jnp.exp(lgb * (c_len - idx))
    zeta_b = jnp.exp(lgb * idx)
    cd_b = jnp.exp(lgb[0:1] * c_len)

    def chunk(cidx):
        start = pl.multiple_of(cidx * c_len, c_len)
        return (q_ref[pl.ds(start, c_len), :], k_ref[pl.ds(start, c_len), :], v_ref[pl.ds(start, c_len), :], start)

    def update_state(state_scr, kc, vc, zeta, cd):
        zv = (vc.astype(F32) * zeta).astype(BF16)
        kv = lax.dot_general(kc, zv, (((0,), (0,)), ((), ())), preferred_element_type=F32)
        state_scr[...] = state_scr[...] * cd + kv

    def left_to_right(cidx):
        qc, kc, vc, start = chunk(cidx)
        scores = lax.dot_general(qc, kc, (((1,), (1,)), ((), ())), preferred_element_type=F32) * decay
        o = jnp.dot(scores.astype(BF16), vc, preferred_element_type=F32)
        o = o + xi_f * jnp.dot(qc, state_f_scr[...].astype(BF16), preferred_element_type=F32)
        update_state(state_f_scr, kc, vc, zeta_f, cd_f)
        return o, start

    def right_to_left(cidx):
        qc, kc, vc, start = chunk(cidx)
        o = xi_b * jnp.dot(qc, state_b_scr[...].astype(BF16), preferred_element_type=F32)
        update_state(state_b_scr, kc, vc, zeta_b, cd_b)
        return o, start

    def finish(y, start):
        y = y * lax.rsqrt(jnp.mean(y * y, axis=-1, keepdims=True) + EPS) * gain_ref[...]
        o_ref[pl.ds(start, c_len), :] = y.astype(BF16)

    state_f_scr[...] = jnp.zeros_like(state_f_scr)
    state_b_scr[...] = jnp.zeros_like(state_b_scr)
    half = n_chunks // 2

    def first_visits(t, carry):
        o_f, start_f = left_to_right(t)
        o_b, start_b = right_to_left(n_chunks - 1 - t)
        acc_scr[pl.ds(start_f, c_len), :] = o_f
        acc_scr[pl.ds(start_b, c_len), :] = o_b
        return carry

    def second_visits(t, carry):
        o_f, start_f = left_to_right(t)
        o_b, start_b = right_to_left(n_chunks - 1 - t)
        finish(acc_scr[pl.ds(start_f, c_len), :] + o_f, start_f)
        finish(acc_scr[pl.ds(start_b, c_len), :] + o_b, start_b)
        return carry

    lax.fori_loop(0, half, first_visits, 0, unroll=8)
    lax.fori_loop(half, n_chunks, second_visits, 0, unroll=4)


def _retention_mixer(x, mod, w_in, decay_logit, gn_gain, w_o, batch, seq):
    n, d = x.shape
    qk_cols = RET_HEADS * RET_QK_DIM
    v_cols = RET_HEADS * RET_V_DIM
    col_scale = jnp.concatenate([jnp.ones((qk_cols,), F32), jnp.full((qk_cols,), RET_QK_DIM ** -0.5, F32),
                                 jnp.ones((2 * v_cols,), F32)])
    w = (w_in * col_scale[None, :]).astype(BF16)
    pos = jnp.arange(seq, dtype=F32)
    inv = 1.0 / (RET_THETA ** jnp.linspace(0.0, 1.0, RET_QK_DIM // 2, dtype=F32))
    ang = pos[:, None] * inv
    tm, tn = TILE_PROJ, qk_cols
    tiles_per_seq = seq // tm
    half = RET_QK_DIM // 2
    extras = [jnp.cos(ang), jnp.sin(ang)]
    extra_specs = [pl.BlockSpec((tm, half), lambda i, j: (i % tiles_per_seq, 0)),
                   pl.BlockSpec((tm, half), lambda i, j: (i % tiles_per_seq, 0))]
    proj = _project(x, mod, 0, w, extras, extra_specs, _ret_proj_epilogue,
                    jax.ShapeDtypeStruct((n, w.shape[1]), BF16), pl.BlockSpec((tm, w.shape[1]), lambda i, j: (i, 0)),
                    tm, tn, seq, "ret_proj")

    n_chunks = seq // RET_CHUNK
    k_off = qk_cols // RET_QK_DIM
    v_off = 2 * qk_cols // RET_V_DIM
    y = pl.pallas_call(
        functools.partial(_ret_kernel, n_chunks=n_chunks),
        grid=(batch, RET_HEADS),
        in_specs=[pl.BlockSpec(memory_space=pltpu.SMEM),
                  pl.BlockSpec((seq, RET_QK_DIM), lambda b, h: (b, h)),
                  pl.BlockSpec((seq, RET_QK_DIM), lambda b, h: (b, k_off + h)),
                  pl.BlockSpec((seq, RET_V_DIM), lambda b, h: (b, v_off + h)),
                  pl.BlockSpec((1, RET_V_DIM), lambda b, h: (0, 0))],
        out_specs=pl.BlockSpec((seq, RET_V_DIM), lambda b, h: (b, h)),
        out_shape=jax.ShapeDtypeStruct((n, v_cols), BF16),
        scratch_shapes=[pltpu.VMEM((seq, RET_V_DIM), F32), pltpu.VMEM((RET_QK_DIM, RET_V_DIM), F32),
                        pltpu.VMEM((RET_QK_DIM, RET_V_DIM), F32)],
        compiler_params=_cparams("parallel", "arbitrary"),
        name="retention",
    )(decay_logit.astype(F32), proj, proj, proj, gn_gain.reshape(1, RET_V_DIM).astype(F32))
    gate_col = (2 * qk_cols + v_cols) // v_cols
    return _out_project(y, 0, v_cols, w_o.astype(BF16), x, mod, 0, seq, TILE_PROJ, "ret_out",
                        gate_src=proj, gate_col=gate_col)


def _conv_pw1_epilogue(acc, j, extras, outs):
    b_ref = extras[0]
    u = acc + b_ref[...]
    dm = u.shape[1] // 2
    outs[0][...] = (u[:, :dm] * jax.nn.sigmoid(u[:, dm:])).astype(BF16)


def _conv_kernel(prev_ref, cur_ref, next_ref, wdw_ref, bdw_ref, lng_ref, lnb_ref, w2_ref, b2_ref, x_ref, mod_ref,
                 o_ref, ext_scr, shift_scr, c_scr, *, tiles_per_seq, row_block):
    tm = cur_ref.shape[0]
    t = pl.program_id(0) % tiles_per_seq
    keep_prev = jnp.where(t == 0, 0.0, 1.0)
    keep_next = jnp.where(t == tiles_per_seq - 1, 0.0, 1.0)
    ext_scr[0:CONV_HALO, :] = prev_ref[...].astype(F32) * keep_prev
    ext_scr[CONV_HALO:CONV_HALO + tm, :] = cur_ref[...].astype(F32)
    ext_scr[CONV_HALO + tm:, :] = next_ref[...].astype(F32) * keep_next
    base = CONV_HALO - CONV_PAD
    span = tm + 2 * CONV_HALO - SUBLANES
    for r in range(1, SUBLANES):
        shift_scr[r - 1, 0:span, :] = ext_scr[r:r + span, :]
    for rb in range(tm // row_block):
        r0 = rb * row_block
        acc = jnp.zeros((row_block, cur_ref.shape[1]), F32) + bdw_ref[...]
        for j in range(CONV_WIDTH):
            phase = (base + j) % SUBLANES
            start = r0 + base + j - phase
            if phase == 0:
                tap = ext_scr[start:start + row_block, :]
            else:
                tap = shift_scr[phase - 1, start:start + row_block, :]
            acc = acc + jnp.tile(wdw_ref[j], (row_block // SUBLANES, 1)) * tap
        c_scr[r0:r0 + row_block, :] = acc
    u = c_scr[...]
    mu = jnp.mean(u, axis=-1, keepdims=True)
    var = jnp.mean(jnp.square(u - mu), axis=-1, keepdims=True)
    u = (u - mu) * lax.rsqrt(var + EPS) * lng_ref[...] + lnb_ref[...]
    u = _silu(u).astype(BF16)
    z = jnp.dot(u, w2_ref[...], preferred_element_type=F32) + b2_ref[...]
    o_ref[...] = x_ref[...] + mod_ref[0][2:3] * z


def _conv_mixer(x, mod, w_pw1, b_pw1, w_dw, b_dw, ln_gain, ln_bias, w_pw2, b_pw2, seq):
    n, d = x.shape
    tm = TILE_PROJ
    u = _project(x, mod, 0, w_pw1.astype(BF16), [b_pw1.reshape(1, 2 * d).astype(F32)],
                 [pl.BlockSpec((1, 2 * d), lambda i, j: (0, 0))], _conv_pw1_epilogue,
                 jax.ShapeDtypeStruct((n, d), BF16), pl.BlockSpec((tm, d), lambda i, j: (i, 0)),
                 tm, 2 * d, seq, "conv_pw1")
    tm = TILE_CONV
    tiles_per_seq = seq // tm
    halo_per_tile = tm // CONV_HALO
    n_halo = n // CONV_HALO
    wdw = jnp.broadcast_to(w_dw.reshape(CONV_WIDTH, 1, d).astype(F32), (CONV_WIDTH, SUBLANES, d))
    vec = lambda a: a.reshape(1, d).astype(F32)
    row_spec = pl.BlockSpec((1, d), lambda i: (0, 0))
    return pl.pallas_call(
        functools.partial(_conv_kernel, tiles_per_seq=tiles_per_seq, row_block=CONV_ROW_BLOCK),
        grid=(n // tm,),
        in_specs=[pl.BlockSpec((CONV_HALO, d), lambda i: (jnp.maximum(i * halo_per_tile - 1, 0), 0)),
                  pl.BlockSpec((tm, d), lambda i: (i, 0)),
                  pl.BlockSpec((CONV_HALO, d), lambda i: (jnp.minimum((i + 1) * halo_per_tile, n_halo - 1), 0)),
                  pl.BlockSpec((CONV_WIDTH, SUBLANES, d), lambda i: (0, 0, 0)),
                  row_spec, row_spec, row_spec,
                  pl.BlockSpec((d, d), lambda i: (0, 0)),
                  row_spec,
                  pl.BlockSpec((tm, d), lambda i: (i, 0)),
                  pl.BlockSpec((1, 6, d), lambda i: (i // tiles_per_seq, 0, 0))],
        out_specs=pl.BlockSpec((tm, d), lambda i: (i, 0)),
        out_shape=jax.ShapeDtypeStruct((n, d), F32),
        scratch_shapes=[pltpu.VMEM((tm + 2 * CONV_HALO, d), F32),
                        pltpu.VMEM((SUBLANES - 1, tm + 2 * CONV_HALO, d), F32), pltpu.VMEM((tm, d), F32)],
        compiler_params=_cparams("parallel"),
        name="conv_module",
    )(u, u, u, wdw, vec(b_dw), vec(ln_gain), vec(ln_bias), w_pw2.astype(BF16), vec(b_pw2), x, mod)


def _t5_bucket(rel):
    half = T5_BUCKETS // 2
    exact = half // 2
    n = jnp.abs(rel)
    log_ratio = jnp.log(jnp.maximum(n, 1).astype(F32) / exact) / math.log(T5_MAX_DIST / exact)
    large = jnp.minimum(exact + (log_ratio * (half - exact)).astype(jnp.int32), half - 1)
    return jnp.where(rel > 0, half, 0) + jnp.where(n < exact, n, large)


def _bias_tile_kernel(table_ref, idx_ref, o_ref):
    h = pl.program_id(0)
    idx = idx_ref[0]
    acc = jnp.zeros(idx.shape, F32)
    for b in range(T5_BUCKETS):
        acc = jnp.where(idx == b, table_ref[b, h], acc)
    o_ref[0, 0] = acc * LOG2_E


def _diff_kernel(q_ref, k_ref, v_ref, bias_ref, lam_ref, gain_ref, o_ref, s_scr, m_scr, acc_scr, *, n_chunks, out_scale,
                 lambda_init):
    tq = q_ref.shape[0]
    qi = pl.program_id(2)
    q = q_ref[...]
    lane = lax.broadcasted_iota(jnp.int32, (tq, LANES), 1)
    low = lane < DIFF_HEAD_DIM
    zero = jnp.zeros((tq, LANES), BF16)
    lhs = jnp.concatenate([jnp.where(low, q, zero), jnp.where(low, zero, q)], axis=0)

    q_sub = tq // KEY_CHUNK

    def add_bias(s, j):
        tiles = [bias_ref[0, jnp.clip(j - (q_sub * qi + a), -2, 2) + 2] for a in range(q_sub)]
        return s + jnp.concatenate(tiles + tiles, axis=0)

    o = _softmax_pv(lhs, k_ref, v_ref, s_scr, m_scr, acc_scr, n_chunks, add_bias)
    lam_v = lam_ref[...]
    lam = (jnp.exp(jnp.sum(lam_v[0:1] * lam_v[1:2], axis=-1, keepdims=True))
           - jnp.exp(jnp.sum(lam_v[2:3] * lam_v[3:4], axis=-1, keepdims=True)) + lambda_init)
    y = o[:tq] - lam * o[tq:]
    y = y * lax.rsqrt(jnp.mean(y * y, axis=-1, keepdims=True) + EPS) * gain_ref[...] * out_scale
    o_ref[...] = y.astype(BF16)


def _diff_mixer(x, mod, w_qkv, lam_q1, lam_k1, lam_q2, lam_k2, subln_gain, w_o, rel_bias, lambda_init,
                batch, seq):
    n, d = x.shape
    col_scale = jnp.concatenate([jnp.full((d,), DIFF_HEAD_DIM ** -0.5 * LOG2_E, F32), jnp.ones((2 * d,), F32)])
    w = (w_qkv * col_scale[None, :]).astype(BF16)
    tm = TILE_PROJ_NARROW
    qkv = _project(x, mod, 0, w, [], [], _plain_epilogue,
                   jax.ShapeDtypeStruct((n, 3 * d), BF16), pl.BlockSpec((tm, 3 * d), lambda i, j: (i, 0)),
                   tm, d, seq, "diff_qkv")

    tb = KEY_CHUNK
    ar = jnp.arange(KEY_CHUNK, dtype=jnp.int32)
    offs = (jnp.arange(5, dtype=jnp.int32) - 2) * KEY_CHUNK
    rel = (ar[None, None, :] + offs[:, None, None]) - ar[None, :, None]
    bucket = _t5_bucket(rel).astype(jnp.int32)
    bias = pl.pallas_call(
        _bias_tile_kernel,
        grid=(DIFF_HEADS, 5),
        in_specs=[pl.BlockSpec(memory_space=pltpu.SMEM),
                  pl.BlockSpec((1, tb, KEY_CHUNK), lambda h, t: (t, 0, 0))],
        out_specs=pl.BlockSpec((1, 1, tb, KEY_CHUNK), lambda h, t: (h, t, 0, 0)),
        out_shape=jax.ShapeDtypeStruct((DIFF_HEADS, 5, tb, KEY_CHUNK), F32),
        compiler_params=_cparams("arbitrary", "arbitrary"),
        name="rel_bias_tiles",
    )(rel_bias.astype(F32), bucket)

    lam = jnp.stack([lam_q1, lam_k1, lam_q2, lam_k2]).astype(F32)
    n_chunks = seq // KEY_CHUNK
    tq = TILE_DIFF_Q
    q_tiles = seq // tq
    o = pl.pallas_call(
        functools.partial(_diff_kernel, n_chunks=n_chunks, out_scale=1.0 - lambda_init, lambda_init=lambda_init),
        grid=(batch, DIFF_HEADS, q_tiles),
        in_specs=[pl.BlockSpec((tq, LANES), lambda b, h, i: (b * q_tiles + i, h)),
                  pl.BlockSpec((seq, LANES), lambda b, h, i: (b, DIFF_HEADS + h)),
                  pl.BlockSpec((seq, LANES), lambda b, h, i: (b, 2 * DIFF_HEADS + h)),
                  pl.BlockSpec((1, 5, tb, KEY_CHUNK), lambda b, h, i: (h, 0, 0, 0)),
                  pl.BlockSpec((4, DIFF_HEAD_DIM), lambda b, h, i: (0, 0)),
                  pl.BlockSpec((1, LANES), lambda b, h, i: (0, 0))],
        out_specs=pl.BlockSpec((tq, LANES), lambda b, h, i: (b * q_tiles + i, h)),
        out_shape=jax.ShapeDtypeStruct((n, d), BF16),
        scratch_shapes=[pltpu.VMEM((n_chunks, 2 * tq, KEY_CHUNK), F32),
                        pltpu.VMEM((2 * tq, LANES), F32), pltpu.VMEM((2 * tq, 2 * LANES), F32)],
        compiler_params=_cparams("parallel", "parallel", "arbitrary"),
        name="diff_attention",
    )(qkv, qkv, qkv, bias, lam, subln_gain.reshape(1, LANES).astype(F32))
    return _out_project(o, 0, d, w_o.astype(BF16), x, mod, 0, seq, TILE_PROJ_NARROW, "diff_out")


def _final_norm_kernel(x_ref, g_ref, o_ref):
    x = x_ref[...]
    o_ref[...] = x * lax.rsqrt(jnp.mean(x * x, axis=-1, keepdims=True) + EPS) * g_ref[...]


def _final_norm(x, gain, tm):
    n, d = x.shape
    return pl.pallas_call(
        _final_norm_kernel,
        grid=(n // tm,),
        in_specs=[pl.BlockSpec((tm, d), lambda i: (i, 0)), pl.BlockSpec((1, d), lambda i: (0, 0))],
        out_specs=pl.BlockSpec((tm, d), lambda i: (i, 0)),
        out_shape=jax.ShapeDtypeStruct((n, d), F32),
        compiler_params=_cparams("parallel"),
        name="final_norm",
    )(x, gain.reshape(1, d).astype(F32))


def kernel(x, c, mod_w, mod_b, att_w_qkv, att_q_gain, att_k_gain, att_w_o, ret_w_in, ret_decay_logit, ret_gn_gain, ret_w_o, cnv_w_pw1, cnv_b_pw1, cnv_w_dw, cnv_b_dw, cnv_ln_gain, cnv_ln_bias, cnv_w_pw2, cnv_b_pw2, dif_w_qkv, dif_lam_q1, dif_lam_k1, dif_lam_q2, dif_lam_k2, dif_subln_gain, dif_w_o, rel_bias, ffn_w_gate, ffn_w_up, ffn_w_down, moe_w_router, moe_b_router, moe_w_gate, moe_w_up, moe_w_down, final_gain):
    batch, seq, d = x.shape
    depth = mod_w.shape[0]
    xf = x.reshape(batch * seq, d)
    mods = _modulation(c, mod_w, mod_b)
    moe_gate, moe_up, moe_down = moe_w_gate.astype(BF16), moe_w_up.astype(BF16), moe_w_down.astype(BF16)
    for i in range(depth):
        mod = mods[i]
        kind, j = i % 4, i // 4
        if kind == 0:
            xf = _attention_mixer(xf, mod, att_w_qkv[j], att_q_gain[j], att_k_gain[j], att_w_o[j], batch, seq)
        elif kind == 1:
            xf = _retention_mixer(xf, mod, ret_w_in[j], ret_decay_logit[j], ret_gn_gain[j], ret_w_o[j], batch, seq)
        elif kind == 2:
            xf = _conv_mixer(xf, mod, cnv_w_pw1[j], cnv_b_pw1[j], cnv_w_dw[j], cnv_b_dw[j], cnv_ln_gain[j],
                             cnv_ln_bias[j], cnv_w_pw2[j], cnv_b_pw2[j], seq)
        else:
            lambda_init = 0.8 - 0.6 * math.exp(-0.3 * i)
            xf = _diff_mixer(xf, mod, dif_w_qkv[j], dif_lam_q1[j], dif_lam_k1[j], dif_lam_q2[j], dif_lam_k2[j],
                             dif_subln_gain[j], dif_w_o[j], rel_bias, lambda_init, batch, seq)
        if i % 2 == 0:
            m = i // 2
            xf = _ffn(xf, mod, ffn_w_gate[m].astype(BF16), ffn_w_up[m].astype(BF16), ffn_w_down[m].astype(BF16),
                      seq, TILE_PROJ)
        else:
            m = i // 2
            xf = _moe(xf, mod, moe_w_router[m], moe_b_router[m], moe_gate, moe_up, moe_down, m, seq,
                      out_gain=final_gain if i == depth - 1 else None)
    if depth % 2 == 1:
        xf = _final_norm(xf, final_gain, TILE_PROJ)
    return xf.reshape(batch, seq, d)
```

```python
import functools
import math

import jax
import jax.numpy as jnp
from jax import lax
from jax.experimental import pallas as pl
from jax.experimental.pallas import tpu as pltpu

F32 = jnp.float32
BF16 = jnp.bfloat16

EPS = 1e-6
GRID_W = 64

ATT_HEADS = 16
ATT_KV_HEADS = 4
ATT_HEAD_DIM = 64
ROPE_THETA = 10000.0

RET_HEADS = 4
RET_QK_DIM = 256
RET_V_DIM = 512
RET_THETA = 10000.0
RET_CHUNK = 256

CONV_WIDTH = 31
CONV_PAD = CONV_WIDTH // 2
CONV_HALO = 16

DIFF_HEADS = 8
DIFF_HEAD_DIM = 64
T5_BUCKETS = 32
T5_MAX_DIST = 128

N_EXPERTS = 8
LANES = 128
SUBLANES = 8
KEY_CHUNK = 256
LOG2_E = 1.4426950408889634
VMEM_LIMIT_BYTES = 56 * 1024 * 1024

TILE_PROJ = 512
TILE_PROJ_NARROW = 1024
TILE_ROUTE = 512
TILE_SLOT = 512
TILE_EXPERT_HIDDEN = 1792
TILE_CONV = 256
CONV_ROW_BLOCK = 32
TILE_ATT_Q = 512
TILE_DIFF_Q = 1024


def _cparams(*sem):
    return pltpu.CompilerParams(dimension_semantics=sem, vmem_limit_bytes=VMEM_LIMIT_BYTES)


def _silu(v):
    return v * jax.nn.sigmoid(v)


def _norm_mod(x, mod, which):
    shift = mod[3 * which:3 * which + 1]
    scale = mod[3 * which + 1:3 * which + 2]
    y = x * lax.rsqrt(jnp.mean(x * x, axis=-1, keepdims=True) + EPS)
    return y * (1.0 + scale) + shift


def _split_dot(a, b):
    hi = a.astype(BF16)
    lo = (a - hi.astype(F32)).astype(BF16)
    return (jnp.dot(hi, b, preferred_element_type=F32) + jnp.dot(lo, b, preferred_element_type=F32))


def _mod_kernel(c_ref, w_ref, b_ref, o_ref):
    cond = _silu(c_ref[...])
    o_ref[0] = jnp.dot(cond, w_ref[0], preferred_element_type=F32, precision=lax.Precision.HIGHEST) + b_ref[0]


def _modulation(c, mod_w, mod_b):
    depth, d, d6 = mod_w.shape
    b = c.shape[0]
    tn = d
    out = pl.pallas_call(
        _mod_kernel,
        grid=(depth, d6 // tn),
        in_specs=[pl.BlockSpec((b, d), lambda i, j: (0, 0)),
                  pl.BlockSpec((1, d, tn), lambda i, j: (i, 0, j)),
                  pl.BlockSpec((1, 1, tn), lambda i, j: (i, 0, j))],
        out_specs=pl.BlockSpec((1, b, tn), lambda i, j: (i, 0, j)),
        out_shape=jax.ShapeDtypeStruct((depth, b, d6), F32),
        compiler_params=_cparams("arbitrary", "arbitrary"),
        name="modulation",
    )(c, mod_w, mod_b.reshape(depth, 1, d6))
    return out.reshape(depth, b, 6, d)


def _proj_kernel(x_ref, mod_ref, w_ref, *rest, which, n_extra, epilogue, tn):
    extras = rest[:n_extra]
    outs = rest[n_extra:]
    h = _norm_mod(x_ref[...], mod_ref[0], which).astype(BF16)
    for j in range(w_ref.shape[1] // tn):
        acc = jnp.dot(h, w_ref[:, j * tn:(j + 1) * tn], preferred_element_type=F32)
        epilogue(acc, j, extras, outs)


def _project(x, mod, which, w, extras, extra_specs, epilogue, out_shapes, out_specs, tm, tn, seq, name):
    n, d = x.shape
    c = w.shape[1]
    tiles_per_seq = seq // tm
    kern = functools.partial(_proj_kernel, which=which, n_extra=len(extras), epilogue=epilogue, tn=tn)
    return pl.pallas_call(
        kern,
        grid=(n // tm, 1),
        in_specs=[pl.BlockSpec((tm, d), lambda i, j: (i, 0)),
                  pl.BlockSpec((1, 6, d), lambda i, j: (i // tiles_per_seq, 0, 0)),
                  pl.BlockSpec((d, c), lambda i, j: (0, 0), pipeline_mode=pl.Buffered(1))] + extra_specs,
        out_specs=out_specs,
        out_shape=out_shapes,
        compiler_params=_cparams("parallel", "arbitrary"),
        name=name,
    )(x, mod, w, *extras)


def _plain_epilogue(acc, j, extras, outs):
    tn = acc.shape[1]
    outs[0][:, j * tn:(j + 1) * tn] = acc.astype(outs[0].dtype)


def _oproj_kernel(*refs, which, gated, has_bias):
    refs = list(refs)
    y_ref = refs.pop(0)
    g_ref = refs.pop(0) if gated else None
    w_ref = refs.pop(0)
    b_ref = refs.pop(0) if has_bias else None
    x_ref, mod_ref, o_ref = refs
    y = y_ref[...]
    if gated:
        y = (y.astype(F32) * _silu(g_ref[...].astype(F32))).astype(BF16)
    z = jnp.dot(y, w_ref[...], preferred_element_type=F32)
    if has_bias:
        z = z + b_ref[...]
    gate = mod_ref[0][3 * which + 2:3 * which + 3]
    o_ref[...] = x_ref[...] + gate * z


def _out_project(y, y_col, k, w, x, mod, which, seq, tm, name, gate_src=None, gate_col=0, bias=None):
    n, d = x.shape
    tiles_per_seq = seq // tm
    args = [y]
    specs = [pl.BlockSpec((tm, k), lambda i: (i, y_col))]
    if gate_src is not None:
        args.append(gate_src)
        specs.append(pl.BlockSpec((tm, k), lambda i: (i, gate_col)))
    args.append(w)
    specs.append(pl.BlockSpec((k, d), lambda i: (0, 0)))
    if bias is not None:
        args.append(bias.reshape(1, d))
        specs.append(pl.BlockSpec((1, d), lambda i: (0, 0)))
    args += [x, mod]
    specs += [pl.BlockSpec((tm, d), lambda i: (i, 0)),
              pl.BlockSpec((1, 6, d), lambda i: (i // tiles_per_seq, 0, 0))]
    kern = functools.partial(_oproj_kernel, which=which, gated=gate_src is not None, has_bias=bias is not None)
    return pl.pallas_call(
        kern,
        grid=(n // tm,),
        in_specs=specs,
        out_specs=pl.BlockSpec((tm, d), lambda i: (i, 0)),
        out_shape=jax.ShapeDtypeStruct((n, d), F32),
        compiler_params=_cparams("parallel"),
        name=name,
    )(*args)


def _ffn_kernel(x_ref, mod_ref, wg_ref, wu_ref, wd_ref, o_ref):
    x = x_ref[...]
    mod = mod_ref[0]
    h = _norm_mod(x, mod, 1).astype(BF16)
    g = jnp.dot(h, wg_ref[...], preferred_element_type=F32)
    u = jnp.dot(h, wu_ref[...], preferred_element_type=F32)
    a = (_silu(g) * u).astype(BF16)
    o_ref[...] = x + mod[5:6] * jnp.dot(a, wd_ref[...], preferred_element_type=F32)


def _ffn(x, mod, wg, wu, wd, seq, tm):
    n, d = x.shape
    ff = wg.shape[1]
    tiles_per_seq = seq // tm
    resident = pl.Buffered(1)
    return pl.pallas_call(
        _ffn_kernel,
        grid=(n // tm,),
        in_specs=[pl.BlockSpec((tm, d), lambda i: (i, 0)),
                  pl.BlockSpec((1, 6, d), lambda i: (i // tiles_per_seq, 0, 0)),
                  pl.BlockSpec((d, ff), lambda i: (0, 0), pipeline_mode=resident),
                  pl.BlockSpec((d, ff), lambda i: (0, 0), pipeline_mode=resident),
                  pl.BlockSpec((ff, d), lambda i: (0, 0), pipeline_mode=resident)],
        out_specs=pl.BlockSpec((tm, d), lambda i: (i, 0)),
        out_shape=jax.ShapeDtypeStruct((n, d), F32),
        compiler_params=_cparams("parallel"),
        name="dense_ffn",
    )(x, mod, wg, wu, wd)


def _router_kernel(x_ref, mod_ref, wr_ref, br_ref, tri_ref, route_ref, route_t_ref, count_ref, carry_scr):
    @pl.when(pl.program_id(0) == 0)
    def _():
        carry_scr[...] = jnp.zeros_like(carry_scr)

    h = _norm_mod(x_ref[...], mod_ref[0], 1)
    h_hi = h.astype(BF16)
    h_lo = (h - h_hi.astype(F32)).astype(BF16)
    logits = (jnp.dot(h_hi, wr_ref[0], preferred_element_type=F32)
              + (jnp.dot(h_hi, wr_ref[1], preferred_element_type=F32)
                 + jnp.dot(h_lo, wr_ref[0], preferred_element_type=F32))) + br_ref[...]
    lane = lax.broadcasted_iota(jnp.int32, logits.shape, 1)
    neg = jnp.float32(-jnp.inf)
    logits = jnp.where(lane < N_EXPERTS, logits, neg)
    m1 = jnp.max(logits, axis=-1, keepdims=True)
    i1 = jnp.min(jnp.where(logits == m1, lane, LANES), axis=-1, keepdims=True)
    rest = jnp.where(lane == i1, neg, logits)
    m2 = jnp.max(rest, axis=-1, keepdims=True)
    i2 = jnp.min(jnp.where(rest == m2, lane, LANES), axis=-1, keepdims=True)
    e2 = jnp.exp(m2 - m1)
    w1 = 1.0 / (1.0 + e2)
    w2 = e2 / (1.0 + e2)
    sel = jnp.where((lane == i1) | (lane == i2), 1.0, 0.0)
    incl = jnp.dot(tri_ref[...], sel.astype(BF16), preferred_element_type=F32)
    excl = incl - sel + carry_scr[...]
    r1 = jnp.sum(jnp.where(lane == i1, excl, 0.0), axis=-1, keepdims=True)
    r2 = jnp.sum(jnp.where(lane == i2, excl, 0.0), axis=-1, keepdims=True)
    carry_scr[...] += incl[incl.shape[0] - 1:, :]
    count_ref[...] = carry_scr[...]
    fields = (i1.astype(F32), i2.astype(F32), w1, w2, r1, r2)
    route = jnp.zeros(logits.shape, F32)
    for k, field in enumerate(fields):
        route = jnp.where(lane == k, field, route)
    route_ref[...] = route
    route_t_ref[...] = route.T[:SUBLANES]


def _router(x, mod, w_router, b_router, seq, tm):
    n, d = x.shape
    tiles_per_seq = seq // tm
    wr = jnp.pad(w_router, ((0, 0), (0, LANES - N_EXPERTS))).astype(F32)
    wr_hi = wr.astype(BF16)
    wr = jnp.stack([wr_hi, (wr - wr_hi.astype(F32)).astype(BF16)])
    br = jnp.pad(b_router, (0, LANES - N_EXPERTS)).reshape(1, LANES)
    tri = (jnp.arange(tm)[:, None] >= jnp.arange(tm)[None, :]).astype(BF16)
    return pl.pallas_call(
        _router_kernel,
        grid=(n // tm,),
        in_specs=[pl.BlockSpec((tm, d), lambda i: (i, 0)),
                  pl.BlockSpec((1, 6, d), lambda i: (i // tiles_per_seq, 0, 0)),
                  pl.BlockSpec((2, d, LANES), lambda i: (0, 0, 0)),
                  pl.BlockSpec((1, LANES), lambda i: (0, 0)),
                  pl.BlockSpec((tm, tm), lambda i: (0, 0))],
        out_specs=[pl.BlockSpec((tm, LANES), lambda i: (i, 0)), pl.BlockSpec((SUBLANES, tm), lambda i: (0, i)),
                   pl.BlockSpec((1, LANES), lambda i: (0, 0))],
        out_shape=[jax.ShapeDtypeStruct((n, LANES), F32), jax.ShapeDtypeStruct((SUBLANES, n), F32),
                   jax.ShapeDtypeStruct((1, LANES), F32)],
        scratch_shapes=[pltpu.VMEM((1, LANES), F32)],
        compiler_params=_cparams("arbitrary"),
        name="moe_router",
    )(x, mod, wr, br, tri)


def _dispatch_kernel(zero_rows_ref, d1_ref, d2_ref, x_ref, mod_ref, xs_ref, h_scr, zero_scr, sems):
    tm = x_ref.shape[0]

    @pl.when(pl.program_id(0) == 0)
    def _():
        zero_scr[...] = jnp.zeros_like(zero_scr)
        te = zero_scr.shape[0]
        for k in range(zero_rows_ref.shape[0]):
            start = pl.multiple_of(zero_rows_ref[k], te)
            copy = pltpu.make_async_copy(zero_scr, xs_ref.at[pl.ds(start, te), :], sems.at[2])
            copy.start()
            copy.wait()

    h_scr[...] = _norm_mod(x_ref[...], mod_ref[0], 1)

    def row_copy(r, dest, which):
        return pltpu.make_async_copy(h_scr.at[pl.ds(r, 1), :], xs_ref.at[pl.ds(dest, 1), :], sems.at[which])

    def issue(r, carry):
        row_copy(r, d1_ref[0, 0, r], 0).start()
        row_copy(r, d2_ref[0, 0, r], 1).start()
        return carry

    lax.fori_loop(0, tm, issue, 0, unroll=4)
    for which in range(2):
        pltpu.make_async_copy(h_scr, xs_ref.at[pl.ds(0, tm), :], sems.at[which]).wait()


def _expert_kernel(tile_e_ref, n_act_ref, x_ref, wg_ref, wu_ref, wd_ref, o_ref, *, tf):
    del tile_e_ref
    s = pl.program_id(0)

    @pl.when(s < n_act_ref[0])
    def _():
        h = x_ref[...].astype(BF16)
        out = None
        for f in range(wg_ref.shape[3] // tf):
            cols = slice(f * tf, (f + 1) * tf)
            g = jnp.dot(h, wg_ref[0, 0, :, cols], preferred_element_type=F32)
            u = jnp.dot(h, wu_ref[0, 0, :, cols], preferred_element_type=F32)
            a = (_silu(g) * u).astype(BF16)
            part = jnp.dot(a, wd_ref[0, 0, cols, :], preferred_element_type=F32)
            out = part if out is None else out + part
        o_ref[...] = out

    @pl.when(s >= n_act_ref[0])
    def _():
        o_ref[...] = jnp.zeros_like(o_ref)


def _combine_kernel(d1_ref, d2_ref, route_ref, x_ref, mod_ref, *rest, out_norm):
    if out_norm:
        gain_ref, ys_ref, o_ref, buf, sems = rest
    else:
        ys_ref, o_ref, buf, sems = rest
    tm = x_ref.shape[0]

    def row_copy(r, src, which):
        return pltpu.make_async_copy(ys_ref.at[pl.ds(src, 1), :], buf.at[which, pl.ds(r, 1), :], sems.at[which])

    def issue(r, carry):
        row_copy(r, d1_ref[0, 0, r], 0).start()
        row_copy(r, d2_ref[0, 0, r], 1).start()
        return carry

    lax.fori_loop(0, tm, issue, 0, unroll=4)
    for which in range(2):
        pltpu.make_async_copy(ys_ref.at[pl.ds(0, tm), :], buf.at[which], sems.at[which]).wait()
    route = route_ref[...]
    y = route[:, 2:3] * buf[0] + route[:, 3:4] * buf[1]
    out = x_ref[...] + mod_ref[0][5:6] * y
    if out_norm:
        out = out * lax.rsqrt(jnp.mean(out * out, axis=-1, keepdims=True) + EPS) * gain_ref[...]
    o_ref[...] = out


def _moe(x, mod, w_router, b_router, w_gate, w_up, w_down, layer, seq, out_gain=None):
    n, d = x.shape
    _, n_e, _, ff = w_gate.shape
    tm, te, tf = TILE_ROUTE, TILE_SLOT, TILE_EXPERT_HIDDEN
    tiles_per_seq = seq // tm
    n_tok_tiles = n // tm
    n_slots = 2 * n + n_e * te
    n_tiles = n_slots // te

    route, route_t, counts = _router(x, mod, w_router, b_router, seq, tm)

    counts = counts[0, :n_e].astype(jnp.int32)
    padded = ((counts + te - 1) // te) * te
    ends = jnp.cumsum(padded)
    offsets = ends - padded
    n_act = (ends[-1] // te).astype(jnp.int32)
    tile_start = jnp.arange(n_tiles, dtype=jnp.int32) * te
    tile_e = jnp.sum((tile_start[:, None] >= ends[None, :]).astype(jnp.int32), axis=1)
    tile_e = jnp.minimum(tile_e, tile_e[jnp.maximum(n_act - 1, 0)]).astype(jnp.int32)
    e1 = route_t[0].astype(jnp.int32)
    e2 = route_t[1].astype(jnp.int32)
    dest1 = (offsets[e1] + route_t[4].astype(jnp.int32)).reshape(n_tok_tiles, 1, tm)
    dest2 = (offsets[e2] + route_t[5].astype(jnp.int32)).reshape(n_tok_tiles, 1, tm)
    n_act = n_act.reshape(1)

    slot_spec = pl.BlockSpec((1, 1, tm), lambda i: (i, 0, 0), memory_space=pltpu.SMEM)
    zero_rows = jnp.concatenate([jnp.maximum(ends - te, 0),
                                 (n_tiles - n_e + jnp.arange(n_e, dtype=jnp.int32)) * te]).astype(jnp.int32)
    xs = pl.pallas_call(
        _dispatch_kernel,
        grid=(n_tok_tiles,),
        in_specs=[pl.BlockSpec(memory_space=pltpu.SMEM), slot_spec, slot_spec,
                  pl.BlockSpec((tm, d), lambda i: (i, 0)),
                  pl.BlockSpec((1, 6, d), lambda i: (i // tiles_per_seq, 0, 0))],
        out_specs=pl.BlockSpec(memory_space=pl.ANY),
        out_shape=jax.ShapeDtypeStruct((n_slots, d), F32),
        scratch_shapes=[pltpu.VMEM((tm, d), F32), pltpu.VMEM((te, d), F32), pltpu.SemaphoreType.DMA((3,))],
        compiler_params=_cparams("arbitrary"),
        name="moe_dispatch",
    )(zero_rows, dest1, dest2, x, mod)

    def live(s, n_act_ref):
        return jnp.minimum(s, jnp.maximum(n_act_ref[0] - 1, 0))

    def expert_weights(shape):
        return pl.BlockSpec(shape, lambda s, te_ref, na_ref: (layer, te_ref[s], 0, 0), pipeline_mode=pl.Buffered(1))

    ys = pl.pallas_call(
        functools.partial(_expert_kernel, tf=tf),
        grid_spec=pltpu.PrefetchScalarGridSpec(
            num_scalar_prefetch=2,
            grid=(n_tiles,),
            in_specs=[pl.BlockSpec((te, d), lambda s, te_ref, na_ref: (live(s, na_ref), 0)),
                      expert_weights((1, 1, d, ff)), expert_weights((1, 1, d, ff)), expert_weights((1, 1, ff, d))],
            out_specs=pl.BlockSpec((te, d), lambda s, te_ref, na_ref: (s, 0))),
        out_shape=jax.ShapeDtypeStruct((n_slots, d), F32),
        compiler_params=_cparams("arbitrary"),
        name="moe_experts",
    )(tile_e, n_act, xs, w_gate, w_up, w_down)

    out_norm = out_gain is not None
    gain_args = [out_gain.reshape(1, d).astype(F32)] if out_norm else []
    gain_specs = [pl.BlockSpec((1, d), lambda i: (0, 0))] if out_norm else []
    return pl.pallas_call(
        functools.partial(_combine_kernel, out_norm=out_norm),
        grid=(n_tok_tiles,),
        in_specs=[slot_spec, slot_spec,
                  pl.BlockSpec((tm, LANES), lambda i: (i, 0)),
                  pl.BlockSpec((tm, d), lambda i: (i, 0)),
                  pl.BlockSpec((1, 6, d), lambda i: (i // tiles_per_seq, 0, 0))] + gain_specs
                 + [pl.BlockSpec(memory_space=pl.ANY)],
        out_specs=pl.BlockSpec((tm, d), lambda i: (i, 0)),
        out_shape=jax.ShapeDtypeStruct((n, d), F32),
        scratch_shapes=[pltpu.VMEM((2, tm, d), F32), pltpu.SemaphoreType.DMA((2,))],
        compiler_params=_cparams("arbitrary"),
        name="moe_combine",
    )(dest1, dest2, route, x, mod, *gain_args, ys)


def _softmax_pv(lhs, k_ref, v_ref, s_scr, m_scr, acc_scr, n_chunks, bias_fn=None):
    rows = lhs.shape[0]
    group = min(8, n_chunks)
    m_scr[...] = jnp.full(m_scr.shape, -jnp.inf, F32)

    for j in range(n_chunks):
        kc = k_ref[j * KEY_CHUNK:(j + 1) * KEY_CHUNK, :]
        s = lax.dot_general(lhs, kc, (((1,), (1,)), ((), ())), preferred_element_type=F32)
        if bias_fn is not None:
            s = bias_fn(s, j)
        s_scr[j] = s
        m_scr[...] = jnp.maximum(m_scr[...], jnp.maximum(s[:, :LANES], s[:, LANES:]))
    m_scr[...] = jnp.broadcast_to(jnp.max(m_scr[...], axis=-1, keepdims=True), m_scr.shape)
    acc_scr[...] = jnp.zeros_like(acc_scr)
    ones = jnp.ones((group * KEY_CHUNK, LANES), BF16)

    def pv_body(t, carry):
        mb = m_scr[...]
        parts = []
        for u in range(group):
            s = s_scr[group * t + u]
            parts.append(jnp.exp2(s[:, :LANES] - mb).astype(BF16))
            parts.append(jnp.exp2(s[:, LANES:] - mb).astype(BF16))
        p = jnp.concatenate(parts, axis=1)
        start = pl.multiple_of(t * (group * KEY_CHUNK), group * KEY_CHUNK)
        v_aug = jnp.concatenate([v_ref[pl.ds(start, group * KEY_CHUNK), :], ones], axis=1)
        acc_scr[...] += jnp.dot(p, v_aug, preferred_element_type=F32)
        return carry

    lax.fori_loop(0, n_chunks // group, pv_body, 0)
    acc = acc_scr[...]
    return acc[:, :LANES] / acc[:, LANES:]


def _att_qkv_epilogue(acc, j, extras, outs, n_qk):
    gain_ref, cos_ref, sin_ref, g1_ref, g2_ref = extras
    q_ref, k_ref, v_ref = outs
    qk = acc[:, :n_qk]
    ms = _split_dot(qk * qk, g1_ref[...])
    r = lax.rsqrt(ms + EPS)
    y = qk * _split_dot(r, g2_ref[...]) * gain_ref[...]
    cos = cos_ref[...]
    sin = sin_ref[...]
    lane = lax.broadcasted_iota(jnp.int32, cos.shape, 1)
    first_half = (lane & (ATT_HEAD_DIM // 2)) == 0
    nq = q_ref.shape[1]
    for c in range(n_qk // LANES):
        yc = y[:, c * LANES:(c + 1) * LANES]
        partner = jnp.where(first_half, pltpu.roll(yc, LANES - ATT_HEAD_DIM // 2, 1),
                            pltpu.roll(yc, ATT_HEAD_DIM // 2, 1))
        out = (yc * cos + partner * sin).astype(BF16)
        if c * LANES < nq:
            q_ref[:, c * LANES:(c + 1) * LANES] = out
        else:
            k_ref[:, c * LANES - nq:(c + 1) * LANES - nq] = out
    v_ref[...] = acc[:, n_qk:].astype(BF16)


def _att_kernel(q_ref, k_ref, v_ref, o_ref, s_scr, m_scr, acc_scr, *, n_chunks):
    tq = q_ref.shape[0]
    q = q_ref[...]
    lane = lax.broadcasted_iota(jnp.int32, (tq, LANES), 1)
    low = lane < ATT_HEAD_DIM
    zero = jnp.zeros((tq, LANES), BF16)
    parts = []
    for pair in range(2):
        qp = q[:, pair * LANES:(pair + 1) * LANES]
        parts.append(jnp.where(low, qp, zero))
        parts.append(jnp.where(low, zero, qp))
    lhs = jnp.concatenate(parts, axis=0)
    o = _softmax_pv(lhs, k_ref, v_ref, s_scr, m_scr, acc_scr, n_chunks)
    for pair in range(2):
        o_ref[:, pair * LANES:(pair + 1) * LANES] = jnp.where(
            low, o[(2 * pair) * tq:(2 * pair + 1) * tq], o[(2 * pair + 1) * tq:(2 * pair + 2) * tq]).astype(BF16)


def _attention_mixer(x, mod, w_qkv, q_gain, k_gain, w_o, batch, seq):
    n, d = x.shape
    hd = ATT_HEAD_DIM
    half = hd // 2
    n_freq = half // 2
    groups = ATT_HEADS // ATT_KV_HEADS
    perm = jnp.concatenate([jnp.arange(n_freq), half + jnp.arange(n_freq),
                            n_freq + jnp.arange(n_freq), half + n_freq + jnp.arange(n_freq)])
    q_cols = (jnp.arange(ATT_HEADS)[:, None] * hd + perm[None, :]).reshape(-1)
    k_base = ATT_HEADS * hd
    v_base = k_base + ATT_KV_HEADS * hd
    kv_rep = jnp.repeat(jnp.arange(ATT_KV_HEADS), 2)
    k_cols = (k_base + kv_rep[:, None] * hd + perm[None, :]).reshape(-1)
    v_cols = (v_base + kv_rep[:, None] * hd + jnp.arange(hd)[None, :]).reshape(-1)
    w = w_qkv[:, jnp.concatenate([q_cols, k_cols, v_cols])].astype(BF16)
    n_q = ATT_HEADS * hd
    n_kv = 2 * ATT_KV_HEADS * hd
    n_qk = n_q + n_kv
    gain_row = jnp.concatenate([jnp.tile(q_gain[perm] * (hd ** -0.5 * LOG2_E), ATT_HEADS),
                                jnp.tile(k_gain[perm], 2 * ATT_KV_HEADS)]).reshape(1, n_qk).astype(F32)
    pos = jnp.arange(seq, dtype=jnp.int32)
    row = (pos // GRID_W).astype(F32)
    col = (pos % GRID_W).astype(F32)
    inv = ROPE_THETA ** (-jnp.arange(n_freq, dtype=F32) / n_freq)
    ang = jnp.concatenate([row[:, None] * inv, col[:, None] * inv], axis=-1)
    cos_t = jnp.tile(jnp.cos(ang), (1, 2 * LANES // hd))
    sin_t = jnp.tile(jnp.concatenate([-jnp.sin(ang), jnp.sin(ang)], axis=-1), (1, LANES // hd))
    grp = jnp.arange(n_qk) // hd
    g1 = (grp[:, None] == jnp.arange(LANES)[None, :]).astype(F32) / hd
    g2 = (jnp.arange(LANES)[:, None] == grp[None, :]).astype(F32)
    tm = TILE_PROJ
    tiles_per_seq = seq // tm
    extras = [gain_row, cos_t, sin_t, g1.astype(BF16), g2.astype(BF16)]
    extra_specs = [pl.BlockSpec((1, n_qk), lambda i, j: (0, 0)),
                   pl.BlockSpec((tm, LANES), lambda i, j: (i % tiles_per_seq, 0)),
                   pl.BlockSpec((tm, LANES), lambda i, j: (i % tiles_per_seq, 0)),
                   pl.BlockSpec((n_qk, LANES), lambda i, j: (0, 0)),
                   pl.BlockSpec((LANES, n_qk), lambda i, j: (0, 0))]
    q, k, v = _project(
        x, mod, 0, w, extras, extra_specs, functools.partial(_att_qkv_epilogue, n_qk=n_qk),
        [jax.ShapeDtypeStruct((n, n_q), BF16), jax.ShapeDtypeStruct((n, n_kv), BF16),
         jax.ShapeDtypeStruct((n, n_kv), BF16)],
        [pl.BlockSpec((tm, n_q), lambda i, j: (i, 0)), pl.BlockSpec((tm, n_kv), lambda i, j: (i, 0)),
         pl.BlockSpec((tm, n_kv), lambda i, j: (i, 0))],
        tm, w.shape[1], seq, "att_qkv")

    tq = TILE_ATT_Q
    n_chunks = seq // KEY_CHUNK
    q_tiles = seq // tq
    o = pl.pallas_call(
        functools.partial(_att_kernel, n_chunks=n_chunks),
        grid=(batch, ATT_KV_HEADS, q_tiles),
        in_specs=[pl.BlockSpec((tq, groups * hd), lambda b, g, i: (b * q_tiles + i, g)),
                  pl.BlockSpec((seq, LANES), lambda b, g, i: (b, g)),
                  pl.BlockSpec((seq, LANES), lambda b, g, i: (b, g))],
        out_specs=pl.BlockSpec((tq, groups * hd), lambda b, g, i: (b * q_tiles + i, g)),
        out_shape=jax.ShapeDtypeStruct((n, n_q), BF16),
        scratch_shapes=[pltpu.VMEM((n_chunks, groups * tq, KEY_CHUNK), F32),
                        pltpu.VMEM((groups * tq, LANES), F32), pltpu.VMEM((groups * tq, 2 * LANES), F32)],
        compiler_params=_cparams("parallel", "parallel", "arbitrary"),
        name="gqa_attention",
    )(q, k, v)
    return _out_project(o, 0, n_q, w_o.astype(BF16), x, mod, 0, seq, TILE_PROJ_NARROW, "att_out")


def _ret_proj_epilogue(acc, j, extras, outs):
    cos_ref, sin_ref = extras
    o_ref = outs[0]
    tn = acc.shape[1]
    base = j * tn
    if j < 2:
        cos = cos_ref[...]
        sin = sin_ref[...]
        half = RET_QK_DIM // 2
        for hh in range(RET_HEADS):
            lo = hh * RET_QK_DIM
            x1 = acc[:, lo:lo + half]
            x2 = acc[:, lo + half:lo + RET_QK_DIM]
            o_ref[:, base + lo:base + lo + half] = (x1 * cos - x2 * sin).astype(BF16)
            o_ref[:, base + lo + half:base + lo + RET_QK_DIM] = (x1 * sin + x2 * cos).astype(BF16)
    else:
        o_ref[:, base:base + tn] = acc.astype(BF16)


def _ret_kernel(logit_ref, q_ref, k_ref, v_ref, gain_ref, o_ref, acc_scr, state_f_scr, state_b_scr, *, n_chunks):
    h = pl.program_id(1)
    c_len = RET_CHUNK

    def log_gamma(direction, shape):
        z = jnp.full(shape, logit_ref[direction, h], F32)
        return jnp.minimum(z, 0.0) - jnp.log1p(jnp.exp(-jnp.abs(z)))

    ri = lax.broadcasted_iota(jnp.int32, (c_len, c_len), 0)
    ci = lax.broadcasted_iota(jnp.int32, (c_len, c_len), 1)
    dist = (ri - ci).astype(F32)
    lg_f = log_gamma(0, (c_len, c_len))
    lg_b = log_gamma(1, (c_len, c_len))
    decay = jnp.where(dist >= 0, jnp.exp(lg_f * jnp.maximum(dist, 0.0)), jnp.exp(lg_b * jnp.maximum(-dist, 0.0)))
    idx = lax.broadcasted_iota(jnp.int32, (c_len, 1), 0).astype(F32)
    lgf = log_gamma(0, (c_len, 1))
    lgb = log_gamma(1, (c_len, 1))
    xi_f = jnp.exp(lgf * (idx + 1.0))
    zeta_f = jnp.exp(lgf * (c_len - 1.0 - idx))
    cd_f = jnp.exp(lgf[0:1] * c_len)
    xi_b = jnp.exp(lgb * (c_len - idx))
    zeta_b = jnp.exp(lgb * idx)
    cd_b = jnp.exp(lgb[0:1] * c_len)

    def chunk(cidx):
        start = pl.multiple_of(cidx * c_len, c_len)
        return (q_ref[pl.ds(start, c_len), :], k_ref[pl.ds(start, c_len), :], v_ref[pl.ds(start, c_len), :], start)

    def update_state(state_scr, kc, vc, zeta, cd):
        zv = (vc.astype(F32) * zeta).astype(BF16)
        kv = lax.dot_general(kc, zv, (((0,), (0,)), ((), ())), preferred_element_type=F32)
        state_scr[...] = state_scr[...] * cd + kv

    def left_to_right(cidx):
        qc, kc, vc, start = chunk(cidx)
        scores = lax.dot_general(qc, kc, (((1,), (1,)), ((), ())), preferred_element_type=F32) * decay
        o = jnp.dot(scores.astype(BF16), vc, preferred_element_type=F32)
        o = o + xi_f * jnp.dot(qc, state_f_scr[...].astype(BF16), preferred_element_type=F32)
        update_state(state_f_scr, kc, vc, zeta_f, cd_f)
        return o, start

    def right_to_left(cidx):
        qc, kc, vc, start = chunk(cidx)
        o = xi_b * jnp.dot(qc, state_b_scr[...].astype(BF16), preferred_element_type=F32)
        update_state(state_b_scr, kc, vc, zeta_b, cd_b)
        return o, start

    def finish(y, start):
        y = y * lax.rsqrt(jnp.mean(y * y, axis=-1, keepdims=True) + EPS) * gain_ref[...]
        o_ref[pl.ds(start, c_len), :] = y.astype(BF16)

    state_f_scr[...] = jnp.zeros_like(state_f_scr)
    state_b_scr[...] = jnp.zeros_like(state_b_scr)
    half = n_chunks // 2

    def first_visits(t, carry):
        o_f, start_f = left_to_right(t)
        o_b, start_b = right_to_left(n_chunks - 1 - t)
        acc_scr[pl.ds(start_f, c_len), :] = o_f
        acc_scr[pl.ds(start_b, c_len), :] = o_b
        return carry

    def second_visits(t, carry):
        o_f, start_f = left_to_right(t)
        o_b, start_b = right_to_left(n_chunks - 1 - t)
        finish(acc_scr[pl.ds(start_f, c_len), :] + o_f, start_f)
        finish(acc_scr[pl.ds(start_b, c_len), :] + o_b, start_b)
        return carry

    lax.fori_loop(0, half, first_visits, 0, unroll=8)
    lax.fori_loop(half, n_chunks, second_visits, 0, unroll=4)


def _retention_mixer(x, mod, w_in, decay_logit, gn_gain, w_o, batch, seq):
    n, d = x.shape
    qk_cols = RET_HEADS * RET_QK_DIM
    v_cols = RET_HEADS * RET_V_DIM
    col_scale = jnp.concatenate([jnp.ones((qk_cols,), F32), jnp.full((qk_cols,), RET_QK_DIM ** -0.5, F32),
                                 jnp.ones((2 * v_cols,), F32)])
    w = (w_in * col_scale[None, :]).astype(BF16)
    pos = jnp.arange(seq, dtype=F32)
    inv = 1.0 / (RET_THETA ** jnp.linspace(0.0, 1.0, RET_QK_DIM // 2, dtype=F32))
    ang = pos[:, None] * inv
    tm, tn = TILE_PROJ, qk_cols
    tiles_per_seq = seq // tm
    half = RET_QK_DIM // 2
    extras = [jnp.cos(ang), jnp.sin(ang)]
    extra_specs = [pl.BlockSpec((tm, half), lambda i, j: (i % tiles_per_seq, 0)),
                   pl.BlockSpec((tm, half), lambda i, j: (i % tiles_per_seq, 0))]
    proj = _project(x, mod, 0, w, extras, extra_specs, _ret_proj_epilogue,
                    jax.ShapeDtypeStruct((n, w.shape[1]), BF16), pl.BlockSpec((tm, w.shape[1]), lambda i, j: (i, 0)),
                    tm, tn, seq, "ret_proj")

    n_chunks = seq // RET_CHUNK
    k_off = qk_cols // RET_QK_DIM
    v_off = 2 * qk_cols // RET_V_DIM
    y = pl.pallas_call(
        functools.partial(_ret_kernel, n_chunks=n_chunks),
        grid=(batch, RET_HEADS),
        in_specs=[pl.BlockSpec(memory_space=pltpu.SMEM),
                  pl.BlockSpec((seq, RET_QK_DIM), lambda b, h: (b, h)),
                  pl.BlockSpec((seq, RET_QK_DIM), lambda b, h: (b, k_off + h)),
                  pl.BlockSpec((seq, RET_V_DIM), lambda b, h: (b, v_off + h)),
                  pl.BlockSpec((1, RET_V_DIM), lambda b, h: (0, 0))],
        out_specs=pl.BlockSpec((seq, RET_V_DIM), lambda b, h: (b, h)),
        out_shape=jax.ShapeDtypeStruct((n, v_cols), BF16),
        scratch_shapes=[pltpu.VMEM((seq, RET_V_DIM), F32), pltpu.VMEM((RET_QK_DIM, RET_V_DIM), F32),
                        pltpu.VMEM((RET_QK_DIM, RET_V_DIM), F32)],
        compiler_params=_cparams("parallel", "arbitrary"),
        name="retention",
    )(decay_logit.astype(F32), proj, proj, proj, gn_gain.reshape(1, RET_V_DIM).astype(F32))
    gate_col = (2 * qk_cols + v_cols) // v_cols
    return _out_project(y, 0, v_cols, w_o.astype(BF16), x, mod, 0, seq, TILE_PROJ, "ret_out",
                        gate_src=proj, gate_col=gate_col)


def _conv_pw1_epilogue(acc, j, extras, outs):
    b_ref = extras[0]
    u = acc + b_ref[...]
    dm = u.shape[1] // 2
    outs[0][...] = (u[:, :dm] * jax.nn.sigmoid(u[:, dm:])).astype(BF16)


def _conv_kernel(prev_ref, cur_ref, next_ref, wdw_ref, bdw_ref, lng_ref, lnb_ref, w2_ref, b2_ref, x_ref, mod_ref,
                 o_ref, ext_scr, shift_scr, c_scr, *, tiles_per_seq, row_block):
    tm = cur_ref.shape[0]
    t = pl.program_id(0) % tiles_per_seq
    keep_prev = jnp.where(t == 0, 0.0, 1.0)
    keep_next = jnp.where(t == tiles_per_seq - 1, 0.0, 1.0)
    ext_scr[0:CONV_HALO, :] = prev_ref[...].astype(F32) * keep_prev
    ext_scr[CONV_HALO:CONV_HALO + tm, :] = cur_ref[...].astype(F32)
    ext_scr[CONV_HALO + tm:, :] = next_ref[...].astype(F32) * keep_next
    base = CONV_HALO - CONV_PAD
    span = tm + 2 * CONV_HALO - SUBLANES
    for r in range(1, SUBLANES):
        shift_scr[r - 1, 0:span, :] = ext_scr[r:r + span, :]
    for rb in range(tm // row_block):
        r0 = rb * row_block
        acc = jnp.zeros((row_block, cur_ref.shape[1]), F32) + bdw_ref[...]
        for j in range(CONV_WIDTH):
            phase = (base + j) % SUBLANES
            start = r0 + base + j - phase
            if phase == 0:
                tap = ext_scr[start:start + row_block, :]
            else:
                tap = shift_scr[phase - 1, start:start + row_block, :]
            acc = acc + jnp.tile(wdw_ref[j], (row_block // SUBLANES, 1)) * tap
        c_scr[r0:r0 + row_block, :] = acc
    u = c_scr[...]
    mu = jnp.mean(u, axis=-1, keepdims=True)
    var = jnp.mean(jnp.square(u - mu), axis=-1, keepdims=True)
    u = (u - mu) * lax.rsqrt(var + EPS) * lng_ref[...] + lnb_ref[...]
    u = _silu(u).astype(BF16)
    z = jnp.dot(u, w2_ref[...], preferred_element_type=F32) + b2_ref[...]
    o_ref[...] = x_ref[...] + mod_ref[0][2:3] * z


def _conv_mixer(x, mod, w_pw1, b_pw1, w_dw, b_dw, ln_gain, ln_bias, w_pw2, b_pw2, seq):
    n, d = x.shape
    tm = TILE_PROJ
    u = _project(x, mod, 0, w_pw1.astype(BF16), [b_pw1.reshape(1, 2 * d).astype(F32)],
                 [pl.BlockSpec((1, 2 * d), lambda i, j: (0, 0))], _conv_pw1_epilogue,
                 jax.ShapeDtypeStruct((n, d), BF16), pl.BlockSpec((tm, d), lambda i, j: (i, 0)),
                 tm, 2 * d, seq, "conv_pw1")
    tm = TILE_CONV
    tiles_per_seq = seq // tm
    halo_per_tile = tm // CONV_HALO
    n_halo = n // CONV_HALO
    wdw = jnp.broadcast_to(w_dw.reshape(CONV_WIDTH, 1, d).astype(F32), (CONV_WIDTH, SUBLANES, d))
    vec = lambda a: a.reshape(1, d).astype(F32)
    row_spec = pl.BlockSpec((1, d), lambda i: (0, 0))
    return pl.pallas_call(
        functools.partial(_conv_kernel, tiles_per_seq=tiles_per_seq, row_block=CONV_ROW_BLOCK),
        grid=(n // tm,),
        in_specs=[pl.BlockSpec((CONV_HALO, d), lambda i: (jnp.maximum(i * halo_per_tile - 1, 0), 0)),
                  pl.BlockSpec((tm, d), lambda i: (i, 0)),
                  pl.BlockSpec((CONV_HALO, d), lambda i: (jnp.minimum((i + 1) * halo_per_tile, n_halo - 1), 0)),
                  pl.BlockSpec((CONV_WIDTH, SUBLANES, d), lambda i: (0, 0, 0)),
                  row_spec, row_spec, row_spec,
                  pl.BlockSpec((d, d), lambda i: (0, 0)),
                  row_spec,
                  pl.BlockSpec((tm, d), lambda i: (i, 0)),
                  pl.BlockSpec((1, 6, d), lambda i: (i // tiles_per_seq, 0, 0))],
        out_specs=pl.BlockSpec((tm, d), lambda i: (i, 0)),
        out_shape=jax.ShapeDtypeStruct((n, d), F32),
        scratch_shapes=[pltpu.VMEM((tm + 2 * CONV_HALO, d), F32),
                        pltpu.VMEM((SUBLANES - 1, tm + 2 * CONV_HALO, d), F32), pltpu.VMEM((tm, d), F32)],
        compiler_params=_cparams("parallel"),
        name="conv_module",
    )(u, u, u, wdw, vec(b_dw), vec(ln_gain), vec(ln_bias), w_pw2.astype(BF16), vec(b_pw2), x, mod)


def _t5_bucket(rel):
    half = T5_BUCKETS // 2
    exact = half // 2
    n = jnp.abs(rel)
    log_ratio = jnp.log(jnp.maximum(n, 1).astype(F32) / exact) / math.log(T5_MAX_DIST / exact)
    large = jnp.minimum(exact + (log_ratio * (half - exact)).astype(jnp.int32), half - 1)
    return jnp.where(rel > 0, half, 0) + jnp.where(n < exact, n, large)


def _bias_tile_kernel(table_ref, idx_ref, o_ref):
    h = pl.program_id(0)
    idx = idx_ref[0]
    acc = jnp.zeros(idx.shape, F32)
    for b in range(T5_BUCKETS):
        acc = jnp.where(idx == b, table_ref[b, h], acc)
    o_ref[0, 0] = acc * LOG2_E


def _diff_kernel(q_ref, k_ref, v_ref, bias_ref, lam_ref, gain_ref, o_ref, s_scr, m_scr, acc_scr, *, n_chunks, out_scale,
                 lambda_init):
    tq = q_ref.shape[0]
    qi = pl.program_id(2)
    q = q_ref[...]
    lane = lax.broadcasted_iota(jnp.int32, (tq, LANES), 1)
    low = lane < DIFF_HEAD_DIM
    zero = jnp.zeros((tq, LANES), BF16)
    lhs = jnp.concatenate([jnp.where(low, q, zero), jnp.where(low, zero, q)], axis=0)

    q_sub = tq // KEY_CHUNK

    def add_bias(s, j):
        tiles = [bias_ref[0, jnp.clip(j - (q_sub * qi + a), -2, 2) + 2] for a in range(q_sub)]
        return s + jnp.concatenate(tiles + tiles, axis=0)

    o = _softmax_pv(lhs, k_ref, v_ref, s_scr, m_scr, acc_scr, n_chunks, add_bias)
    lam_v = lam_ref[...]
    lam = (jnp.exp(jnp.sum(lam_v[0:1] * lam_v[1:2], axis=-1, keepdims=True))
           - jnp.exp(jnp.sum(lam_v[2:3] * lam_v[3:4], axis=-1, keepdims=True)) + lambda_init)
    y = o[:tq] - lam * o[tq:]
    y = y * lax.rsqrt(jnp.mean(y * y, axis=-1, keepdims=True) + EPS) * gain_ref[...] * out_scale
    o_ref[...] = y.astype(BF16)


def _diff_mixer(x, mod, w_qkv, lam_q1, lam_k1, lam_q2, lam_k2, subln_gain, w_o, rel_bias, lambda_init,
                batch, seq):
    n, d = x.shape
    col_scale = jnp.concatenate([jnp.full((d,), DIFF_HEAD_DIM ** -0.5 * LOG2_E, F32), jnp.ones((2 * d,), F32)])
    w = (w_qkv * col_scale[None, :]).astype(BF16)
    tm = TILE_PROJ_NARROW
    qkv = _project(x, mod, 0, w, [], [], _plain_epilogue,
                   jax.ShapeDtypeStruct((n, 3 * d), BF16), pl.BlockSpec((tm, 3 * d), lambda i, j: (i, 0)),
                   tm, d, seq, "diff_qkv")

    tb = KEY_CHUNK
    ar = jnp.arange(KEY_CHUNK, dtype=jnp.int32)
    offs = (jnp.arange(5, dtype=jnp.int32) - 2) * KEY_CHUNK
    rel = (ar[None, None, :] + offs[:, None, None]) - ar[None, :, None]
    bucket = _t5_bucket(rel).astype(jnp.int32)
    bias = pl.pallas_call(
        _bias_tile_kernel,
        grid=(DIFF_HEADS, 5),
        in_specs=[pl.BlockSpec(memory_space=pltpu.SMEM),
                  pl.BlockSpec((1, tb, KEY_CHUNK), lambda h, t: (t, 0, 0))],
        out_specs=pl.BlockSpec((1, 1, tb, KEY_CHUNK), lambda h, t: (h, t, 0, 0)),
        out_shape=jax.ShapeDtypeStruct((DIFF_HEADS, 5, tb, KEY_CHUNK), F32),
        compiler_params=_cparams("arbitrary", "arbitrary"),
        name="rel_bias_tiles",
    )(rel_bias.astype(F32), bucket)

    lam = jnp.stack([lam_q1, lam_k1, lam_q2, lam_k2]).astype(F32)
    n_chunks = seq // KEY_CHUNK
    tq = TILE_DIFF_Q
    q_tiles = seq // tq
    o = pl.pallas_call(
        functools.partial(_diff_kernel, n_chunks=n_chunks, out_scale=1.0 - lambda_init, lambda_init=lambda_init),
        grid=(batch, DIFF_HEADS, q_tiles),
        in_specs=[pl.BlockSpec((tq, LANES), lambda b, h, i: (b * q_tiles + i, h)),
                  pl.BlockSpec((seq, LANES), lambda b, h, i: (b, DIFF_HEADS + h)),
                  pl.BlockSpec((seq, LANES), lambda b, h, i: (b, 2 * DIFF_HEADS + h)),
                  pl.BlockSpec((1, 5, tb, KEY_CHUNK), lambda b, h, i: (h, 0, 0, 0)),
                  pl.BlockSpec((4, DIFF_HEAD_DIM), lambda b, h, i: (0, 0)),
                  pl.BlockSpec((1, LANES), lambda b, h, i: (0, 0))],
        out_specs=pl.BlockSpec((tq, LANES), lambda b, h, i: (b * q_tiles + i, h)),
        out_shape=jax.ShapeDtypeStruct((n, d), BF16),
        scratch_shapes=[pltpu.VMEM((n_chunks, 2 * tq, KEY_CHUNK), F32),
                        pltpu.VMEM((2 * tq, LANES), F32), pltpu.VMEM((2 * tq, 2 * LANES), F32)],
        compiler_params=_cparams("parallel", "parallel", "arbitrary"),
        name="diff_attention",
    )(qkv, qkv, qkv, bias, lam, subln_gain.reshape(1, LANES).astype(F32))
    return _out_project(o, 0, d, w_o.astype(BF16), x, mod, 0, seq, TILE_PROJ_NARROW, "diff_out")


def _final_norm_kernel(x_ref, g_ref, o_ref):
    x = x_ref[...]
    o_ref[...] = x * lax.rsqrt(jnp.mean(x * x, axis=-1, keepdims=True) + EPS) * g_ref[...]


def _final_norm(x, gain, tm):
    n, d = x.shape
    return pl.pallas_call(
        _final_norm_kernel,
        grid=(n // tm,),
        in_specs=[pl.BlockSpec((tm, d), lambda i: (i, 0)), pl.BlockSpec((1, d), lambda i: (0, 0))],
        out_specs=pl.BlockSpec((tm, d), lambda i: (i, 0)),
        out_shape=jax.ShapeDtypeStruct((n, d), F32),
        compiler_params=_cparams("parallel"),
        name="final_norm",
    )(x, gain.reshape(1, d).astype(F32))


def kernel(x, c, mod_w, mod_b, att_w_qkv, att_q_gain, att_k_gain, att_w_o, ret_w_in, ret_decay_logit, ret_gn_gain, ret_w_o, cnv_w_pw1, cnv_b_pw1, cnv_w_dw, cnv_b_dw, cnv_ln_gain, cnv_ln_bias, cnv_w_pw2, cnv_b_pw2, dif_w_qkv, dif_lam_q1, dif_lam_k1, dif_lam_q2, dif_lam_k2, dif_subln_gain, dif_w_o, rel_bias, ffn_w_gate, ffn_w_up, ffn_w_down, moe_w_router, moe_b_router, moe_w_gate, moe_w_up, moe_w_down, final_gain):
    batch, seq, d = x.shape
    depth = mod_w.shape[0]
    xf = x.reshape(batch * seq, d)
    mods = _modulation(c, mod_w, mod_b)
    moe_gate, moe_up, moe_down = moe_w_gate.astype(BF16), moe_w_up.astype(BF16), moe_w_down.astype(BF16)
    for i in range(depth):
        mod = mods[i]
        kind, j = i % 4, i // 4
        if kind == 0:
            xf = _attention_mixer(xf, mod, att_w_qkv[j], att_q_gain[j], att_k_gain[j], att_w_o[j], batch, seq)
        elif kind == 1:
            xf = _retention_mixer(xf, mod, ret_w_in[j], ret_decay_logit[j], ret_gn_gain[j], ret_w_o[j], batch, seq)
        elif kind == 2:
            xf = _conv_mixer(xf, mod, cnv_w_pw1[j], cnv_b_pw1[j], cnv_w_dw[j], cnv_b_dw[j], cnv_ln_gain[j],
                             cnv_ln_bias[j], cnv_w_pw2[j], cnv_b_pw2[j], seq)
        else:
            lambda_init = 0.8 - 0.6 * math.exp(-0.3 * i)
            xf = _diff_mixer(xf, mod, dif_w_qkv[j], dif_lam_q1[j], dif_lam_k1[j], dif_lam_q2[j], dif_lam_k2[j],
                             dif_subln_gain[j], dif_w_o[j], rel_bias, lambda_init, batch, seq)
        if i % 2 == 0:
            m = i // 2
            xf = _ffn(xf, mod, ffn_w_gate[m].astype(BF16), ffn_w_up[m].astype(BF16), ffn_w_down[m].astype(BF16),
                      seq, TILE_PROJ)
        else:
            m = i // 2
            xf = _moe(xf, mod, moe_w_router[m], moe_b_router[m], moe_gate, moe_up, moe_down, m, seq,
                      out_gain=final_gain if i == depth - 1 else None)
    if depth % 2 == 1:
        xf = _final_norm(xf, final_gain, TILE_PROJ)
    return xf.reshape(batch, seq, d)
```

```python
import functools
import math

import jax
import jax.numpy as jnp
from jax import lax
from jax.experimental import pallas as pl
from jax.experimental.pallas import tpu as pltpu

F32 = jnp.float32
BF16 = jnp.bfloat16

EPS = 1e-6
GRID_W = 64

ATT_HEADS = 16
ATT_KV_HEADS = 4
ATT_HEAD_DIM = 64
ROPE_THETA = 10000.0

RET_HEADS = 4
RET_QK_DIM = 256
RET_V_DIM = 512
RET_THETA = 10000.0
RET_CHUNK = 256

CONV_WIDTH = 31
CONV_PAD = CONV_WIDTH // 2
CONV_HALO = 16

DIFF_HEADS = 8
DIFF_HEAD_DIM = 64
T5_BUCKETS = 32
T5_MAX_DIST = 128

N_EXPERTS = 8
LANES = 128
SUBLANES = 8
KEY_CHUNK = 256
LOG2_E = 1.4426950408889634
VMEM_LIMIT_BYTES = 56 * 1024 * 1024

TILE_PROJ = 512
TILE_PROJ_NARROW = 1024
TILE_ROUTE = 512
TILE_SLOT = 512
TILE_EXPERT_HIDDEN = 1792
TILE_CONV = 512
CONV_ROW_BLOCK = 32
TILE_ATT_Q = 512
TILE_DIFF_Q = 1024


def _cparams(*sem):
    return pltpu.CompilerParams(dimension_semantics=sem, vmem_limit_bytes=VMEM_LIMIT_BYTES)


def _silu(v):
    return v * jax.nn.sigmoid(v)


def _norm_mod(x, mod, which):
    shift = mod[3 * which:3 * which + 1]
    scale = mod[3 * which + 1:3 * which + 2]
    y = x * lax.rsqrt(jnp.mean(x * x, axis=-1, keepdims=True) + EPS)
    return y * (1.0 + scale) + shift


def _split_dot(a, b):
    hi = a.astype(BF16)
    lo = (a - hi.astype(F32)).astype(BF16)
    return (jnp.dot(hi, b, preferred_element_type=F32) + jnp.dot(lo, b, preferred_element_type=F32))


def _mod_kernel(c_ref, w_ref, b_ref, o_ref):
    cond = _silu(c_ref[...])
    o_ref[0] = jnp.dot(cond, w_ref[0], preferred_element_type=F32, precision=lax.Precision.HIGHEST) + b_ref[0]


def _modulation(c, mod_w, mod_b):
    depth, d, d6 = mod_w.shape
    b = c.shape[0]
    tn = d
    out = pl.pallas_call(
        _mod_kernel,
        grid=(depth, d6 // tn),
        in_specs=[pl.BlockSpec((b, d), lambda i, j: (0, 0)),
                  pl.BlockSpec((1, d, tn), lambda i, j: (i, 0, j)),
                  pl.BlockSpec((1, 1, tn), lambda i, j: (i, 0, j))],
        out_specs=pl.BlockSpec((1, b, tn), lambda i, j: (i, 0, j)),
        out_shape=jax.ShapeDtypeStruct((depth, b, d6), F32),
        compiler_params=_cparams("arbitrary", "arbitrary"),
        name="modulation",
    )(c, mod_w, mod_b.reshape(depth, 1, d6))
    return out.reshape(depth, b, 6, d)


def _proj_kernel(x_ref, mod_ref, w_ref, *rest, which, n_extra, epilogue, tn):
    extras = rest[:n_extra]
    outs = rest[n_extra:]
    h = _norm_mod(x_ref[...], mod_ref[0], which).astype(BF16)
    for j in range(w_ref.shape[1] // tn):
        acc = jnp.dot(h, w_ref[:, j * tn:(j + 1) * tn], preferred_element_type=F32)
        epilogue(acc, j, extras, outs)


def _project(x, mod, which, w, extras, extra_specs, epilogue, out_shapes, out_specs, tm, tn, seq, name):
    n, d = x.shape
    c = w.shape[1]
    tiles_per_seq = seq // tm
    kern = functools.partial(_proj_kernel, which=which, n_extra=len(extras), epilogue=epilogue, tn=tn)
    return pl.pallas_call(
        kern,
        grid=(n // tm, 1),
        in_specs=[pl.BlockSpec((tm, d), lambda i, j: (i, 0)),
                  pl.BlockSpec((1, 6, d), lambda i, j: (i // tiles_per_seq, 0, 0)),
                  pl.BlockSpec((d, c), lambda i, j: (0, 0), pipeline_mode=pl.Buffered(1))] + extra_specs,
        out_specs=out_specs,
        out_shape=out_shapes,
        compiler_params=_cparams("parallel", "arbitrary"),
        name=name,
    )(x, mod, w, *extras)


def _plain_epilogue(acc, j, extras, outs):
    tn = acc.shape[1]
    outs[0][:, j * tn:(j + 1) * tn] = acc.astype(outs[0].dtype)


def _oproj_kernel(*refs, which, gated, has_bias):
    refs = list(refs)
    y_ref = refs.pop(0)
    g_ref = refs.pop(0) if gated else None
    w_ref = refs.pop(0)
    b_ref = refs.pop(0) if has_bias else None
    x_ref, mod_ref, o_ref = refs
    y = y_ref[...]
    if gated:
        y = (y.astype(F32) * _silu(g_ref[...].astype(F32))).astype(BF16)
    z = jnp.dot(y, w_ref[...], preferred_element_type=F32)
    if has_bias:
        z = z + b_ref[...]
    gate = mod_ref[0][3 * which + 2:3 * which + 3]
    o_ref[...] = x_ref[...] + gate * z


def _out_project(y, y_col, k, w, x, mod, which, seq, tm, name, gate_src=None, gate_col=0, bias=None):
    n, d = x.shape
    tiles_per_seq = seq // tm
    args = [y]
    specs = [pl.BlockSpec((tm, k), lambda i: (i, y_col))]
    if gate_src is not None:
        args.append(gate_src)
        specs.append(pl.BlockSpec((tm, k), lambda i: (i, gate_col)))
    args.append(w)
    specs.append(pl.BlockSpec((k, d), lambda i: (0, 0)))
    if bias is not None:
        args.append(bias.reshape(1, d))
        specs.append(pl.BlockSpec((1, d), lambda i: (0, 0)))
    args += [x, mod]
    specs += [pl.BlockSpec((tm, d), lambda i: (i, 0)),
              pl.BlockSpec((1, 6, d), lambda i: (i // tiles_per_seq, 0, 0))]
    kern = functools.partial(_oproj_kernel, which=which, gated=gate_src is not None, has_bias=bias is not None)
    return pl.pallas_call(
        kern,
        grid=(n // tm,),
        in_specs=specs,
        out_specs=pl.BlockSpec((tm, d), lambda i: (i, 0)),
        out_shape=jax.ShapeDtypeStruct((n, d), F32),
        compiler_params=_cparams("parallel"),
        name=name,
    )(*args)


def _ffn_kernel(x_ref, mod_ref, wg_ref, wu_ref, wd_ref, o_ref):
    x = x_ref[...]
    mod = mod_ref[0]
    h = _norm_mod(x, mod, 1).astype(BF16)
    g = jnp.dot(h, wg_ref[...], preferred_element_type=F32)
    u = jnp.dot(h, wu_ref[...], preferred_element_type=F32)
    a = (_silu(g) * u).astype(BF16)
    o_ref[...] = x + mod[5:6] * jnp.dot(a, wd_ref[...], preferred_element_type=F32)


def _ffn(x, mod, wg, wu, wd, seq, tm):
    n, d = x.shape
    ff = wg.shape[1]
    tiles_per_seq = seq // tm
    resident = pl.Buffered(1)
    return pl.pallas_call(
        _ffn_kernel,
        grid=(n // tm,),
        in_specs=[pl.BlockSpec((tm, d), lambda i: (i, 0)),
                  pl.BlockSpec((1, 6, d), lambda i: (i // tiles_per_seq, 0, 0)),
                  pl.BlockSpec((d, ff), lambda i: (0, 0), pipeline_mode=resident),
                  pl.BlockSpec((d, ff), lambda i: (0, 0), pipeline_mode=resident),
                  pl.BlockSpec((ff, d), lambda i: (0, 0), pipeline_mode=resident)],
        out_specs=pl.BlockSpec((tm, d), lambda i: (i, 0)),
        out_shape=jax.ShapeDtypeStruct((n, d), F32),
        compiler_params=_cparams("parallel"),
        name="dense_ffn",
    )(x, mod, wg, wu, wd)


def _router_kernel(x_ref, mod_ref, wr_ref, br_ref, tri_ref, route_ref, route_t_ref, count_ref, carry_scr):
    @pl.when(pl.program_id(0) == 0)
    def _():
        carry_scr[...] = jnp.zeros_like(carry_scr)

    h = _norm_mod(x_ref[...], mod_ref[0], 1)
    h_hi = h.astype(BF16)
    h_lo = (h - h_hi.astype(F32)).astype(BF16)
    logits = (jnp.dot(h_hi, wr_ref[0], preferred_element_type=F32)
              + (jnp.dot(h_hi, wr_ref[1], preferred_element_type=F32)
                 + jnp.dot(h_lo, wr_ref[0], preferred_element_type=F32))) + br_ref[...]
    lane = lax.broadcasted_iota(jnp.int32, logits.shape, 1)
    neg = jnp.float32(-jnp.inf)
    logits = jnp.where(lane < N_EXPERTS, logits, neg)
    m1 = jnp.max(logits, axis=-1, keepdims=True)
    i1 = jnp.min(jnp.where(logits == m1, lane, LANES), axis=-1, keepdims=True)
    rest = jnp.where(lane == i1, neg, logits)
    m2 = jnp.max(rest, axis=-1, keepdims=True)
    i2 = jnp.min(jnp.where(rest == m2, lane, LANES), axis=-1, keepdims=True)
    e2 = jnp.exp(m2 - m1)
    w1 = 1.0 / (1.0 + e2)
    w2 = e2 / (1.0 + e2)
    sel = jnp.where((lane == i1) | (lane == i2), 1.0, 0.0)
    incl = jnp.dot(tri_ref[...], sel.astype(BF16), preferred_element_type=F32)
    excl = incl - sel + carry_scr[...]
    r1 = jnp.sum(jnp.where(lane == i1, excl, 0.0), axis=-1, keepdims=True)
    r2 = jnp.sum(jnp.where(lane == i2, excl, 0.0), axis=-1, keepdims=True)
    carry_scr[...] += incl[incl.shape[0] - 1:, :]
    count_ref[...] = carry_scr[...]
    fields = (i1.astype(F32), i2.astype(F32), w1, w2, r1, r2)
    route = jnp.zeros(logits.shape, F32)
    for k, field in enumerate(fields):
        route = jnp.where(lane == k, field, route)
    route_ref[...] = route
    route_t_ref[...] = route.T[:SUBLANES]


def _router(x, mod, w_router, b_router, seq, tm):
    n, d = x.shape
    tiles_per_seq = seq // tm
    wr = jnp.pad(w_router, ((0, 0), (0, LANES - N_EXPERTS))).astype(F32)
    wr_hi = wr.astype(BF16)
    wr = jnp.stack([wr_hi, (wr - wr_hi.astype(F32)).astype(BF16)])
    br = jnp.pad(b_router, (0, LANES - N_EXPERTS)).reshape(1, LANES)
    tri = (jnp.arange(tm)[:, None] >= jnp.arange(tm)[None, :]).astype(BF16)
    return pl.pallas_call(
        _router_kernel,
        grid=(n // tm,),
        in_specs=[pl.BlockSpec((tm, d), lambda i: (i, 0)),
                  pl.BlockSpec((1, 6, d), lambda i: (i // tiles_per_seq, 0, 0)),
                  pl.BlockSpec((2, d, LANES), lambda i: (0, 0, 0)),
                  pl.BlockSpec((1, LANES), lambda i: (0, 0)),
                  pl.BlockSpec((tm, tm), lambda i: (0, 0))],
        out_specs=[pl.BlockSpec((tm, LANES), lambda i: (i, 0)), pl.BlockSpec((SUBLANES, tm), lambda i: (0, i)),
                   pl.BlockSpec((1, LANES), lambda i: (0, 0))],
        out_shape=[jax.ShapeDtypeStruct((n, LANES), F32), jax.ShapeDtypeStruct((SUBLANES, n), F32),
                   jax.ShapeDtypeStruct((1, LANES), F32)],
        scratch_shapes=[pltpu.VMEM((1, LANES), F32)],
        compiler_params=_cparams("arbitrary"),
        name="moe_router",
    )(x, mod, wr, br, tri)


def _dispatch_kernel(zero_rows_ref, d1_ref, d2_ref, x_ref, mod_ref, xs_ref, h_scr, zero_scr, sems):
    tm = x_ref.shape[0]

    @pl.when(pl.program_id(0) == 0)
    def _():
        zero_scr[...] = jnp.zeros_like(zero_scr)
        te = zero_scr.shape[0]
        for k in range(zero_rows_ref.shape[0]):
            start = pl.multiple_of(zero_rows_ref[k], te)
            copy = pltpu.make_async_copy(zero_scr, xs_ref.at[pl.ds(start, te), :], sems.at[2])
            copy.start()
            copy.wait()

    h_scr[...] = _norm_mod(x_ref[...], mod_ref[0], 1)

    def row_copy(r, dest, which):
        return pltpu.make_async_copy(h_scr.at[pl.ds(r, 1), :], xs_ref.at[pl.ds(dest, 1), :], sems.at[which])

    def issue(r, carry):
        row_copy(r, d1_ref[0, 0, r], 0).start()
        row_copy(r, d2_ref[0, 0, r], 1).start()
        return carry

    lax.fori_loop(0, tm, issue, 0, unroll=4)
    for which in range(2):
        pltpu.make_async_copy(h_scr, xs_ref.at[pl.ds(0, tm), :], sems.at[which]).wait()


def _expert_kernel(tile_e_ref, n_act_ref, x_ref, wg_ref, wu_ref, wd_ref, o_ref, *, tf):
    del tile_e_ref
    s = pl.program_id(0)

    @pl.when(s < n_act_ref[0])
    def _():
        h = x_ref[...].astype(BF16)
        out = None
        for f in range(wg_ref.shape[3] // tf):
            cols = slice(f * tf, (f + 1) * tf)
            g = jnp.dot(h, wg_ref[0, 0, :, cols], preferred_element_type=F32)
            u = jnp.dot(h, wu_ref[0, 0, :, cols], preferred_element_type=F32)
            a = (_silu(g) * u).astype(BF16)
            part = jnp.dot(a, wd_ref[0, 0, cols, :], preferred_element_type=F32)
            out = part if out is None else out + part
        o_ref[...] = out

    @pl.when(s >= n_act_ref[0])
    def _():
        o_ref[...] = jnp.zeros_like(o_ref)


def _combine_kernel(d1_ref, d2_ref, route_ref, x_ref, mod_ref, *rest, out_norm):
    if out_norm:
        gain_ref, ys_ref, o_ref, buf, sems = rest
    else:
        ys_ref, o_ref, buf, sems = rest
    tm = x_ref.shape[0]

    def row_copy(r, src, which):
        return pltpu.make_async_copy(ys_ref.at[pl.ds(src, 1), :], buf.at[which, pl.ds(r, 1), :], sems.at[which])

    def issue(r, carry):
        row_copy(r, d1_ref[0, 0, r], 0).start()
        row_copy(r, d2_ref[0, 0, r], 1).start()
        return carry

    lax.fori_loop(0, tm, issue, 0, unroll=4)
    for which in range(2):
        pltpu.make_async_copy(ys_ref.at[pl.ds(0, tm), :], buf.at[which], sems.at[which]).wait()
    route = route_ref[...]
    y = route[:, 2:3] * buf[0] + route[:, 3:4] * buf[1]
    out = x_ref[...] + mod_ref[0][5:6] * y
    if out_norm:
        out = out * lax.rsqrt(jnp.mean(out * out, axis=-1, keepdims=True) + EPS) * gain_ref[...]
    o_ref[...] = out


def _moe(x, mod, w_router, b_router, w_gate, w_up, w_down, layer, seq, out_gain=None):
    n, d = x.shape
    _, n_e, _, ff = w_gate.shape
    tm, te, tf = TILE_ROUTE, TILE_SLOT, TILE_EXPERT_HIDDEN
    tiles_per_seq = seq // tm
    n_tok_tiles = n // tm
    n_slots = 2 * n + n_e * te
    n_tiles = n_slots // te

    route, route_t, counts = _router(x, mod, w_router, b_router, seq, tm)

    counts = counts[0, :n_e].astype(jnp.int32)
    padded = ((counts + te - 1) // te) * te
    ends = jnp.cumsum(padded)
    offsets = ends - padded
    n_act = (ends[-1] // te).astype(jnp.int32)
    tile_start = jnp.arange(n_tiles, dtype=jnp.int32) * te
    tile_e = jnp.sum((tile_start[:, None] >= ends[None, :]).astype(jnp.int32), axis=1)
    tile_e = jnp.minimum(tile_e, tile_e[jnp.maximum(n_act - 1, 0)]).astype(jnp.int32)
    e1 = route_t[0].astype(jnp.int32)
    e2 = route_t[1].astype(jnp.int32)
    dest1 = (offsets[e1] + route_t[4].astype(jnp.int32)).reshape(n_tok_tiles, 1, tm)
    dest2 = (offsets[e2] + route_t[5].astype(jnp.int32)).reshape(n_tok_tiles, 1, tm)
    n_act = n_act.reshape(1)

    slot_spec = pl.BlockSpec((1, 1, tm), lambda i: (i, 0, 0), memory_space=pltpu.SMEM)
    zero_rows = jnp.concatenate([jnp.maximum(ends - te, 0),
                                 (n_tiles - n_e + jnp.arange(n_e, dtype=jnp.int32)) * te]).astype(jnp.int32)
    xs = pl.pallas_call(
        _dispatch_kernel,
        grid=(n_tok_tiles,),
        in_specs=[pl.BlockSpec(memory_space=pltpu.SMEM), slot_spec, slot_spec,
                  pl.BlockSpec((tm, d), lambda i: (i, 0)),
                  pl.BlockSpec((1, 6, d), lambda i: (i // tiles_per_seq, 0, 0))],
        out_specs=pl.BlockSpec(memory_space=pl.ANY),
        out_shape=jax.ShapeDtypeStruct((n_slots, d), F32),
        scratch_shapes=[pltpu.VMEM((tm, d), F32), pltpu.VMEM((te, d), F32), pltpu.SemaphoreType.DMA((3,))],
        compiler_params=_cparams("arbitrary"),
        name="moe_dispatch",
    )(zero_rows, dest1, dest2, x, mod)

    def live(s, n_act_ref):
        return jnp.minimum(s, jnp.maximum(n_act_ref[0] - 1, 0))

    def expert_weights(shape):
        return pl.BlockSpec(shape, lambda s, te_ref, na_ref: (layer, te_ref[s], 0, 0), pipeline_mode=pl.Buffered(1))

    ys = pl.pallas_call(
        functools.partial(_expert_kernel, tf=tf),
        grid_spec=pltpu.PrefetchScalarGridSpec(
            num_scalar_prefetch=2,
            grid=(n_tiles,),
            in_specs=[pl.BlockSpec((te, d), lambda s, te_ref, na_ref: (live(s, na_ref), 0)),
                      expert_weights((1, 1, d, ff)), expert_weights((1, 1, d, ff)), expert_weights((1, 1, ff, d))],
            out_specs=pl.BlockSpec((te, d), lambda s, te_ref, na_ref: (s, 0))),
        out_shape=jax.ShapeDtypeStruct((n_slots, d), F32),
        compiler_params=_cparams("arbitrary"),
        name="moe_experts",
    )(tile_e, n_act, xs, w_gate, w_up, w_down)

    out_norm = out_gain is not None
    gain_args = [out_gain.reshape(1, d).astype(F32)] if out_norm else []
    gain_specs = [pl.BlockSpec((1, d), lambda i: (0, 0))] if out_norm else []
    return pl.pallas_call(
        functools.partial(_combine_kernel, out_norm=out_norm),
        grid=(n_tok_tiles,),
        in_specs=[slot_spec, slot_spec,
                  pl.BlockSpec((tm, LANES), lambda i: (i, 0)),
                  pl.BlockSpec((tm, d), lambda i: (i, 0)),
                  pl.BlockSpec((1, 6, d), lambda i: (i // tiles_per_seq, 0, 0))] + gain_specs
                 + [pl.BlockSpec(memory_space=pl.ANY)],
        out_specs=pl.BlockSpec((tm, d), lambda i: (i, 0)),
        out_shape=jax.ShapeDtypeStruct((n, d), F32),
        scratch_shapes=[pltpu.VMEM((2, tm, d), F32), pltpu.SemaphoreType.DMA((2,))],
        compiler_params=_cparams("arbitrary"),
        name="moe_combine",
    )(dest1, dest2, route, x, mod, *gain_args, ys)


def _softmax_pv(lhs, k_ref, v_ref, s_scr, m_scr, acc_scr, n_chunks, bias_fn=None):
    rows = lhs.shape[0]
    group = min(8, n_chunks)
    m_scr[...] = jnp.full(m_scr.shape, -jnp.inf, F32)

    for j in range(n_chunks):
        kc = k_ref[j * KEY_CHUNK:(j + 1) * KEY_CHUNK, :]
        s = lax.dot_general(lhs, kc, (((1,), (1,)), ((), ())), preferred_element_type=F32)
        if bias_fn is not None:
            s = bias_fn(s, j)
        s_scr[j] = s
        m_scr[...] = jnp.maximum(m_scr[...], jnp.maximum(s[:, :LANES], s[:, LANES:]))
    m_scr[...] = jnp.broadcast_to(jnp.max(m_scr[...], axis=-1, keepdims=True), m_scr.shape)
    acc_scr[...] = jnp.zeros_like(acc_scr)
    ones = jnp.ones((group * KEY_CHUNK, LANES), BF16)

    def pv_body(t, carry):
        mb = m_scr[...]
        parts = []
        for u in range(group):
            s = s_scr[group * t + u]
            parts.append(jnp.exp2(s[:, :LANES] - mb).astype(BF16))
            parts.append(jnp.exp2(s[:, LANES:] - mb).astype(BF16))
        p = jnp.concatenate(parts, axis=1)
        start = pl.multiple_of(t * (group * KEY_CHUNK), group * KEY_CHUNK)
        v_aug = jnp.concatenate([v_ref[pl.ds(start, group * KEY_CHUNK), :], ones], axis=1)
        acc_scr[...] += jnp.dot(p, v_aug, preferred_element_type=F32)
        return carry

    lax.fori_loop(0, n_chunks // group, pv_body, 0)
    acc = acc_scr[...]
    return acc[:, :LANES] / acc[:, LANES:]


def _att_qkv_epilogue(acc, j, extras, outs, n_qk):
    gain_ref, cos_ref, sin_ref, g1_ref, g2_ref = extras
    q_ref, k_ref, v_ref = outs
    qk = acc[:, :n_qk]
    ms = _split_dot(qk * qk, g1_ref[...])
    r = lax.rsqrt(ms + EPS)
    y = qk * _split_dot(r, g2_ref[...]) * gain_ref[...]
    cos = cos_ref[...]
    sin = sin_ref[...]
    lane = lax.broadcasted_iota(jnp.int32, cos.shape, 1)
    first_half = (lane & (ATT_HEAD_DIM // 2)) == 0
    nq = q_ref.shape[1]
    for c in range(n_qk // LANES):
        yc = y[:, c * LANES:(c + 1) * LANES]
        partner = jnp.where(first_half, pltpu.roll(yc, LANES - ATT_HEAD_DIM // 2, 1),
                            pltpu.roll(yc, ATT_HEAD_DIM // 2, 1))
        out = (yc * cos + partner * sin).astype(BF16)
        if c * LANES < nq:
            q_ref[:, c * LANES:(c + 1) * LANES] = out
        else:
            k_ref[:, c * LANES - nq:(c + 1) * LANES - nq] = out
    v_ref[...] = acc[:, n_qk:].astype(BF16)


def _att_kernel(q_ref, k_ref, v_ref, o_ref, s_scr, m_scr, acc_scr, *, n_chunks):
    tq = q_ref.shape[0]
    q = q_ref[...]
    lane = lax.broadcasted_iota(jnp.int32, (tq, LANES), 1)
    low = lane < ATT_HEAD_DIM
    zero = jnp.zeros((tq, LANES), BF16)
    parts = []
    for pair in range(2):
        qp = q[:, pair * LANES:(pair + 1) * LANES]
        parts.append(jnp.where(low, qp, zero))
        parts.append(jnp.where(low, zero, qp))
    lhs = jnp.concatenate(parts, axis=0)
    o = _softmax_pv(lhs, k_ref, v_ref, s_scr, m_scr, acc_scr, n_chunks)
    for pair in range(2):
        o_ref[:, pair * LANES:(pair + 1) * LANES] = jnp.where(
            low, o[(2 * pair) * tq:(2 * pair + 1) * tq], o[(2 * pair + 1) * tq:(2 * pair + 2) * tq]).astype(BF16)


def _attention_mixer(x, mod, w_qkv, q_gain, k_gain, w_o, batch, seq):
    n, d = x.shape
    hd = ATT_HEAD_DIM
    half = hd // 2
    n_freq = half // 2
    groups = ATT_HEADS // ATT_KV_HEADS
    perm = jnp.concatenate([jnp.arange(n_freq), half + jnp.arange(n_freq),
                            n_freq + jnp.arange(n_freq), half + n_freq + jnp.arange(n_freq)])
    q_cols = (jnp.arange(ATT_HEADS)[:, None] * hd + perm[None, :]).reshape(-1)
    k_base = ATT_HEADS * hd
    v_base = k_base + ATT_KV_HEADS * hd
    kv_rep = jnp.repeat(jnp.arange(ATT_KV_HEADS), 2)
    k_cols = (k_base + kv_rep[:, None] * hd + perm[None, :]).reshape(-1)
    v_cols = (v_base + kv_rep[:, None] * hd + jnp.arange(hd)[None, :]).reshape(-1)
    w = w_qkv[:, jnp.concatenate([q_cols, k_cols, v_cols])].astype(BF16)
    n_q = ATT_HEADS * hd
    n_kv = 2 * ATT_KV_HEADS * hd
    n_qk = n_q + n_kv
    gain_row = jnp.concatenate([jnp.tile(q_gain[perm] * (hd ** -0.5 * LOG2_E), ATT_HEADS),
                                jnp.tile(k_gain[perm], 2 * ATT_KV_HEADS)]).reshape(1, n_qk).astype(F32)
    pos = jnp.arange(seq, dtype=jnp.int32)
    row = (pos // GRID_W).astype(F32)
    col = (pos % GRID_W).astype(F32)
    inv = ROPE_THETA ** (-jnp.arange(n_freq, dtype=F32) / n_freq)
    ang = jnp.concatenate([row[:, None] * inv, col[:, None] * inv], axis=-1)
    cos_t = jnp.tile(jnp.cos(ang), (1, 2 * LANES // hd))
    sin_t = jnp.tile(jnp.concatenate([-jnp.sin(ang), jnp.sin(ang)], axis=-1), (1, LANES // hd))
    grp = jnp.arange(n_qk) // hd
    g1 = (grp[:, None] == jnp.arange(LANES)[None, :]).astype(F32) / hd
    g2 = (jnp.arange(LANES)[:, None] == grp[None, :]).astype(F32)
    tm = TILE_PROJ
    tiles_per_seq = seq // tm
    extras = [gain_row, cos_t, sin_t, g1.astype(BF16), g2.astype(BF16)]
    extra_specs = [pl.BlockSpec((1, n_qk), lambda i, j: (0, 0)),
                   pl.BlockSpec((tm, LANES), lambda i, j: (i % tiles_per_seq, 0)),
                   pl.BlockSpec((tm, LANES), lambda i, j: (i % tiles_per_seq, 0)),
                   pl.BlockSpec((n_qk, LANES), lambda i, j: (0, 0)),
                   pl.BlockSpec((LANES, n_qk), lambda i, j: (0, 0))]
    q, k, v = _project(
        x, mod, 0, w, extras, extra_specs, functools.partial(_att_qkv_epilogue, n_qk=n_qk),
        [jax.ShapeDtypeStruct((n, n_q), BF16), jax.ShapeDtypeStruct((n, n_kv), BF16),
         jax.ShapeDtypeStruct((n, n_kv), BF16)],
        [pl.BlockSpec((tm, n_q), lambda i, j: (i, 0)), pl.BlockSpec((tm, n_kv), lambda i, j: (i, 0)),
         pl.BlockSpec((tm, n_kv), lambda i, j: (i, 0))],
        tm, w.shape[1], seq, "att_qkv")

    tq = TILE_ATT_Q
    n_chunks = seq // KEY_CHUNK
    q_tiles = seq // tq
    o = pl.pallas_call(
        functools.partial(_att_kernel, n_chunks=n_chunks),
        grid=(batch, ATT_KV_HEADS, q_tiles),
        in_specs=[pl.BlockSpec((tq, groups * hd), lambda b, g, i: (b * q_tiles + i, g)),
                  pl.BlockSpec((seq, LANES), lambda b, g, i: (b, g)),
                  pl.BlockSpec((seq, LANES), lambda b, g, i: (b, g))],
        out_specs=pl.BlockSpec((tq, groups * hd), lambda b, g, i: (b * q_tiles + i, g)),
        out_shape=jax.ShapeDtypeStruct((n, n_q), BF16),
        scratch_shapes=[pltpu.VMEM((n_chunks, groups * tq, KEY_CHUNK), F32),
                        pltpu.VMEM((groups * tq, LANES), F32), pltpu.VMEM((groups * tq, 2 * LANES), F32)],
        compiler_params=_cparams("parallel", "parallel", "arbitrary"),
        name="gqa_attention",
    )(q, k, v)
    return _out_project(o, 0, n_q, w_o.astype(BF16), x, mod, 0, seq, TILE_PROJ_NARROW, "att_out")


def _ret_proj_epilogue(acc, j, extras, outs):
    cos_ref, sin_ref = extras
    o_ref = outs[0]
    tn = acc.shape[1]
    base = j * tn
    if j < 2:
        cos = cos_ref[...]
        sin = sin_ref[...]
        half = RET_QK_DIM // 2
        for hh in range(RET_HEADS):
            lo = hh * RET_QK_DIM
            x1 = acc[:, lo:lo + half]
            x2 = acc[:, lo + half:lo + RET_QK_DIM]
            o_ref[:, base + lo:base + lo + half] = (x1 * cos - x2 * sin).astype(BF16)
            o_ref[:, base + lo + half:base + lo + RET_QK_DIM] = (x1 * sin + x2 * cos).astype(BF16)
    else:
        o_ref[:, base:base + tn] = acc.astype(BF16)


def _ret_kernel(logit_ref, q_ref, k_ref, v_ref, gain_ref, o_ref, acc_scr, state_f_scr, state_b_scr, *, n_chunks):
    h = pl.program_id(1)
    c_len = RET_CHUNK

    def log_gamma(direction, shape):
        z = jnp.full(shape, logit_ref[direction, h], F32)
        return jnp.minimum(z, 0.0) - jnp.log1p(jnp.exp(-jnp.abs(z)))

    ri = lax.broadcasted_iota(jnp.int32, (c_len, c_len), 0)
    ci = lax.broadcasted_iota(jnp.int32, (c_len, c_len), 1)
    dist = (ri - ci).astype(F32)
    lg_f = log_gamma(0, (c_len, c_len))
    lg_b = log_gamma(1, (c_len, c_len))
    decay = jnp.where(dist >= 0, jnp.exp(lg_f * jnp.maximum(dist, 0.0)), jnp.exp(lg_b * jnp.maximum(-dist, 0.0)))
    idx = lax.broadcasted_iota(jnp.int32, (c_len, 1), 0).astype(F32)
    lgf = log_gamma(0, (c_len, 1))
    lgb = log_gamma(1, (c_len, 1))
    xi_f = jnp.exp(lgf * (idx + 1.0))
    zeta_f = jnp.exp(lgf * (c_len - 1.0 - idx))
    cd_f = jnp.exp(lgf[0:1] * c_len)
    xi_b = jnp.exp(lgb * (c_len - idx))
    zeta_b = jnp.exp(lgb * idx)
    cd_b = jnp.exp(lgb[0:1] * c_len)

    def chunk(cidx):
        start = pl.multiple_of(cidx * c_len, c_len)
        return (q_ref[pl.ds(start, c_len), :], k_ref[pl.ds(start, c_len), :], v_ref[pl.ds(start, c_len), :], start)

    def update_state(state_scr, kc, vc, zeta, cd):
        zv = (vc.astype(F32) * zeta).astype(BF16)
        kv = lax.dot_general(kc, zv, (((0,), (0,)), ((), ())), preferred_element_type=F32)
        state_scr[...] = state_scr[...] * cd + kv

    def left_to_right(cidx):
        qc, kc, vc, start = chunk(cidx)
        scores = lax.dot_general(qc, kc, (((1,), (1,)), ((), ())), preferred_element_type=F32) * decay
        o = jnp.dot(scores.astype(BF16), vc, preferred_element_type=F32)
        o = o + xi_f * jnp.dot(qc, state_f_scr[...].astype(BF16), preferred_element_type=F32)
        update_state(state_f_scr, kc, vc, zeta_f, cd_f)
        return o, start

    def right_to_left(cidx):
        qc, kc, vc, start = chunk(cidx)
        o = xi_b * jnp.dot(qc, state_b_scr[...].astype(BF16), preferred_element_type=F32)
        update_state(state_b_scr, kc, vc, zeta_b, cd_b)
        return o, start

    def finish(y, start):
        y = y * lax.rsqrt(jnp.mean(y * y, axis=-1, keepdims=True) + EPS) * gain_ref[...]
        o_ref[pl.ds(start, c_len), :] = y.astype(BF16)

    state_f_scr[...] = jnp.zeros_like(state_f_scr)
    state_b_scr[...] = jnp.zeros_like(state_b_scr)
    half = n_chunks // 2

    def first_visits(t, carry):
        o_f, start_f = left_to_right(t)
        o_b, start_b = right_to_left(n_chunks - 1 - t)
        acc_scr[pl.ds(start_f, c_len), :] = o_f
        acc_scr[pl.ds(start_b, c_len), :] = o_b
        return carry

    def second_visits(t, carry):
        o_f, start_f = left_to_right(t)
        o_b, start_b = right_to_left(n_chunks - 1 - t)
        finish(acc_scr[pl.ds(start_f, c_len), :] + o_f, start_f)
        finish(acc_scr[pl.ds(start_b, c_len), :] + o_b, start_b)
        return carry

    lax.fori_loop(0, half, first_visits, 0, unroll=8)
    lax.fori_loop(half, n_chunks, second_visits, 0, unroll=4)


def _retention_mixer(x, mod, w_in, decay_logit, gn_gain, w_o, batch, seq):
    n, d = x.shape
    qk_cols = RET_HEADS * RET_QK_DIM
    v_cols = RET_HEADS * RET_V_DIM
    col_scale = jnp.concatenate([jnp.ones((qk_cols,), F32), jnp.full((qk_cols,), RET_QK_DIM ** -0.5, F32),
                                 jnp.ones((2 * v_cols,), F32)])
    w = (w_in * col_scale[None, :]).astype(BF16)
    pos = jnp.arange(seq, dtype=F32)
    inv = 1.0 / (RET_THETA ** jnp.linspace(0.0, 1.0, RET_QK_DIM // 2, dtype=F32))
    ang = pos[:, None] * inv
    tm, tn = TILE_PROJ, qk_cols
    tiles_per_seq = seq // tm
    half = RET_QK_DIM // 2
    extras = [jnp.cos(ang), jnp.sin(ang)]
    extra_specs = [pl.BlockSpec((tm, half), lambda i, j: (i % tiles_per_seq, 0)),
                   pl.BlockSpec((tm, half), lambda i, j: (i % tiles_per_seq, 0))]
    proj = _project(x, mod, 0, w, extras, extra_specs, _ret_proj_epilogue,
                    jax.ShapeDtypeStruct((n, w.shape[1]), BF16), pl.BlockSpec((tm, w.shape[1]), lambda i, j: (i, 0)),
                    tm, tn, seq, "ret_proj")

    n_chunks = seq // RET_CHUNK
    k_off = qk_cols // RET_QK_DIM
    v_off = 2 * qk_cols // RET_V_DIM
    y = pl.pallas_call(
        functools.partial(_ret_kernel, n_chunks=n_chunks),
        grid=(batch, RET_HEADS),
        in_specs=[pl.BlockSpec(memory_space=pltpu.SMEM),
                  pl.BlockSpec((seq, RET_QK_DIM), lambda b, h: (b, h)),
                  pl.BlockSpec((seq, RET_QK_DIM), lambda b, h: (b, k_off + h)),
                  pl.BlockSpec((seq, RET_V_DIM), lambda b, h: (b, v_off + h)),
                  pl.BlockSpec((1, RET_V_DIM), lambda b, h: (0, 0))],
        out_specs=pl.BlockSpec((seq, RET_V_DIM), lambda b, h: (b, h)),
        out_shape=jax.ShapeDtypeStruct((n, v_cols), BF16),
        scratch_shapes=[pltpu.VMEM((seq, RET_V_DIM), F32), pltpu.VMEM((RET_QK_DIM, RET_V_DIM), F32),
                        pltpu.VMEM((RET_QK_DIM, RET_V_DIM), F32)],
        compiler_params=_cparams("parallel", "arbitrary"),
        name="retention",
    )(decay_logit.astype(F32), proj, proj, proj, gn_gain.reshape(1, RET_V_DIM).astype(F32))
    gate_col = (2 * qk_cols + v_cols) // v_cols
    return _out_project(y, 0, v_cols, w_o.astype(BF16), x, mod, 0, seq, TILE_PROJ, "ret_out",
                        gate_src=proj, gate_col=gate_col)


def _conv_pw1_epilogue(acc, j, extras, outs):
    b_ref = extras[0]
    u = acc + b_ref[...]
    dm = u.shape[1] // 2
    outs[0][...] = (u[:, :dm] * jax.nn.sigmoid(u[:, dm:])).astype(BF16)


def _conv_kernel(prev_ref, cur_ref, next_ref, wdw_ref, bdw_ref, lng_ref, lnb_ref, w2_ref, b2_ref, x_ref, mod_ref,
                 o_ref, ext_scr, shift_scr, c_scr, *, tiles_per_seq, row_block):
    tm = cur_ref.shape[0]
    t = pl.program_id(0) % tiles_per_seq
    keep_prev = jnp.where(t == 0, 0.0, 1.0)
    keep_next = jnp.where(t == tiles_per_seq - 1, 0.0, 1.0)
    ext_scr[0:CONV_HALO, :] = prev_ref[...].astype(F32) * keep_prev
    ext_scr[CONV_HALO:CONV_HALO + tm, :] = cur_ref[...].astype(F32)
    ext_scr[CONV_HALO + tm:, :] = next_ref[...].astype(F32) * keep_next
    base = CONV_HALO - CONV_PAD
    span = tm + 2 * CONV_HALO - SUBLANES
    for r in range(1, SUBLANES):
        shift_scr[r - 1, 0:span, :] = ext_scr[r:r + span, :]
    for rb in range(tm // row_block):
        r0 = rb * row_block
        acc = jnp.zeros((row_block, cur_ref.shape[1]), F32) + bdw_ref[...]
        for j in range(CONV_WIDTH):
            phase = (base + j) % SUBLANES
            start = r0 + base + j - phase
            if phase == 0:
                tap = ext_scr[start:start + row_block, :]
            else:
                tap = shift_scr[phase - 1, start:start + row_block, :]
            acc = acc + jnp.tile(wdw_ref[j], (row_block // SUBLANES, 1)) * tap
        c_scr[r0:r0 + row_block, :] = acc
    u = c_scr[...]
    mu = jnp.mean(u, axis=-1, keepdims=True)
    var = jnp.mean(jnp.square(u - mu), axis=-1, keepdims=True)
    u = (u - mu) * lax.rsqrt(var + EPS) * lng_ref[...] + lnb_ref[...]
    u = _silu(u).astype(BF16)
    z = jnp.dot(u, w2_ref[...], preferred_element_type=F32) + b2_ref[...]
    o_ref[...] = x_ref[...] + mod_ref[0][2:3] * z


def _conv_mixer(x, mod, w_pw1, b_pw1, w_dw, b_dw, ln_gain, ln_bias, w_pw2, b_pw2, seq):
    n, d = x.shape
    tm = TILE_PROJ
    u = _project(x, mod, 0, w_pw1.astype(BF16), [b_pw1.reshape(1, 2 * d).astype(F32)],
                 [pl.BlockSpec((1, 2 * d), lambda i, j: (0, 0))], _conv_pw1_epilogue,
                 jax.ShapeDtypeStruct((n, d), BF16), pl.BlockSpec((tm, d), lambda i, j: (i, 0)),
                 tm, 2 * d, seq, "conv_pw1")
    tm = TILE_CONV
    tiles_per_seq = seq // tm
    halo_per_tile = tm // CONV_HALO
    n_halo = n // CONV_HALO
    wdw = jnp.broadcast_to(w_dw.reshape(CONV_WIDTH, 1, d).astype(F32), (CONV_WIDTH, SUBLANES, d))
    vec = lambda a: a.reshape(1, d).astype(F32)
    row_spec = pl.BlockSpec((1, d), lambda i: (0, 0))
    return pl.pallas_call(
        functools.partial(_conv_kernel, tiles_per_seq=tiles_per_seq, row_block=CONV_ROW_BLOCK),
        grid=(n // tm,),
        in_specs=[pl.BlockSpec((CONV_HALO, d), lambda i: (jnp.maximum(i * halo_per_tile - 1, 0), 0)),
                  pl.BlockSpec((tm, d), lambda i: (i, 0)),
                  pl.BlockSpec((CONV_HALO, d), lambda i: (jnp.minimum((i + 1) * halo_per_tile, n_halo - 1), 0)),
                  pl.BlockSpec((CONV_WIDTH, SUBLANES, d), lambda i: (0, 0, 0)),
                  row_spec, row_spec, row_spec,
                  pl.BlockSpec((d, d), lambda i: (0, 0)),
                  row_spec,
                  pl.BlockSpec((tm, d), lambda i: (i, 0)),
                  pl.BlockSpec((1, 6, d), lambda i: (i // tiles_per_seq, 0, 0))],
        out_specs=pl.BlockSpec((tm, d), lambda i: (i, 0)),
        out_shape=jax.ShapeDtypeStruct((n, d), F32),
        scratch_shapes=[pltpu.VMEM((tm + 2 * CONV_HALO, d), F32),
                        pltpu.VMEM((SUBLANES - 1, tm + 2 * CONV_HALO, d), F32), pltpu.VMEM((tm, d), F32)],
        compiler_params=_cparams("parallel"),
        name="conv_module",
    )(u, u, u, wdw, vec(b_dw), vec(ln_gain), vec(ln_bias), w_pw2.astype(BF16), vec(b_pw2), x, mod)


def _t5_bucket(rel):
    half = T5_BUCKETS // 2
    exact = half // 2
    n = jnp.abs(rel)
    log_ratio = jnp.log(jnp.maximum(n, 1).astype(F32) / exact) / math.log(T5_MAX_DIST / exact)
    large = jnp.minimum(exact + (log_ratio * (half - exact)).astype(jnp.int32), half - 1)
    return jnp.where(rel > 0, half, 0) + jnp.where(n < exact, n, large)


def _bias_tile_kernel(table_ref, idx_ref, o_ref):
    h = pl.program_id(0)
    idx = idx_ref[0]
    acc = jnp.zeros(idx.shape, F32)
    for b in range(T5_BUCKETS):
        acc = jnp.where(idx == b, table_ref[b, h], acc)
    o_ref[0, 0] = acc * LOG2_E


def _diff_kernel(q_ref, k_ref, v_ref, bias_ref, lam_ref, gain_ref, o_ref, s_scr, m_scr, acc_scr, *, n_chunks, out_scale,
                 lambda_init):
    tq = q_ref.shape[0]
    qi = pl.program_id(2)
    q = q_ref[...]
    lane = lax.broadcasted_iota(jnp.int32, (tq, LANES), 1)
    low = lane < DIFF_HEAD_DIM
    zero = jnp.zeros((tq, LANES), BF16)
    lhs = jnp.concatenate([jnp.where(low, q, zero), jnp.where(low, zero, q)], axis=0)

    q_sub = tq // KEY_CHUNK

    def add_bias(s, j):
        tiles = [bias_ref[0, jnp.clip(j - (q_sub * qi + a), -2, 2) + 2] for a in range(q_sub)]
        return s + jnp.concatenate(tiles + tiles, axis=0)

    o = _softmax_pv(lhs, k_ref, v_ref, s_scr, m_scr, acc_scr, n_chunks, add_bias)
    lam_v = lam_ref[...]
    lam = (jnp.exp(jnp.sum(lam_v[0:1] * lam_v[1:2], axis=-1, keepdims=True))
           - jnp.exp(jnp.sum(lam_v[2:3] * lam_v[3:4], axis=-1, keepdims=True)) + lambda_init)
    y = o[:tq] - lam * o[tq:]
    y = y * lax.rsqrt(jnp.mean(y * y, axis=-1, keepdims=True) + EPS) * gain_ref[...] * out_scale
    o_ref[...] = y.astype(BF16)


def _diff_mixer(x, mod, w_qkv, lam_q1, lam_k1, lam_q2, lam_k2, subln_gain, w_o, rel_bias, lambda_init,
                batch, seq):
    n, d = x.shape
    col_scale = jnp.concatenate([jnp.full((d,), DIFF_HEAD_DIM ** -0.5 * LOG2_E, F32), jnp.ones((2 * d,), F32)])
    w = (w_qkv * col_scale[None, :]).astype(BF16)
    tm = TILE_PROJ_NARROW
    qkv = _project(x, mod, 0, w, [], [], _plain_epilogue,
                   jax.ShapeDtypeStruct((n, 3 * d), BF16), pl.BlockSpec((tm, 3 * d), lambda i, j: (i, 0)),
                   tm, d, seq, "diff_qkv")

    tb = KEY_CHUNK
    ar = jnp.arange(KEY_CHUNK, dtype=jnp.int32)
    offs = (jnp.arange(5, dtype=jnp.int32) - 2) * KEY_CHUNK
    rel = (ar[None, None, :] + offs[:, None, None]) - ar[None, :, None]
    bucket = _t5_bucket(rel).astype(jnp.int32)
    bias = pl.pallas_call(
        _bias_tile_kernel,
        grid=(DIFF_HEADS, 5),
        in_specs=[pl.BlockSpec(memory_space=pltpu.SMEM),
                  pl.BlockSpec((1, tb, KEY_CHUNK), lambda h, t: (t, 0, 0))],
        out_specs=pl.BlockSpec((1, 1, tb, KEY_CHUNK), lambda h, t: (h, t, 0, 0)),
        out_shape=jax.ShapeDtypeStruct((DIFF_HEADS, 5, tb, KEY_CHUNK), F32),
        compiler_params=_cparams("arbitrary", "arbitrary"),
        name="rel_bias_tiles",
    )(rel_bias.astype(F32), bucket)

    lam = jnp.stack([lam_q1, lam_k1, lam_q2, lam_k2]).astype(F32)
    n_chunks = seq // KEY_CHUNK
    tq = TILE_DIFF_Q
    q_tiles = seq // tq
    o = pl.pallas_call(
        functools.partial(_diff_kernel, n_chunks=n_chunks, out_scale=1.0 - lambda_init, lambda_init=lambda_init),
        grid=(batch, DIFF_HEADS, q_tiles),
        in_specs=[pl.BlockSpec((tq, LANES), lambda b, h, i: (b * q_tiles + i, h)),
                  pl.BlockSpec((seq, LANES), lambda b, h, i: (b, DIFF_HEADS + h)),
                  pl.BlockSpec((seq, LANES), lambda b, h, i: (b, 2 * DIFF_HEADS + h)),
                  pl.BlockSpec((1, 5, tb, KEY_CHUNK), lambda b, h, i: (h, 0, 0, 0)),
                  pl.BlockSpec((4, DIFF_HEAD_DIM), lambda b, h, i: (0, 0)),
                  pl.BlockSpec((1, LANES), lambda b, h, i: (0, 0))],
        out_specs=pl.BlockSpec((tq, LANES), lambda b, h, i: (b * q_tiles + i, h)),
        out_shape=jax.ShapeDtypeStruct((n, d), BF16),
        scratch_shapes=[pltpu.VMEM((n_chunks, 2 * tq, KEY_CHUNK), F32),
                        pltpu.VMEM((2 * tq, LANES), F32), pltpu.VMEM((2 * tq, 2 * LANES), F32)],
        compiler_params=_cparams("parallel", "parallel", "arbitrary"),
        name="diff_attention",
    )(qkv, qkv, qkv, bias, lam, subln_gain.reshape(1, LANES).astype(F32))
    return _out_project(o, 0, d, w_o.astype(BF16), x, mod, 0, seq, TILE_PROJ_NARROW, "diff_out")


def _final_norm_kernel(x_ref, g_ref, o_ref):
    x = x_ref[...]
    o_ref[...] = x * lax.rsqrt(jnp.mean(x * x, axis=-1, keepdims=True) + EPS) * g_ref[...]


def _final_norm(x, gain, tm):
    n, d = x.shape
    return pl.pallas_call(
        _final_norm_kernel,
        grid=(n // tm,),
        in_specs=[pl.BlockSpec((tm, d), lambda i: (i, 0)), pl.BlockSpec((1, d), lambda i: (0, 0))],
        out_specs=pl.BlockSpec((tm, d), lambda i: (i, 0)),
        out_shape=jax.ShapeDtypeStruct((n, d), F32),
        compiler_params=_cparams("parallel"),
        name="final_norm",
    )(x, gain.reshape(1, d).astype(F32))


def kernel(x, c, mod_w, mod_b, att_w_qkv, att_q_gain, att_k_gain, att_w_o, ret_w_in, ret_decay_logit, ret_gn_gain, ret_w_o, cnv_w_pw1, cnv_b_pw1, cnv_w_dw, cnv_b_dw, cnv_ln_gain, cnv_ln_bias, cnv_w_pw2, cnv_b_pw2, dif_w_qkv, dif_lam_q1, dif_lam_k1, dif_lam_q2, dif_lam_k2, dif_subln_gain, dif_w_o, rel_bias, ffn_w_gate, ffn_w_up, ffn_w_down, moe_w_router, moe_b_router, moe_w_gate, moe_w_up, moe_w_down, final_gain):
    batch, seq, d = x.shape
    depth = mod_w.shape[0]
    xf = x.reshape(batch * seq, d)
    mods = _modulation(c, mod_w, mod_b)
    moe_gate, moe_up, moe_down = moe_w_gate.astype(BF16), moe_w_up.astype(BF16), moe_w_down.astype(BF16)
    for i in range(depth):
        mod = mods[i]
        kind, j = i % 4, i // 4
        if kind == 0:
            xf = _attention_mixer(xf, mod, att_w_qkv[j], att_q_gain[j], att_k_gain[j], att_w_o[j], batch, seq)
        elif kind == 1:
            xf = _retention_mixer(xf, mod, ret_w_in[j], ret_decay_logit[j], ret_gn_gain[j], ret_w_o[j], batch, seq)
        elif kind == 2:
            xf = _conv_mixer(xf, mod, cnv_w_pw1[j], cnv_b_pw1[j], cnv_w_dw[j], cnv_b_dw[j], cnv_ln_gain[j],
                             cnv_ln_bias[j], cnv_w_pw2[j], cnv_b_pw2[j], seq)
        else:
            lambda_init = 0.8 - 0.6 * math.exp(-0.3 * i)
            xf = _diff_mixer(xf, mod, dif_w_qkv[j], dif_lam_q1[j], dif_lam_k1[j], dif_lam_q2[j], dif_lam_k2[j],
                             dif_subln_gain[j], dif_w_o[j], rel_bias, lambda_init, batch, seq)
        if i % 2 == 0:
            m = i // 2
            xf = _ffn(xf, mod, ffn_w_gate[m].astype(BF16), ffn_w_up[m].astype(BF16), ffn_w_down[m].astype(BF16),
                      seq, TILE_PROJ)
        else:
            m = i // 2
            xf = _moe(xf, mod, moe_w_router[m], moe_b_router[m], moe_gate, moe_up, moe_down, m, seq,
                      out_gain=final_gain if i == depth - 1 else None)
    if depth % 2 == 1:
        xf = _final_norm(xf, final_gain, TILE_PROJ)
    return xf.reshape(batch, seq, d)
```

```python
import functools
import math

import jax
import jax.numpy as jnp
from jax import lax
from jax.experimental import pallas as pl
from jax.experimental.pallas import tpu as pltpu

F32 = jnp.float32
BF16 = jnp.bfloat16

EPS = 1e-6
GRID_W = 64

ATT_HEADS = 16
ATT_KV_HEADS = 4
ATT_HEAD_DIM = 64
ROPE_THETA = 10000.0

RET_HEADS = 4
RET_QK_DIM = 256
RET_V_DIM = 512
RET_THETA = 10000.0
RET_CHUNK = 256

CONV_WIDTH = 31
CONV_PAD = CONV_WIDTH // 2
CONV_HALO = 16

DIFF_HEADS = 8
DIFF_HEAD_DIM = 64
T5_BUCKETS = 32
T5_MAX_DIST = 128

N_EXPERTS = 8
LANES = 128
SUBLANES = 8
KEY_CHUNK = 256
LOG2_E = 1.4426950408889634
VMEM_LIMIT_BYTES = 56 * 1024 * 1024

TILE_PROJ = 512
TILE_PROJ_NARROW = 1024
TILE_ROUTE = 512
TILE_SLOT = 512
TILE_EXPERT_HIDDEN = 1792
TILE_CONV = 512
CONV_ROW_BLOCK = 32
TILE_ATT_Q = 512
TILE_DIFF_Q = 1024


def _cparams(*sem):
    return pltpu.CompilerParams(dimension_semantics=sem, vmem_limit_bytes=VMEM_LIMIT_BYTES)


def _silu(v):
    return v * jax.nn.sigmoid(v)


def _norm_mod(x, mod, which):
    shift = mod[3 * which:3 * which + 1]
    scale = mod[3 * which + 1:3 * which + 2]
    y = x * lax.rsqrt(jnp.mean(x * x, axis=-1, keepdims=True) + EPS)
    return y * (1.0 + scale) + shift


def _split_dot(a, b):
    hi = a.astype(BF16)
    lo = (a - hi.astype(F32)).astype(BF16)
    return (jnp.dot(hi, b, preferred_element_type=F32) + jnp.dot(lo, b, preferred_element_type=F32))


def _mod_kernel(c_ref, w_ref, b_ref, o_ref):
    cond = _silu(c_ref[...])
    o_ref[0] = jnp.dot(cond, w_ref[0], preferred_element_type=F32, precision=lax.Precision.HIGHEST) + b_ref[0]


def _modulation(c, mod_w, mod_b):
    depth, d, d6 = mod_w.shape
    b = c.shape[0]
    tn = d
    out = pl.pallas_call(
        _mod_kernel,
        grid=(depth, d6 // tn),
        in_specs=[pl.BlockSpec((b, d), lambda i, j: (0, 0)),
                  pl.BlockSpec((1, d, tn), lambda i, j: (i, 0, j)),
                  pl.BlockSpec((1, 1, tn), lambda i, j: (i, 0, j))],
        out_specs=pl.BlockSpec((1, b, tn), lambda i, j: (i, 0, j)),
        out_shape=jax.ShapeDtypeStruct((depth, b, d6), F32),
        compiler_params=_cparams("arbitrary", "arbitrary"),
        name="modulation",
    )(c, mod_w, mod_b.reshape(depth, 1, d6))
    return out.reshape(depth, b, 6, d)


def _proj_kernel(x_ref, mod_ref, w_ref, *rest, which, n_extra, epilogue, tn):
    extras = rest[:n_extra]
    outs = rest[n_extra:]
    h = _norm_mod(x_ref[...], mod_ref[0], which).astype(BF16)
    for j in range(w_ref.shape[1] // tn):
        acc = jnp.dot(h, w_ref[:, j * tn:(j + 1) * tn], preferred_element_type=F32)
        epilogue(acc, j, extras, outs)


def _project(x, mod, which, w, extras, extra_specs, epilogue, out_shapes, out_specs, tm, tn, seq, name):
    n, d = x.shape
    c = w.shape[1]
    tiles_per_seq = seq // tm
    kern = functools.partial(_proj_kernel, which=which, n_extra=len(extras), epilogue=epilogue, tn=tn)
    return pl.pallas_call(
        kern,
        grid=(n // tm, 1),
        in_specs=[pl.BlockSpec((tm, d), lambda i, j: (i, 0)),
                  pl.BlockSpec((1, 6, d), lambda i, j: (i // tiles_per_seq, 0, 0)),
                  pl.BlockSpec((d, c), lambda i, j: (0, 0), pipeline_mode=pl.Buffered(1))] + extra_specs,
        out_specs=out_specs,
        out_shape=out_shapes,
        compiler_params=_cparams("parallel", "arbitrary"),
        name=name,
    )(x, mod, w, *extras)


def _plain_epilogue(acc, j, extras, outs):
    tn = acc.shape[1]
    outs[0][:, j * tn:(j + 1) * tn] = acc.astype(outs[0].dtype)


def _oproj_kernel(*refs, which, gated, has_bias):
    refs = list(refs)
    y_ref = refs.pop(0)
    g_ref = refs.pop(0) if gated else None
    w_ref = refs.pop(0)
    b_ref = refs.pop(0) if has_bias else None
    x_ref, mod_ref, o_ref = refs
    y = y_ref[...]
    if gated:
        y = (y.astype(F32) * _silu(g_ref[...].astype(F32))).astype(BF16)
    z = jnp.dot(y, w_ref[...], preferred_element_type=F32)
    if has_bias:
        z = z + b_ref[...]
    gate = mod_ref[0][3 * which + 2:3 * which + 3]
    o_ref[...] = x_ref[...] + gate * z


def _out_project(y, y_col, k, w, x, mod, which, seq, tm, name, gate_src=None, gate_col=0, bias=None):
    n, d = x.shape
    tiles_per_seq = seq // tm
    args = [y]
    specs = [pl.BlockSpec((tm, k), lambda i: (i, y_col))]
    if gate_src is not None:
        args.append(gate_src)
        specs.append(pl.BlockSpec((tm, k), lambda i: (i, gate_col)))
    args.append(w)
    specs.append(pl.BlockSpec((k, d), lambda i: (0, 0)))
    if bias is not None:
        args.append(bias.reshape(1, d))
        specs.append(pl.BlockSpec((1, d), lambda i: (0, 0)))
    args += [x, mod]
    specs += [pl.BlockSpec((tm, d), lambda i: (i, 0)),
              pl.BlockSpec((1, 6, d), lambda i: (i // tiles_per_seq, 0, 0))]
    kern = functools.partial(_oproj_kernel, which=which, gated=gate_src is not None, has_bias=bias is not None)
    return pl.pallas_call(
        kern,
        grid=(n // tm,),
        in_specs=specs,
        out_specs=pl.BlockSpec((tm, d), lambda i: (i, 0)),
        out_shape=jax.ShapeDtypeStruct((n, d), F32),
        compiler_params=_cparams("parallel"),
        name=name,
    )(*args)


def _ffn_kernel(x_ref, mod_ref, wg_ref, wu_ref, wd_ref, o_ref):
    x = x_ref[...]
    mod = mod_ref[0]
    h = _norm_mod(x, mod, 1).astype(BF16)
    g = jnp.dot(h, wg_ref[...], preferred_element_type=F32)
    u = jnp.dot(h, wu_ref[...], preferred_element_type=F32)
    a = (_silu(g) * u).astype(BF16)
    o_ref[...] = x + mod[5:6] * jnp.dot(a, wd_ref[...], preferred_element_type=F32)


def _ffn(x, mod, wg, wu, wd, seq, tm):
    n, d = x.shape
    ff = wg.shape[1]
    tiles_per_seq = seq // tm
    resident = pl.Buffered(1)
    return pl.pallas_call(
        _ffn_kernel,
        grid=(n // tm,),
        in_specs=[pl.BlockSpec((tm, d), lambda i: (i, 0)),
                  pl.BlockSpec((1, 6, d), lambda i: (i // tiles_per_seq, 0, 0)),
                  pl.BlockSpec((d, ff), lambda i: (0, 0), pipeline_mode=resident),
                  pl.BlockSpec((d, ff), lambda i: (0, 0), pipeline_mode=resident),
                  pl.BlockSpec((ff, d), lambda i: (0, 0), pipeline_mode=resident)],
        out_specs=pl.BlockSpec((tm, d), lambda i: (i, 0)),
        out_shape=jax.ShapeDtypeStruct((n, d), F32),
        compiler_params=_cparams("parallel"),
        name="dense_ffn",
    )(x, mod, wg, wu, wd)


def _router_kernel(x_ref, mod_ref, wr_ref, br_ref, tri_ref, route_ref, route_t_ref, count_ref, carry_scr):
    @pl.when(pl.program_id(0) == 0)
    def _():
        carry_scr[...] = jnp.zeros_like(carry_scr)

    h = _norm_mod(x_ref[...], mod_ref[0], 1)
    h_hi = h.astype(BF16)
    h_lo = (h - h_hi.astype(F32)).astype(BF16)
    logits = (jnp.dot(h_hi, wr_ref[0], preferred_element_type=F32)
              + (jnp.dot(h_hi, wr_ref[1], preferred_element_type=F32)
                 + jnp.dot(h_lo, wr_ref[0], preferred_element_type=F32))) + br_ref[...]
    lane = lax.broadcasted_iota(jnp.int32, logits.shape, 1)
    neg = jnp.float32(-jnp.inf)
    logits = jnp.where(lane < N_EXPERTS, logits, neg)
    m1 = jnp.max(logits, axis=-1, keepdims=True)
    i1 = jnp.min(jnp.where(logits == m1, lane, LANES), axis=-1, keepdims=True)
    rest = jnp.where(lane == i1, neg, logits)
    m2 = jnp.max(rest, axis=-1, keepdims=True)
    i2 = jnp.min(jnp.where(rest == m2, lane, LANES), axis=-1, keepdims=True)
    e2 = jnp.exp(m2 - m1)
    w1 = 1.0 / (1.0 + e2)
    w2 = e2 / (1.0 + e2)
    sel = jnp.where((lane == i1) | (lane == i2), 1.0, 0.0)
    incl = jnp.dot(tri_ref[...], sel.astype(BF16), preferred_element_type=F32)
    excl = incl - sel + carry_scr[...]
    r1 = jnp.sum(jnp.where(lane == i1, excl, 0.0), axis=-1, keepdims=True)
    r2 = jnp.sum(jnp.where(lane == i2, excl, 0.0), axis=-1, keepdims=True)
    carry_scr[...] += incl[incl.shape[0] - 1:, :]
    count_ref[...] = carry_scr[...]
    fields = (i1.astype(F32), i2.astype(F32), w1, w2, r1, r2)
    route = jnp.zeros(logits.shape, F32)
    for k, field in enumerate(fields):
        route = jnp.where(lane == k, field, route)
    route_ref[...] = route
    route_t_ref[...] = route.T[:SUBLANES]


def _router(x, mod, w_router, b_router, seq, tm):
    n, d = x.shape
    tiles_per_seq = seq // tm
    wr = jnp.pad(w_router, ((0, 0), (0, LANES - N_EXPERTS))).astype(F32)
    wr_hi = wr.astype(BF16)
    wr = jnp.stack([wr_hi, (wr - wr_hi.astype(F32)).astype(BF16)])
    br = jnp.pad(b_router, (0, LANES - N_EXPERTS)).reshape(1, LANES)
    tri = (jnp.arange(tm)[:, None] >= jnp.arange(tm)[None, :]).astype(BF16)
    return pl.pallas_call(
        _router_kernel,
        grid=(n // tm,),
        in_specs=[pl.BlockSpec((tm, d), lambda i: (i, 0)),
                  pl.BlockSpec((1, 6, d), lambda i: (i // tiles_per_seq, 0, 0)),
                  pl.BlockSpec((2, d, LANES), lambda i: (0, 0, 0)),
                  pl.BlockSpec((1, LANES), lambda i: (0, 0)),
                  pl.BlockSpec((tm, tm), lambda i: (0, 0))],
        out_specs=[pl.BlockSpec((tm, LANES), lambda i: (i, 0)), pl.BlockSpec((SUBLANES, tm), lambda i: (0, i)),
                   pl.BlockSpec((1, LANES), lambda i: (0, 0))],
        out_shape=[jax.ShapeDtypeStruct((n, LANES), F32), jax.ShapeDtypeStruct((SUBLANES, n), F32),
                   jax.ShapeDtypeStruct((1, LANES), F32)],
        scratch_shapes=[pltpu.VMEM((1, LANES), F32)],
        compiler_params=_cparams("arbitrary"),
        name="moe_router",
    )(x, mod, wr, br, tri)


def _dispatch_kernel(zero_rows_ref, d1_ref, d2_ref, x_ref, mod_ref, xs_ref, h_scr, zero_scr, sems):
    tm = x_ref.shape[0]

    @pl.when(pl.program_id(0) == 0)
    def _():
        zero_scr[...] = jnp.zeros_like(zero_scr)
        te = zero_scr.shape[0]
        for k in range(zero_rows_ref.shape[0]):
            start = pl.multiple_of(zero_rows_ref[k], te)
            copy = pltpu.make_async_copy(zero_scr, xs_ref.at[pl.ds(start, te), :], sems.at[2])
            copy.start()
            copy.wait()

    h_scr[...] = _norm_mod(x_ref[...], mod_ref[0], 1)

    def row_copy(r, dest, which):
        return pltpu.make_async_copy(h_scr.at[pl.ds(r, 1), :], xs_ref.at[pl.ds(dest, 1), :], sems.at[which])

    def issue(r, carry):
        row_copy(r, d1_ref[0, 0, r], 0).start()
        row_copy(r, d2_ref[0, 0, r], 1).start(priority=1)
        return carry

    lax.fori_loop(0, tm, issue, 0, unroll=4)
    for which in range(2):
        pltpu.make_async_copy(h_scr, xs_ref.at[pl.ds(0, tm), :], sems.at[which]).wait()


def _expert_kernel(tile_e_ref, n_act_ref, x_ref, wg_ref, wu_ref, wd_ref, o_ref, *, tf):
    del tile_e_ref
    s = pl.program_id(0)

    @pl.when(s < n_act_ref[0])
    def _():
        h = x_ref[...].astype(BF16)
        out = None
        for f in range(wg_ref.shape[3] // tf):
            cols = slice(f * tf, (f + 1) * tf)
            g = jnp.dot(h, wg_ref[0, 0, :, cols], preferred_element_type=F32)
            u = jnp.dot(h, wu_ref[0, 0, :, cols], preferred_element_type=F32)
            a = (_silu(g) * u).astype(BF16)
            part = jnp.dot(a, wd_ref[0, 0, cols, :], preferred_element_type=F32)
            out = part if out is None else out + part
        o_ref[...] = out

    @pl.when(s >= n_act_ref[0])
    def _():
        o_ref[...] = jnp.zeros_like(o_ref)


def _combine_kernel(d1_ref, d2_ref, route_ref, x_ref, mod_ref, *rest, out_norm):
    if out_norm:
        gain_ref, ys_ref, o_ref, buf, sems = rest
    else:
        ys_ref, o_ref, buf, sems = rest
    tm = x_ref.shape[0]

    def row_copy(r, src, which):
        return pltpu.make_async_copy(ys_ref.at[pl.ds(src, 1), :], buf.at[which, pl.ds(r, 1), :], sems.at[which])

    def issue(r, carry):
        row_copy(r, d1_ref[0, 0, r], 0).start()
        row_copy(r, d2_ref[0, 0, r], 1).start(priority=1)
        return carry

    lax.fori_loop(0, tm, issue, 0, unroll=4)
    for which in range(2):
        pltpu.make_async_copy(ys_ref.at[pl.ds(0, tm), :], buf.at[which], sems.at[which]).wait()
    route = route_ref[...]
    y = route[:, 2:3] * buf[0] + route[:, 3:4] * buf[1]
    out = x_ref[...] + mod_ref[0][5:6] * y
    if out_norm:
        out = out * lax.rsqrt(jnp.mean(out * out, axis=-1, keepdims=True) + EPS) * gain_ref[...]
    o_ref[...] = out


def _moe(x, mod, w_router, b_router, w_gate, w_up, w_down, layer, seq, out_gain=None):
    n, d = x.shape
    _, n_e, _, ff = w_gate.shape
    tm, te, tf = TILE_ROUTE, TILE_SLOT, TILE_EXPERT_HIDDEN
    tiles_per_seq = seq // tm
    n_tok_tiles = n // tm
    n_slots = 2 * n + n_e * te
    n_tiles = n_slots // te

    route, route_t, counts = _router(x, mod, w_router, b_router, seq, tm)

    counts = counts[0, :n_e].astype(jnp.int32)
    padded = ((counts + te - 1) // te) * te
    ends = jnp.cumsum(padded)
    offsets = ends - padded
    n_act = (ends[-1] // te).astype(jnp.int32)
    tile_start = jnp.arange(n_tiles, dtype=jnp.int32) * te
    tile_e = jnp.sum((tile_start[:, None] >= ends[None, :]).astype(jnp.int32), axis=1)
    tile_e = jnp.minimum(tile_e, tile_e[jnp.maximum(n_act - 1, 0)]).astype(jnp.int32)
    e1 = route_t[0].astype(jnp.int32)
    e2 = route_t[1].astype(jnp.int32)
    dest1 = (offsets[e1] + route_t[4].astype(jnp.int32)).reshape(n_tok_tiles, 1, tm)
    dest2 = (offsets[e2] + route_t[5].astype(jnp.int32)).reshape(n_tok_tiles, 1, tm)
    n_act = n_act.reshape(1)

    slot_spec = pl.BlockSpec((1, 1, tm), lambda i: (i, 0, 0), memory_space=pltpu.SMEM)
    zero_rows = jnp.concatenate([jnp.maximum(ends - te, 0),
                                 (n_tiles - n_e + jnp.arange(n_e, dtype=jnp.int32)) * te]).astype(jnp.int32)
    xs = pl.pallas_call(
        _dispatch_kernel,
        grid=(n_tok_tiles,),
        in_specs=[pl.BlockSpec(memory_space=pltpu.SMEM), slot_spec, slot_spec,
                  pl.BlockSpec((tm, d), lambda i: (i, 0)),
                  pl.BlockSpec((1, 6, d), lambda i: (i // tiles_per_seq, 0, 0))],
        out_specs=pl.BlockSpec(memory_space=pl.ANY),
        out_shape=jax.ShapeDtypeStruct((n_slots, d), F32),
        scratch_shapes=[pltpu.VMEM((tm, d), F32), pltpu.VMEM((te, d), F32), pltpu.SemaphoreType.DMA((3,))],
        compiler_params=_cparams("arbitrary"),
        name="moe_dispatch",
    )(zero_rows, dest1, dest2, x, mod)

    def live(s, n_act_ref):
        return jnp.minimum(s, jnp.maximum(n_act_ref[0] - 1, 0))

    def expert_weights(shape):
        return pl.BlockSpec(shape, lambda s, te_ref, na_ref: (layer, te_ref[s], 0, 0), pipeline_mode=pl.Buffered(1))

    ys = pl.pallas_call(
        functools.partial(_expert_kernel, tf=tf),
        grid_spec=pltpu.PrefetchScalarGridSpec(
            num_scalar_prefetch=2,
            grid=(n_tiles,),
            in_specs=[pl.BlockSpec((te, d), lambda s, te_ref, na_ref: (live(s, na_ref), 0)),
                      expert_weights((1, 1, d, ff)), expert_weights((1, 1, d, ff)), expert_weights((1, 1, ff, d))],
            out_specs=pl.BlockSpec((te, d), lambda s, te_ref, na_ref: (s, 0))),
        out_shape=jax.ShapeDtypeStruct((n_slots, d), F32),
        compiler_params=_cparams("arbitrary"),
        name="moe_experts",
    )(tile_e, n_act, xs, w_gate, w_up, w_down)

    out_norm = out_gain is not None
    gain_args = [out_gain.reshape(1, d).astype(F32)] if out_norm else []
    gain_specs = [pl.BlockSpec((1, d), lambda i: (0, 0))] if out_norm else []
    return pl.pallas_call(
        functools.partial(_combine_kernel, out_norm=out_norm),
        grid=(n_tok_tiles,),
        in_specs=[slot_spec, slot_spec,
                  pl.BlockSpec((tm, LANES), lambda i: (i, 0)),
                  pl.BlockSpec((tm, d), lambda i: (i, 0)),
                  pl.BlockSpec((1, 6, d), lambda i: (i // tiles_per_seq, 0, 0))] + gain_specs
                 + [pl.BlockSpec(memory_space=pl.ANY)],
        out_specs=pl.BlockSpec((tm, d), lambda i: (i, 0)),
        out_shape=jax.ShapeDtypeStruct((n, d), F32),
        scratch_shapes=[pltpu.VMEM((2, tm, d), F32), pltpu.SemaphoreType.DMA((2,))],
        compiler_params=_cparams("arbitrary"),
        name="moe_combine",
    )(dest1, dest2, route, x, mod, *gain_args, ys)


def _softmax_pv(lhs, k_ref, v_ref, s_scr, m_scr, acc_scr, n_chunks, bias_fn=None):
    rows = lhs.shape[0]
    group = min(8, n_chunks)
    m_scr[...] = jnp.full(m_scr.shape, -jnp.inf, F32)

    for j in range(n_chunks):
        kc = k_ref[j * KEY_CHUNK:(j + 1) * KEY_CHUNK, :]
        s = lax.dot_general(lhs, kc, (((1,), (1,)), ((), ())), preferred_element_type=F32)
        if bias_fn is not None:
            s = bias_fn(s, j)
        s_scr[j] = s
        m_scr[...] = jnp.maximum(m_scr[...], jnp.maximum(s[:, :LANES], s[:, LANES:]))
    m_scr[...] = jnp.broadcast_to(jnp.max(m_scr[...], axis=-1, keepdims=True), m_scr.shape)
    acc_scr[...] = jnp.zeros_like(acc_scr)
    ones = jnp.ones((group * KEY_CHUNK, LANES), BF16)

    def pv_body(t, carry):
        mb = m_scr[...]
        parts = []
        for u in range(group):
            s = s_scr[group * t + u]
            parts.append(jnp.exp2(s[:, :LANES] - mb).astype(BF16))
            parts.append(jnp.exp2(s[:, LANES:] - mb).astype(BF16))
        p = jnp.concatenate(parts, axis=1)
        start = pl.multiple_of(t * (group * KEY_CHUNK), group * KEY_CHUNK)
        v_aug = jnp.concatenate([v_ref[pl.ds(start, group * KEY_CHUNK), :], ones], axis=1)
        acc_scr[...] += jnp.dot(p, v_aug, preferred_element_type=F32)
        return carry

    lax.fori_loop(0, n_chunks // group, pv_body, 0)
    acc = acc_scr[...]
    return acc[:, :LANES] / acc[:, LANES:]


def _att_qkv_epilogue(acc, j, extras, outs, n_qk):
    gain_ref, cos_ref, sin_ref, g1_ref, g2_ref = extras
    q_ref, k_ref, v_ref = outs
    qk = acc[:, :n_qk]
    ms = _split_dot(qk * qk, g1_ref[...])
    r = lax.rsqrt(ms + EPS)
    y = qk * _split_dot(r, g2_ref[...]) * gain_ref[...]
    cos = cos_ref[...]
    sin = sin_ref[...]
    lane = lax.broadcasted_iota(jnp.int32, cos.shape, 1)
    first_half = (lane & (ATT_HEAD_DIM // 2)) == 0
    nq = q_ref.shape[1]
    for c in range(n_qk // LANES):
        yc = y[:, c * LANES:(c + 1) * LANES]
        partner = jnp.where(first_half, pltpu.roll(yc, LANES - ATT_HEAD_DIM // 2, 1),
                            pltpu.roll(yc, ATT_HEAD_DIM // 2, 1))
        out = (yc * cos + partner * sin).astype(BF16)
        if c * LANES < nq:
            q_ref[:, c * LANES:(c + 1) * LANES] = out
        else:
            k_ref[:, c * LANES - nq:(c + 1) * LANES - nq] = out
    v_ref[...] = acc[:, n_qk:].astype(BF16)


def _att_kernel(q_ref, k_ref, v_ref, o_ref, s_scr, m_scr, acc_scr, *, n_chunks):
    tq = q_ref.shape[0]
    q = q_ref[...]
    lane = lax.broadcasted_iota(jnp.int32, (tq, LANES), 1)
    low = lane < ATT_HEAD_DIM
    zero = jnp.zeros((tq, LANES), BF16)
    parts = []
    for pair in range(2):
        qp = q[:, pair * LANES:(pair + 1) * LANES]
        parts.append(jnp.where(low, qp, zero))
        parts.append(jnp.where(low, zero, qp))
    lhs = jnp.concatenate(parts, axis=0)
    o = _softmax_pv(lhs, k_ref, v_ref, s_scr, m_scr, acc_scr, n_chunks)
    for pair in range(2):
        o_ref[:, pair * LANES:(pair + 1) * LANES] = jnp.where(
            low, o[(2 * pair) * tq:(2 * pair + 1) * tq], o[(2 * pair + 1) * tq:(2 * pair + 2) * tq]).astype(BF16)


def _attention_mixer(x, mod, w_qkv, q_gain, k_gain, w_o, batch, seq):
    n, d = x.shape
    hd = ATT_HEAD_DIM
    half = hd // 2
    n_freq = half // 2
    groups = ATT_HEADS // ATT_KV_HEADS
    perm = jnp.concatenate([jnp.arange(n_freq), half + jnp.arange(n_freq),
                            n_freq + jnp.arange(n_freq), half + n_freq + jnp.arange(n_freq)])
    q_cols = (jnp.arange(ATT_HEADS)[:, None] * hd + perm[None, :]).reshape(-1)
    k_base = ATT_HEADS * hd
    v_base = k_base + ATT_KV_HEADS * hd
    kv_rep = jnp.repeat(jnp.arange(ATT_KV_HEADS), 2)
    k_cols = (k_base + kv_rep[:, None] * hd + perm[None, :]).reshape(-1)
    v_cols = (v_base + kv_rep[:, None] * hd + jnp.arange(hd)[None, :]).reshape(-1)
    w = w_qkv[:, jnp.concatenate([q_cols, k_cols, v_cols])].astype(BF16)
    n_q = ATT_HEADS * hd
    n_kv = 2 * ATT_KV_HEADS * hd
    n_qk = n_q + n_kv
    gain_row = jnp.concatenate([jnp.tile(q_gain[perm] * (hd ** -0.5 * LOG2_E), ATT_HEADS),
                                jnp.tile(k_gain[perm], 2 * ATT_KV_HEADS)]).reshape(1, n_qk).astype(F32)
    pos = jnp.arange(seq, dtype=jnp.int32)
    row = (pos // GRID_W).astype(F32)
    col = (pos % GRID_W).astype(F32)
    inv = ROPE_THETA ** (-jnp.arange(n_freq, dtype=F32) / n_freq)
    ang = jnp.concatenate([row[:, None] * inv, col[:, None] * inv], axis=-1)
    cos_t = jnp.tile(jnp.cos(ang), (1, 2 * LANES // hd))
    sin_t = jnp.tile(jnp.concatenate([-jnp.sin(ang), jnp.sin(ang)], axis=-1), (1, LANES // hd))
    grp = jnp.arange(n_qk) // hd
    g1 = (grp[:, None] == jnp.arange(LANES)[None, :]).astype(F32) / hd
    g2 = (jnp.arange(LANES)[:, None] == grp[None, :]).astype(F32)
    tm = TILE_PROJ
    tiles_per_seq = seq // tm
    extras = [gain_row, cos_t, sin_t, g1.astype(BF16), g2.astype(BF16)]
    extra_specs = [pl.BlockSpec((1, n_qk), lambda i, j: (0, 0)),
                   pl.BlockSpec((tm, LANES), lambda i, j: (i % tiles_per_seq, 0)),
                   pl.BlockSpec((tm, LANES), lambda i, j: (i % tiles_per_seq, 0)),
                   pl.BlockSpec((n_qk, LANES), lambda i, j: (0, 0)),
                   pl.BlockSpec((LANES, n_qk), lambda i, j: (0, 0))]
    q, k, v = _project(
        x, mod, 0, w, extras, extra_specs, functools.partial(_att_qkv_epilogue, n_qk=n_qk),
        [jax.ShapeDtypeStruct((n, n_q), BF16), jax.ShapeDtypeStruct((n, n_kv), BF16),
         jax.ShapeDtypeStruct((n, n_kv), BF16)],
        [pl.BlockSpec((tm, n_q), lambda i, j: (i, 0)), pl.BlockSpec((tm, n_kv), lambda i, j: (i, 0)),
         pl.BlockSpec((tm, n_kv), lambda i, j: (i, 0))],
        tm, w.shape[1], seq, "att_qkv")

    tq = TILE_ATT_Q
    n_chunks = seq // KEY_CHUNK
    q_tiles = seq // tq
    o = pl.pallas_call(
        functools.partial(_att_kernel, n_chunks=n_chunks),
        grid=(batch, ATT_KV_HEADS, q_tiles),
        in_specs=[pl.BlockSpec((tq, groups * hd), lambda b, g, i: (b * q_tiles + i, g)),
                  pl.BlockSpec((seq, LANES), lambda b, g, i: (b, g)),
                  pl.BlockSpec((seq, LANES), lambda b, g, i: (b, g))],
        out_specs=pl.BlockSpec((tq, groups * hd), lambda b, g, i: (b * q_tiles + i, g)),
        out_shape=jax.ShapeDtypeStruct((n, n_q), BF16),
        scratch_shapes=[pltpu.VMEM((n_chunks, groups * tq, KEY_CHUNK), F32),
                        pltpu.VMEM((groups * tq, LANES), F32), pltpu.VMEM((groups * tq, 2 * LANES), F32)],
        compiler_params=_cparams("parallel", "parallel", "arbitrary"),
        name="gqa_attention",
    )(q, k, v)
    return _out_project(o, 0, n_q, w_o.astype(BF16), x, mod, 0, seq, TILE_PROJ_NARROW, "att_out")


def _ret_proj_epilogue(acc, j, extras, outs):
    cos_ref, sin_ref = extras
    o_ref = outs[0]
    tn = acc.shape[1]
    base = j * tn
    if j < 2:
        cos = cos_ref[...]
        sin = sin_ref[...]
        half = RET_QK_DIM // 2
        for hh in range(RET_HEADS):
            lo = hh * RET_QK_DIM
            x1 = acc[:, lo:lo + half]
            x2 = acc[:, lo + half:lo + RET_QK_DIM]
            o_ref[:, base + lo:base + lo + half] = (x1 * cos - x2 * sin).astype(BF16)
            o_ref[:, base + lo + half:base + lo + RET_QK_DIM] = (x1 * sin + x2 * cos).astype(BF16)
    else:
        o_ref[:, base:base + tn] = acc.astype(BF16)


def _ret_kernel(logit_ref, q_ref, k_ref, v_ref, gain_ref, o_ref, acc_scr, state_f_scr, state_b_scr, *, n_chunks):
    h = pl.program_id(1)
    c_len = RET_CHUNK

    def log_gamma(direction, shape):
        z = jnp.full(shape, logit_ref[direction, h], F32)
        return jnp.minimum(z, 0.0) - jnp.log1p(jnp.exp(-jnp.abs(z)))

    ri = lax.broadcasted_iota(jnp.int32, (c_len, c_len), 0)
    ci = lax.broadcasted_iota(jnp.int32, (c_len, c_len), 1)
    dist = (ri - ci).astype(F32)
    lg_f = log_gamma(0, (c_len, c_len))
    lg_b = log_gamma(1, (c_len, c_len))
    decay = jnp.where(dist >= 0, jnp.exp(lg_f * jnp.maximum(dist, 0.0)), jnp.exp(lg_b * jnp.maximum(-dist, 0.0)))
    idx = lax.broadcasted_iota(jnp.int32, (c_len, 1), 0).astype(F32)
    lgf = log_gamma(0, (c_len, 1))
    lgb = log_gamma(1, (c_len, 1))
    xi_f = jnp.exp(lgf * (idx + 1.0))
    zeta_f = jnp.exp(lgf * (c_len - 1.0 - idx))
    cd_f = jnp.exp(lgf[0:1] * c_len)
    xi_b = jnp.exp(lgb * (c_len - idx))
    zeta_b = jnp.exp(lgb * idx)
    cd_b = jnp.exp(lgb[0:1] * c_len)

    def chunk(cidx):
        start = pl.multiple_of(cidx * c_len, c_len)
        return (q_ref[pl.ds(start, c_len), :], k_ref[pl.ds(start, c_len), :], v_ref[pl.ds(start, c_len), :], start)

    def update_state(state_scr, kc, vc, zeta, cd):
        zv = (vc.astype(F32) * zeta).astype(BF16)
        kv = lax.dot_general(kc, zv, (((0,), (0,)), ((), ())), preferred_element_type=F32)
        state_scr[...] = state_scr[...] * cd + kv

    def left_to_right(cidx):
        qc, kc, vc, start = chunk(cidx)
        scores = lax.dot_general(qc, kc, (((1,), (1,)), ((), ())), preferred_element_type=F32) * decay
        o = jnp.dot(scores.astype(BF16), vc, preferred_element_type=F32)
        o = o + xi_f * jnp.dot(qc, state_f_scr[...].astype(BF16), preferred_element_type=F32)
        update_state(state_f_scr, kc, vc, zeta_f, cd_f)
        return o, start

    def right_to_left(cidx):
        qc, kc, vc, start = chunk(cidx)
        o = xi_b * jnp.dot(qc, state_b_scr[...].astype(BF16), preferred_element_type=F32)
        update_state(state_b_scr, kc, vc, zeta_b, cd_b)
        return o, start

    def finish(y, start):
        y = y * lax.rsqrt(jnp.mean(y * y, axis=-1, keepdims=True) + EPS) * gain_ref[...]
        o_ref[pl.ds(start, c_len), :] = y.astype(BF16)

    state_f_scr[...] = jnp.zeros_like(state_f_scr)
    state_b_scr[...] = jnp.zeros_like(state_b_scr)
    half = n_chunks // 2

    def first_visits(t, carry):
        o_f, start_f = left_to_right(t)
        o_b, start_b = right_to_left(n_chunks - 1 - t)
        acc_scr[pl.ds(start_f, c_len), :] = o_f
        acc_scr[pl.ds(start_b, c_len), :] = o_b
        return carry

    def second_visits(t, carry):
        o_f, start_f = left_to_right(t)
        o_b, start_b = right_to_left(n_chunks - 1 - t)
        finish(acc_scr[pl.ds(start_f, c_len), :] + o_f, start_f)
        finish(acc_scr[pl.ds(start_b, c_len), :] + o_b, start_b)
        return carry

    lax.fori_loop(0, half, first_visits, 0, unroll=8)
    lax.fori_loop(half, n_chunks, second_visits, 0, unroll=4)


def _retention_mixer(x, mod, w_in, decay_logit, gn_gain, w_o, batch, seq):
    n, d = x.shape
    qk_cols = RET_HEADS * RET_QK_DIM
    v_cols = RET_HEADS * RET_V_DIM
    col_scale = jnp.concatenate([jnp.ones((qk_cols,), F32), jnp.full((qk_cols,), RET_QK_DIM ** -0.5, F32),
                                 jnp.ones((2 * v_cols,), F32)])
    w = (w_in * col_scale[None, :]).astype(BF16)
    pos = jnp.arange(seq, dtype=F32)
    inv = 1.0 / (RET_THETA ** jnp.linspace(0.0, 1.0, RET_QK_DIM // 2, dtype=F32))
    ang = pos[:, None] * inv
    tm, tn = TILE_PROJ, qk_cols
    tiles_per_seq = seq // tm
    half = RET_QK_DIM // 2
    extras = [jnp.cos(ang), jnp.sin(ang)]
    extra_specs = [pl.BlockSpec((tm, half), lambda i, j: (i % tiles_per_seq, 0)),
                   pl.BlockSpec((tm, half), lambda i, j: (i % tiles_per_seq, 0))]
    proj = _project(x, mod, 0, w, extras, extra_specs, _ret_proj_epilogue,
                    jax.ShapeDtypeStruct((n, w.shape[1]), BF16), pl.BlockSpec((tm, w.shape[1]), lambda i, j: (i, 0)),
                    tm, tn, seq, "ret_proj")

    n_chunks = seq // RET_CHUNK
    k_off = qk_cols // RET_QK_DIM
    v_off = 2 * qk_cols // RET_V_DIM
    y = pl.pallas_call(
        functools.partial(_ret_kernel, n_chunks=n_chunks),
        grid=(batch, RET_HEADS),
        in_specs=[pl.BlockSpec(memory_space=pltpu.SMEM),
                  pl.BlockSpec((seq, RET_QK_DIM), lambda b, h: (b, h)),
                  pl.BlockSpec((seq, RET_QK_DIM), lambda b, h: (b, k_off + h)),
                  pl.BlockSpec((seq, RET_V_DIM), lambda b, h: (b, v_off + h)),
                  pl.BlockSpec((1, RET_V_DIM), lambda b, h: (0, 0))],
        out_specs=pl.BlockSpec((seq, RET_V_DIM), lambda b, h: (b, h)),
        out_shape=jax.ShapeDtypeStruct((n, v_cols), BF16),
        scratch_shapes=[pltpu.VMEM((seq, RET_V_DIM), F32), pltpu.VMEM((RET_QK_DIM, RET_V_DIM), F32),
                        pltpu.VMEM((RET_QK_DIM, RET_V_DIM), F32)],
        compiler_params=_cparams("parallel", "arbitrary"),
        name="retention",
    )(decay_logit.astype(F32), proj, proj, proj, gn_gain.reshape(1, RET_V_DIM).astype(F32))
    gate_col = (2 * qk_cols + v_cols) // v_cols
    return _out_project(y, 0, v_cols, w_o.astype(BF16), x, mod, 0, seq, TILE_PROJ, "ret_out",
                        gate_src=proj, gate_col=gate_col)


def _conv_pw1_epilogue(acc, j, extras, outs):
    b_ref = extras[0]
    u = acc + b_ref[...]
    dm = u.shape[1] // 2
    outs[0][...] = (u[:, :dm] * jax.nn.sigmoid(u[:, dm:])).astype(BF16)


def _conv_kernel(prev_ref, cur_ref, next_ref, wdw_ref, bdw_ref, lng_ref, lnb_ref, w2_ref, b2_ref, x_ref, mod_ref,
                 o_ref, ext_scr, shift_scr, c_scr, *, tiles_per_seq, row_block):
    tm = cur_ref.shape[0]
    t = pl.program_id(0) % tiles_per_seq
    keep_prev = jnp.where(t == 0, 0.0, 1.0)
    keep_next = jnp.where(t == tiles_per_seq - 1, 0.0, 1.0)
    ext_scr[0:CONV_HALO, :] = prev_ref[...].astype(F32) * keep_prev
    ext_scr[CONV_HALO:CONV_HALO + tm, :] = cur_ref[...].astype(F32)
    ext_scr[CONV_HALO + tm:, :] = next_ref[...].astype(F32) * keep_next
    base = CONV_HALO - CONV_PAD
    span = tm + 2 * CONV_HALO - SUBLANES
    for r in range(1, SUBLANES):
        shift_scr[r - 1, 0:span, :] = ext_scr[r:r + span, :]
    for rb in range(tm // row_block):
        r0 = rb * row_block
        acc = jnp.zeros((row_block, cur_ref.shape[1]), F32) + bdw_ref[...]
        for j in range(CONV_WIDTH):
            phase = (base + j) % SUBLANES
            start = r0 + base + j - phase
            if phase == 0:
                tap = ext_scr[start:start + row_block, :]
            else:
                tap = shift_scr[phase - 1, start:start + row_block, :]
            acc = acc + jnp.tile(wdw_ref[j], (row_block // SUBLANES, 1)) * tap
        c_scr[r0:r0 + row_block, :] = acc
    u = c_scr[...]
    mu = jnp.mean(u, axis=-1, keepdims=True)
    var = jnp.mean(jnp.square(u - mu), axis=-1, keepdims=True)
    u = (u - mu) * lax.rsqrt(var + EPS) * lng_ref[...] + lnb_ref[...]
    u = _silu(u).astype(BF16)
    z = jnp.dot(u, w2_ref[...], preferred_element_type=F32) + b2_ref[...]
    o_ref[...] = x_ref[...] + mod_ref[0][2:3] * z


def _conv_mixer(x, mod, w_pw1, b_pw1, w_dw, b_dw, ln_gain, ln_bias, w_pw2, b_pw2, seq):
    n, d = x.shape
    tm = TILE_PROJ
    u = _project(x, mod, 0, w_pw1.astype(BF16), [b_pw1.reshape(1, 2 * d).astype(F32)],
                 [pl.BlockSpec((1, 2 * d), lambda i, j: (0, 0))], _conv_pw1_epilogue,
                 jax.ShapeDtypeStruct((n, d), BF16), pl.BlockSpec((tm, d), lambda i, j: (i, 0)),
                 tm, 2 * d, seq, "conv_pw1")
    tm = TILE_CONV
    tiles_per_seq = seq // tm
    halo_per_tile = tm // CONV_HALO
    n_halo = n // CONV_HALO
    wdw = jnp.broadcast_to(w_dw.reshape(CONV_WIDTH, 1, d).astype(F32), (CONV_WIDTH, SUBLANES, d))
    vec = lambda a: a.reshape(1, d).astype(F32)
    row_spec = pl.BlockSpec((1, d), lambda i: (0, 0))
    return pl.pallas_call(
        functools.partial(_conv_kernel, tiles_per_seq=tiles_per_seq, row_block=CONV_ROW_BLOCK),
        grid=(n // tm,),
        in_specs=[pl.BlockSpec((CONV_HALO, d), lambda i: (jnp.maximum(i * halo_per_tile - 1, 0), 0)),
                  pl.BlockSpec((tm, d), lambda i: (i, 0)),
                  pl.BlockSpec((CONV_HALO, d), lambda i: (jnp.minimum((i + 1) * halo_per_tile, n_halo - 1), 0)),
                  pl.BlockSpec((CONV_WIDTH, SUBLANES, d), lambda i: (0, 0, 0)),
                  row_spec, row_spec, row_spec,
                  pl.BlockSpec((d, d), lambda i: (0, 0)),
                  row_spec,
                  pl.BlockSpec((tm, d), lambda i: (i, 0)),
                  pl.BlockSpec((1, 6, d), lambda i: (i // tiles_per_seq, 0, 0))],
        out_specs=pl.BlockSpec((tm, d), lambda i: (i, 0)),
        out_shape=jax.ShapeDtypeStruct((n, d), F32),
        scratch_shapes=[pltpu.VMEM((tm + 2 * CONV_HALO, d), F32),
                        pltpu.VMEM((SUBLANES - 1, tm + 2 * CONV_HALO, d), F32), pltpu.VMEM((tm, d), F32)],
        compiler_params=_cparams("parallel"),
        name="conv_module",
    )(u, u, u, wdw, vec(b_dw), vec(ln_gain), vec(ln_bias), w_pw2.astype(BF16), vec(b_pw2), x, mod)


def _t5_bucket(rel):
    half = T5_BUCKETS // 2
    exact = half // 2
    n = jnp.abs(rel)
    log_ratio = jnp.log(jnp.maximum(n, 1).astype(F32) / exact) / math.log(T5_MAX_DIST / exact)
    large = jnp.minimum(exact + (log_ratio * (half - exact)).astype(jnp.int32), half - 1)
    return jnp.where(rel > 0, half, 0) + jnp.where(n < exact, n, large)


def _bias_tile_kernel(table_ref, idx_ref, o_ref):
    h = pl.program_id(0)
    idx = idx_ref[0]
    acc = jnp.zeros(idx.shape, F32)
    for b in range(T5_BUCKETS):
        acc = jnp.where(idx == b, table_ref[b, h], acc)
    o_ref[0, 0] = acc * LOG2_E


def _diff_kernel(q_ref, k_ref, v_ref, bias_ref, lam_ref, gain_ref, o_ref, s_scr, m_scr, acc_scr, *, n_chunks, out_scale,
                 lambda_init):
    tq = q_ref.shape[0]
    qi = pl.program_id(2)
    q = q_ref[...]
    lane = lax.broadcasted_iota(jnp.int32, (tq, LANES), 1)
    low = lane < DIFF_HEAD_DIM
    zero = jnp.zeros((tq, LANES), BF16)
    lhs = jnp.concatenate([jnp.where(low, q, zero), jnp.where(low, zero, q)], axis=0)

    q_sub = tq // KEY_CHUNK

    def add_bias(s, j):
        tiles = [bias_ref[0, jnp.clip(j - (q_sub * qi + a), -2, 2) + 2] for a in range(q_sub)]
        return s + jnp.concatenate(tiles + tiles, axis=0)

    o = _softmax_pv(lhs, k_ref, v_ref, s_scr, m_scr, acc_scr, n_chunks, add_bias)
    lam_v = lam_ref[...]
    lam = (jnp.exp(jnp.sum(lam_v[0:1] * lam_v[1:2], axis=-1, keepdims=True))
           - jnp.exp(jnp.sum(lam_v[2:3] * lam_v[3:4], axis=-1, keepdims=True)) + lambda_init)
    y = o[:tq] - lam * o[tq:]
    y = y * lax.rsqrt(jnp.mean(y * y, axis=-1, keepdims=True) + EPS) * gain_ref[...] * out_scale
    o_ref[...] = y.astype(BF16)


def _diff_mixer(x, mod, w_qkv, lam_q1, lam_k1, lam_q2, lam_k2, subln_gain, w_o, rel_bias, lambda_init,
                batch, seq):
    n, d = x.shape
    col_scale = jnp.concatenate([jnp.full((d,), DIFF_HEAD_DIM ** -0.5 * LOG2_E, F32), jnp.ones((2 * d,), F32)])
    w = (w_qkv * col_scale[None, :]).astype(BF16)
    tm = TILE_PROJ_NARROW
    qkv = _project(x, mod, 0, w, [], [], _plain_epilogue,
                   jax.ShapeDtypeStruct((n, 3 * d), BF16), pl.BlockSpec((tm, 3 * d), lambda i, j: (i, 0)),
                   tm, d, seq, "diff_qkv")

    tb = KEY_CHUNK
    ar = jnp.arange(KEY_CHUNK, dtype=jnp.int32)
    offs = (jnp.arange(5, dtype=jnp.int32) - 2) * KEY_CHUNK
    rel = (ar[None, None, :] + offs[:, None, None]) - ar[None, :, None]
    bucket = _t5_bucket(rel).astype(jnp.int32)
    bias = pl.pallas_call(
        _bias_tile_kernel,
        grid=(DIFF_HEADS, 5),
        in_specs=[pl.BlockSpec(memory_space=pltpu.SMEM),
                  pl.BlockSpec((1, tb, KEY_CHUNK), lambda h, t: (t, 0, 0))],
        out_specs=pl.BlockSpec((1, 1, tb, KEY_CHUNK), lambda h, t: (h, t, 0, 0)),
        out_shape=jax.ShapeDtypeStruct((DIFF_HEADS, 5, tb, KEY_CHUNK), F32),
        compiler_params=_cparams("arbitrary", "arbitrary"),
        name="rel_bias_tiles",
    )(rel_bias.astype(F32), bucket)

    lam = jnp.stack([lam_q1, lam_k1, lam_q2, lam_k2]).astype(F32)
    n_chunks = seq // KEY_CHUNK
    tq = TILE_DIFF_Q
    q_tiles = seq // tq
    o = pl.pallas_call(
        functools.partial(_diff_kernel, n_chunks=n_chunks, out_scale=1.0 - lambda_init, lambda_init=lambda_init),
        grid=(batch, DIFF_HEADS, q_tiles),
        in_specs=[pl.BlockSpec((tq, LANES), lambda b, h, i: (b * q_tiles + i, h)),
                  pl.BlockSpec((seq, LANES), lambda b, h, i: (b, DIFF_HEADS + h)),
                  pl.BlockSpec((seq, LANES), lambda b, h, i: (b, 2 * DIFF_HEADS + h)),
                  pl.BlockSpec((1, 5, tb, KEY_CHUNK), lambda b, h, i: (h, 0, 0, 0)),
                  pl.BlockSpec((4, DIFF_HEAD_DIM), lambda b, h, i: (0, 0)),
                  pl.BlockSpec((1, LANES), lambda b, h, i: (0, 0))],
        out_specs=pl.BlockSpec((tq, LANES), lambda b, h, i: (b * q_tiles + i, h)),
        out_shape=jax.ShapeDtypeStruct((n, d), BF16),
        scratch_shapes=[pltpu.VMEM((n_chunks, 2 * tq, KEY_CHUNK), F32),
                        pltpu.VMEM((2 * tq, LANES), F32), pltpu.VMEM((2 * tq, 2 * LANES), F32)],
        compiler_params=_cparams("parallel", "parallel", "arbitrary"),
        name="diff_attention",
    )(qkv, qkv, qkv, bias, lam, subln_gain.reshape(1, LANES).astype(F32))
    return _out_project(o, 0, d, w_o.astype(BF16), x, mod, 0, seq, TILE_PROJ_NARROW, "diff_out")


def _final_norm_kernel(x_ref, g_ref, o_ref):
    x = x_ref[...]
    o_ref[...] = x * lax.rsqrt(jnp.mean(x * x, axis=-1, keepdims=True) + EPS) * g_ref[...]


def _final_norm(x, gain, tm):
    n, d = x.shape
    return pl.pallas_call(
        _final_norm_kernel,
        grid=(n // tm,),
        in_specs=[pl.BlockSpec((tm, d), lambda i: (i, 0)), pl.BlockSpec((1, d), lambda i: (0, 0))],
        out_specs=pl.BlockSpec((tm, d), lambda i: (i, 0)),
        out_shape=jax.ShapeDtypeStruct((n, d), F32),
        compiler_params=_cparams("parallel"),
        name="final_norm",
    )(x, gain.reshape(1, d).astype(F32))


def kernel(x, c, mod_w, mod_b, att_w_qkv, att_q_gain, att_k_gain, att_w_o, ret_w_in, ret_decay_logit, ret_gn_gain, ret_w_o, cnv_w_pw1, cnv_b_pw1, cnv_w_dw, cnv_b_dw, cnv_ln_gain, cnv_ln_bias, cnv_w_pw2, cnv_b_pw2, dif_w_qkv, dif_lam_q1, dif_lam_k1, dif_lam_q2, dif_lam_k2, dif_subln_gain, dif_w_o, rel_bias, ffn_w_gate, ffn_w_up, ffn_w_down, moe_w_router, moe_b_router, moe_w_gate, moe_w_up, moe_w_down, final_gain):
    batch, seq, d = x.shape
    depth = mod_w.shape[0]
    xf = x.reshape(batch * seq, d)
    mods = _modulation(c, mod_w, mod_b)
    moe_gate, moe_up, moe_down = moe_w_gate.astype(BF16), moe_w_up.astype(BF16), moe_w_down.astype(BF16)
    for i in range(depth):
        mod = mods[i]
        kind, j = i % 4, i // 4
        if kind == 0:
            xf = _attention_mixer(xf, mod, att_w_qkv[j], att_q_gain[j], att_k_gain[j], att_w_o[j], batch, seq)
        elif kind == 1:
            xf = _retention_mixer(xf, mod, ret_w_in[j], ret_decay_logit[j], ret_gn_gain[j], ret_w_o[j], batch, seq)
        elif kind == 2:
            xf = _conv_mixer(xf, mod, cnv_w_pw1[j], cnv_b_pw1[j], cnv_w_dw[j], cnv_b_dw[j], cnv_ln_gain[j],
                             cnv_ln_bias[j], cnv_w_pw2[j], cnv_b_pw2[j], seq)
        else:
            lambda_init = 0.8 - 0.6 * math.exp(-0.3 * i)
            xf = _diff_mixer(xf, mod, dif_w_qkv[j], dif_lam_q1[j], dif_lam_k1[j], dif_lam_q2[j], dif_lam_k2[j],
                             dif_subln_gain[j], dif_w_o[j], rel_bias, lambda_init, batch, seq)
        if i % 2 == 0:
            m = i // 2
            xf = _ffn(xf, mod, ffn_w_gate[m].astype(BF16), ffn_w_up[m].astype(BF16), ffn_w_down[m].astype(BF16),
                      seq, TILE_PROJ)
        else:
            m = i // 2
            xf = _moe(xf, mod, moe_w_router[m], moe_b_router[m], moe_gate, moe_up, moe_down, m, seq,
                      out_gain=final_gain if i == depth - 1 else None)
    if depth % 2 == 1:
        xf = _final_norm(xf, final_gain, TILE_PROJ)
    return xf.reshape(batch, seq, d)
```
